```python
import math
import jax
import jax.numpy as jnp
from jax import lax
import numpy as np

D_MODEL = 1024
BATCH = 16
SEQ = 256
DEPTH = 2
DEC_BATCH = 4
DEC_SEQ = 2048
PAST_LEN = 256

GRID_W = 64
RMS_EPS = 1e-6
F32 = jnp.float32
DN_HEADS = 4
DN_DK = 128
DN_DV = 128
DN_W = DN_HEADS * DN_DK
DN_CONV = 5
DN_CHUNK = 64
DA_HEADS = 4
DA_DH = 64
DA_DV = 2 * DA_DH
DA_QK = DA_HEADS * 2 * DA_DH
DA_V = DA_HEADS * DA_DV
Q_BLOCK = 128
ROPE_BASE = 10000.0
HY_W = 512
HY_SHORT = 3
HY_BANDS = 8
HY_EMB = 1 + 2 * HY_BANDS
HY_ORDER = 64
HY_FAST_DECAY = 0.3
HY_SLOW_DECAY = 1.5
HY_TARGET = 1e-2
N_BRANCH = 3
D_FF = ((8 * D_MODEL // 3 + 255) // 256) * 256
COL_SIZES = (DN_W, DN_W, DN_W, 2 * DN_HEADS, 2 * DN_HEADS, DN_W,
             DA_QK, DA_QK, DA_V, 3 * HY_W, N_BRANCH * D_MODEL)
N_IN = sum(COL_SIZES)

kernel_name = 'hybrid_diffusion_deltanet_diffattn_hyena_step'


def rmsnorm(x, g):
    xf = x.astype(F32)
    y = xf * lax.rsqrt(jnp.mean(xf * xf, axis=-1, keepdims=True) + RMS_EPS)
    return (y * g.astype(F32)).astype(x.dtype)


def l2norm(x):
    return x * lax.rsqrt(jnp.sum(x * x, axis=-1, keepdims=True) + RMS_EPS)


def split_columns(p):
    idx = []
    acc = 0
    for s in COL_SIZES[:-1]:
        acc += s
        idx.append(acc)
    return jnp.split(p, idx, axis=-1)


def depthwise_conv_centred(x, w, b=None):
    k, ch = w.shape
    pad = k // 2
    y = lax.conv_general_dilated(x, w[:, None, :].astype(x.dtype), window_strides=(1,),
                                 padding=[(pad, pad)], dimension_numbers=('NWC', 'WIO', 'NWC'),
                                 feature_group_count=ch)
    if b is not None:
        y = y + b.astype(x.dtype)
    return y


def axial_rope_tables(n_tok):
    rows = n_tok // GRID_W
    row_id = jnp.repeat(jnp.arange(rows), GRID_W).astype(F32)
    col_id = (jnp.arange(rows * GRID_W) % GRID_W).astype(F32)
    half = DA_DH // 2
    inv = ROPE_BASE ** (-jnp.arange(0, half, 2, dtype=F32) / half)
    ang_r = row_id[:, None] * inv
    ang_c = col_id[:, None] * inv
    return (jnp.cos(ang_r), jnp.sin(ang_r), jnp.cos(ang_c), jnp.sin(ang_c))


def _rotate(x, cos, sin):
    x1, x2 = jnp.split(x, 2, axis=-1)
    return jnp.concatenate([x1 * cos - x2 * sin, x1 * sin + x2 * cos], axis=-1)


def apply_axial_rope(x, tables):
    cr, sr, cc, sc = [t[None, :, None, None, :] for t in tables]
    xr, xc = jnp.split(x.astype(F32), 2, axis=-1)
    return jnp.concatenate([_rotate(xr, cr, sr), _rotate(xc, cc, sc)], axis=-1).astype(x.dtype)


def chunk_gated_delta(q, k, v, g, beta, s0):
    b, h, n_tok, _ = q.shape
    dv = v.shape[-1]
    c = DN_CHUNK
    n = n_tok // c
    q, k, v = (t.reshape(b, h, n, c, t.shape[-1]) for t in (q, k, v))
    g = g.reshape(b, h, n, c)
    beta = beta.reshape(b, h, n, c)
    gc = jnp.cumsum(g, axis=-1)
    tril = jnp.tril(jnp.ones((c, c), dtype=bool))
    strict = jnp.tril(jnp.ones((c, c), dtype=bool), -1)
    decay = jnp.exp(jnp.where(tril, gc[..., :, None] - gc[..., None, :], -jnp.inf))
    k_beta = k * beta[..., None]
    a = jnp.where(strict, jnp.einsum('bhnik,bhnjk->bhnij', k_beta, k) * decay, 0.0)
    m = a + jnp.eye(c, dtype=a.dtype)
    solve = lambda rhs: lax.linalg.triangular_solve(m, rhs, left_side=True, lower=True,
                                                    unit_diagonal=True)
    u = solve(v * beta[..., None])
    w = solve(k_beta * jnp.exp(gc)[..., None])
    qk = jnp.einsum('bhnik,bhnjk->bhnij', q, k) * decay

    def step(s, inp):
        q_i, k_i, u_i, w_i, qk_i, gc_i = inp
        v_new = u_i - jnp.einsum('bhck,bhkv->bhcv', w_i, s)
        o_i = (jnp.einsum('bhck,bhkv->bhcv', q_i * jnp.exp(gc_i)[..., None], s)
               + jnp.einsum('bhij,bhjv->bhiv', qk_i, v_new))
        g_last = gc_i[..., -1:]
        s = (s * jnp.exp(g_last)[..., None]
             + jnp.einsum('bhck,bhcv->bhkv', k_i * jnp.exp(g_last - gc_i)[..., None], v_new))
        return s, o_i

    xs = tuple(jnp.moveaxis(t, 2, 0) for t in (q, k, u, w, qk, gc))
    s_fin, o = lax.scan(step, s0, xs)
    return jnp.moveaxis(o, 0, 2).reshape(b, h, n_tok, dv), s_fin


def gated_deltanet(q, k, v, beta_logit, alpha_logit, z, conv_w, a_log, dt_bias, norm_g, s0):
    dt = q.dtype
    b, n_tok, _ = q.shape
    qkv = jax.nn.silu(depthwise_conv_centred(jnp.concatenate([q, k, v], axis=-1), conv_w))
    q, k, v = jnp.split(qkv.astype(F32), 3, axis=-1)
    heads = lambda t: t.reshape(b, n_tok, DN_HEADS, -1).transpose(0, 2, 1, 3)
    q = l2norm(heads(q)) * (DN_DK ** -0.5)
    k = l2norm(heads(k))
    v = heads(v)
    per_dir = lambda t: t.astype(F32).reshape(b, n_tok, 2, DN_HEADS).transpose(2, 0, 3, 1)
    beta = jax.nn.sigmoid(per_dir(beta_logit))
    g = -(jnp.exp(a_log.astype(F32))[:, None, :, None]
          * jax.nn.softplus(per_dir(alpha_logit) + dt_bias.astype(F32)[:, None, :, None]))
    s0 = s0.astype(F32)
    o_f, s_f = chunk_gated_delta(q, k, v, g[0], beta[0], s0[:, 0])
    flip = lambda t: jnp.flip(t, axis=2)
    o_b, s_b = chunk_gated_delta(flip(q), flip(k), flip(v), flip(g[1]), flip(beta[1]), s0[:, 1])
    o = (o_f + flip(o_b)).transpose(0, 2, 1, 3)
    o = rmsnorm(o, norm_g) * jax.nn.silu(z.astype(F32).reshape(b, n_tok, DN_HEADS, DN_DV))
    return o.reshape(b, n_tok, DN_W).astype(dt), jnp.stack([s_f, s_b], axis=1).astype(dt)


def diff_softmax_attention(q, k, v, lam):
    b, h, n_q = q.shape[:3]
    nb = n_q // Q_BLOCK
    qb = jnp.moveaxis(q.reshape(b, h, nb, Q_BLOCK, 2, DA_DH), 2, 0)
    scale = DA_DH ** -0.5

    def one_block(qi):
        s = jnp.einsum('bhqmd,bhkmd->bhmqk', qi, k).astype(F32) * scale
        p = jax.nn.softmax(s, axis=-1)
        a = p[:, :, 0] - lam * p[:, :, 1]
        return jnp.einsum('bhqk,bhkv->bhqv', a.astype(v.dtype), v)

    o = lax.map(one_block, qb)
    return jnp.moveaxis(o, 0, 2).reshape(b, h, n_q, DA_DV)


def differential_attention(q, k, v, lam_p, subln_g, layer, rope, ctx_k, ctx_v):
    b, n_tok, _ = q.shape
    q = q.reshape(b, n_tok, DA_HEADS, 2, DA_DH)
    k = k.reshape(b, n_tok, DA_HEADS, 2, DA_DH)
    v = v.reshape(b, n_tok, DA_HEADS, DA_DV).transpose(0, 2, 1, 3)
    lam_init = 0.8 - 0.6 * math.exp(-0.3 * layer)
    lp = lam_p.astype(F32)
    lam = jnp.exp(jnp.sum(lp[0] * lp[1])) - jnp.exp(jnp.sum(lp[2] * lp[3])) + lam_init
    if rope is not None:
        q = apply_axial_rope(q, rope)
        k = apply_axial_rope(k, rope)
    q = q.transpose(0, 2, 1, 3, 4)
    k = k.transpose(0, 2, 1, 3, 4)
    if ctx_k is None:
        keys, vals = k, v
    else:
        n_ctx = ctx_k.shape[2]
        keys = jnp.concatenate([ctx_k.reshape(b, DA_HEADS, n_ctx, 2, DA_DH).astype(k.dtype), k], axis=2)
        vals = jnp.concatenate([ctx_v.astype(v.dtype), v], axis=2)
    o = diff_softmax_attention(q, keys, vals, lam)
    o = rmsnorm(o, subln_g) * (1.0 - lam_init)
    o = o.transpose(0, 2, 1, 3).reshape(b, n_tok, DA_V)
    if ctx_k is None:
        return o, (k.reshape(b, DA_HEADS, n_tok, DA_DV), v)
    return o, None


def hyena_filter(n, w1, b1, freq, w2, b2, w3):
    j = jnp.arange(n, dtype=F32)
    t = j / (n - 1)
    ang = (2.0 * math.pi * j / n)[:, None] * jnp.linspace(1e-4, HY_BANDS - 1, HY_BANDS, dtype=F32)
    z = jnp.concatenate([t[:, None], jnp.cos(ang), -jnp.sin(ang)], axis=-1)
    fr = freq.astype(F32)
    hdn = jnp.sin(fr * (z @ w1.astype(F32) + b1.astype(F32)))
    hdn = jnp.sin(fr * (hdn @ w2.astype(F32) + b2.astype(F32)))
    h = hdn @ w3.astype(F32)
    half = n // 2
    dist = jnp.abs(j - half) / half
    max_decay = math.log(HY_TARGET) / HY_FAST_DECAY
    min_decay = math.log(HY_TARGET) / HY_SLOW_DECAY
    deltas = jnp.abs(jnp.linspace(min_decay, max_decay, HY_W, dtype=F32))
    h = h * jnp.exp(-dist[:, None] * deltas)
    return h / jnp.sum(jnp.abs(h), axis=0, keepdims=True)


def hyena(u, conv_w, conv_b, w1, b1, freq, w2, b2, w3, d_skip):
    n_tok = u.shape[1]
    uc = depthwise_conv_centred(u, conv_w, conv_b)
    x0, x1, v = jnp.split(uc, 3, axis=-1)
    h = hyena_filter(n_tok, w1, b1, freq, w2, b2, w3)
    gv = (v * x1).astype(F32)
    n_fft = 2 * n_tok
    y = jnp.fft.irfft(jnp.fft.rfft(gv, n=n_fft, axis=1) * jnp.fft.rfft(h, n=n_fft, axis=0)[None],
                      n=n_fft, axis=1)[:, n_tok // 2: n_tok // 2 + n_tok]
    y = y + gv * d_skip.astype(F32)
    return (y * x0.astype(F32)).astype(u.dtype)


def token_mixers(h, layer, lp, rope, ctx):
    b = h.shape[0]
    (q_a, k_a, v_a, beta_a, alpha_a, z_a, q_b, k_b, v_b, u_c, gate_logits) = split_columns(h @ lp['w_in'])
    s0 = jnp.zeros((b, 2, DN_HEADS, DN_DK, DN_DV), F32) if ctx is None else ctx[2]
    o_a, s_fin = gated_deltanet(q_a, k_a, v_a, beta_a, alpha_a, z_a, lp['dn_conv_w'], lp['dn_a_log'],
                                lp['dn_dt_bias'], lp['dn_norm_g'], s0)
    o_b, kv = differential_attention(q_b, k_b, v_b, lp['da_lambda'], lp['da_subln_g'], layer, rope,
                                     None if ctx is None else ctx[0], None if ctx is None else ctx[1])
    o_c = hyena(u_c, lp['hy_conv_w'], lp['hy_conv_b'], lp['hy_w1'], lp['hy_b1'], lp['hy_freq'],
                lp['hy_w2'], lp['hy_b2'], lp['hy_w3'], lp['hy_d'])
    g_a, g_b, g_c = jnp.split(jax.nn.sigmoid(gate_logits), 3, axis=-1)
    merged = g_a * (o_a @ lp['w_br_a']) + g_b * (o_b @ lp['w_br_b']) + g_c * (o_c @ lp['w_br_c'])
    out = merged @ lp['w_out']
    new_ctx = (kv[0], kv[1], s_fin) if ctx is None else None
    return out, new_ctx


def trunk_layer(x, cond, layer, lp, rope, ctx):
    mod = cond @ lp['w_mod'] + lp['b_mod']
    sh1, sc1, g1, sh2, sc2, g2 = jnp.split(mod[:, None, :], 6, axis=-1)
    hmix = rmsnorm(x, lp['norm1_g']) * (1 + sc1) + sh1
    mix, new_ctx = token_mixers(hmix, layer, lp, rope, ctx)
    x = x + g1 * mix
    hffn = rmsnorm(x, lp['norm2_g']) * (1 + sc2) + sh2
    gate, up = jnp.split(hffn @ lp['w_ffn_in'], 2, axis=-1)
    x = x + g2 * ((jax.nn.silu(gate) * up) @ lp['w_ffn_out'])
    return x, new_ctx


def setup_inputs(seed: int = 0) -> dict:
    key = jax.random.key(seed)
    ks = iter(jax.random.split(key, 48))
    nrm = lambda shape, s: jax.random.normal(next(ks), shape, F32) * s
    D = D_MODEL
    dt = jnp.exp(jax.random.uniform(next(ks), (DEPTH, 2, DN_HEADS), F32,
                                    minval=math.log(1e-3), maxval=math.log(1e-1)))
    return {
        'x_prompt': nrm((BATCH, SEQ, D), 1.0),
        'x_sample': nrm((DEC_BATCH, DEC_SEQ, D), 1.0),
        'cache_k': nrm((DEC_BATCH, DEPTH, DA_HEADS, PAST_LEN, DA_DV), 1.0),
        'cache_v': nrm((DEC_BATCH, DEPTH, DA_HEADS, PAST_LEN, DA_DV), 1.0),
        'state_dn': nrm((DEC_BATCH, DEPTH, 2, DN_HEADS, DN_DK, DN_DV), DN_DK ** -0.5),
        'c': nrm((DEC_BATCH, D), 1.0),
        'c_ctx': nrm((D,), 1.0),
        'norm1_g': 1.0 + nrm((DEPTH, D), 0.02),
        'norm2_g': 1.0 + nrm((DEPTH, D), 0.02),
        'w_mod': nrm((DEPTH, D, 6 * D), 0.5 * D ** -0.5),
        'b_mod': nrm((DEPTH, 6 * D), 0.02),
        'w_in': nrm((DEPTH, D, N_IN), D ** -0.5),
        'dn_conv_w': nrm((DEPTH, DN_CONV, 3 * DN_W), DN_CONV ** -0.5),
        'dn_a_log': jnp.log(jax.random.uniform(next(ks), (DEPTH, 2, DN_HEADS), F32, minval=1.0, maxval=16.0)),
        'dn_dt_bias': dt + jnp.log(-jnp.expm1(-dt)),
        'dn_norm_g': 1.0 + nrm((DEPTH, DN_DV), 0.02),
        'da_lambda': nrm((DEPTH, 4, DA_DH), 0.1),
        'da_subln_g': 1.0 + nrm((DEPTH, DA_DV), 0.02),
        'hy_conv_w': nrm((DEPTH, HY_SHORT, 3 * HY_W), HY_SHORT ** -0.5),
        'hy_conv_b': nrm((DEPTH, 3 * HY_W), 0.02),
        'hy_w1': nrm((DEPTH, HY_EMB, HY_ORDER), HY_EMB ** -0.5),
        'hy_b1': nrm((DEPTH, HY_ORDER), 0.02),
        'hy_freq': 1.0 + nrm((DEPTH, HY_ORDER), 0.02),
        'hy_w2': nrm((DEPTH, HY_ORDER, HY_ORDER), HY_ORDER ** -0.5),
        'hy_b2': nrm((DEPTH, HY_ORDER), 0.02),
        'hy_w3': nrm((DEPTH, HY_ORDER, HY_W), HY_ORDER ** -0.5),
        'hy_d': nrm((DEPTH, HY_W), 1.0),
        'w_br_a': nrm((DEPTH, DN_W, D), DN_W ** -0.5),
        'w_br_b': nrm((DEPTH, DA_V, D), DA_V ** -0.5),
        'w_br_c': nrm((DEPTH, HY_W, D), HY_W ** -0.5),
        'w_out': nrm((DEPTH, D, D), D ** -0.5),
        'w_ffn_in': nrm((DEPTH, D, 2 * D_FF), D ** -0.5),
        'w_ffn_out': nrm((DEPTH, D_FF, D), D_FF ** -0.5),
        'final_g': 1.0 + nrm((D,), 0.02),
    }


def reference(x_prompt, x_sample, cache_k, cache_v, state_dn, c, c_ctx, norm1_g, norm2_g, w_mod, b_mod,
              w_in, dn_conv_w, dn_a_log, dn_dt_bias, dn_norm_g, da_lambda, da_subln_g, hy_conv_w,
              hy_conv_b, hy_w1, hy_b1, hy_freq, hy_w2, hy_b2, hy_w3, hy_d, w_br_a, w_br_b, w_br_c,
              w_out, w_ffn_in, w_ffn_out, final_g):
    rope = axial_rope_tables(x_sample.shape[1])
    ctx_cond = jax.nn.silu(c_ctx)[None, :]
    lat_cond = jax.nn.silu(c)
    xp = x_prompt
    xs = x_sample
    new_k, new_v, new_s = [], [], []
    for l in range(DEPTH):
        lp = {
            'norm1_g': norm1_g[l], 'norm2_g': norm2_g[l], 'w_mod': w_mod[l], 'b_mod': b_mod[l],
            'w_in': w_in[l], 'dn_conv_w': dn_conv_w[l], 'dn_a_log': dn_a_log[l],
            'dn_dt_bias': dn_dt_bias[l], 'dn_norm_g': dn_norm_g[l], 'da_lambda': da_lambda[l],
            'da_subln_g': da_subln_g[l], 'hy_conv_w': hy_conv_w[l], 'hy_conv_b': hy_conv_b[l],
            'hy_w1': hy_w1[l], 'hy_b1': hy_b1[l], 'hy_freq': hy_freq[l], 'hy_w2': hy_w2[l],
            'hy_b2': hy_b2[l], 'hy_w3': hy_w3[l], 'hy_d': hy_d[l], 'w_br_a': w_br_a[l],
            'w_br_b': w_br_b[l], 'w_br_c': w_br_c[l], 'w_out': w_out[l], 'w_ffn_in': w_ffn_in[l],
            'w_ffn_out': w_ffn_out[l],
        }
        xp, (k_l, v_l, s_l) = trunk_layer(xp, ctx_cond, l, lp, None, None)
        new_k.append(k_l)
        new_v.append(v_l)
        new_s.append(s_l)
        xs, _ = trunk_layer(xs, lat_cond, l, lp, rope, (cache_k[:, l], cache_v[:, l], state_dn[:, l]))
    y_prompt = rmsnorm(xp, final_g)
    y_sample = rmsnorm(xs, final_g)
    new_cache_k = jnp.stack(new_k, axis=1)
    new_cache_v = jnp.stack(new_v, axis=1)
    new_state_dn = jnp.stack(new_s, axis=1)
    return (y_prompt, y_sample, new_cache_k, new_cache_v, new_state_dn)
```

```python
import functools
import math

import jax
import jax.numpy as jnp
from jax import lax
from jax.experimental import pallas as pl
from jax.experimental.pallas import tpu as pltpu

F32 = jnp.float32
BF16 = jnp.bfloat16

D_MODEL = 1024
BATCH = 16
SEQ = 256
DEPTH = 2
DEC_BATCH = 4
DEC_SEQ = 2048
PAST_LEN = 256
GRID_W = 64
RMS_EPS = 1e-6
DN_HEADS = 4
DN_DK = 128
DN_DV = 128
DN_W = DN_HEADS * DN_DK
DN_CONV = 5
DN_CHUNK = 64
DA_HEADS = 4
DA_DH = 64
DA_DV = 2 * DA_DH
ROPE_BASE = 10000.0
HY_W = 512
HY_SHORT = 3
HY_BANDS = 8
HY_EMB = 1 + 2 * HY_BANDS
HY_ORDER = 64
HY_FAST_DECAY = 0.3
HY_SLOW_DECAY = 1.5
HY_TARGET = 1e-2
D_FF = ((8 * D_MODEL // 3 + 255) // 256) * 256

N_PROMPT = BATCH * SEQ
N_SAMPLE = DEC_BATCH * DEC_SEQ
N_TOK = N_PROMPT + N_SAMPLE
LANES = 128
N_MAIN = 8192
CB_QA, CB_KA, CB_VA, CB_ZA = 0, 4, 8, 12
CB_QB, CB_KB, CB_VB = 16, 20, 24
CB_X0, CB_X1, CB_HV = 28, 32, 36
CB_GATE = 40
VMEM_LIMIT = 56 * 1024 * 1024


def _cparams(sem):
    return pltpu.CompilerParams(dimension_semantics=sem, vmem_limit_bytes=VMEM_LIMIT)


def _bdot(a, b):
    return jnp.dot(a.astype(BF16), b.astype(BF16), preferred_element_type=F32)


def _bdot_nt(a, b):
    return lax.dot_general(a.astype(BF16), b.astype(BF16), (((1,), (1,)), ((), ())),
                           preferred_element_type=F32)


def _bdot_tn(a, b):
    return lax.dot_general(a.astype(BF16), b.astype(BF16), (((0,), (0,)), ((), ())),
                           preferred_element_type=F32)


def _split3(x):
    hi = x.astype(BF16)
    r = x - hi.astype(F32)
    mid = r.astype(BF16)
    lo = (r - mid.astype(F32)).astype(BF16)
    return hi, mid, lo


def _dot_exact_lhs(t, x):
    hi, mid, lo = _split3(x)
    tb = t.astype(BF16)
    d = lambda p: jnp.dot(tb, p, preferred_element_type=F32)
    return d(hi) + d(mid) + d(lo)


def _dot_exact_rhs(x, e):
    hi, mid, lo = _split3(x)
    eb = e.astype(BF16)
    d = lambda p: jnp.dot(p, eb, preferred_element_type=F32)
    return d(hi) + d(mid) + d(lo)


def _dot3(a, b):
    ah = a.astype(BF16)
    al = (a - ah.astype(F32)).astype(BF16)
    bh = b.astype(BF16)
    bl = (b - bh.astype(F32)).astype(BF16)
    d = lambda p, q: jnp.dot(p, q, preferred_element_type=F32)
    return d(ah, bh) + d(ah, bl) + d(al, bh)


def _silu(x):
    return x * jax.nn.sigmoid(x)


def _rms(x, g):
    return x * lax.rsqrt(jnp.mean(x * x, axis=-1, keepdims=True) + RMS_EPS) * g


def _mod_row(i, tm):
    n_prompt_tiles = N_PROMPT // tm
    tiles_per_seq = DEC_SEQ // tm
    return jnp.where(i < n_prompt_tiles, 0, 1 + (i - n_prompt_tiles) // tiles_per_seq)


def _mod_kernel(c_ref, w_ref, b_ref, o_ref):
    o_ref[...] = _bdot(_silu(c_ref[...]), w_ref[...]) + b_ref[...]


def _modulation(cond8, w_mod, b_mod):
    n = 6 * D_MODEL
    tn = 1024
    return pl.pallas_call(
        _mod_kernel,
        grid=(n // tn,),
        in_specs=[pl.BlockSpec((8, D_MODEL), lambda j: (0, 0)),
                  pl.BlockSpec((D_MODEL, tn), lambda j: (0, j)),
                  pl.BlockSpec((1, tn), lambda j: (0, j))],
        out_specs=pl.BlockSpec((8, tn), lambda j: (0, j)),
        out_shape=jax.ShapeDtypeStruct((8, n), F32),
        compiler_params=_cparams(("arbitrary",)),
        name="modulation",
    )(cond8, w_mod, b_mod.reshape(1, n))


IN_TM = 1024
IN_TN = 1024
ROW_CHUNK = 256


def _inproj_kernel(x_ref, g_ref, mod_ref, wm_ref, ws_ref, p_ref, ps_ref, h_scr):
    @pl.when(pl.program_id(1) == 0)
    def _():
        g = g_ref[...]
        sh = mod_ref[0, 0:1, :]
        sc1 = 1.0 + mod_ref[0, 1:2, :]

        def body(r, carry):
            rows = pl.ds(pl.multiple_of(r * ROW_CHUNK, ROW_CHUNK), ROW_CHUNK)
            hb = (_rms(x_ref[rows, :], g) * sc1 + sh).astype(BF16)
            h_scr[rows, :] = hb
            ps_ref[rows, :] = jnp.dot(hb, ws_ref[...], preferred_element_type=F32)
            return carry

        lax.fori_loop(0, IN_TM // ROW_CHUNK, body, 0)

    p_ref[...] = jnp.dot(h_scr[...], wm_ref[...], preferred_element_type=F32).astype(BF16)


def _in_projection(x, norm_g, mods, w_main, w_small):
    grid = (N_TOK // IN_TM, N_MAIN // IN_TN)
    return pl.pallas_call(
        _inproj_kernel,
        grid=grid,
        in_specs=[pl.BlockSpec((IN_TM, D_MODEL), lambda i, j: (i, 0)),
                  pl.BlockSpec((1, D_MODEL), lambda i, j: (0, 0)),
                  pl.BlockSpec((1, 6, D_MODEL), lambda i, j: (_mod_row(i, IN_TM), 0, 0)),
                  pl.BlockSpec((D_MODEL, IN_TN), lambda i, j: (0, j)),
                  pl.BlockSpec((D_MODEL, LANES), lambda i, j: (0, 0))],
        out_specs=[pl.BlockSpec((IN_TM, IN_TN), lambda i, j: (i, j)),
                   pl.BlockSpec((IN_TM, LANES), lambda i, j: (i, 0))],
        out_shape=[jax.ShapeDtypeStruct((N_TOK, N_MAIN), BF16),
                   jax.ShapeDtypeStruct((N_TOK, LANES), F32)],
        scratch_shapes=[pltpu.VMEM((IN_TM, D_MODEL), BF16)],
        compiler_params=_cparams(("arbitrary", "arbitrary")),
        name="in_projection",
    )(x, norm_g.reshape(1, D_MODEL), mods, w_main, w_small)


DN_UNIT = 2 * DN_CHUNK
DN_GROUP = 2


def _centred_conv(x, w_ref, n_taps, bias=None):
    n = x.shape[0]
    row = lax.broadcasted_iota(jnp.int32, x.shape, 0)
    half = n_taps // 2
    acc = x * w_ref[half:half + 1, :]
    for tap in range(n_taps):
        d = tap - half
        if d == 0:
            continue
        shifted = pltpu.roll(x, (-d) % n, axis=0)
        valid = (row + d >= 0) & (row + d < n)
        acc = acc + jnp.where(valid, shifted, 0.0) * w_ref[tap:tap + 1, :]
    if bias is not None:
        acc = acc + bias
    return acc


def _l2norm(x):
    return x * lax.rsqrt(jnp.sum(x * x, axis=-1, keepdims=True) + RMS_EPS)


def _softplus(x):
    return jnp.maximum(x, 0.0) + jnp.log1p(jnp.exp(-jnp.abs(x)))


def _dn_unit(q, k, v, beta, g, backward):
    u, c = DN_UNIT, DN_CHUNK
    ri = lax.broadcasted_iota(jnp.int32, (u, u), 0)
    ci = lax.broadcasted_iota(jnp.int32, (u, u), 1)
    same = (ri // c) == (ci // c)
    if backward:
        incl, strict = same & (ri <= ci), same & (ri < ci)
    else:
        incl, strict = same & (ri >= ci), same & (ri > ci)
    hi_rows = ri >= c
    gc = _dot_exact_lhs(incl.astype(F32), g)
    if backward:
        g_tot = (gc[0:1, :], gc[c:c + 1, :])
    else:
        g_tot = (gc[c - 1:c, :], gc[u - 1:u, :])
    gc_row = jnp.sum(jnp.where(ri == ci, gc, 0.0), axis=0, keepdims=True)
    dec = jnp.exp(jnp.where(incl, gc - gc_row, -1e30))
    e_gc = jnp.exp(gc)
    kb = k * beta
    kbf = k.astype(BF16)
    a = jnp.where(strict, _bdot_nt(kb, kbf) * dec, 0.0)
    qk = _bdot_nt(q, kbf) * dec
    x = -a
    r = x
    for _ in range(5):
        x = _dot3(x, x)
        r = r + x + _dot3(r, x)
    rhs = jnp.concatenate([v * beta, kb * e_gc], axis=1)
    uw = rhs + _bdot(r, rhs)
    kd = k * jnp.exp(jnp.where(hi_rows, g_tot[1], g_tot[0]) - gc)
    return uw[:, :DN_DV], uw[:, DN_DV:], q * e_gc, qk, kd, (jnp.exp(g_tot[0]), jnp.exp(g_tot[1]))


def _dn_kernel(*refs, seq_len, has_state):
    if has_state:
        (q_ref, k_ref, v_ref, z_ref, ps_ref, cwq_ref, cwk_ref, cwv_ref, alog_ref, dtb_ref, ng_ref,
         s0_ref, o_ref, q_s, k_s, v_s, bt_s, g_s, oacc_s, u_s, wq_s, qk_s, kd_s, egl_s) = refs
        sfin_ref = None
    else:
        (q_ref, k_ref, v_ref, z_ref, ps_ref, cwq_ref, cwk_ref, cwv_ref, alog_ref, dtb_ref, ng_ref,
         o_ref, sfin_ref, q_s, k_s, v_s, bt_s, g_s, oacc_s, u_s, wq_s, qk_s, kd_s, egl_s) = refs
        s0_ref = None
    n_units = seq_len // DN_UNIT
    c = DN_CHUNK
    head = pl.program_id(1)

    q = _l2norm(_silu(_centred_conv(q_ref[...].astype(F32), cwq_ref, DN_CONV))) * (DN_DK ** -0.5)
    k = _l2norm(_silu(_centred_conv(k_ref[...].astype(F32), cwk_ref, DN_CONV)))
    q_s[...] = q
    k_s[...] = k
    v_s[...] = _silu(_centred_conv(v_ref[...].astype(F32), cwv_ref, DN_CONV))

    ps = ps_ref[...]
    sel_r = lax.broadcasted_iota(jnp.int32, (LANES, LANES), 0)
    lane1 = lax.broadcasted_iota(jnp.int32, (1, LANES), 1)
    for d in range(2):
        col = d * DN_HEADS + head
        beta_logit = _dot_exact_rhs(ps, (sel_r == col).astype(F32))
        alpha = _dot_exact_rhs(ps, (sel_r == 2 * DN_HEADS + col).astype(F32))
        pick = lambda ref: jnp.sum(jnp.where(lane1 == col, ref[...], 0.0), axis=1, keepdims=True)
        bt_s[d] = jax.nn.sigmoid(beta_logit)
        g_s[d] = -(jnp.exp(pick(alog_ref)) * _softplus(alpha + pick(dtb_ref)))

    def unit_group(grp, carry):
        for j in range(DN_GROUP):
            n = grp * DN_GROUP + j
            rows = pl.ds(pl.multiple_of(n * DN_UNIT, DN_UNIT), DN_UNIT)
            qc, kc, vc = q_s[rows, :], k_s[rows, :], v_s[rows, :]
            for d in range(2):
                u, w, qe, qk, kd, egl = _dn_unit(qc, kc, vc, bt_s[d, rows, :], g_s[d, rows, :],
                                                 backward=(d == 1))
                u_s[d, rows, :] = u
                qk_s[d, rows, :] = qk.astype(BF16)
                kd_s[d, rows, :] = kd.astype(BF16)
                wb, qeb = w.astype(BF16), qe.astype(BF16)
                for half in range(2):
                    hr = slice(half * c, (half + 1) * c)
                    wq_s[d, 2 * n + half] = jnp.concatenate([wb[hr], qeb[hr]], axis=0)
                    egl_s[d, 2 * n + half] = egl[half]
        return carry

    lax.fori_loop(0, n_units // DN_GROUP, unit_group, 0)
    oacc_s[...] = jnp.zeros_like(oacc_s)

    if has_state:
        s_f0, s_b0 = s0_ref[0, 0, 0], s0_ref[0, 1, 0]
    else:
        s_f0 = s_b0 = jnp.zeros((DN_DK, DN_DV), F32)
    zero_half = jnp.zeros((c, DN_DV), BF16)

    def scan_step(i, carry):
        s_f, s_b = carry
        out = []
        for d, s, n in ((0, s_f, i), (1, s_b, n_units - 1 - i)):
            rows = pl.ds(pl.multiple_of(n * DN_UNIT, DN_UNIT), DN_UNIT)
            u_u, qk_u, kd_u = u_s[d, rows, :], qk_s[d, rows, :], kd_s[d, rows, :]
            o_half = [None, None]
            for half in ((0, 1) if d == 0 else (1, 0)):
                hr = slice(half * c, (half + 1) * c)
                wq = jnp.dot(wq_s[d, 2 * n + half], s.astype(BF16), preferred_element_type=F32)
                v_new = (u_u[hr] - wq[:c]).astype(BF16)
                v_pad = jnp.concatenate([v_new, zero_half] if half == 0 else [zero_half, v_new], axis=0)
                o_half[half] = wq[c:] + jnp.dot(qk_u[hr], v_pad, preferred_element_type=F32)
                s = egl_s[d, 2 * n + half] * s + _bdot_tn(kd_u, v_pad)
            oacc_s[rows, :] += jnp.concatenate(o_half, axis=0)
            out.append(s)
        return tuple(out)

    s_f, s_b = lax.fori_loop(0, n_units, scan_step, (s_f0, s_b0))
    if sfin_ref is not None:
        sfin_ref[0, 0, 0] = s_f
        sfin_ref[0, 1, 0] = s_b

    o = _rms(oacc_s[...], ng_ref[...]) * _silu(z_ref[...].astype(F32))
    o_ref[...] = o.astype(BF16)


def _deltanet(p_main, p_small, conv_w, a_log, dt_bias, norm_g, s0, *, n_seq, seq_len, row_blk0):
    has_state = s0 is not None
    n_chunks = seq_len // DN_CHUNK
    tok = lambda cb: pl.BlockSpec((seq_len, LANES), lambda b, h: (row_blk0 + b, cb + h))
    cw = lambda cb: pl.BlockSpec((DN_CONV, LANES), lambda b, h: (0, cb + h))
    in_specs = [tok(CB_QA), tok(CB_KA), tok(CB_VA), tok(CB_ZA),
                pl.BlockSpec((seq_len, LANES), lambda b, h: (row_blk0 + b, 0)),
                cw(0), cw(DN_HEADS), cw(2 * DN_HEADS),
                pl.BlockSpec((1, LANES), lambda b, h: (0, 0)),
                pl.BlockSpec((1, LANES), lambda b, h: (0, 0)),
                pl.BlockSpec((1, DN_DV), lambda b, h: (0, 0))]
    pad8 = lambda t: jnp.pad(t.reshape(1, 2 * DN_HEADS), ((0, 0), (0, LANES - 2 * DN_HEADS)))
    args = [p_main, p_main, p_main, p_main, p_small, conv_w, conv_w, conv_w,
            pad8(a_log), pad8(dt_bias), norm_g.reshape(1, DN_DV)]
    o_spec = pl.BlockSpec((seq_len, LANES), lambda b, h: (b, h))
    o_shape = jax.ShapeDtypeStruct((n_seq * seq_len, DN_W), BF16)
    if has_state:
        in_specs.append(pl.BlockSpec((1, 2, 1, DN_DK, DN_DV), lambda b, h: (b, 0, h, 0, 0)))
        args.append(s0)
        out_specs, out_shape = o_spec, o_shape
    else:
        out_specs = [o_spec, pl.BlockSpec((1, 2, 1, DN_DK, DN_DV), lambda b, h: (b, 0, h, 0, 0))]
        out_shape = [o_shape, jax.ShapeDtypeStruct((n_seq, 2, DN_HEADS, DN_DK, DN_DV), F32)]
    scratch = [pltpu.VMEM((seq_len, LANES), F32),
               pltpu.VMEM((seq_len, LANES), F32),
               pltpu.VMEM((seq_len, LANES), F32),
               pltpu.VMEM((2, seq_len, LANES), F32),
               pltpu.VMEM((2, seq_len, LANES), F32),
               pltpu.VMEM((seq_len, LANES), F32),
               pltpu.VMEM((2, seq_len, LANES), F32),
               pltpu.VMEM((2, n_chunks, DN_UNIT, DN_DK), BF16),
               pltpu.VMEM((2, seq_len, LANES), BF16),
               pltpu.VMEM((2, seq_len, LANES), BF16),
               pltpu.VMEM((2, n_chunks, 1, LANES), F32)]
    res = pl.pallas_call(
        functools.partial(_dn_kernel, seq_len=seq_len, has_state=has_state),
        grid=(n_seq, DN_HEADS),
        in_specs=in_specs, out_specs=out_specs, out_shape=out_shape,
        scratch_shapes=scratch,
        compiler_params=_cparams(("arbitrary", "arbitrary")),
        name=f"deltanet_{seq_len}",
    )(*args)
    return (res, None) if has_state else (res[0], res[1])


def _rope(x, cos, sin_signed):
    lane = lax.broadcasted_iota(jnp.int32, x.shape, 1)
    partner = jnp.where((lane % 32) < 16, pltpu.roll(x, LANES - 16, axis=1), pltpu.roll(x, 16, axis=1))
    return x * cos + partner * sin_signed


def _softmax_rows(s_parts):
    m = functools.reduce(jnp.maximum, [jnp.max(s, axis=-1, keepdims=True) for s in s_parts])
    e_parts = [jnp.exp(s - m) for s in s_parts]
    den = functools.reduce(lambda a, b: a + b, [jnp.sum(e, axis=-1, keepdims=True) for e in e_parts])
    inv = 1.0 / den
    return [e * inv for e in e_parts]


def _da_kernel(*refs, layer, latent):
    if latent:
        (q_ref, k_ref, v_ref, lam_ref, sg_ref, cq_ref, sq_ref, ck_ref, sk_ref, ctxk_ref, ctxv_ref,
         o_ref, krot_s) = refs
    else:
        q_ref, k_ref, v_ref, lam_ref, sg_ref, o_ref, ko_ref, vo_ref = refs
    lam_init = 0.8 - 0.6 * math.exp(-0.3 * layer)
    lp = lam_ref[...]
    dots = jnp.sum(jnp.concatenate([lp[0:1] * lp[1:2], lp[2:3] * lp[3:4]], axis=0), axis=1, keepdims=True)
    e = jnp.exp(dots)
    lam = e[0:1, :] - e[1:2, :] + lam_init

    q = q_ref[...].astype(F32)
    if latent:
        @pl.when(pl.program_id(2) == 0)
        def _():
            krot_s[...] = _rope(k_ref[...].astype(F32), ck_ref[...], sk_ref[...]).astype(BF16)
        q = _rope(q, cq_ref[...], sq_ref[...])
        keys = [ctxk_ref[0, 0].astype(BF16), krot_s[...]]
        vals = [ctxv_ref[0, 0].astype(BF16), v_ref[...]]
    else:
        keys = [k_ref[...]]
        vals = [v_ref[...]]
        ko_ref[0, 0] = k_ref[...].astype(F32)
        vo_ref[0, 0] = v_ref[...].astype(F32)
    q = q * (DA_DH ** -0.5)
    lane = lax.broadcasted_iota(jnp.int32, q.shape, 1)
    q1 = jnp.where(lane < DA_DH, q, 0.0).astype(BF16)
    q2 = jnp.where(lane >= DA_DH, q, 0.0).astype(BF16)
    p1 = _softmax_rows([_bdot_nt(q1, kk) for kk in keys])
    p2 = _softmax_rows([_bdot_nt(q2, kk) for kk in keys])
    o = None
    for a1, a2, vv in zip(p1, p2, vals):
        part = _bdot(a1 - lam * a2, vv)
        o = part if o is None else o + part
    o_ref[...] = (_rms(o, sg_ref[...]) * (1.0 - lam_init)).astype(BF16)


def _diff_attention(p_main, lam_p, subln_g, layer, *, n_seq, seq_len, row_blk0, tq, rope=None, ctx=None):
    latent = ctx is not None
    nq = seq_len // tq
    qpb = seq_len // tq
    in_specs = [pl.BlockSpec((tq, LANES), lambda b, h, i: ((row_blk0 + b) * qpb + i, CB_QB + h)),
                pl.BlockSpec((seq_len, LANES), lambda b, h, i: (row_blk0 + b, CB_KB + h)),
                pl.BlockSpec((seq_len, LANES), lambda b, h, i: (row_blk0 + b, CB_VB + h)),
                pl.BlockSpec((4, LANES), lambda b, h, i: (0, 0)),
                pl.BlockSpec((1, DA_DV), lambda b, h, i: (0, 0))]
    args = [p_main, p_main, p_main, jnp.pad(lam_p, ((0, 0), (0, LANES - DA_DH))), subln_g.reshape(1, DA_DV)]
    o_spec = pl.BlockSpec((tq, LANES), lambda b, h, i: (b * qpb + i, h))
    o_shape = jax.ShapeDtypeStruct((n_seq * seq_len, DA_HEADS * DA_DV), BF16)
    scratch = []
    if latent:
        cos, sin_signed = rope
        ctx_k, ctx_v = ctx
        n_ctx = ctx_k.shape[2]
        in_specs += [pl.BlockSpec((tq, LANES), lambda b, h, i: (i, 0)),
                     pl.BlockSpec((tq, LANES), lambda b, h, i: (i, 0)),
                     pl.BlockSpec((seq_len, LANES), lambda b, h, i: (0, 0)),
                     pl.BlockSpec((seq_len, LANES), lambda b, h, i: (0, 0)),
                     pl.BlockSpec((1, 1, n_ctx, DA_DV), lambda b, h, i: (b, h, 0, 0)),
                     pl.BlockSpec((1, 1, n_ctx, DA_DV), lambda b, h, i: (b, h, 0, 0))]
        args += [cos, sin_signed, cos, sin_signed, ctx_k, ctx_v]
        out_specs, out_shape = o_spec, o_shape
        scratch = [pltpu.VMEM((seq_len, LANES), BF16)]
    else:
        kv_spec = pl.BlockSpec((1, 1, seq_len, DA_DV), lambda b, h, i: (b, h, 0, 0))
        kv_shape = jax.ShapeDtypeStruct((n_seq, DA_HEADS, seq_len, DA_DV), F32)
        out_specs = [o_spec, kv_spec, kv_spec]
        out_shape = [o_shape, kv_shape, kv_shape]
    res = pl.pallas_call(
        functools.partial(_da_kernel, layer=layer, latent=latent),
        grid=(n_seq, DA_HEADS, nq),
        in_specs=in_specs, out_specs=out_specs, out_shape=out_shape,
        scratch_shapes=scratch,
        compiler_params=_cparams(("arbitrary", "arbitrary", "arbitrary")),
        name=f"diff_attention_{seq_len}",
    )(*args)
    return (res, None, None) if latent else tuple(res)


def _rope_tables(n_tok):
    half = DA_DH // 2
    inv = ROPE_BASE ** (-jnp.arange(0, half, 2, dtype=F32) / half)
    t = jnp.arange(n_tok)
    ang_r = (t // GRID_W).astype(F32)[:, None] * inv
    ang_c = (t % GRID_W).astype(F32)[:, None] * inv
    cos32 = lambda a: jnp.concatenate([jnp.cos(a), jnp.cos(a)], axis=-1)
    sin32 = lambda a: jnp.concatenate([-jnp.sin(a), jnp.sin(a)], axis=-1)
    cos = jnp.concatenate([cos32(ang_r), cos32(ang_c)] * 2, axis=-1)
    sin_signed = jnp.concatenate([sin32(ang_r), sin32(ang_c)] * 2, axis=-1)
    return cos, sin_signed


def _dft_tables(n):
    k = jnp.arange(n, dtype=jnp.int32)
    phase = lambda a, b: (a[:, None] * b[None, :]) % (2 * n)
    ang_f = phase(k, k).astype(F32) * (math.pi / n)
    alt = jnp.where(k % 2 == 0, 1.0, -1.0).astype(F32)
    f_re = jnp.cos(ang_f)
    f_im = jnp.where(k[:, None] == 0, alt[None, :], -jnp.sin(ang_f))
    tt = k + n // 2
    ang_i = phase(tt, k).astype(F32) * (math.pi / n)
    wk = jnp.where(k == 0, 1.0, 2.0).astype(F32) / (2 * n)
    alt_t = jnp.where(tt % 2 == 0, 1.0, -1.0).astype(F32) / (2 * n)
    i_re = jnp.cos(ang_i) * wk[None, :]
    i_im = jnp.where(k[None, :] == 0, alt_t[:, None], -jnp.sin(ang_i) * wk[None, :])
    return f_re, f_im, i_re, i_im


def _hy_embedding(n):
    j = jnp.arange(n, dtype=F32)
    t = j / (n - 1)
    ang = (2.0 * math.pi * j / n)[:, None] * jnp.linspace(1e-4, HY_BANDS - 1, HY_BANDS, dtype=F32)
    z = jnp.concatenate([t[:, None], jnp.cos(ang), -jnp.sin(ang)], axis=-1)
    half = n // 2
    dist = jnp.abs(j - half) / half
    max_decay = math.log(HY_TARGET) / HY_FAST_DECAY
    min_decay = math.log(HY_TARGET) / HY_SLOW_DECAY
    deltas = jnp.abs(jnp.linspace(min_decay, max_decay, HY_W, dtype=F32))
    return jnp.pad(z, ((0, 0), (0, LANES - HY_EMB))), dist[:, None], deltas[None, :]


def _hy_filter_kernel(z_ref, dist_ref, delta_ref, w1_ref, b1_ref, fr_ref, w2_ref, b2_ref, w3_ref,
                      fre_ref, fim_ref, hre_ref, him_ref, h_s):
    @pl.when(pl.program_id(1) == 0)
    def _():
        fr = fr_ref[...]
        hdn = jnp.sin(fr * (_dot3(z_ref[...], w1_ref[...]) + b1_ref[...]))
        hdn = jnp.sin(fr * (_dot3(hdn, w2_ref[...]) + b2_ref[...]))
        h = _dot3(hdn, w3_ref[...])
        h = h * jnp.exp(-dist_ref[...] * delta_ref[...])
        h_s[...] = h / jnp.sum(jnp.abs(h), axis=0, keepdims=True)

    h = h_s[...]
    hre_ref[...] = _dot3(fre_ref[...], h)
    him_ref[...] = _dot3(fim_ref[...], h)


def _hyena_filter_spectrum(n, emb, tables, w1, b1, freq, w2, b2, w3):
    z, dist, deltas = emb
    f_re, f_im = tables
    pad_o = LANES - HY_ORDER
    w1p = jnp.pad(w1, ((0, LANES - HY_EMB), (0, pad_o)))
    w2p = jnp.pad(w2, ((0, pad_o), (0, pad_o)))
    w3p = jnp.pad(w3, ((0, pad_o), (0, 0)))
    row = lambda t: jnp.pad(t.reshape(1, HY_ORDER), ((0, 0), (0, pad_o)))
    tc = 256
    kt = min(n, 512)
    full = lambda shape: pl.BlockSpec(shape, lambda j, k: (0, 0))
    return pl.pallas_call(
        _hy_filter_kernel,
        grid=(HY_W // tc, n // kt),
        in_specs=[full((n, LANES)), full((n, 1)), pl.BlockSpec((1, tc), lambda j, k: (0, j)),
                  full((LANES, LANES)), full((1, LANES)), full((1, LANES)),
                  full((LANES, LANES)), full((1, LANES)), pl.BlockSpec((LANES, tc), lambda j, k: (0, j)),
                  pl.BlockSpec((kt, n), lambda j, k: (k, 0)), pl.BlockSpec((kt, n), lambda j, k: (k, 0))],
        out_specs=[pl.BlockSpec((kt, tc), lambda j, k: (k, j)), pl.BlockSpec((kt, tc), lambda j, k: (k, j))],
        out_shape=[jax.ShapeDtypeStruct((n, HY_W), F32), jax.ShapeDtypeStruct((n, HY_W), F32)],
        scratch_shapes=[pltpu.VMEM((n, tc), F32)],
        compiler_params=_cparams(("arbitrary", "arbitrary")),
        name=f"hyena_filter_{n}",
    )(z, dist, deltas, w1p, row(b1), row(freq), w2p, row(b2), w3p, f_re, f_im)


def _hy_pre_kernel(x0_ref, x1_ref, v_ref, w0_ref, w1_ref, w2_ref, b0_ref, b1_ref, b2_ref, x0c_ref, gv_ref):
    conv = lambda r, w, b: _centred_conv(r[...].astype(F32), w, HY_SHORT, b[...])
    x0c_ref[...] = conv(x0_ref, w0_ref, b0_ref).astype(BF16)
    gv_ref[...] = (conv(v_ref, w2_ref, b2_ref) * conv(x1_ref, w1_ref, b1_ref)).astype(BF16)


def _hyena_pre(p_main, conv_w, conv_b):
    tc = 256
    ncb = HY_W // tc
    outs = []
    for n_seq, seq_len, row_blk0 in ((BATCH, SEQ, 0), (DEC_BATCH, DEC_SEQ, N_PROMPT // DEC_SEQ)):
        tok = lambda cb: pl.BlockSpec((seq_len, tc), lambda b, j, cb=cb: (row_blk0 + b, cb // 2 + j))
        cw = lambda s: pl.BlockSpec((HY_SHORT, tc), lambda b, j, s=s: (0, s * ncb + j))
        cb_ = lambda s: pl.BlockSpec((1, tc), lambda b, j, s=s: (0, s * ncb + j))
        o_spec = pl.BlockSpec((seq_len, tc), lambda b, j: (b, j))
        o_shape = jax.ShapeDtypeStruct((n_seq * seq_len, HY_W), BF16)
        outs.append(pl.pallas_call(
            _hy_pre_kernel,
            grid=(n_seq, ncb),
            in_specs=[tok(CB_X0), tok(CB_X1), tok(CB_HV), cw(0), cw(1), cw(2), cb_(0), cb_(1), cb_(2)],
            out_specs=[o_spec, o_spec], out_shape=[o_shape, o_shape],
            compiler_params=_cparams(("arbitrary", "arbitrary")),
            name=f"hyena_pre_{seq_len}",
        )(p_main, p_main, p_main, conv_w, conv_w, conv_w,
          conv_b.reshape(1, -1), conv_b.reshape(1, -1), conv_b.reshape(1, -1)))
    return outs


def _hy_conv_kernel(gv_ref, x0_ref, d_ref, hre_ref, him_ref, fre_ref, fim_ref, ire_ref, iim_ref,
                    o_ref, acc_s, *, seq_len, n_seq, kt):
    ki = pl.program_id(2)
    nk = pl.num_programs(2)
    hre, him = hre_ref[...], him_ref[...]
    fre, fim, ire, iim = fre_ref[...], fim_ref[...], ire_ref[...], iim_ref[...]
    freq = lax.broadcasted_iota(jnp.int32, hre.shape, 0) + ki * kt
    dc_row = freq == 0
    for b in range(n_seq):
        rows = pl.ds(b * seq_len, seq_len)
        gv = gv_ref[rows, :]
        gre = jnp.dot(fre, gv, preferred_element_type=F32)
        gim = jnp.dot(fim, gv, preferred_element_type=F32)
        yre = gre * hre - jnp.where(dc_row, 0.0, gim * him)
        yim = jnp.where(dc_row, gim * him, gre * him + gim * hre)
        part = _bdot(ire, yre) + _bdot(iim, yim)

        @pl.when(ki == 0)
        def _():
            acc_s[rows, :] = part

        @pl.when(ki > 0)
        def _():
            acc_s[rows, :] += part

    @pl.when(ki == nk - 1)
    def _():
        y = acc_s[...] + gv_ref[...].astype(F32) * d_ref[...]
        o_ref[...] = (y * x0_ref[...].astype(F32)).astype(BF16)


def _hyena_conv(gv, x0c, d_skip, h_spec, tables, *, n_seq_total, seq_len, n_seq_blk, kt):
    tc = 256
    hre, him = h_spec
    f_re, f_im, i_re, i_im = tables
    rows = n_seq_blk * seq_len
    grid = (n_seq_total // n_seq_blk, HY_W // tc, seq_len // kt)
    tok = pl.BlockSpec((rows, tc), lambda g, j, k: (g, j))
    hs = pl.BlockSpec((kt, tc), lambda g, j, k: (k, j))
    return pl.pallas_call(
        functools.partial(_hy_conv_kernel, seq_len=seq_len, n_seq=n_seq_blk, kt=kt),
        grid=grid,
        in_specs=[tok, tok, pl.BlockSpec((1, tc), lambda g, j, k: (0, j)), hs, hs,
                  pl.BlockSpec((kt, seq_len), lambda g, j, k: (k, 0)),
                  pl.BlockSpec((kt, seq_len), lambda g, j, k: (k, 0)),
                  pl.BlockSpec((seq_len, kt), lambda g, j, k: (0, k)),
                  pl.BlockSpec((seq_len, kt), lambda g, j, k: (0, k))],
        out_specs=tok,
        out_shape=jax.ShapeDtypeStruct((n_seq_total * seq_len, HY_W), BF16),
        scratch_shapes=[pltpu.VMEM((rows, tc), F32)],
        compiler_params=_cparams(("arbitrary", "arbitrary", "arbitrary")),
        name=f"hyena_conv_{seq_len}",
    )(gv, x0c, d_skip.reshape(1, HY_W), hre, him, f_re, f_im, i_re, i_im)


MERGE_TM = 512


def _merge_kernel(x_ref, oa_ref, ob_ref, oc_ref, ga_ref, gb_ref, gc_ref, wa_ref, wb_ref, wc_ref,
                  wo_ref, mod_ref, y_ref):
    sig = lambda r: jax.nn.sigmoid(r[...].astype(F32))
    merged = (sig(ga_ref) * jnp.dot(oa_ref[...], wa_ref[...], preferred_element_type=F32)
              + sig(gb_ref) * jnp.dot(ob_ref[...], wb_ref[...], preferred_element_type=F32)
              + sig(gc_ref) * jnp.dot(oc_ref[...], wc_ref[...], preferred_element_type=F32))
    out = _bdot(merged, wo_ref[...])
    y_ref[...] = x_ref[...] + mod_ref[0, 2:3, :] * out


def _merge(x, o_a, o_b, o_c, p_main, mods, w_a, w_b, w_c, w_out):
    tm = MERGE_TM
    tok = lambda w: pl.BlockSpec((tm, w), lambda i: (i, 0))
    gate = lambda s: pl.BlockSpec((tm, D_MODEL), lambda i, s=s: (i, CB_GATE // 8 + s))
    wfull = lambda r: pl.BlockSpec((r, D_MODEL), lambda i: (0, 0))
    return pl.pallas_call(
        _merge_kernel,
        grid=(N_TOK // tm,),
        in_specs=[tok(D_MODEL), tok(DN_W), tok(DN_W), tok(HY_W), gate(0), gate(1), gate(2),
                  wfull(DN_W), wfull(DN_W), wfull(HY_W), wfull(D_MODEL),
                  pl.BlockSpec((1, 6, D_MODEL), lambda i: (_mod_row(i, tm), 0, 0))],
        out_specs=tok(D_MODEL),
        out_shape=jax.ShapeDtypeStruct((N_TOK, D_MODEL), F32),
        compiler_params=_cparams(("arbitrary",)),
        name="merge_out_projection",
    )(x, o_a, o_b, o_c, p_main, p_main, p_main, w_a, w_b, w_c, w_out, mods)


FFN_TM = 1024
FFN_TF = 256


def _ffn_kernel(x_ref, g_ref, mod_ref, wg_ref, wu_ref, wo_ref, fg_ref, y_ref, h_scr, acc_scr, *, final):
    f = pl.program_id(1)
    nf = pl.num_programs(1)

    @pl.when(f == 0)
    def _():
        g = g_ref[...]
        sh = mod_ref[0, 3:4, :]
        sc1 = 1.0 + mod_ref[0, 4:5, :]

        def body(r, carry):
            rows = pl.ds(pl.multiple_of(r * ROW_CHUNK, ROW_CHUNK), ROW_CHUNK)
            h_scr[rows, :] = (_rms(x_ref[rows, :], g) * sc1 + sh).astype(BF16)
            return carry

        lax.fori_loop(0, FFN_TM // ROW_CHUNK, body, 0)

    h = h_scr[...]
    gate = jnp.dot(h, wg_ref[...], preferred_element_type=F32)
    up = jnp.dot(h, wu_ref[...], preferred_element_type=F32)
    part = _bdot(_silu(gate) * up, wo_ref[...])

    @pl.when(f == 0)
    def _():
        acc_scr[...] = part

    @pl.when(f > 0)
    def _():
        acc_scr[...] += part

    @pl.when(f == nf - 1)
    def _():
        y = x_ref[...] + mod_ref[0, 5:6, :] * acc_scr[...]
        if final:
            y = _rms(y, fg_ref[...])
        y_ref[...] = y


def _ffn(x, norm_g, mods, w_in, w_out, final_g, *, final):
    tm, tf = FFN_TM, FFN_TF
    nf = D_FF // tf
    return pl.pallas_call(
        functools.partial(_ffn_kernel, final=final),
        grid=(N_TOK // tm, nf),
        in_specs=[pl.BlockSpec((tm, D_MODEL), lambda i, f: (i, 0)),
                  pl.BlockSpec((1, D_MODEL), lambda i, f: (0, 0)),
                  pl.BlockSpec((1, 6, D_MODEL), lambda i, f: (_mod_row(i, tm), 0, 0)),
                  pl.BlockSpec((D_MODEL, tf), lambda i, f: (0, f)),
                  pl.BlockSpec((D_MODEL, tf), lambda i, f: (0, nf + f)),
                  pl.BlockSpec((tf, D_MODEL), lambda i, f: (f, 0)),
                  pl.BlockSpec((1, D_MODEL), lambda i, f: (0, 0))],
        out_specs=pl.BlockSpec((tm, D_MODEL), lambda i, f: (i, 0)),
        out_shape=jax.ShapeDtypeStruct((N_TOK, D_MODEL), F32),
        scratch_shapes=[pltpu.VMEM((tm, D_MODEL), BF16), pltpu.VMEM((tm, D_MODEL), F32)],
        compiler_params=_cparams(("arbitrary", "arbitrary")),
        name="ffn",
    )(x, norm_g.reshape(1, D_MODEL), mods, w_in, w_in, w_out, final_g.reshape(1, D_MODEL))


def kernel(x_prompt, x_sample, cache_k, cache_v, state_dn, c, c_ctx, norm1_g, norm2_g, w_mod, b_mod,
           w_in, dn_conv_w, dn_a_log, dn_dt_bias, dn_norm_g, da_lambda, da_subln_g, hy_conv_w,
           hy_conv_b, hy_w1, hy_b1, hy_freq, hy_w2, hy_b2, hy_w3, hy_d, w_br_a, w_br_b, w_br_c,
           w_out, w_ffn_in, w_ffn_out, final_g):
    x = jnp.concatenate([x_prompt.reshape(N_PROMPT, D_MODEL), x_sample.reshape(N_SAMPLE, D_MODEL)], axis=0)
    cond8 = jnp.concatenate([c_ctx[None, :], c, jnp.zeros((8 - 1 - DEC_BATCH, D_MODEL), F32)], axis=0)
    rope = _rope_tables(DEC_SEQ)
    tab_pf = _dft_tables(SEQ)
    tab_sf = _dft_tables(DEC_SEQ)
    tab_p = tuple(t.astype(BF16) for t in tab_pf)
    tab_s = tuple(t.astype(BF16) for t in tab_sf)
    emb_p, emb_s = _hy_embedding(SEQ), _hy_embedding(DEC_SEQ)
    sample_blk0 = N_PROMPT // DEC_SEQ

    new_k, new_v, new_s = [], [], []
    for l in range(DEPTH):
        mods = _modulation(cond8, w_mod[l], b_mod[l]).reshape(8, 6, D_MODEL)
        w_l = w_in[l]
        w_main = jnp.concatenate([w_l[:, :3 * DN_W], w_l[:, 3 * DN_W + 4 * DN_HEADS:]], axis=1).astype(BF16)
        w_small = jnp.pad(w_l[:, 3 * DN_W:3 * DN_W + 4 * DN_HEADS],
                          ((0, 0), (0, LANES - 4 * DN_HEADS))).astype(BF16)
        p_main, p_small = _in_projection(x, norm1_g[l], mods, w_main, w_small)

        dn_args = (p_main, p_small, dn_conv_w[l], dn_a_log[l], dn_dt_bias[l], dn_norm_g[l])
        oa_p, s_fin = _deltanet(*dn_args, None, n_seq=BATCH, seq_len=SEQ, row_blk0=0)
        oa_s, _ = _deltanet(*dn_args, state_dn[:, l], n_seq=DEC_BATCH, seq_len=DEC_SEQ, row_blk0=sample_blk0)

        ob_p, k_l, v_l = _diff_attention(p_main, da_lambda[l], da_subln_g[l], l, n_seq=BATCH, seq_len=SEQ,
                                         row_blk0=0, tq=SEQ)
        ob_s, _, _ = _diff_attention(p_main, da_lambda[l], da_subln_g[l], l, n_seq=DEC_BATCH,
                                     seq_len=DEC_SEQ, row_blk0=sample_blk0, tq=256, rope=rope,
                                     ctx=(cache_k[:, l], cache_v[:, l]))

        hy_w = (hy_w1[l], hy_b1[l], hy_freq[l], hy_w2[l], hy_b2[l], hy_w3[l])
        hspec_p = _hyena_filter_spectrum(SEQ, emb_p, tab_pf[:2], *hy_w)
        hspec_s = _hyena_filter_spectrum(DEC_SEQ, emb_s, tab_sf[:2], *hy_w)
        (x0_p, gv_p), (x0_s, gv_s) = _hyena_pre(p_main, hy_conv_w[l], hy_conv_b[l])
        oc_p = _hyena_conv(gv_p, x0_p, hy_d[l], hspec_p, tab_p, n_seq_total=BATCH, seq_len=SEQ,
                           n_seq_blk=BATCH, kt=SEQ)
        oc_s = _hyena_conv(gv_s, x0_s, hy_d[l], hspec_s, tab_s, n_seq_total=DEC_BATCH, seq_len=DEC_SEQ,
                           n_seq_blk=2, kt=512)

        cat = lambda a, b: jnp.concatenate([a, b], axis=0)
        x = _merge(x, cat(oa_p, oa_s), cat(ob_p, ob_s), cat(oc_p, oc_s), p_main, mods,
                   w_br_a[l].astype(BF16), w_br_b[l].astype(BF16), w_br_c[l].astype(BF16),
                   w_out[l].astype(BF16))
        x = _ffn(x, norm2_g[l], mods, w_ffn_in[l].astype(BF16), w_ffn_out[l].astype(BF16), final_g,
                 final=(l == DEPTH - 1))
        new_k.append(k_l)
        new_v.append(v_l)
        new_s.append(s_fin)

    y_prompt = x[:N_PROMPT].reshape(BATCH, SEQ, D_MODEL)
    y_sample = x[N_PROMPT:].reshape(DEC_BATCH, DEC_SEQ, D_MODEL)
    return (y_prompt, y_sample, jnp.stack(new_k, axis=1), jnp.stack(new_v, axis=1), jnp.stack(new_s, axis=1))
```

```python
import functools
import math

import jax
import jax.numpy as jnp
from jax import lax
from jax.experimental import pallas as pl
from jax.experimental.pallas import tpu as pltpu

F32 = jnp.float32
BF16 = jnp.bfloat16

D_MODEL = 1024
BATCH = 16
SEQ = 256
DEPTH = 2
DEC_BATCH = 4
DEC_SEQ = 2048
PAST_LEN = 256
GRID_W = 64
RMS_EPS = 1e-6
DN_HEADS = 4
DN_DK = 128
DN_DV = 128
DN_W = DN_HEADS * DN_DK
DN_CONV = 5
DN_CHUNK = 64
DA_HEADS = 4
DA_DH = 64
DA_DV = 2 * DA_DH
ROPE_BASE = 10000.0
HY_W = 512
HY_SHORT = 3
HY_BANDS = 8
HY_EMB = 1 + 2 * HY_BANDS
HY_ORDER = 64
HY_FAST_DECAY = 0.3
HY_SLOW_DECAY = 1.5
HY_TARGET = 1e-2
D_FF = ((8 * D_MODEL // 3 + 255) // 256) * 256

N_PROMPT = BATCH * SEQ
N_SAMPLE = DEC_BATCH * DEC_SEQ
N_TOK = N_PROMPT + N_SAMPLE
LANES = 128
N_MAIN = 8192
CB_QA, CB_KA, CB_VA, CB_ZA = 0, 4, 8, 12
CB_QB, CB_KB, CB_VB = 16, 20, 24
CB_X0, CB_X1, CB_HV = 28, 32, 36
CB_GATE = 40
VMEM_LIMIT = 56 * 1024 * 1024


def _cparams(sem):
    return pltpu.CompilerParams(dimension_semantics=sem, vmem_limit_bytes=VMEM_LIMIT)


def _bdot(a, b):
    return jnp.dot(a.astype(BF16), b.astype(BF16), preferred_element_type=F32)


def _bdot_nt(a, b):
    return lax.dot_general(a.astype(BF16), b.astype(BF16), (((1,), (1,)), ((), ())),
                           preferred_element_type=F32)


def _bdot_tn(a, b):
    return lax.dot_general(a.astype(BF16), b.astype(BF16), (((0,), (0,)), ((), ())),
                           preferred_element_type=F32)


def _split3(x):
    hi = x.astype(BF16)
    r = x - hi.astype(F32)
    mid = r.astype(BF16)
    lo = (r - mid.astype(F32)).astype(BF16)
    return hi, mid, lo


def _dot_exact_lhs(t, x):
    hi, mid, lo = _split3(x)
    tb = t.astype(BF16)
    d = lambda p: jnp.dot(tb, p, preferred_element_type=F32)
    return d(hi) + d(mid) + d(lo)


def _dot_exact_rhs(x, e):
    hi, mid, lo = _split3(x)
    eb = e.astype(BF16)
    d = lambda p: jnp.dot(p, eb, preferred_element_type=F32)
    return d(hi) + d(mid) + d(lo)


def _dot3(a, b):
    ah = a.astype(BF16)
    al = (a - ah.astype(F32)).astype(BF16)
    bh = b.astype(BF16)
    bl = (b - bh.astype(F32)).astype(BF16)
    d = lambda p, q: jnp.dot(p, q, preferred_element_type=F32)
    return d(ah, bh) + d(ah, bl) + d(al, bh)


def _silu(x):
    return x * jax.nn.sigmoid(x)


def _rms(x, g):
    return x * lax.rsqrt(jnp.mean(x * x, axis=-1, keepdims=True) + RMS_EPS) * g


def _mod_row(i, tm):
    n_prompt_tiles = N_PROMPT // tm
    tiles_per_seq = DEC_SEQ // tm
    return jnp.where(i < n_prompt_tiles, 0, 1 + (i - n_prompt_tiles) // tiles_per_seq)


def _mod_kernel(c_ref, w_ref, b_ref, o_ref):
    o_ref[...] = _bdot(_silu(c_ref[...]), w_ref[...]) + b_ref[...]


def _modulation(cond8, w_mod, b_mod):
    n = 6 * D_MODEL
    tn = 1024
    return pl.pallas_call(
        _mod_kernel,
        grid=(n // tn,),
        in_specs=[pl.BlockSpec((8, D_MODEL), lambda j: (0, 0)),
                  pl.BlockSpec((D_MODEL, tn), lambda j: (0, j)),
                  pl.BlockSpec((1, tn), lambda j: (0, j))],
        out_specs=pl.BlockSpec((8, tn), lambda j: (0, j)),
        out_shape=jax.ShapeDtypeStruct((8, n), F32),
        compiler_params=_cparams(("arbitrary",)),
        name="modulation",
    )(cond8, w_mod, b_mod.reshape(1, n))


IN_TM = 1024
IN_TN = 1024
ROW_CHUNK = 256


def _inproj_kernel(x_ref, g_ref, mod_ref, wm_ref, ws_ref, p_ref, ps_ref, h_scr):
    @pl.when(pl.program_id(1) == 0)
    def _():
        g = g_ref[...]
        sh = mod_ref[0, 0:1, :]
        sc1 = 1.0 + mod_ref[0, 1:2, :]

        def body(r, carry):
            rows = pl.ds(pl.multiple_of(r * ROW_CHUNK, ROW_CHUNK), ROW_CHUNK)
            hb = (_rms(x_ref[rows, :], g) * sc1 + sh).astype(BF16)
            h_scr[rows, :] = hb
            ps_ref[rows, :] = jnp.dot(hb, ws_ref[...], preferred_element_type=F32)
            return carry

        lax.fori_loop(0, IN_TM // ROW_CHUNK, body, 0)

    p_ref[...] = jnp.dot(h_scr[...], wm_ref[...], preferred_element_type=F32).astype(BF16)


def _in_projection(x, norm_g, mods, w_main, w_small):
    grid = (N_TOK // IN_TM, N_MAIN // IN_TN)
    return pl.pallas_call(
        _inproj_kernel,
        grid=grid,
        in_specs=[pl.BlockSpec((IN_TM, D_MODEL), lambda i, j: (i, 0)),
                  pl.BlockSpec((1, D_MODEL), lambda i, j: (0, 0)),
                  pl.BlockSpec((1, 6, D_MODEL), lambda i, j: (_mod_row(i, IN_TM), 0, 0)),
                  pl.BlockSpec((D_MODEL, IN_TN), lambda i, j: (0, j)),
                  pl.BlockSpec((D_MODEL, LANES), lambda i, j: (0, 0))],
        out_specs=[pl.BlockSpec((IN_TM, IN_TN), lambda i, j: (i, j)),
                   pl.BlockSpec((IN_TM, LANES), lambda i, j: (i, 0))],
        out_shape=[jax.ShapeDtypeStruct((N_TOK, N_MAIN), BF16),
                   jax.ShapeDtypeStruct((N_TOK, LANES), F32)],
        scratch_shapes=[pltpu.VMEM((IN_TM, D_MODEL), BF16)],
        compiler_params=_cparams(("arbitrary", "arbitrary")),
        name="in_projection",
    )(x, norm_g.reshape(1, D_MODEL), mods, w_main, w_small)


DN_UNIT = 2 * DN_CHUNK
DN_GROUP = 4


CONV_PAD = 8


def _zero_conv_borders(pad_ref):
    n = pad_ref.shape[0] - 2 * CONV_PAD
    zeros = jnp.zeros((CONV_PAD, pad_ref.shape[1]), F32)
    pad_ref[0:CONV_PAD, :] = zeros
    pad_ref[CONV_PAD + n:, :] = zeros


CONV_ROWS = 512


def _centred_conv(pad_ref, load, store, n, w_ref, n_taps, bias=None):
    rb = min(n, CONV_ROWS)
    half = n_taps // 2
    for r0 in range(0, n, rb):
        pad_ref[CONV_PAD + r0:CONV_PAD + r0 + rb, :] = load(r0, rb)
    for r0 in range(0, n, rb):
        acc = None
        for tap in range(n_taps):
            lo = CONV_PAD + r0 + tap - half
            term = pad_ref[lo:lo + rb, :] * w_ref[tap:tap + 1, :]
            acc = term if acc is None else acc + term
        store(r0, acc if bias is None else acc + bias)


def _l2norm(x):
    return x * lax.rsqrt(jnp.sum(x * x, axis=-1, keepdims=True) + RMS_EPS)


def _softplus(x):
    return jnp.maximum(x, 0.0) + jnp.log1p(jnp.exp(-jnp.abs(x)))


DN_BASE = 16


def _unit_tri_inverse(a_list, ri, ci):
    eye = (ri == ci).astype(F32)
    blk = lambda b: (ri // b) == (ci // b)
    y = [jnp.where(blk(DN_BASE), -a, 0.0) for a in a_list]
    p = [eye + yi for yi in y]
    for _ in range(3):
        y = [_bdot(yi, yi) for yi in y]
        p = [pi + _bdot(yi, pi) for yi, pi in zip(y, p)]
    b = DN_BASE
    while b < DN_CHUNK:
        off_mask = blk(2 * b) & ~blk(b)
        t = [_bdot(jnp.where(off_mask, a, 0.0), pi) for a, pi in zip(a_list, p)]
        p = [pi - _bdot(pi, ti) for pi, ti in zip(p, t)]
        b *= 2
    return p


def _dn_units(chains):
    u, c = DN_UNIT, DN_CHUNK
    ri = lax.broadcasted_iota(jnp.int32, (u, u), 0)
    ci = lax.broadcasted_iota(jnp.int32, (u, u), 1)
    same = (ri // c) == (ci // c)
    eye_mask = ri == ci
    hi_rows = ri >= c
    incl_of = {False: same & (ri >= ci), True: same & (ri <= ci)}
    strict_of = {False: same & (ri > ci), True: same & (ri < ci)}
    incl = [incl_of[ch['backward']] for ch in chains]
    strict = [strict_of[ch['backward']] for ch in chains]
    gc = [_dot_exact_lhs(m.astype(F32), ch['g']) for m, ch in zip(incl, chains)]
    g_tot = [(x[0:1, :], x[c:c + 1, :]) if ch['backward'] else (x[c - 1:c, :], x[u - 1:u, :])
             for x, ch in zip(gc, chains)]
    gc_row = [jnp.sum(jnp.where(eye_mask, x, 0.0), axis=0, keepdims=True) for x in gc]
    dec = [jnp.exp(jnp.where(m, x - xr, -1e30)) for m, x, xr in zip(incl, gc, gc_row)]
    e_gc = [jnp.exp(x) for x in gc]
    a = [jnp.where(m, ch['kk'] * ch['beta'] * d, 0.0) for m, ch, d in zip(strict, chains, dec)]
    qk = [ch['qk_raw'] * d for ch, d in zip(chains, dec)]
    eye = eye_mask.astype(F32)
    r = [p - eye for p in _unit_tri_inverse(a, ri, ci)]
    rhs = [jnp.concatenate([ch['v'] * ch['beta'], ch['k'] * (ch['beta'] * e)], axis=1)
           for ch, e in zip(chains, e_gc)]
    uwb = [(x + _bdot(ri_, x)).astype(BF16) for ri_, x in zip(r, rhs)]
    qkuw = [_bdot(x, y) for x, y in zip(qk, uwb)]
    kd = [ch['k'] * jnp.exp(jnp.where(hi_rows, gt[1], gt[0]) - x) for ch, gt, x in zip(chains, g_tot, gc)]
    tp = [[_bdot_tn(jnp.where(keep, x, 0.0), y) for x, y in zip(kd, uwb)]
          for keep in (~hi_rows, hi_rows)]
    out = []
    for i, ch in enumerate(chains):
        o_local = qkuw[i][:, :DN_DV]
        q_eff = ch['q'] * e_gc[i] - qkuw[i][:, DN_DV:]
        per_chunk = [(-tp[h][i][:, DN_DV:], tp[h][i][:, :DN_DV], jnp.exp(g_tot[i][h])) for h in range(2)]
        out.append((o_local, q_eff, per_chunk))
    return out


def _dn_kernel(*refs, seq_len, has_state, hps):
    if has_state:
        (q_ref, k_ref, v_ref, z_ref, ps_ref, cwq_ref, cwk_ref, cwv_ref, alog_ref, dtb_ref, ng_ref,
         s0_ref, o_ref, q_s, k_s, v_s, bt_s, g_s, oacc_s, qe_s, th_s, psi_s, egl_s, st_s, pad_s) = refs
        sfin_ref = None
    else:
        (q_ref, k_ref, v_ref, z_ref, ps_ref, cwq_ref, cwk_ref, cwv_ref, alog_ref, dtb_ref, ng_ref,
         o_ref, sfin_ref, q_s, k_s, v_s, bt_s, g_s, oacc_s, qe_s, th_s, psi_s, egl_s, st_s, pad_s) = refs
        s0_ref = None
    n_units = seq_len // DN_UNIT
    n_chunks = seq_len // DN_CHUNK
    c = DN_CHUNK
    sel_r = lax.broadcasted_iota(jnp.int32, (LANES, 4 * LANES), 0)
    sel_blk = lax.broadcasted_iota(jnp.int32, (LANES, 4 * LANES), 1) // LANES
    sels = []
    for j in range(hps):
        head = pl.program_id(1) * hps + j
        src_col = head + DN_HEADS * jnp.where(sel_blk == 0, 0, jnp.where(sel_blk == 1, 2, jnp.where(sel_blk == 2, 1, 3)))
        sels.append((sel_r == src_col).astype(BF16))

    def gate_rows(r, carry):
        rows = pl.ds(pl.multiple_of(r * ROW_CHUNK, ROW_CHUNK), ROW_CHUNK)
        ps = ps_ref[rows, :]
        lane = lax.broadcasted_iota(jnp.int32, ps.shape, 1)
        gate_cols = jnp.where(lane < 2 * DN_HEADS, jax.nn.sigmoid(ps),
                              -(jnp.exp(alog_ref[...]) * _softplus(ps + dtb_ref[...])))
        parts = _split3(gate_cols)
        for j in range(hps):
            ext = functools.reduce(lambda a, b: a + b,
                                   [jnp.dot(part, sels[j], preferred_element_type=F32) for part in parts])
            for d in range(2):
                bt_s[j, d, rows, :] = ext[:, (2 * d) * LANES:(2 * d + 1) * LANES]
                g_s[j, d, rows, :] = ext[:, (2 * d + 1) * LANES:(2 * d + 2) * LANES]
        return carry

    lax.fori_loop(0, seq_len // ROW_CHUNK, gate_rows, 0)

    _zero_conv_borders(pad_s)
    for j in range(hps):
        cols = slice(j * LANES, (j + 1) * LANES)
        for x_ref, w_ref, out_s, post in (
                (q_ref, cwq_ref, q_s, lambda y: _l2norm(_silu(y)) * (DN_DK ** -0.5)),
                (k_ref, cwk_ref, k_s, lambda y: _l2norm(_silu(y))),
                (v_ref, cwv_ref, v_s, _silu)):
            def store(r0, y, out_s=out_s, post=post):
                out_s[j, r0:r0 + y.shape[0], :] = post(y)
            _centred_conv(pad_s, lambda r0, rb, x_ref=x_ref: x_ref[r0:r0 + rb, cols].astype(F32), store,
                          seq_len, w_ref.at[:, cols], DN_CONV)
        for d in range(2):
            st_s[j, d] = s0_ref[0, d, j] if has_state else jnp.zeros((DN_DK, DN_DV), F32)

    def unit_group(grp, carry):
        where, chains = [], []
        for t in range(DN_GROUP):
            idx = grp * DN_GROUP + t
            j = idx // n_units
            n = idx % n_units
            rows = pl.ds(pl.multiple_of(n * DN_UNIT, DN_UNIT), DN_UNIT)
            qc, kc, vc = q_s[j, rows, :], k_s[j, rows, :], v_s[j, rows, :]
            kcb = kc.astype(BF16)
            kk = _bdot_nt(kcb, kcb)
            qk_raw = _bdot_nt(qc, kcb)
            for d in range(2):
                where.append((j, n, rows, d))
                chains.append(dict(q=qc, k=kc, v=vc, kk=kk, qk_raw=qk_raw, beta=bt_s[j, d, rows, :],
                                   g=g_s[j, d, rows, :], backward=(d == 1)))
        results = _dn_units(chains)
        for (j, n, rows, d), (o_loc, q_eff, per_chunk) in zip(where, results):
            qe_s[j, d, rows, :] = q_eff.astype(BF16)
            for half, (theta, psi, egl) in enumerate(per_chunk):
                th_s[j, d, 2 * n + half] = theta.astype(BF16)
                psi_s[j, d, 2 * n + half] = psi
                egl_s[j, d, 2 * n + half] = egl
            if d == 0:
                o_fwd = o_loc
            else:
                oacc_s[j, rows, :] = o_fwd + o_loc
        return carry

    lax.fori_loop(0, hps * n_units // DN_GROUP, unit_group, 0)

    def scan_step(i, carry):
        for j in range(hps):
            for d, n in ((0, i), (1, n_chunks - 1 - i)):
                rows = pl.ds(pl.multiple_of(n * c, c), c)
                s = st_s[j, d]
                sb = s.astype(BF16)
                oacc_s[j, rows, :] += jnp.dot(qe_s[j, d, rows, :], sb, preferred_element_type=F32)
                st_s[j, d] = (egl_s[j, d, n] * s + jnp.dot(th_s[j, d, n], sb, preferred_element_type=F32)
                              + psi_s[j, d, n])
        return carry

    lax.fori_loop(0, n_chunks, scan_step, 0)

    if sfin_ref is not None:
        for j in range(hps):
            for d in range(2):
                sfin_ref[0, d, j] = st_s[j, d]

    def out_rows(r, carry):
        rows = pl.ds(pl.multiple_of(r * ROW_CHUNK, ROW_CHUNK), ROW_CHUNK)
        for j in range(hps):
            cols = slice(j * LANES, (j + 1) * LANES)
            o = _rms(oacc_s[j, rows, :], ng_ref[...]) * _silu(z_ref[rows, cols].astype(F32))
            o_ref[rows, cols] = o.astype(BF16)
        return carry

    lax.fori_loop(0, seq_len // ROW_CHUNK, out_rows, 0)


def _deltanet(p_main, p_small, conv_w, a_log, dt_bias, norm_g, s0, *, n_seq, seq_len, row_blk0, hps):
    has_state = s0 is not None
    n_chunks = seq_len // DN_CHUNK
    width = hps * LANES
    tok = lambda cb: pl.BlockSpec((seq_len, width), lambda b, h: (row_blk0 + b, cb // hps + h))
    cw = lambda cb: pl.BlockSpec((DN_CONV, width), lambda b, h: (0, cb // hps + h))
    in_specs = [tok(CB_QA), tok(CB_KA), tok(CB_VA), tok(CB_ZA),
                pl.BlockSpec((seq_len, LANES), lambda b, h: (row_blk0 + b, 0)),
                cw(0), cw(DN_HEADS), cw(2 * DN_HEADS),
                pl.BlockSpec((1, LANES), lambda b, h: (0, 0)),
                pl.BlockSpec((1, LANES), lambda b, h: (0, 0)),
                pl.BlockSpec((1, DN_DV), lambda b, h: (0, 0))]
    pad8 = lambda t: jnp.pad(t.reshape(1, 2 * DN_HEADS), ((0, 0), (2 * DN_HEADS, LANES - 4 * DN_HEADS)))
    args = [p_main, p_main, p_main, p_main, p_small, conv_w, conv_w, conv_w,
            pad8(a_log), pad8(dt_bias), norm_g.reshape(1, DN_DV)]
    o_spec = pl.BlockSpec((seq_len, width), lambda b, h: (b, h))
    o_shape = jax.ShapeDtypeStruct((n_seq * seq_len, DN_W), BF16)
    state_spec = pl.BlockSpec((1, 2, hps, DN_DK, DN_DV), lambda b, h: (b, 0, h, 0, 0))
    if has_state:
        in_specs.append(state_spec)
        args.append(s0)
        out_specs, out_shape = o_spec, o_shape
    else:
        out_specs = [o_spec, state_spec]
        out_shape = [o_shape, jax.ShapeDtypeStruct((n_seq, 2, DN_HEADS, DN_DK, DN_DV), F32)]
    scratch = [pltpu.VMEM((hps, seq_len, LANES), F32),
               pltpu.VMEM((hps, seq_len, LANES), F32),
               pltpu.VMEM((hps, seq_len, LANES), F32),
               pltpu.VMEM((hps, 2, seq_len, LANES), F32),
               pltpu.VMEM((hps, 2, seq_len, LANES), F32),
               pltpu.VMEM((hps, seq_len, LANES), F32),
               pltpu.VMEM((hps, 2, seq_len, LANES), BF16),
               pltpu.VMEM((hps, 2, n_chunks, DN_DK, DN_DK), BF16),
               pltpu.VMEM((hps, 2, n_chunks, DN_DK, DN_DV), F32),
               pltpu.VMEM((hps, 2, n_chunks, 1, LANES), F32),
               pltpu.VMEM((hps, 2, DN_DK, DN_DV), F32),
               pltpu.VMEM((seq_len + 2 * CONV_PAD, LANES), F32)]
    res = pl.pallas_call(
        functools.partial(_dn_kernel, seq_len=seq_len, has_state=has_state, hps=hps),
        grid=(n_seq, DN_HEADS // hps),
        in_specs=in_specs, out_specs=out_specs, out_shape=out_shape,
        scratch_shapes=scratch,
        compiler_params=_cparams(("arbitrary", "arbitrary")),
        name=f"deltanet_{seq_len}",
    )(*args)
    return (res, None) if has_state else (res[0], res[1])


def _rope(x, cos, sin_signed):
    lane = lax.broadcasted_iota(jnp.int32, x.shape, 1)
    partner = jnp.where((lane % 32) < 16, pltpu.roll(x, LANES - 16, axis=1), pltpu.roll(x, 16, axis=1))
    return x * cos + partner * sin_signed


def _softmax_rows(s_parts):
    m = functools.reduce(jnp.maximum, [jnp.max(s, axis=-1, keepdims=True) for s in s_parts])
    e_parts = [jnp.exp(s - m) for s in s_parts]
    den = functools.reduce(lambda a, b: a + b, [jnp.sum(e, axis=-1, keepdims=True) for e in e_parts])
    inv = 1.0 / den
    return [e * inv for e in e_parts]


def _da_kernel(*refs, layer, latent):
    if latent:
        (q_ref, k_ref, v_ref, lam_ref, sg_ref, cq_ref, sq_ref, ck_ref, sk_ref, ctxk_ref, ctxv_ref,
         o_ref, krot_s) = refs
    else:
        q_ref, k_ref, v_ref, lam_ref, sg_ref, o_ref, ko_ref, vo_ref = refs
    lam_init = 0.8 - 0.6 * math.exp(-0.3 * layer)
    lp = lam_ref[...]
    dots = jnp.sum(jnp.concatenate([lp[0:1] * lp[1:2], lp[2:3] * lp[3:4]], axis=0), axis=1, keepdims=True)
    e = jnp.exp(dots)
    lam = e[0:1, :] - e[1:2, :] + lam_init

    q = q_ref[...].astype(F32)
    if latent:
        @pl.when(pl.program_id(2) == 0)
        def _():
            krot_s[...] = _rope(k_ref[...].astype(F32), ck_ref[...], sk_ref[...]).astype(BF16)
        q = _rope(q, cq_ref[...], sq_ref[...])
        keys = [ctxk_ref[0, 0].astype(BF16), krot_s[...]]
        vals = [ctxv_ref[0, 0].astype(BF16), v_ref[...]]
    else:
        keys = [k_ref[...]]
        vals = [v_ref[...]]
        ko_ref[0, 0] = k_ref[...].astype(F32)
        vo_ref[0, 0] = v_ref[...].astype(F32)
    q = q * (DA_DH ** -0.5)
    lane = lax.broadcasted_iota(jnp.int32, q.shape, 1)
    q1 = jnp.where(lane < DA_DH, q, 0.0).astype(BF16)
    q2 = jnp.where(lane >= DA_DH, q, 0.0).astype(BF16)
    p1 = _softmax_rows([_bdot_nt(q1, kk) for kk in keys])
    p2 = _softmax_rows([_bdot_nt(q2, kk) for kk in keys])
    o = None
    for a1, a2, vv in zip(p1, p2, vals):
        part = _bdot(a1 - lam * a2, vv)
        o = part if o is None else o + part
    o_ref[...] = (_rms(o, sg_ref[...]) * (1.0 - lam_init)).astype(BF16)


def _diff_attention(p_main, lam_p, subln_g, layer, *, n_seq, seq_len, row_blk0, tq, rope=None, ctx=None):
    latent = ctx is not None
    nq = seq_len // tq
    qpb = seq_len // tq
    in_specs = [pl.BlockSpec((tq, LANES), lambda b, h, i: ((row_blk0 + b) * qpb + i, CB_QB + h)),
                pl.BlockSpec((seq_len, LANES), lambda b, h, i: (row_blk0 + b, CB_KB + h)),
                pl.BlockSpec((seq_len, LANES), lambda b, h, i: (row_blk0 + b, CB_VB + h)),
                pl.BlockSpec((4, LANES), lambda b, h, i: (0, 0)),
                pl.BlockSpec((1, DA_DV), lambda b, h, i: (0, 0))]
    args = [p_main, p_main, p_main, jnp.pad(lam_p, ((0, 0), (0, LANES - DA_DH))), subln_g.reshape(1, DA_DV)]
    o_spec = pl.BlockSpec((tq, LANES), lambda b, h, i: (b * qpb + i, h))
    o_shape = jax.ShapeDtypeStruct((n_seq * seq_len, DA_HEADS * DA_DV), BF16)
    scratch = []
    if latent:
        cos, sin_signed = rope
        ctx_k, ctx_v = ctx
        n_ctx = ctx_k.shape[2]
        in_specs += [pl.BlockSpec((tq, LANES), lambda b, h, i: (i, 0)),
                     pl.BlockSpec((tq, LANES), lambda b, h, i: (i, 0)),
                     pl.BlockSpec((seq_len, LANES), lambda b, h, i: (0, 0)),
                     pl.BlockSpec((seq_len, LANES), lambda b, h, i: (0, 0)),
                     pl.BlockSpec((1, 1, n_ctx, DA_DV), lambda b, h, i: (b, h, 0, 0)),
                     pl.BlockSpec((1, 1, n_ctx, DA_DV), lambda b, h, i: (b, h, 0, 0))]
        args += [cos, sin_signed, cos, sin_signed, ctx_k, ctx_v]
        out_specs, out_shape = o_spec, o_shape
        scratch = [pltpu.VMEM((seq_len, LANES), BF16)]
    else:
        kv_spec = pl.BlockSpec((1, 1, seq_len, DA_DV), lambda b, h, i: (b, h, 0, 0))
        kv_shape = jax.ShapeDtypeStruct((n_seq, DA_HEADS, seq_len, DA_DV), F32)
        out_specs = [o_spec, kv_spec, kv_spec]
        out_shape = [o_shape, kv_shape, kv_shape]
    res = pl.pallas_call(
        functools.partial(_da_kernel, layer=layer, latent=latent),
        grid=(n_seq, DA_HEADS, nq),
        in_specs=in_specs, out_specs=out_specs, out_shape=out_shape,
        scratch_shapes=scratch,
        compiler_params=_cparams(("arbitrary", "arbitrary", "arbitrary")),
        name=f"diff_attention_{seq_len}",
    )(*args)
    return (res, None, None) if latent else tuple(res)


def _rope_tables(n_tok):
    half = DA_DH // 2
    inv = ROPE_BASE ** (-jnp.arange(0, half, 2, dtype=F32) / half)
    t = jnp.arange(n_tok)
    ang_r = (t // GRID_W).astype(F32)[:, None] * inv
    ang_c = (t % GRID_W).astype(F32)[:, None] * inv
    cos32 = lambda a: jnp.concatenate([jnp.cos(a), jnp.cos(a)], axis=-1)
    sin32 = lambda a: jnp.concatenate([-jnp.sin(a), jnp.sin(a)], axis=-1)
    cos = jnp.concatenate([cos32(ang_r), cos32(ang_c)] * 2, axis=-1)
    sin_signed = jnp.concatenate([sin32(ang_r), sin32(ang_c)] * 2, axis=-1)
    return cos, sin_signed


def _dft_tables(n):
    t_lo = 64
    k = jnp.arange(n, dtype=jnp.int32)[:, None]
    ang = lambda m: ((k * m[None, :]) % (2 * n)).astype(F32) * (math.pi / n)
    a = ang(t_lo * jnp.arange(n // t_lo, dtype=jnp.int32))
    b = ang(jnp.arange(t_lo, dtype=jnp.int32))
    ca, sa, cb, sb = jnp.cos(a)[:, :, None], jnp.sin(a)[:, :, None], jnp.cos(b)[:, None, :], jnp.sin(b)[:, None, :]
    cos_t = (ca * cb - sa * sb).reshape(n, n)
    nsin_t = -(sa * cb + ca * sb).reshape(n, n)

    def hi_lo(x):
        hi = x.astype(BF16)
        return hi, (x - hi.astype(F32)).astype(BF16)

    return hi_lo(cos_t) + hi_lo(nsin_t)


def _hy_embedding(n):
    j = jnp.arange(n, dtype=F32)
    t = j / (n - 1)
    ang = (2.0 * math.pi * j / n)[:, None] * jnp.linspace(1e-4, HY_BANDS - 1, HY_BANDS, dtype=F32)
    z = jnp.concatenate([t[:, None], jnp.cos(ang), -jnp.sin(ang)], axis=-1)
    half = n // 2
    dist = jnp.abs(j - half) / half
    max_decay = math.log(HY_TARGET) / HY_FAST_DECAY
    min_decay = math.log(HY_TARGET) / HY_SLOW_DECAY
    deltas = jnp.abs(jnp.linspace(min_decay, max_decay, HY_W, dtype=F32))
    return jnp.pad(z, ((0, 0), (0, LANES - HY_EMB))), dist[:, None], deltas[None, :]


def _alt_rows(n):
    t = lax.broadcasted_iota(jnp.int32, (8, n), 1)
    return (1 - 2 * (t % 2)).astype(F32)


def _hy_filter_kernel(z_ref, dist_ref, delta_ref, w1_ref, b1_ref, fr_ref, w2_ref, b2_ref, w3_ref,
                      chi_ref, clo_ref, shi_ref, slo_ref, hre_ref, him_ref, hny_ref, hh_s, hl_s):
    @pl.when(pl.program_id(1) == 0)
    def _():
        fr = fr_ref[...]
        hdn = jnp.sin(fr * (_dot3(z_ref[...], w1_ref[...]) + b1_ref[...]))
        hdn = jnp.sin(fr * (_dot3(hdn, w2_ref[...]) + b2_ref[...]))
        h = _dot3(hdn, w3_ref[...])
        h = h * jnp.exp(-dist_ref[...] * delta_ref[...])
        h = h / jnp.sum(jnp.abs(h), axis=0, keepdims=True)
        hh = h.astype(BF16)
        hh_s[...] = hh
        hl_s[...] = (h - hh.astype(F32)).astype(BF16)
        hny_ref[...] = _dot_exact_lhs(_alt_rows(h.shape[0]), h)

    hh, hl = hh_s[...], hl_s[...]
    d = lambda t, x: jnp.dot(t[...], x, preferred_element_type=F32)
    hre_ref[...] = d(chi_ref, hh) + (d(chi_ref, hl) + d(clo_ref, hh))
    him_ref[...] = d(shi_ref, hh) + (d(shi_ref, hl) + d(slo_ref, hh))


def _hyena_filter_spectrum(n, emb, tables, w1, b1, freq, w2, b2, w3):
    z, dist, deltas = emb
    pad_o = LANES - HY_ORDER
    w1p = jnp.pad(w1, ((0, LANES - HY_EMB), (0, pad_o)))
    w2p = jnp.pad(w2, ((0, pad_o), (0, pad_o)))
    w3p = jnp.pad(w3, ((0, pad_o), (0, 0)))
    row = lambda t: jnp.pad(t.reshape(1, HY_ORDER), ((0, 0), (0, pad_o)))
    tc = 256
    kt = min(n, 512)
    full = lambda shape: pl.BlockSpec(shape, lambda j, k: (0, 0))
    tab = pl.BlockSpec((kt, n), lambda j, k: (k, 0))
    spec = pl.BlockSpec((kt, tc), lambda j, k: (k, j))
    return pl.pallas_call(
        _hy_filter_kernel,
        grid=(HY_W // tc, n // kt),
        in_specs=[full((n, LANES)), full((n, 1)), pl.BlockSpec((1, tc), lambda j, k: (0, j)),
                  full((LANES, LANES)), full((1, LANES)), full((1, LANES)),
                  full((LANES, LANES)), full((1, LANES)), pl.BlockSpec((LANES, tc), lambda j, k: (0, j)),
                  tab, tab, tab, tab],
        out_specs=[spec, spec, pl.BlockSpec((8, tc), lambda j, k: (0, j))],
        out_shape=[jax.ShapeDtypeStruct((n, HY_W), F32), jax.ShapeDtypeStruct((n, HY_W), F32),
                   jax.ShapeDtypeStruct((8, HY_W), F32)],
        scratch_shapes=[pltpu.VMEM((n, tc), BF16), pltpu.VMEM((n, tc), BF16)],
        compiler_params=_cparams(("arbitrary", "arbitrary")),
        name=f"hyena_filter_{n}",
    )(z, dist, deltas, w1p, row(b1), row(freq), w2p, row(b2), w3p, *tables)


def _hy_pre_kernel(x0_ref, x1_ref, v_ref, w0_ref, w1_ref, w2_ref, b0_ref, b1_ref, b2_ref, x0c_ref, gv_ref,
                   pad_s, x1_s):
    n = x0_ref.shape[0]
    _zero_conv_borders(pad_s)

    def conv(x_ref, w_ref, b_ref, store):
        _centred_conv(pad_s, lambda r0, rb: x_ref[r0:r0 + rb, :].astype(F32), store, n, w_ref, HY_SHORT,
                      b_ref[...])

    def store_x0(r0, y):
        x0c_ref[r0:r0 + y.shape[0], :] = y.astype(BF16)

    def store_x1(r0, y):
        x1_s[r0:r0 + y.shape[0], :] = y

    def store_gv(r0, y):
        rows = slice(r0, r0 + y.shape[0])
        gv_ref[rows, :] = (y * x1_s[rows, :]).astype(BF16)

    conv(x0_ref, w0_ref, b0_ref, store_x0)
    conv(x1_ref, w1_ref, b1_ref, store_x1)
    conv(v_ref, w2_ref, b2_ref, store_gv)


def _hyena_pre(p_main, conv_w, conv_b):
    tc = 256
    ncb = HY_W // tc
    outs = []
    for n_seq, seq_len, row_blk0 in ((BATCH, SEQ, 0), (DEC_BATCH, DEC_SEQ, N_PROMPT // DEC_SEQ)):
        tok = lambda cb: pl.BlockSpec((seq_len, tc), lambda b, j, cb=cb: (row_blk0 + b, cb // 2 + j))
        cw = lambda s: pl.BlockSpec((HY_SHORT, tc), lambda b, j, s=s: (0, s * ncb + j))
        cb_ = lambda s: pl.BlockSpec((1, tc), lambda b, j, s=s: (0, s * ncb + j))
        o_spec = pl.BlockSpec((seq_len, tc), lambda b, j: (b, j))
        o_shape = jax.ShapeDtypeStruct((n_seq * seq_len, HY_W), BF16)
        outs.append(pl.pallas_call(
            _hy_pre_kernel,
            grid=(n_seq, ncb),
            in_specs=[tok(CB_X0), tok(CB_X1), tok(CB_HV), cw(0), cw(1), cw(2), cb_(0), cb_(1), cb_(2)],
            out_specs=[o_spec, o_spec], out_shape=[o_shape, o_shape],
            scratch_shapes=[pltpu.VMEM((seq_len + 2 * CONV_PAD, tc), F32), pltpu.VMEM((seq_len, tc), F32)],
            compiler_params=_cparams(("arbitrary", "arbitrary")),
            name=f"hyena_pre_{seq_len}",
        )(p_main, p_main, p_main, conv_w, conv_w, conv_w,
          conv_b.reshape(1, -1), conv_b.reshape(1, -1), conv_b.reshape(1, -1)))
    return outs


def _hy_conv_kernel(gv_ref, x0_ref, d_ref, hre_ref, him_ref, hny_ref, cf_ref, sf_ref, ci_ref, si_ref,
                    o_ref, acc_s, *, seq_len, n_seq, kt):
    n = seq_len
    ki = pl.program_id(2)
    nk = pl.num_programs(2)
    hre, him = hre_ref[...], him_ref[...]
    cf, sf, ci, si = cf_ref[...], sf_ref[...], ci_ref[...], si_ref[...]
    freq = lax.broadcasted_iota(jnp.int32, hre.shape, 0) + ki * kt
    wk = jnp.where(freq == 0, 0.5 / n, 1.0 / n)
    quarter = freq % 4
    for b in range(n_seq):
        rows = pl.ds(b * seq_len, seq_len)
        gv = gv_ref[rows, :]
        gre = jnp.dot(cf, gv, preferred_element_type=F32)
        gim = jnp.dot(sf, gv, preferred_element_type=F32)
        yre = (gre * hre - gim * him) * wk
        yim = (gre * him + gim * hre) * wk
        are = jnp.where(quarter == 0, yre, jnp.where(quarter == 1, -yim, jnp.where(quarter == 2, -yre, yim)))
        aim = jnp.where(quarter == 0, yim, jnp.where(quarter == 1, yre, jnp.where(quarter == 2, -yim, -yre)))
        part = _bdot(ci, are) + _bdot(si, aim)

        @pl.when(ki == 0)
        def _():
            g_ny = jnp.dot(_alt_rows(n).astype(BF16), gv, preferred_element_type=F32)[0:1, :]
            y_ny = g_ny * hny_ref[0:1, :] * ((-1.0) ** (n // 2) / (2 * n))
            t = lax.broadcasted_iota(jnp.int32, part.shape, 0)
            acc_s[rows, :] = part + (1 - 2 * (t % 2)).astype(F32) * y_ny

        @pl.when(ki > 0)
        def _():
            acc_s[rows, :] += part

    @pl.when(ki == nk - 1)
    def _():
        y = acc_s[...] + gv_ref[...].astype(F32) * d_ref[...]
        o_ref[...] = (y * x0_ref[...].astype(F32)).astype(BF16)


def _hyena_conv(gv, x0c, d_skip, h_spec, tables, *, n_seq_total, seq_len, n_seq_blk, kt):
    tc = 256
    hre, him, hny = h_spec
    cos_t, nsin_t = tables
    rows = n_seq_blk * seq_len
    grid = (n_seq_total // n_seq_blk, HY_W // tc, seq_len // kt)
    tok = pl.BlockSpec((rows, tc), lambda g, j, k: (g, j))
    hs = pl.BlockSpec((kt, tc), lambda g, j, k: (k, j))
    fwd = pl.BlockSpec((kt, seq_len), lambda g, j, k: (k, 0))
    inv = pl.BlockSpec((seq_len, kt), lambda g, j, k: (0, k))
    return pl.pallas_call(
        functools.partial(_hy_conv_kernel, seq_len=seq_len, n_seq=n_seq_blk, kt=kt),
        grid=grid,
        in_specs=[tok, tok, pl.BlockSpec((1, tc), lambda g, j, k: (0, j)), hs, hs,
                  pl.BlockSpec((8, tc), lambda g, j, k: (0, j)), fwd, fwd, inv, inv],
        out_specs=tok,
        out_shape=jax.ShapeDtypeStruct((n_seq_total * seq_len, HY_W), BF16),
        scratch_shapes=[pltpu.VMEM((rows, tc), F32)],
        compiler_params=_cparams(("arbitrary", "arbitrary", "arbitrary")),
        name=f"hyena_conv_{seq_len}",
    )(gv, x0c, d_skip.reshape(1, HY_W), hre, him, hny, cos_t, nsin_t, cos_t, nsin_t)


MERGE_TM = 512


def _merge_kernel(x_ref, oa_ref, ob_ref, oc_ref, ga_ref, gb_ref, gc_ref, wa_ref, wb_ref, wc_ref,
                  wo_ref, mod_ref, y_ref):
    sig = lambda r: jax.nn.sigmoid(r[...].astype(F32))
    merged = (sig(ga_ref) * jnp.dot(oa_ref[...], wa_ref[...], preferred_element_type=F32)
              + sig(gb_ref) * jnp.dot(ob_ref[...], wb_ref[...], preferred_element_type=F32)
              + sig(gc_ref) * jnp.dot(oc_ref[...], wc_ref[...], preferred_element_type=F32))
    out = _bdot(merged, wo_ref[...])
    y_ref[...] = x_ref[...] + mod_ref[0, 2:3, :] * out


def _merge(x, o_a, o_b, o_c, p_main, mods, w_a, w_b, w_c, w_out):
    tm = MERGE_TM
    tok = lambda w: pl.BlockSpec((tm, w), lambda i: (i, 0))
    gate = lambda s: pl.BlockSpec((tm, D_MODEL), lambda i, s=s: (i, CB_GATE // 8 + s))
    wfull = lambda r: pl.BlockSpec((r, D_MODEL), lambda i: (0, 0))
    return pl.pallas_call(
        _merge_kernel,
        grid=(N_TOK // tm,),
        in_specs=[tok(D_MODEL), tok(DN_W), tok(DN_W), tok(HY_W), gate(0), gate(1), gate(2),
                  wfull(DN_W), wfull(DN_W), wfull(HY_W), wfull(D_MODEL),
                  pl.BlockSpec((1, 6, D_MODEL), lambda i: (_mod_row(i, tm), 0, 0))],
        out_specs=tok(D_MODEL),
        out_shape=jax.ShapeDtypeStruct((N_TOK, D_MODEL), F32),
        compiler_params=_cparams(("arbitrary",)),
        name="merge_out_projection",
    )(x, o_a, o_b, o_c, p_main, p_main, p_main, w_a, w_b, w_c, w_out, mods)


FFN_TM = 1024
FFN_TF = 256


def _ffn_kernel(x_ref, g_ref, mod_ref, wg_ref, wu_ref, wo_ref, fg_ref, y_ref, h_scr, acc_scr, *, final):
    f = pl.program_id(1)
    nf = pl.num_programs(1)

    @pl.when(f == 0)
    def _():
        g = g_ref[...]
        sh = mod_ref[0, 3:4, :]
        sc1 = 1.0 + mod_ref[0, 4:5, :]

        def body(r, carry):
            rows = pl.ds(pl.multiple_of(r * ROW_CHUNK, ROW_CHUNK), ROW_CHUNK)
            h_scr[rows, :] = (_rms(x_ref[rows, :], g) * sc1 + sh).astype(BF16)
            return carry

        lax.fori_loop(0, FFN_TM // ROW_CHUNK, body, 0)

    h = h_scr[...]
    gate = jnp.dot(h, wg_ref[...], preferred_element_type=F32)
    up = jnp.dot(h, wu_ref[...], preferred_element_type=F32)
    part = _bdot(_silu(gate) * up, wo_ref[...])

    @pl.when(f == 0)
    def _():
        acc_scr[...] = part

    @pl.when(f > 0)
    def _():
        acc_scr[...] += part

    @pl.when(f == nf - 1)
    def _():
        y = x_ref[...] + mod_ref[0, 5:6, :] * acc_scr[...]
        if final:
            y = _rms(y, fg_ref[...])
        y_ref[...] = y


def _ffn(x, norm_g, mods, w_in, w_out, final_g, *, final):
    tm, tf = FFN_TM, FFN_TF
    nf = D_FF // tf
    return pl.pallas_call(
        functools.partial(_ffn_kernel, final=final),
        grid=(N_TOK // tm, nf),
        in_specs=[pl.BlockSpec((tm, D_MODEL), lambda i, f: (i, 0)),
                  pl.BlockSpec((1, D_MODEL), lambda i, f: (0, 0)),
                  pl.BlockSpec((1, 6, D_MODEL), lambda i, f: (_mod_row(i, tm), 0, 0)),
                  pl.BlockSpec((D_MODEL, tf), lambda i, f: (0, f)),
                  pl.BlockSpec((D_MODEL, tf), lambda i, f: (0, nf + f)),
                  pl.BlockSpec((tf, D_MODEL), lambda i, f: (f, 0)),
                  pl.BlockSpec((1, D_MODEL), lambda i, f: (0, 0))],
        out_specs=pl.BlockSpec((tm, D_MODEL), lambda i, f: (i, 0)),
        out_shape=jax.ShapeDtypeStruct((N_TOK, D_MODEL), F32),
        scratch_shapes=[pltpu.VMEM((tm, D_MODEL), BF16), pltpu.VMEM((tm, D_MODEL), F32)],
        compiler_params=_cparams(("arbitrary", "arbitrary")),
        name="ffn",
    )(x, norm_g.reshape(1, D_MODEL), mods, w_in, w_in, w_out, final_g.reshape(1, D_MODEL))


def kernel(x_prompt, x_sample, cache_k, cache_v, state_dn, c, c_ctx, norm1_g, norm2_g, w_mod, b_mod,
           w_in, dn_conv_w, dn_a_log, dn_dt_bias, dn_norm_g, da_lambda, da_subln_g, hy_conv_w,
           hy_conv_b, hy_w1, hy_b1, hy_freq, hy_w2, hy_b2, hy_w3, hy_d, w_br_a, w_br_b, w_br_c,
           w_out, w_ffn_in, w_ffn_out, final_g):
    x = jnp.concatenate([x_prompt.reshape(N_PROMPT, D_MODEL), x_sample.reshape(N_SAMPLE, D_MODEL)], axis=0)
    cond8 = jnp.concatenate([c_ctx[None, :], c, jnp.zeros((8 - 1 - DEC_BATCH, D_MODEL), F32)], axis=0)
    rope = _rope_tables(DEC_SEQ)
    tab_pf = _dft_tables(SEQ)
    tab_sf = _dft_tables(DEC_SEQ)
    tab_p = (tab_pf[0], tab_pf[2])
    tab_s = (tab_sf[0], tab_sf[2])
    emb_p, emb_s = _hy_embedding(SEQ), _hy_embedding(DEC_SEQ)
    sample_blk0 = N_PROMPT // DEC_SEQ

    new_k, new_v, new_s = [], [], []
    for l in range(DEPTH):
        mods = _modulation(cond8, w_mod[l], b_mod[l]).reshape(8, 6, D_MODEL)
        w_l = w_in[l]
        w_main = jnp.concatenate([w_l[:, :3 * DN_W], w_l[:, 3 * DN_W + 4 * DN_HEADS:]], axis=1).astype(BF16)
        w_small = jnp.pad(w_l[:, 3 * DN_W:3 * DN_W + 4 * DN_HEADS],
                          ((0, 0), (0, LANES - 4 * DN_HEADS))).astype(BF16)
        p_main, p_small = _in_projection(x, norm1_g[l], mods, w_main, w_small)

        dn_args = (p_main, p_small, dn_conv_w[l], dn_a_log[l], dn_dt_bias[l], dn_norm_g[l])
        oa_p, s_fin = _deltanet(*dn_args, None, n_seq=BATCH, seq_len=SEQ, row_blk0=0, hps=4)
        oa_s, _ = _deltanet(*dn_args, state_dn[:, l], n_seq=DEC_BATCH, seq_len=DEC_SEQ, row_blk0=sample_blk0,
                            hps=2)

        ob_p, k_l, v_l = _diff_attention(p_main, da_lambda[l], da_subln_g[l], l, n_seq=BATCH, seq_len=SEQ,
                                         row_blk0=0, tq=SEQ)
        ob_s, _, _ = _diff_attention(p_main, da_lambda[l], da_subln_g[l], l, n_seq=DEC_BATCH,
                                     seq_len=DEC_SEQ, row_blk0=sample_blk0, tq=256, rope=rope,
                                     ctx=(cache_k[:, l], cache_v[:, l]))

        hy_w = (hy_w1[l], hy_b1[l], hy_freq[l], hy_w2[l], hy_b2[l], hy_w3[l])
        hspec_p = _hyena_filter_spectrum(SEQ, emb_p, tab_pf, *hy_w)
        hspec_s = _hyena_filter_spectrum(DEC_SEQ, emb_s, tab_sf, *hy_w)
        (x0_p, gv_p), (x0_s, gv_s) = _hyena_pre(p_main, hy_conv_w[l], hy_conv_b[l])
        oc_p = _hyena_conv(gv_p, x0_p, hy_d[l], hspec_p, tab_p, n_seq_total=BATCH, seq_len=SEQ,
                           n_seq_blk=BATCH, kt=SEQ)
        oc_s = _hyena_conv(gv_s, x0_s, hy_d[l], hspec_s, tab_s, n_seq_total=DEC_BATCH, seq_len=DEC_SEQ,
                           n_seq_blk=2, kt=512)

        cat = lambda a, b: jnp.concatenate([a, b], axis=0)
        x = _merge(x, cat(oa_p, oa_s), cat(ob_p, ob_s), cat(oc_p, oc_s), p_main, mods,
                   w_br_a[l].astype(BF16), w_br_b[l].astype(BF16), w_br_c[l].astype(BF16),
                   w_out[l].astype(BF16))
        x = _ffn(x, norm2_g[l], mods, w_ffn_in[l].astype(BF16), w_ffn_out[l].astype(BF16), final_g,
                 final=(l == DEPTH - 1))
        new_k.append(k_l)
        new_v.append(v_l)
        new_s.append(s_fin)

    y_prompt = x[:N_PROMPT].reshape(BATCH, SEQ, D_MODEL)
    y_sample = x[N_PROMPT:].reshape(DEC_BATCH, DEC_SEQ, D_MODEL)
    return (y_prompt, y_sample, jnp.stack(new_k, axis=1), jnp.stack(new_v, axis=1), jnp.stack(new_s, axis=1))
```

```python
import functools
import math

import jax
import jax.numpy as jnp
from jax import lax
from jax.experimental import pallas as pl
from jax.experimental.pallas import tpu as pltpu

F32 = jnp.float32
BF16 = jnp.bfloat16

D_MODEL = 1024
BATCH = 16
SEQ = 256
DEPTH = 2
DEC_BATCH = 4
DEC_SEQ = 2048
PAST_LEN = 256
GRID_W = 64
RMS_EPS = 1e-6
DN_HEADS = 4
DN_DK = 128
DN_DV = 128
DN_W = DN_HEADS * DN_DK
DN_CONV = 5
DN_CHUNK = 64
DA_HEADS = 4
DA_DH = 64
DA_DV = 2 * DA_DH
ROPE_BASE = 10000.0
HY_W = 512
HY_SHORT = 3
HY_BANDS = 8
HY_EMB = 1 + 2 * HY_BANDS
HY_ORDER = 64
HY_FAST_DECAY = 0.3
HY_SLOW_DECAY = 1.5
HY_TARGET = 1e-2
D_FF = ((8 * D_MODEL // 3 + 255) // 256) * 256

N_PROMPT = BATCH * SEQ
N_SAMPLE = DEC_BATCH * DEC_SEQ
N_TOK = N_PROMPT + N_SAMPLE
LANES = 128
N_MAIN = 8192
CB_QA, CB_KA, CB_VA, CB_ZA = 0, 4, 8, 12
CB_QB, CB_KB, CB_VB = 16, 20, 24
CB_X0, CB_X1, CB_HV = 28, 32, 36
CB_GATE = 40
VMEM_LIMIT = 56 * 1024 * 1024


def _cparams(sem):
    return pltpu.CompilerParams(dimension_semantics=sem, vmem_limit_bytes=VMEM_LIMIT)


def _bdot(a, b):
    return jnp.dot(a.astype(BF16), b.astype(BF16), preferred_element_type=F32)


def _bdot_nt(a, b):
    return lax.dot_general(a.astype(BF16), b.astype(BF16), (((1,), (1,)), ((), ())),
                           preferred_element_type=F32)


def _bdot_tn(a, b):
    return lax.dot_general(a.astype(BF16), b.astype(BF16), (((0,), (0,)), ((), ())),
                           preferred_element_type=F32)


def _split3(x):
    hi = x.astype(BF16)
    r = x - hi.astype(F32)
    mid = r.astype(BF16)
    lo = (r - mid.astype(F32)).astype(BF16)
    return hi, mid, lo


def _dot_exact_lhs(t, x):
    hi, mid, lo = _split3(x)
    tb = t.astype(BF16)
    d = lambda p: jnp.dot(tb, p, preferred_element_type=F32)
    return d(hi) + d(mid) + d(lo)


def _dot_exact_rhs(x, e):
    hi, mid, lo = _split3(x)
    eb = e.astype(BF16)
    d = lambda p: jnp.dot(p, eb, preferred_element_type=F32)
    return d(hi) + d(mid) + d(lo)


def _dot3(a, b):
    ah = a.astype(BF16)
    al = (a - ah.astype(F32)).astype(BF16)
    bh = b.astype(BF16)
    bl = (b - bh.astype(F32)).astype(BF16)
    d = lambda p, q: jnp.dot(p, q, preferred_element_type=F32)
    return d(ah, bh) + d(ah, bl) + d(al, bh)


def _silu(x):
    return x * jax.nn.sigmoid(x)


def _rms(x, g):
    return x * lax.rsqrt(jnp.mean(x * x, axis=-1, keepdims=True) + RMS_EPS) * g


def _mod_row(i, tm):
    n_prompt_tiles = N_PROMPT // tm
    tiles_per_seq = DEC_SEQ // tm
    return jnp.where(i < n_prompt_tiles, 0, 1 + (i - n_prompt_tiles) // tiles_per_seq)


def _mod_kernel(c_ref, w_ref, b_ref, o_ref):
    o_ref[...] = _bdot(_silu(c_ref[...]), w_ref[...]) + b_ref[...]


def _modulation(cond8, w_mod, b_mod, layer):
    n = 6 * D_MODEL
    tn = 1024
    return pl.pallas_call(
        _mod_kernel,
        grid=(n // tn,),
        in_specs=[pl.BlockSpec((8, D_MODEL), lambda j: (0, 0)),
                  pl.BlockSpec((None, D_MODEL, tn), lambda j: (layer, 0, j)),
                  pl.BlockSpec((1, tn), lambda j: (0, j))],
        out_specs=pl.BlockSpec((8, tn), lambda j: (0, j)),
        out_shape=jax.ShapeDtypeStruct((8, n), F32),
        compiler_params=_cparams(("arbitrary",)),
        name="modulation",
    )(cond8, w_mod, b_mod.reshape(1, n))


IN_TM = 2048
IN_TN = 1024
ROW_CHUNK = 256


def _inproj_kernel(x_ref, g_ref, mod_ref, wm_ref, ws_ref, p_ref, ps_ref, h_scr):
    @pl.when(pl.program_id(1) == 0)
    def _():
        g = g_ref[...]
        sh = mod_ref[0, 0:1, :]
        sc1 = 1.0 + mod_ref[0, 1:2, :]

        def body(r, carry):
            rows = pl.ds(pl.multiple_of(r * ROW_CHUNK, ROW_CHUNK), ROW_CHUNK)
            hb = (_rms(x_ref[rows, :], g) * sc1 + sh).astype(BF16)
            h_scr[rows, :] = hb
            ps_ref[rows, :] = jnp.dot(hb, ws_ref[...], preferred_element_type=F32)
            return carry

        lax.fori_loop(0, IN_TM // ROW_CHUNK, body, 0)

    p_ref[...] = jnp.dot(h_scr[...], wm_ref[...], preferred_element_type=F32).astype(BF16)


def _in_projection(x, norm_g, mods, w_main, w_small):
    grid = (N_TOK // IN_TM, N_MAIN // IN_TN)
    return pl.pallas_call(
        _inproj_kernel,
        grid=grid,
        in_specs=[pl.BlockSpec((IN_TM, D_MODEL), lambda i, j: (i, 0)),
                  pl.BlockSpec((1, D_MODEL), lambda i, j: (0, 0)),
                  pl.BlockSpec((1, 6, D_MODEL), lambda i, j: (_mod_row(i, IN_TM), 0, 0)),
                  pl.BlockSpec((D_MODEL, IN_TN), lambda i, j: (0, j)),
                  pl.BlockSpec((D_MODEL, LANES), lambda i, j: (0, 0))],
        out_specs=[pl.BlockSpec((IN_TM, IN_TN), lambda i, j: (i, j)),
                   pl.BlockSpec((IN_TM, LANES), lambda i, j: (i, 0))],
        out_shape=[jax.ShapeDtypeStruct((N_TOK, N_MAIN), BF16),
                   jax.ShapeDtypeStruct((N_TOK, LANES), F32)],
        scratch_shapes=[pltpu.VMEM((IN_TM, D_MODEL), BF16)],
        compiler_params=_cparams(("arbitrary", "arbitrary")),
        name="in_projection",
    )(x, norm_g.reshape(1, D_MODEL), mods, w_main, w_small)


DN_UNIT = 2 * DN_CHUNK
DN_GROUP = 4


CONV_PAD = 8


def _zero_conv_borders(pad_ref):
    n = pad_ref.shape[0] - 2 * CONV_PAD
    zeros = jnp.zeros((CONV_PAD, pad_ref.shape[1]), F32)
    pad_ref[0:CONV_PAD, :] = zeros
    pad_ref[CONV_PAD + n:, :] = zeros


CONV_ROWS = 512


def _centred_conv(pad_ref, load, store, n, w_ref, n_taps, bias=None):
    rb = min(n, CONV_ROWS)
    half = n_taps // 2
    for r0 in range(0, n, rb):
        pad_ref[CONV_PAD + r0:CONV_PAD + r0 + rb, :] = load(r0, rb)
    for r0 in range(0, n, rb):
        acc = None
        for tap in range(n_taps):
            lo = CONV_PAD + r0 + tap - half
            term = pad_ref[lo:lo + rb, :] * w_ref[tap:tap + 1, :]
            acc = term if acc is None else acc + term
        store(r0, acc if bias is None else acc + bias)


def _l2norm(x):
    return x * lax.rsqrt(jnp.sum(x * x, axis=-1, keepdims=True) + RMS_EPS)


def _softplus(x):
    return jnp.maximum(x, 0.0) + jnp.log1p(jnp.exp(-jnp.abs(x)))


DN_BASE = 16


def _unit_tri_inverse(a_list, ri, ci):
    eye = (ri == ci).astype(F32)
    blk = lambda b: (ri // b) == (ci // b)
    y = [jnp.where(blk(DN_BASE), -a, 0.0) for a in a_list]
    p = [eye + yi for yi in y]
    for _ in range(3):
        y = [_bdot(yi, yi) for yi in y]
        p = [pi + _bdot(yi, pi) for yi, pi in zip(y, p)]
    b = DN_BASE
    while b < DN_CHUNK:
        off_mask = blk(2 * b) & ~blk(b)
        t = [_bdot(jnp.where(off_mask, a, 0.0), pi) for a, pi in zip(a_list, p)]
        p = [pi - _bdot(pi, ti) for pi, ti in zip(p, t)]
        b *= 2
    return p


def _dn_units(chains):
    u, c = DN_UNIT, DN_CHUNK
    ri = lax.broadcasted_iota(jnp.int32, (u, u), 0)
    ci = lax.broadcasted_iota(jnp.int32, (u, u), 1)
    same = (ri // c) == (ci // c)
    eye_mask = ri == ci
    hi_rows = ri >= c
    incl_of = {False: same & (ri >= ci), True: same & (ri <= ci)}
    strict_of = {False: same & (ri > ci), True: same & (ri < ci)}
    incl = [incl_of[ch['backward']] for ch in chains]
    strict = [strict_of[ch['backward']] for ch in chains]
    gc = [_dot_exact_lhs(m.astype(F32), ch['g']) for m, ch in zip(incl, chains)]
    g_tot = [(x[0:1, :], x[c:c + 1, :]) if ch['backward'] else (x[c - 1:c, :], x[u - 1:u, :])
             for x, ch in zip(gc, chains)]
    gc_row = [jnp.sum(jnp.where(eye_mask, x, 0.0), axis=0, keepdims=True) for x in gc]
    dec = [jnp.exp(jnp.where(m, x - xr, -1e30)) for m, x, xr in zip(incl, gc, gc_row)]
    e_gc = [jnp.exp(x) for x in gc]
    a = [jnp.where(m, ch['kk'] * ch['beta'] * d, 0.0) for m, ch, d in zip(strict, chains, dec)]
    qk = [ch['qk_raw'] * d for ch, d in zip(chains, dec)]
    eye = eye_mask.astype(F32)
    r = [p - eye for p in _unit_tri_inverse(a, ri, ci)]
    rhs = [jnp.concatenate([ch['v'] * ch['beta'], ch['k'] * (ch['beta'] * e)], axis=1)
           for ch, e in zip(chains, e_gc)]
    uwb = [(x + _bdot(ri_, x)).astype(BF16) for ri_, x in zip(r, rhs)]
    qkuw = [_bdot(x, y) for x, y in zip(qk, uwb)]
    kd = [ch['k'] * jnp.exp(jnp.where(hi_rows, gt[1], gt[0]) - x) for ch, gt, x in zip(chains, g_tot, gc)]
    tp = [[_bdot_tn(jnp.where(keep, x, 0.0), y) for x, y in zip(kd, uwb)]
          for keep in (~hi_rows, hi_rows)]
    out = []
    for i, ch in enumerate(chains):
        o_local = qkuw[i][:, :DN_DV]
        q_eff = ch['q'] * e_gc[i] - qkuw[i][:, DN_DV:]
        per_chunk = [(-tp[h][i][:, DN_DV:], tp[h][i][:, :DN_DV], jnp.exp(g_tot[i][h])) for h in range(2)]
        out.append((o_local, q_eff, per_chunk))
    return out


def _dn_kernel(*refs, seq_len, has_state, hps):
    if has_state:
        (q_ref, k_ref, v_ref, z_ref, ps_ref, cwq_ref, cwk_ref, cwv_ref, alog_ref, dtb_ref, ng_ref,
         s0_ref, o_ref, q_s, k_s, v_s, bt_s, g_s, oacc_s, qe_s, th_s, psi_s, egl_s, st_s, pad_s) = refs
        sfin_ref = None
    else:
        (q_ref, k_ref, v_ref, z_ref, ps_ref, cwq_ref, cwk_ref, cwv_ref, alog_ref, dtb_ref, ng_ref,
         o_ref, sfin_ref, q_s, k_s, v_s, bt_s, g_s, oacc_s, qe_s, th_s, psi_s, egl_s, st_s, pad_s) = refs
        s0_ref = None
    n_units = seq_len // DN_UNIT
    n_chunks = seq_len // DN_CHUNK
    c = DN_CHUNK
    sel_r = lax.broadcasted_iota(jnp.int32, (LANES, 4 * LANES), 0)
    sel_blk = lax.broadcasted_iota(jnp.int32, (LANES, 4 * LANES), 1) // LANES
    sels = []
    for j in range(hps):
        head = pl.program_id(1) * hps + j
        src_col = head + DN_HEADS * jnp.where(sel_blk == 0, 0, jnp.where(sel_blk == 1, 2, jnp.where(sel_blk == 2, 1, 3)))
        sels.append((sel_r == src_col).astype(BF16))

    def gate_rows(r, carry):
        rows = pl.ds(pl.multiple_of(r * ROW_CHUNK, ROW_CHUNK), ROW_CHUNK)
        ps = ps_ref[rows, :]
        lane = lax.broadcasted_iota(jnp.int32, ps.shape, 1)
        gate_cols = jnp.where(lane < 2 * DN_HEADS, jax.nn.sigmoid(ps),
                              -(jnp.exp(alog_ref[...]) * _softplus(ps + dtb_ref[...])))
        parts = _split3(gate_cols)
        for j in range(hps):
            ext = functools.reduce(lambda a, b: a + b,
                                   [jnp.dot(part, sels[j], preferred_element_type=F32) for part in parts])
            for d in range(2):
                bt_s[j, d, rows, :] = ext[:, (2 * d) * LANES:(2 * d + 1) * LANES]
                g_s[j, d, rows, :] = ext[:, (2 * d + 1) * LANES:(2 * d + 2) * LANES]
        return carry

    lax.fori_loop(0, seq_len // ROW_CHUNK, gate_rows, 0)

    _zero_conv_borders(pad_s)
    for j in range(hps):
        cols = slice(j * LANES, (j + 1) * LANES)
        for x_ref, w_ref, out_s, post in (
                (q_ref, cwq_ref, q_s, lambda y: _l2norm(_silu(y)) * (DN_DK ** -0.5)),
                (k_ref, cwk_ref, k_s, lambda y: _l2norm(_silu(y))),
                (v_ref, cwv_ref, v_s, _silu)):
            def store(r0, y, out_s=out_s, post=post):
                out_s[j, r0:r0 + y.shape[0], :] = post(y)
            _centred_conv(pad_s, lambda r0, rb, x_ref=x_ref: x_ref[r0:r0 + rb, cols].astype(F32), store,
                          seq_len, w_ref.at[:, cols], DN_CONV)
        for d in range(2):
            st_s[j, d] = s0_ref[0, 0, d, j] if has_state else jnp.zeros((DN_DK, DN_DV), F32)

    def unit_group(grp, carry):
        where, chains = [], []
        for t in range(DN_GROUP):
            idx = grp * DN_GROUP + t
            j = idx // n_units
            n = idx % n_units
            rows = pl.ds(pl.multiple_of(n * DN_UNIT, DN_UNIT), DN_UNIT)
            qc, kc, vc = q_s[j, rows, :], k_s[j, rows, :], v_s[j, rows, :]
            kcb = kc.astype(BF16)
            kk = _bdot_nt(kcb, kcb)
            qk_raw = _bdot_nt(qc, kcb)
            for d in range(2):
                where.append((j, n, rows, d))
                chains.append(dict(q=qc, k=kc, v=vc, kk=kk, qk_raw=qk_raw, beta=bt_s[j, d, rows, :],
                                   g=g_s[j, d, rows, :], backward=(d == 1)))
        results = _dn_units(chains)
        for (j, n, rows, d), (o_loc, q_eff, per_chunk) in zip(where, results):
            qe_s[j, d, rows, :] = q_eff.astype(BF16)
            for half, (theta, psi, egl) in enumerate(per_chunk):
                th_s[j, d, 2 * n + half] = theta.astype(BF16)
                psi_s[j, d, 2 * n + half] = psi
                egl_s[j, d, 2 * n + half] = egl
            if d == 0:
                o_fwd = o_loc
            else:
                oacc_s[j, rows, :] = o_fwd + o_loc
        return carry

    lax.fori_loop(0, hps * n_units // DN_GROUP, unit_group, 0)

    def scan_step(i, carry):
        for j in range(hps):
            for d, n in ((0, i), (1, n_chunks - 1 - i)):
                rows = pl.ds(pl.multiple_of(n * c, c), c)
                s = st_s[j, d]
                sb = s.astype(BF16)
                oacc_s[j, rows, :] += jnp.dot(qe_s[j, d, rows, :], sb, preferred_element_type=F32)
                st_s[j, d] = (egl_s[j, d, n] * s + jnp.dot(th_s[j, d, n], sb, preferred_element_type=F32)
                              + psi_s[j, d, n])
        return carry

    lax.fori_loop(0, n_chunks, scan_step, 0)

    if sfin_ref is not None:
        for j in range(hps):
            for d in range(2):
                sfin_ref[0, d, j] = st_s[j, d]

    def out_rows(r, carry):
        rows = pl.ds(pl.multiple_of(r * ROW_CHUNK, ROW_CHUNK), ROW_CHUNK)
        for j in range(hps):
            cols = slice(j * LANES, (j + 1) * LANES)
            o = _rms(oacc_s[j, rows, :], ng_ref[...]) * _silu(z_ref[rows, cols].astype(F32))
            o_ref[rows, cols] = o.astype(BF16)
        return carry

    lax.fori_loop(0, seq_len // ROW_CHUNK, out_rows, 0)


def _deltanet(p_main, p_small, conv_w, a_log, dt_bias, norm_g, s0, *, n_seq, seq_len, row_blk0, hps, layer):
    has_state = s0 is not None
    n_chunks = seq_len // DN_CHUNK
    width = hps * LANES
    tok = lambda cb: pl.BlockSpec((seq_len, width), lambda b, h: (row_blk0 + b, cb // hps + h))
    cw = lambda cb: pl.BlockSpec((DN_CONV, width), lambda b, h: (0, cb // hps + h))
    in_specs = [tok(CB_QA), tok(CB_KA), tok(CB_VA), tok(CB_ZA),
                pl.BlockSpec((seq_len, LANES), lambda b, h: (row_blk0 + b, 0)),
                cw(0), cw(DN_HEADS), cw(2 * DN_HEADS),
                pl.BlockSpec((1, LANES), lambda b, h: (0, 0)),
                pl.BlockSpec((1, LANES), lambda b, h: (0, 0)),
                pl.BlockSpec((1, DN_DV), lambda b, h: (0, 0))]
    pad8 = lambda t: jnp.pad(t.reshape(1, 2 * DN_HEADS), ((0, 0), (2 * DN_HEADS, LANES - 4 * DN_HEADS)))
    args = [p_main, p_main, p_main, p_main, p_small, conv_w, conv_w, conv_w,
            pad8(a_log), pad8(dt_bias), norm_g.reshape(1, DN_DV)]
    o_spec = pl.BlockSpec((seq_len, width), lambda b, h: (b, h))
    o_shape = jax.ShapeDtypeStruct((n_seq * seq_len, DN_W), BF16)
    if has_state:
        in_specs.append(pl.BlockSpec((1, 1, 2, hps, DN_DK, DN_DV), lambda b, h: (b, layer, 0, h, 0, 0)))
        args.append(s0)
        out_specs, out_shape = o_spec, o_shape
    else:
        out_specs = [o_spec, pl.BlockSpec((1, 2, hps, DN_DK, DN_DV), lambda b, h: (b, 0, h, 0, 0))]
        out_shape = [o_shape, jax.ShapeDtypeStruct((n_seq, 2, DN_HEADS, DN_DK, DN_DV), F32)]
    scratch = [pltpu.VMEM((hps, seq_len, LANES), F32),
               pltpu.VMEM((hps, seq_len, LANES), F32),
               pltpu.VMEM((hps, seq_len, LANES), F32),
               pltpu.VMEM((hps, 2, seq_len, LANES), F32),
               pltpu.VMEM((hps, 2, seq_len, LANES), F32),
               pltpu.VMEM((hps, seq_len, LANES), F32),
               pltpu.VMEM((hps, 2, seq_len, LANES), BF16),
               pltpu.VMEM((hps, 2, n_chunks, DN_DK, DN_DK), BF16),
               pltpu.VMEM((hps, 2, n_chunks, DN_DK, DN_DV), F32),
               pltpu.VMEM((hps, 2, n_chunks, 1, LANES), F32),
               pltpu.VMEM((hps, 2, DN_DK, DN_DV), F32),
               pltpu.VMEM((seq_len + 2 * CONV_PAD, LANES), F32)]
    res = pl.pallas_call(
        functools.partial(_dn_kernel, seq_len=seq_len, has_state=has_state, hps=hps),
        grid=(n_seq, DN_HEADS // hps),
        in_specs=in_specs, out_specs=out_specs, out_shape=out_shape,
        scratch_shapes=scratch,
        compiler_params=_cparams(("arbitrary", "arbitrary")),
        name=f"deltanet_{seq_len}",
    )(*args)
    return (res, None) if has_state else (res[0], res[1])


def _rope(x, cos, sin_signed):
    lane = lax.broadcasted_iota(jnp.int32, x.shape, 1)
    partner = jnp.where((lane % 32) < 16, pltpu.roll(x, LANES - 16, axis=1), pltpu.roll(x, 16, axis=1))
    return x * cos + partner * sin_signed


DA_ROWS = 128


def _exp2_rows(s_parts):
    m = functools.reduce(jnp.maximum, [jnp.max(s, axis=-1, keepdims=True) for s in s_parts])
    e_parts = [jnp.exp2(s - m) for s in s_parts]
    den = functools.reduce(lambda a, b: a + b, [jnp.sum(e, axis=-1, keepdims=True) for e in e_parts])
    return e_parts, den


def _da_kernel(*refs, layer, latent):
    if latent:
        (q_ref, k_ref, v_ref, lam_ref, sg_ref, cq_ref, sq_ref, ck_ref, sk_ref, ctxk_ref, ctxv_ref,
         o_ref, krot_s) = refs
    else:
        q_ref, k_ref, v_ref, lam_ref, sg_ref, o_ref, ko_ref, vo_ref = refs
    lam_init = 0.8 - 0.6 * math.exp(-0.3 * layer)
    lp = lam_ref[...]
    dots = jnp.sum(jnp.concatenate([lp[0:1] * lp[1:2], lp[2:3] * lp[3:4]], axis=0), axis=1, keepdims=True)
    e = jnp.exp(dots)
    lam = e[0:1, :] - e[1:2, :] + lam_init

    q = q_ref[...].astype(F32)
    if latent:
        @pl.when(pl.program_id(2) == 0)
        def _():
            krot_s[...] = _rope(k_ref[...].astype(F32), ck_ref[...], sk_ref[...]).astype(BF16)
        q = _rope(q, cq_ref[...], sq_ref[...])
        keys = [ctxk_ref[0, 0, 0].astype(BF16), krot_s[...]]
        vals = [ctxv_ref[0, 0, 0].astype(BF16), v_ref[...]]
    else:
        keys = [k_ref[...]]
        vals = [v_ref[...]]
        ko_ref[0, 0] = k_ref[...].astype(F32)
        vo_ref[0, 0] = v_ref[...].astype(F32)
    q = q * (DA_DH ** -0.5 * math.log2(math.e))
    tq = q.shape[0]
    lane = lax.broadcasted_iota(jnp.int32, q.shape, 1)
    q12 = jnp.concatenate([jnp.where(lane < DA_DH, q, 0.0), jnp.where(lane >= DA_DH, q, 0.0)], axis=0).astype(BF16)
    groups = [q12[r:r + DA_ROWS] for r in range(0, 2 * tq, DA_ROWS)]
    scores = [[_bdot_nt(qg, kk) for kk in keys] for qg in groups]
    soft = [_exp2_rows(sg) for sg in scores]
    pv = []
    for e_parts, den in soft:
        acc = None
        for e, vv in zip(e_parts, vals):
            part = _bdot(e, vv)
            acc = part if acc is None else acc + part
        pv.append(acc * (1.0 / den))
    pv = jnp.concatenate(pv, axis=0)
    o = pv[:tq] - lam * pv[tq:]
    o_ref[...] = (_rms(o, sg_ref[...]) * (1.0 - lam_init)).astype(BF16)


def _diff_attention(p_main, lam_p, subln_g, layer, *, n_seq, seq_len, row_blk0, tq, rope=None, ctx=None):
    latent = ctx is not None
    nq = seq_len // tq
    qpb = seq_len // tq
    in_specs = [pl.BlockSpec((tq, LANES), lambda b, h, i: ((row_blk0 + b) * qpb + i, CB_QB + h)),
                pl.BlockSpec((seq_len, LANES), lambda b, h, i: (row_blk0 + b, CB_KB + h)),
                pl.BlockSpec((seq_len, LANES), lambda b, h, i: (row_blk0 + b, CB_VB + h)),
                pl.BlockSpec((4, LANES), lambda b, h, i: (0, 0)),
                pl.BlockSpec((1, DA_DV), lambda b, h, i: (0, 0))]
    args = [p_main, p_main, p_main, jnp.pad(lam_p, ((0, 0), (0, LANES - DA_DH))), subln_g.reshape(1, DA_DV)]
    o_spec = pl.BlockSpec((tq, LANES), lambda b, h, i: (b * qpb + i, h))
    o_shape = jax.ShapeDtypeStruct((n_seq * seq_len, DA_HEADS * DA_DV), BF16)
    scratch = []
    if latent:
        cos, sin_signed = rope
        ctx_k, ctx_v = ctx
        n_ctx = ctx_k.shape[3]
        in_specs += [pl.BlockSpec((tq, LANES), lambda b, h, i: (i, 0)),
                     pl.BlockSpec((tq, LANES), lambda b, h, i: (i, 0)),
                     pl.BlockSpec((seq_len, LANES), lambda b, h, i: (0, 0)),
                     pl.BlockSpec((seq_len, LANES), lambda b, h, i: (0, 0)),
                     pl.BlockSpec((1, 1, 1, n_ctx, DA_DV), lambda b, h, i: (b, layer, h, 0, 0)),
                     pl.BlockSpec((1, 1, 1, n_ctx, DA_DV), lambda b, h, i: (b, layer, h, 0, 0))]
        args += [cos, sin_signed, cos, sin_signed, ctx_k, ctx_v]
        out_specs, out_shape = o_spec, o_shape
        scratch = [pltpu.VMEM((seq_len, LANES), BF16)]
    else:
        kv_spec = pl.BlockSpec((1, 1, seq_len, DA_DV), lambda b, h, i: (b, h, 0, 0))
        kv_shape = jax.ShapeDtypeStruct((n_seq, DA_HEADS, seq_len, DA_DV), F32)
        out_specs = [o_spec, kv_spec, kv_spec]
        out_shape = [o_shape, kv_shape, kv_shape]
    res = pl.pallas_call(
        functools.partial(_da_kernel, layer=layer, latent=latent),
        grid=(n_seq, DA_HEADS, nq),
        in_specs=in_specs, out_specs=out_specs, out_shape=out_shape,
        scratch_shapes=scratch,
        compiler_params=_cparams(("arbitrary", "arbitrary", "arbitrary")),
        name=f"diff_attention_{seq_len}",
    )(*args)
    return (res, None, None) if latent else tuple(res)


def _rope_tables(n_tok):
    half = DA_DH // 2
    inv = ROPE_BASE ** (-jnp.arange(0, half, 2, dtype=F32) / half)
    t = jnp.arange(n_tok)
    ang_r = (t // GRID_W).astype(F32)[:, None] * inv
    ang_c = (t % GRID_W).astype(F32)[:, None] * inv
    cos32 = lambda a: jnp.concatenate([jnp.cos(a), jnp.cos(a)], axis=-1)
    sin32 = lambda a: jnp.concatenate([-jnp.sin(a), jnp.sin(a)], axis=-1)
    cos = jnp.concatenate([cos32(ang_r), cos32(ang_c)] * 2, axis=-1)
    sin_signed = jnp.concatenate([sin32(ang_r), sin32(ang_c)] * 2, axis=-1)
    return cos, sin_signed


def _dft_tables(n):
    t_lo = 64
    k = jnp.arange(n, dtype=jnp.int32)[:, None]
    ang = lambda m: ((k * m[None, :]) % (2 * n)).astype(F32) * (math.pi / n)
    a = ang(t_lo * jnp.arange(n // t_lo, dtype=jnp.int32))
    b = ang(jnp.arange(t_lo, dtype=jnp.int32))
    ca, sa, cb, sb = jnp.cos(a)[:, :, None], jnp.sin(a)[:, :, None], jnp.cos(b)[:, None, :], jnp.sin(b)[:, None, :]
    cos_t = (ca * cb - sa * sb).reshape(n, n)
    nsin_t = -(sa * cb + ca * sb).reshape(n, n)

    def hi_lo(x):
        hi = x.astype(BF16)
        return hi, (x - hi.astype(F32)).astype(BF16)

    return hi_lo(cos_t) + hi_lo(nsin_t)


def _hy_embedding(n):
    j = jnp.arange(n, dtype=F32)
    t = j / (n - 1)
    ang = (2.0 * math.pi * j / n)[:, None] * jnp.linspace(1e-4, HY_BANDS - 1, HY_BANDS, dtype=F32)
    z = jnp.concatenate([t[:, None], jnp.cos(ang), -jnp.sin(ang)], axis=-1)
    half = n // 2
    dist = jnp.abs(j - half) / half
    max_decay = math.log(HY_TARGET) / HY_FAST_DECAY
    min_decay = math.log(HY_TARGET) / HY_SLOW_DECAY
    deltas = jnp.abs(jnp.linspace(min_decay, max_decay, HY_W, dtype=F32))
    return jnp.pad(z, ((0, 0), (0, LANES - HY_EMB))), dist[:, None], deltas[None, :]


def _alt_rows(n):
    t = lax.broadcasted_iota(jnp.int32, (8, n), 1)
    return (1 - 2 * (t % 2)).astype(F32)


def _hy_filter_kernel(z_ref, dist_ref, delta_ref, w1_ref, b1_ref, fr_ref, w2_ref, b2_ref, w3_ref,
                      chi_ref, clo_ref, shi_ref, slo_ref, hre_ref, him_ref, hny_ref, hh_s, hl_s):
    @pl.when(pl.program_id(1) == 0)
    def _():
        fr = fr_ref[...]
        hdn = jnp.sin(fr * (_dot3(z_ref[...], w1_ref[...]) + b1_ref[...]))
        hdn = jnp.sin(fr * (_dot3(hdn, w2_ref[...]) + b2_ref[...]))
        h = _dot3(hdn, w3_ref[...])
        h = h * jnp.exp(-dist_ref[...] * delta_ref[...])
        h = h / jnp.sum(jnp.abs(h), axis=0, keepdims=True)
        hh = h.astype(BF16)
        hh_s[...] = hh
        hl_s[...] = (h - hh.astype(F32)).astype(BF16)
        hny_ref[...] = _dot_exact_lhs(_alt_rows(h.shape[0]), h)

    hh, hl = hh_s[...], hl_s[...]
    d = lambda t, x: jnp.dot(t[...], x, preferred_element_type=F32)
    hre_ref[...] = d(chi_ref, hh) + (d(chi_ref, hl) + d(clo_ref, hh))
    him_ref[...] = d(shi_ref, hh) + (d(shi_ref, hl) + d(slo_ref, hh))


def _hyena_filter_spectrum(n, emb, tables, w1, b1, freq, w2, b2, w3):
    z, dist, deltas = emb
    pad_o = LANES - HY_ORDER
    w1p = jnp.pad(w1, ((0, LANES - HY_EMB), (0, pad_o)))
    w2p = jnp.pad(w2, ((0, pad_o), (0, pad_o)))
    w3p = jnp.pad(w3, ((0, pad_o), (0, 0)))
    row = lambda t: jnp.pad(t.reshape(1, HY_ORDER), ((0, 0), (0, pad_o)))
    tc = 256
    kt = min(n, 512)
    full = lambda shape: pl.BlockSpec(shape, lambda j, k: (0, 0))
    tab = pl.BlockSpec((kt, n), lambda j, k: (k, 0))
    spec = pl.BlockSpec((kt, tc), lambda j, k: (k, j))
    return pl.pallas_call(
        _hy_filter_kernel,
        grid=(HY_W // tc, n // kt),
        in_specs=[full((n, LANES)), full((n, 1)), pl.BlockSpec((1, tc), lambda j, k: (0, j)),
                  full((LANES, LANES)), full((1, LANES)), full((1, LANES)),
                  full((LANES, LANES)), full((1, LANES)), pl.BlockSpec((LANES, tc), lambda j, k: (0, j)),
                  tab, tab, tab, tab],
        out_specs=[spec, spec, pl.BlockSpec((8, tc), lambda j, k: (0, j))],
        out_shape=[jax.ShapeDtypeStruct((n, HY_W), F32), jax.ShapeDtypeStruct((n, HY_W), F32),
                   jax.ShapeDtypeStruct((8, HY_W), F32)],
        scratch_shapes=[pltpu.VMEM((n, tc), BF16), pltpu.VMEM((n, tc), BF16)],
        compiler_params=_cparams(("arbitrary", "arbitrary")),
        name=f"hyena_filter_{n}",
    )(z, dist, deltas, w1p, row(b1), row(freq), w2p, row(b2), w3p, *tables)


def _hy_pre_kernel(x0_ref, x1_ref, v_ref, w0_ref, w1_ref, w2_ref, b0_ref, b1_ref, b2_ref, x0c_ref, gv_ref,
                   pad_s, x1_s):
    n = x0_ref.shape[0]
    _zero_conv_borders(pad_s)

    def conv(x_ref, w_ref, b_ref, store):
        _centred_conv(pad_s, lambda r0, rb: x_ref[r0:r0 + rb, :].astype(F32), store, n, w_ref, HY_SHORT,
                      b_ref[...])

    def store_x0(r0, y):
        x0c_ref[r0:r0 + y.shape[0], :] = y.astype(BF16)

    def store_x1(r0, y):
        x1_s[r0:r0 + y.shape[0], :] = y

    def store_gv(r0, y):
        rows = slice(r0, r0 + y.shape[0])
        gv_ref[rows, :] = (y * x1_s[rows, :]).astype(BF16)

    conv(x0_ref, w0_ref, b0_ref, store_x0)
    conv(x1_ref, w1_ref, b1_ref, store_x1)
    conv(v_ref, w2_ref, b2_ref, store_gv)


def _hyena_pre(p_main, conv_w, conv_b):
    tc = 256
    ncb = HY_W // tc
    outs = []
    for n_seq, seq_len, row_blk0 in ((BATCH, SEQ, 0), (DEC_BATCH, DEC_SEQ, N_PROMPT // DEC_SEQ)):
        tok = lambda cb: pl.BlockSpec((seq_len, tc), lambda b, j, cb=cb: (row_blk0 + b, cb // 2 + j))
        cw = lambda s: pl.BlockSpec((HY_SHORT, tc), lambda b, j, s=s: (0, s * ncb + j))
        cb_ = lambda s: pl.BlockSpec((1, tc), lambda b, j, s=s: (0, s * ncb + j))
        o_spec = pl.BlockSpec((seq_len, tc), lambda b, j: (b, j))
        o_shape = jax.ShapeDtypeStruct((n_seq * seq_len, HY_W), BF16)
        outs.append(pl.pallas_call(
            _hy_pre_kernel,
            grid=(n_seq, ncb),
            in_specs=[tok(CB_X0), tok(CB_X1), tok(CB_HV), cw(0), cw(1), cw(2), cb_(0), cb_(1), cb_(2)],
            out_specs=[o_spec, o_spec], out_shape=[o_shape, o_shape],
            scratch_shapes=[pltpu.VMEM((seq_len + 2 * CONV_PAD, tc), F32), pltpu.VMEM((seq_len, tc), F32)],
            compiler_params=_cparams(("arbitrary", "arbitrary")),
            name=f"hyena_pre_{seq_len}",
        )(p_main, p_main, p_main, conv_w, conv_w, conv_w,
          conv_b.reshape(1, -1), conv_b.reshape(1, -1), conv_b.reshape(1, -1)))
    return outs


def _hy_conv_kernel(gv_ref, x0_ref, d_ref, hre_ref, him_ref, hny_ref, cf_ref, sf_ref, ci_ref, si_ref,
                    o_ref, acc_s, *, seq_len, n_seq, kt):
    n = seq_len
    ki = pl.program_id(2)
    nk = pl.num_programs(2)
    hre, him = hre_ref[...], him_ref[...]
    cf, sf, ci, si = cf_ref[...], sf_ref[...], ci_ref[...], si_ref[...]
    freq = lax.broadcasted_iota(jnp.int32, hre.shape, 0) + ki * kt
    wk = jnp.where(freq == 0, 0.5 / n, 1.0 / n)
    quarter = freq % 4
    for b in range(n_seq):
        rows = pl.ds(b * seq_len, seq_len)
        gv = gv_ref[rows, :]
        gre = jnp.dot(cf, gv, preferred_element_type=F32)
        gim = jnp.dot(sf, gv, preferred_element_type=F32)
        yre = (gre * hre - gim * him) * wk
        yim = (gre * him + gim * hre) * wk
        are = jnp.where(quarter == 0, yre, jnp.where(quarter == 1, -yim, jnp.where(quarter == 2, -yre, yim)))
        aim = jnp.where(quarter == 0, yim, jnp.where(quarter == 1, yre, jnp.where(quarter == 2, -yim, -yre)))
        part = _bdot(ci, are) + _bdot(si, aim)

        @pl.when(ki == 0)
        def _():
            g_ny = jnp.dot(_alt_rows(n).astype(BF16), gv, preferred_element_type=F32)[0:1, :]
            y_ny = g_ny * hny_ref[0:1, :] * ((-1.0) ** (n // 2) / (2 * n))
            t = lax.broadcasted_iota(jnp.int32, part.shape, 0)
            acc_s[rows, :] = part + (1 - 2 * (t % 2)).astype(F32) * y_ny

        @pl.when(ki > 0)
        def _():
            acc_s[rows, :] += part

    @pl.when(ki == nk - 1)
    def _():
        y = acc_s[...] + gv_ref[...].astype(F32) * d_ref[...]
        o_ref[...] = (y * x0_ref[...].astype(F32)).astype(BF16)


def _hyena_conv(gv, x0c, d_skip, h_spec, tables, *, n_seq_total, seq_len, n_seq_blk, kt):
    tc = 256
    hre, him, hny = h_spec
    cos_t, nsin_t = tables
    rows = n_seq_blk * seq_len
    grid = (n_seq_total // n_seq_blk, HY_W // tc, seq_len // kt)
    tok = pl.BlockSpec((rows, tc), lambda g, j, k: (g, j))
    hs = pl.BlockSpec((kt, tc), lambda g, j, k: (k, j))
    fwd = pl.BlockSpec((kt, seq_len), lambda g, j, k: (k, 0))
    inv = pl.BlockSpec((seq_len, kt), lambda g, j, k: (0, k))
    return pl.pallas_call(
        functools.partial(_hy_conv_kernel, seq_len=seq_len, n_seq=n_seq_blk, kt=kt),
        grid=grid,
        in_specs=[tok, tok, pl.BlockSpec((1, tc), lambda g, j, k: (0, j)), hs, hs,
                  pl.BlockSpec((8, tc), lambda g, j, k: (0, j)), fwd, fwd, inv, inv],
        out_specs=tok,
        out_shape=jax.ShapeDtypeStruct((n_seq_total * seq_len, HY_W), BF16),
        scratch_shapes=[pltpu.VMEM((rows, tc), F32)],
        compiler_params=_cparams(("arbitrary", "arbitrary", "arbitrary")),
        name=f"hyena_conv_{seq_len}",
    )(gv, x0c, d_skip.reshape(1, HY_W), hre, him, hny, cos_t, nsin_t, cos_t, nsin_t)


MERGE_TM = 512


def _merge_kernel(x_ref, oap_ref, obp_ref, ocp_ref, oas_ref, obs_ref, ocs_ref, ga_ref, gb_ref, gc_ref,
                  wa_ref, wb_ref, wc_ref, wo_ref, mod_ref, y_ref, wa_s, wb_s, wc_s, wo_s):
    i = pl.program_id(0)

    @pl.when(i == 0)
    def _():
        for src, dst in ((wa_ref, wa_s), (wb_ref, wb_s), (wc_ref, wc_s), (wo_ref, wo_s)):
            dst[...] = src[...].astype(BF16)

    is_prompt = i < N_PROMPT // MERGE_TM
    pick = lambda p_ref, s_ref: jnp.where(is_prompt, p_ref[...], s_ref[...])
    sig = lambda r: jax.nn.sigmoid(r[...].astype(F32))
    merged = (sig(ga_ref) * jnp.dot(pick(oap_ref, oas_ref), wa_s[...], preferred_element_type=F32)
              + sig(gb_ref) * jnp.dot(pick(obp_ref, obs_ref), wb_s[...], preferred_element_type=F32)
              + sig(gc_ref) * jnp.dot(pick(ocp_ref, ocs_ref), wc_s[...], preferred_element_type=F32))
    out = _bdot(merged, wo_s[...])
    y_ref[...] = x_ref[...] + mod_ref[0, 2:3, :] * out


def _merge(x, branches_p, branches_s, p_main, mods, w_a, w_b, w_c, w_out, layer):
    tm = MERGE_TM
    npt = N_PROMPT // tm
    tok = lambda w: pl.BlockSpec((tm, w), lambda i: (i, 0))
    tok_p = pl.BlockSpec((tm, DN_W), lambda i: (jnp.minimum(i, npt - 1), 0))
    tok_s = pl.BlockSpec((tm, DN_W), lambda i: (jnp.maximum(i - npt, 0), 0))
    gate = lambda s: pl.BlockSpec((tm, D_MODEL), lambda i, s=s: (i, CB_GATE // 8 + s))
    wfull = lambda r: pl.BlockSpec((None, r, D_MODEL), lambda i: (layer, 0, 0))
    wscr = lambda r: pltpu.VMEM((r, D_MODEL), BF16)
    return pl.pallas_call(
        _merge_kernel,
        grid=(N_TOK // tm,),
        in_specs=[tok(D_MODEL), tok_p, tok_p, tok_p, tok_s, tok_s, tok_s, gate(0), gate(1), gate(2),
                  wfull(DN_W), wfull(DN_W), wfull(HY_W), wfull(D_MODEL),
                  pl.BlockSpec((1, 6, D_MODEL), lambda i: (_mod_row(i, tm), 0, 0))],
        out_specs=tok(D_MODEL),
        out_shape=jax.ShapeDtypeStruct((N_TOK, D_MODEL), F32),
        scratch_shapes=[wscr(DN_W), wscr(DN_W), wscr(HY_W), wscr(D_MODEL)],
        compiler_params=_cparams(("arbitrary",)),
        name="merge_out_projection",
    )(x, *branches_p, *branches_s, p_main, p_main, p_main, w_a, w_b, w_c, w_out, mods)


FFN_TM = 512
FFN_TF = D_FF // 2


def _ffn_kernel(x_ref, g_ref, mod_ref, wg_ref, wu_ref, wo_ref, fg_ref, *rest, final):
    if final:
        yp_ref, ys_ref, h_scr, acc_scr = rest
    else:
        y_ref, h_scr, acc_scr = rest
    f = pl.program_id(1)
    nf = pl.num_programs(1)

    @pl.when(f == 0)
    def _():
        g = g_ref[...]
        sh = mod_ref[0, 3:4, :]
        sc1 = 1.0 + mod_ref[0, 4:5, :]

        def body(r, carry):
            rows = pl.ds(pl.multiple_of(r * ROW_CHUNK, ROW_CHUNK), ROW_CHUNK)
            h_scr[rows, :] = (_rms(x_ref[rows, :], g) * sc1 + sh).astype(BF16)
            return carry

        lax.fori_loop(0, FFN_TM // ROW_CHUNK, body, 0)

    h = h_scr[...]
    gate = jnp.dot(h, wg_ref[...], preferred_element_type=F32)
    up = jnp.dot(h, wu_ref[...], preferred_element_type=F32)
    part = _bdot(_silu(gate) * up, wo_ref[...])

    @pl.when(f == 0)
    def _():
        acc_scr[...] = part

    @pl.when(f > 0)
    def _():
        acc_scr[...] += part

    @pl.when(f == nf - 1)
    def _():
        y = x_ref[...] + mod_ref[0, 5:6, :] * acc_scr[...]
        if final:
            y = _rms(y, fg_ref[...])
            is_prompt = pl.program_id(0) < N_PROMPT // FFN_TM

            @pl.when(is_prompt)
            def _():
                yp_ref[...] = y

            @pl.when(jnp.logical_not(is_prompt))
            def _():
                ys_ref[...] = y
        else:
            y_ref[...] = y


def _ffn(x, norm_g, mods, w_in, w_out, final_g, *, final):
    tm, tf = FFN_TM, FFN_TF
    nf = D_FF // tf
    if final:
        npt = N_PROMPT // tm
        out_specs = [pl.BlockSpec((tm, D_MODEL), lambda i, f: (jnp.minimum(i, npt - 1), 0)),
                     pl.BlockSpec((tm, D_MODEL), lambda i, f: (jnp.maximum(i - npt, 0), 0))]
        out_shape = [jax.ShapeDtypeStruct((N_PROMPT, D_MODEL), F32), jax.ShapeDtypeStruct((N_SAMPLE, D_MODEL), F32)]
    else:
        out_specs = pl.BlockSpec((tm, D_MODEL), lambda i, f: (i, 0))
        out_shape = jax.ShapeDtypeStruct((N_TOK, D_MODEL), F32)
    return pl.pallas_call(
        functools.partial(_ffn_kernel, final=final),
        grid=(N_TOK // tm, nf),
        in_specs=[pl.BlockSpec((tm, D_MODEL), lambda i, f: (i, 0)),
                  pl.BlockSpec((1, D_MODEL), lambda i, f: (0, 0)),
                  pl.BlockSpec((1, 6, D_MODEL), lambda i, f: (_mod_row(i, tm), 0, 0)),
                  pl.BlockSpec((D_MODEL, tf), lambda i, f: (0, f)),
                  pl.BlockSpec((D_MODEL, tf), lambda i, f: (0, nf + f)),
                  pl.BlockSpec((tf, D_MODEL), lambda i, f: (f, 0)),
                  pl.BlockSpec((1, D_MODEL), lambda i, f: (0, 0))],
        out_specs=out_specs,
        out_shape=out_shape,
        scratch_shapes=[pltpu.VMEM((tm, D_MODEL), BF16), pltpu.VMEM((tm, D_MODEL), F32)],
        compiler_params=_cparams(("arbitrary", "arbitrary")),
        name="ffn",
    )(x, norm_g.reshape(1, D_MODEL), mods, w_in, w_in, w_out, final_g.reshape(1, D_MODEL))


def kernel(x_prompt, x_sample, cache_k, cache_v, state_dn, c, c_ctx, norm1_g, norm2_g, w_mod, b_mod,
           w_in, dn_conv_w, dn_a_log, dn_dt_bias, dn_norm_g, da_lambda, da_subln_g, hy_conv_w,
           hy_conv_b, hy_w1, hy_b1, hy_freq, hy_w2, hy_b2, hy_w3, hy_d, w_br_a, w_br_b, w_br_c,
           w_out, w_ffn_in, w_ffn_out, final_g):
    x = jnp.concatenate([x_prompt.reshape(N_PROMPT, D_MODEL), x_sample.reshape(N_SAMPLE, D_MODEL)], axis=0)
    cond8 = jnp.concatenate([c_ctx[None, :], c, jnp.zeros((8 - 1 - DEC_BATCH, D_MODEL), F32)], axis=0)
    rope = _rope_tables(DEC_SEQ)
    tab_pf = _dft_tables(SEQ)
    tab_sf = _dft_tables(DEC_SEQ)
    tab_p = (tab_pf[0], tab_pf[2])
    tab_s = (tab_sf[0], tab_sf[2])
    emb_p, emb_s = _hy_embedding(SEQ), _hy_embedding(DEC_SEQ)
    sample_blk0 = N_PROMPT // DEC_SEQ

    new_k, new_v, new_s = [], [], []
    for l in range(DEPTH):
        mods = _modulation(cond8, w_mod, b_mod[l], l).reshape(8, 6, D_MODEL)
        w_l = w_in[l]
        w_main = jnp.concatenate([w_l[:, :3 * DN_W], w_l[:, 3 * DN_W + 4 * DN_HEADS:]], axis=1).astype(BF16)
        w_small = jnp.pad(w_l[:, 3 * DN_W:3 * DN_W + 4 * DN_HEADS],
                          ((0, 0), (0, LANES - 4 * DN_HEADS))).astype(BF16)
        p_main, p_small = _in_projection(x, norm1_g[l], mods, w_main, w_small)

        dn_args = (p_main, p_small, dn_conv_w[l], dn_a_log[l], dn_dt_bias[l], dn_norm_g[l])
        oa_p, s_fin = _deltanet(*dn_args, None, n_seq=BATCH, seq_len=SEQ, row_blk0=0, hps=4, layer=l)
        oa_s, _ = _deltanet(*dn_args, state_dn, n_seq=DEC_BATCH, seq_len=DEC_SEQ, row_blk0=sample_blk0,
                            hps=2, layer=l)

        ob_p, k_l, v_l = _diff_attention(p_main, da_lambda[l], da_subln_g[l], l, n_seq=BATCH, seq_len=SEQ,
                                         row_blk0=0, tq=SEQ)
        ob_s, _, _ = _diff_attention(p_main, da_lambda[l], da_subln_g[l], l, n_seq=DEC_BATCH,
                                     seq_len=DEC_SEQ, row_blk0=sample_blk0, tq=256, rope=rope,
                                     ctx=(cache_k, cache_v))

        hy_w = (hy_w1[l], hy_b1[l], hy_freq[l], hy_w2[l], hy_b2[l], hy_w3[l])
        hspec_p = _hyena_filter_spectrum(SEQ, emb_p, tab_pf, *hy_w)
        hspec_s = _hyena_filter_spectrum(DEC_SEQ, emb_s, tab_sf, *hy_w)
        (x0_p, gv_p), (x0_s, gv_s) = _hyena_pre(p_main, hy_conv_w[l], hy_conv_b[l])
        oc_p = _hyena_conv(gv_p, x0_p, hy_d[l], hspec_p, tab_p, n_seq_total=BATCH, seq_len=SEQ,
                           n_seq_blk=BATCH, kt=SEQ)
        oc_s = _hyena_conv(gv_s, x0_s, hy_d[l], hspec_s, tab_s, n_seq_total=DEC_BATCH, seq_len=DEC_SEQ,
                           n_seq_blk=2, kt=512)

        x = _merge(x, (oa_p, ob_p, oc_p), (oa_s, ob_s, oc_s), p_main, mods,
                   w_br_a, w_br_b, w_br_c, w_out, l)
        x = _ffn(x, norm2_g[l], mods, w_ffn_in[l].astype(BF16), w_ffn_out[l].astype(BF16), final_g,
                 final=(l == DEPTH - 1))
        new_k.append(k_l)
        new_v.append(v_l)
        new_s.append(s_fin)

    y_prompt = x[0].reshape(BATCH, SEQ, D_MODEL)
    y_sample = x[1].reshape(DEC_BATCH, DEC_SEQ, D_MODEL)
    return (y_prompt, y_sample, jnp.stack(new_k, axis=1), jnp.stack(new_v, axis=1), jnp.stack(new_s, axis=1))
```

```python
import functools
import math

import jax
import jax.numpy as jnp
from jax import lax
from jax.experimental import pallas as pl
from jax.experimental.pallas import tpu as pltpu

F32 = jnp.float32
BF16 = jnp.bfloat16

D_MODEL = 1024
BATCH = 16
SEQ = 256
DEPTH = 2
DEC_BATCH = 4
DEC_SEQ = 2048
PAST_LEN = 256
GRID_W = 64
RMS_EPS = 1e-6
DN_HEADS = 4
DN_DK = 128
DN_DV = 128
DN_W = DN_HEADS * DN_DK
DN_CONV = 5
DN_CHUNK = 64
DA_HEADS = 4
DA_DH = 64
DA_DV = 2 * DA_DH
ROPE_BASE = 10000.0
HY_W = 512
HY_SHORT = 3
HY_BANDS = 8
HY_EMB = 1 + 2 * HY_BANDS
HY_ORDER = 64
HY_FAST_DECAY = 0.3
HY_SLOW_DECAY = 1.5
HY_TARGET = 1e-2
D_FF = ((8 * D_MODEL // 3 + 255) // 256) * 256

N_PROMPT = BATCH * SEQ
N_SAMPLE = DEC_BATCH * DEC_SEQ
N_TOK = N_PROMPT + N_SAMPLE
LANES = 128
N_MAIN = 8192
CB_QA, CB_KA, CB_VA, CB_ZA = 0, 4, 8, 12
CB_QB, CB_KB, CB_VB = 16, 20, 24
CB_X0, CB_X1, CB_HV = 28, 32, 36
CB_GATE = 40
VMEM_LIMIT = 56 * 1024 * 1024


def _cparams(sem):
    return pltpu.CompilerParams(dimension_semantics=sem, vmem_limit_bytes=VMEM_LIMIT)


def _bdot(a, b):
    return jnp.dot(a.astype(BF16), b.astype(BF16), preferred_element_type=F32)


def _bdot_nt(a, b):
    return lax.dot_general(a.astype(BF16), b.astype(BF16), (((1,), (1,)), ((), ())),
                           preferred_element_type=F32)


def _bdot_tn(a, b):
    return lax.dot_general(a.astype(BF16), b.astype(BF16), (((0,), (0,)), ((), ())),
                           preferred_element_type=F32)


def _split3(x):
    hi = x.astype(BF16)
    r = x - hi.astype(F32)
    mid = r.astype(BF16)
    lo = (r - mid.astype(F32)).astype(BF16)
    return hi, mid, lo


def _dot_exact_lhs(t, x):
    hi, mid, lo = _split3(x)
    tb = t.astype(BF16)
    d = lambda p: jnp.dot(tb, p, preferred_element_type=F32)
    return d(hi) + d(mid) + d(lo)


def _dot_exact_rhs(x, e):
    hi, mid, lo = _split3(x)
    eb = e.astype(BF16)
    d = lambda p: jnp.dot(p, eb, preferred_element_type=F32)
    return d(hi) + d(mid) + d(lo)


def _dot3(a, b):
    ah = a.astype(BF16)
    al = (a - ah.astype(F32)).astype(BF16)
    bh = b.astype(BF16)
    bl = (b - bh.astype(F32)).astype(BF16)
    d = lambda p, q: jnp.dot(p, q, preferred_element_type=F32)
    return d(ah, bh) + d(ah, bl) + d(al, bh)


def _silu(x):
    return x * jax.nn.sigmoid(x)


def _rms(x, g):
    return x * lax.rsqrt(jnp.mean(x * x, axis=-1, keepdims=True) + RMS_EPS) * g


def _mod_row(i, tm):
    n_prompt_tiles = N_PROMPT // tm
    tiles_per_seq = DEC_SEQ // tm
    return jnp.where(i < n_prompt_tiles, 0, 1 + (i - n_prompt_tiles) // tiles_per_seq)


def _mod_kernel(c_ref, w_ref, b_ref, o_ref):
    o_ref[...] = _bdot(_silu(c_ref[...]), w_ref[...]) + b_ref[...]


def _modulation(cond8, w_mod, b_mod, layer):
    n = 6 * D_MODEL
    tn = 1024
    return pl.pallas_call(
        _mod_kernel,
        grid=(n // tn,),
        in_specs=[pl.BlockSpec((8, D_MODEL), lambda j: (0, 0)),
                  pl.BlockSpec((None, D_MODEL, tn), lambda j: (layer, 0, j)),
                  pl.BlockSpec((1, tn), lambda j: (0, j))],
        out_specs=pl.BlockSpec((8, tn), lambda j: (0, j)),
        out_shape=jax.ShapeDtypeStruct((8, n), F32),
        compiler_params=_cparams(("arbitrary",)),
        name="modulation",
    )(cond8, w_mod, b_mod.reshape(1, n))


IN_TM = 2048
IN_TN = 1024
ROW_CHUNK = 256


def _inproj_kernel(x_ref, g_ref, mod_ref, wm_ref, ws_ref, p_ref, ps_ref, h_scr):
    @pl.when(pl.program_id(1) == 0)
    def _():
        g = g_ref[...]
        sh = mod_ref[0, 0:1, :]
        sc1 = 1.0 + mod_ref[0, 1:2, :]

        def body(r, carry):
            rows = pl.ds(pl.multiple_of(r * ROW_CHUNK, ROW_CHUNK), ROW_CHUNK)
            hb = (_rms(x_ref[rows, :], g) * sc1 + sh).astype(BF16)
            h_scr[rows, :] = hb
            ps_ref[rows, :] = jnp.dot(hb, ws_ref[...], preferred_element_type=F32)
            return carry

        lax.fori_loop(0, IN_TM // ROW_CHUNK, body, 0)

    p_ref[...] = jnp.dot(h_scr[...], wm_ref[...], preferred_element_type=F32).astype(BF16)


def _in_projection(x, norm_g, mods, w_main, w_small):
    grid = (N_TOK // IN_TM, N_MAIN // IN_TN)
    return pl.pallas_call(
        _inproj_kernel,
        grid=grid,
        in_specs=[pl.BlockSpec((IN_TM, D_MODEL), lambda i, j: (i, 0)),
                  pl.BlockSpec((1, D_MODEL), lambda i, j: (0, 0)),
                  pl.BlockSpec((1, 6, D_MODEL), lambda i, j: (_mod_row(i, IN_TM), 0, 0)),
                  pl.BlockSpec((D_MODEL, IN_TN), lambda i, j: (0, j)),
                  pl.BlockSpec((D_MODEL, LANES), lambda i, j: (0, 0))],
        out_specs=[pl.BlockSpec((IN_TM, IN_TN), lambda i, j: (i, j)),
                   pl.BlockSpec((IN_TM, LANES), lambda i, j: (i, 0))],
        out_shape=[jax.ShapeDtypeStruct((N_TOK, N_MAIN), BF16),
                   jax.ShapeDtypeStruct((N_TOK, LANES), F32)],
        scratch_shapes=[pltpu.VMEM((IN_TM, D_MODEL), BF16)],
        compiler_params=_cparams(("arbitrary", "arbitrary")),
        name="in_projection",
    )(x, norm_g.reshape(1, D_MODEL), mods, w_main, w_small)


DN_UNIT = 2 * DN_CHUNK
DN_GROUP = 8


CONV_PAD = 8


def _zero_conv_borders(pad_ref):
    n = pad_ref.shape[0] - 2 * CONV_PAD
    zeros = jnp.zeros((CONV_PAD, pad_ref.shape[1]), F32)
    pad_ref[0:CONV_PAD, :] = zeros
    pad_ref[CONV_PAD + n:, :] = zeros


CONV_ROWS = 512


def _centred_conv(pad_ref, load, store, n, w_ref, n_taps, bias=None):
    rb = min(n, CONV_ROWS)
    half = n_taps // 2
    for r0 in range(0, n, rb):
        pad_ref[CONV_PAD + r0:CONV_PAD + r0 + rb, :] = load(r0, rb)
    for r0 in range(0, n, rb):
        acc = None
        for tap in range(n_taps):
            lo = CONV_PAD + r0 + tap - half
            term = pad_ref[lo:lo + rb, :] * w_ref[tap:tap + 1, :]
            acc = term if acc is None else acc + term
        store(r0, acc if bias is None else acc + bias)


def _l2norm(x):
    return x * lax.rsqrt(jnp.sum(x * x, axis=-1, keepdims=True) + RMS_EPS)


def _softplus(x):
    return jnp.maximum(x, 0.0) + jnp.log1p(jnp.exp(-jnp.abs(x)))


DN_BASE = 16


def _unit_tri_inverse(a_list, ri, ci):
    eye = (ri == ci).astype(F32)
    blk = lambda b: (ri // b) == (ci // b)
    y = [jnp.where(blk(DN_BASE), -a, 0.0) for a in a_list]
    p = [eye + yi for yi in y]
    for _ in range(3):
        y = [_bdot(yi, yi) for yi in y]
        p = [pi + _bdot(yi, pi) for yi, pi in zip(y, p)]
    b = DN_BASE
    while b < DN_CHUNK:
        off_mask = blk(2 * b) & ~blk(b)
        t = [_bdot(jnp.where(off_mask, a, 0.0), pi) for a, pi in zip(a_list, p)]
        p = [pi - _bdot(pi, ti) for pi, ti in zip(p, t)]
        b *= 2
    return p


def _dn_units(chains):
    u, c = DN_UNIT, DN_CHUNK
    ri = lax.broadcasted_iota(jnp.int32, (u, u), 0)
    ci = lax.broadcasted_iota(jnp.int32, (u, u), 1)
    same = (ri // c) == (ci // c)
    eye_mask = ri == ci
    hi_rows = ri >= c
    incl_of = {False: same & (ri >= ci), True: same & (ri <= ci)}
    strict_of = {False: same & (ri > ci), True: same & (ri < ci)}
    incl = [incl_of[ch['backward']] for ch in chains]
    strict = [strict_of[ch['backward']] for ch in chains]
    gc = [_dot_exact_lhs(m.astype(F32), ch['g']) for m, ch in zip(incl, chains)]
    g_tot = [(x[0:1, :], x[c:c + 1, :]) if ch['backward'] else (x[c - 1:c, :], x[u - 1:u, :])
             for x, ch in zip(gc, chains)]
    gc_row = [jnp.sum(jnp.where(eye_mask, x, 0.0), axis=0, keepdims=True) for x in gc]
    dec = [jnp.exp(jnp.where(m, x - xr, -1e30)) for m, x, xr in zip(incl, gc, gc_row)]
    e_gc = [jnp.exp(x) for x in gc]
    a = [jnp.where(m, ch['kk'] * ch['beta'] * d, 0.0) for m, ch, d in zip(strict, chains, dec)]
    qk = [ch['qk_raw'] * d for ch, d in zip(chains, dec)]
    eye = eye_mask.astype(F32)
    r = [p - eye for p in _unit_tri_inverse(a, ri, ci)]
    rhs = [jnp.concatenate([ch['v'] * ch['beta'], ch['k'] * (ch['beta'] * e)], axis=1)
           for ch, e in zip(chains, e_gc)]
    uwb = [(x + _bdot(ri_, x)).astype(BF16) for ri_, x in zip(r, rhs)]
    qkuw = [_bdot(x, y) for x, y in zip(qk, uwb)]
    kd = [ch['k'] * jnp.exp(jnp.where(hi_rows, gt[1], gt[0]) - x) for ch, gt, x in zip(chains, g_tot, gc)]
    tp = [[_bdot_tn(jnp.where(keep, x, 0.0), y) for x, y in zip(kd, uwb)]
          for keep in (~hi_rows, hi_rows)]
    out = []
    for i, ch in enumerate(chains):
        o_local = qkuw[i][:, :DN_DV]
        q_eff = ch['q'] * e_gc[i] - qkuw[i][:, DN_DV:]
        per_chunk = [(-tp[h][i][:, DN_DV:], tp[h][i][:, :DN_DV], jnp.exp(g_tot[i][h])) for h in range(2)]
        out.append((o_local, q_eff, per_chunk))
    return out


def _dn_kernel(*refs, seq_len, has_state, hps):
    if has_state:
        (q_ref, k_ref, v_ref, z_ref, ps_ref, cwq_ref, cwk_ref, cwv_ref, alog_ref, dtb_ref, ng_ref,
         s0_ref, o_ref, q_s, k_s, v_s, bt_s, g_s, oacc_s, qe_s, th_s, psi_s, egl_s, st_s, pad_s) = refs
        sfin_ref = None
    else:
        (q_ref, k_ref, v_ref, z_ref, ps_ref, cwq_ref, cwk_ref, cwv_ref, alog_ref, dtb_ref, ng_ref,
         o_ref, sfin_ref, q_s, k_s, v_s, bt_s, g_s, oacc_s, qe_s, th_s, psi_s, egl_s, st_s, pad_s) = refs
        s0_ref = None
    n_units = seq_len // DN_UNIT
    n_chunks = seq_len // DN_CHUNK
    c = DN_CHUNK
    sel_r = lax.broadcasted_iota(jnp.int32, (LANES, 4 * LANES), 0)
    sel_blk = lax.broadcasted_iota(jnp.int32, (LANES, 4 * LANES), 1) // LANES
    sels = []
    for j in range(hps):
        head = pl.program_id(1) * hps + j
        src_col = head + DN_HEADS * jnp.where(sel_blk == 0, 0, jnp.where(sel_blk == 1, 2, jnp.where(sel_blk == 2, 1, 3)))
        sels.append((sel_r == src_col).astype(BF16))

    def gate_rows(r, carry):
        rows = pl.ds(pl.multiple_of(r * ROW_CHUNK, ROW_CHUNK), ROW_CHUNK)
        ps = ps_ref[rows, :]
        lane = lax.broadcasted_iota(jnp.int32, ps.shape, 1)
        gate_cols = jnp.where(lane < 2 * DN_HEADS, jax.nn.sigmoid(ps),
                              -(jnp.exp(alog_ref[...]) * _softplus(ps + dtb_ref[...])))
        parts = _split3(gate_cols)
        for j in range(hps):
            ext = functools.reduce(lambda a, b: a + b,
                                   [jnp.dot(part, sels[j], preferred_element_type=F32) for part in parts])
            for d in range(2):
                bt_s[j, d, rows, :] = ext[:, (2 * d) * LANES:(2 * d + 1) * LANES]
                g_s[j, d, rows, :] = ext[:, (2 * d + 1) * LANES:(2 * d + 2) * LANES]
        return carry

    lax.fori_loop(0, seq_len // ROW_CHUNK, gate_rows, 0)

    _zero_conv_borders(pad_s)
    for j in range(hps):
        cols = slice(j * LANES, (j + 1) * LANES)
        for x_ref, w_ref, out_s, post in (
                (q_ref, cwq_ref, q_s, lambda y: _l2norm(_silu(y)) * (DN_DK ** -0.5)),
                (k_ref, cwk_ref, k_s, lambda y: _l2norm(_silu(y))),
                (v_ref, cwv_ref, v_s, _silu)):
            def store(r0, y, out_s=out_s, post=post):
                out_s[j, r0:r0 + y.shape[0], :] = post(y)
            _centred_conv(pad_s, lambda r0, rb, x_ref=x_ref: x_ref[r0:r0 + rb, cols].astype(F32), store,
                          seq_len, w_ref.at[:, cols], DN_CONV)
        for d in range(2):
            st_s[j, d] = s0_ref[0, 0, d, j] if has_state else jnp.zeros((DN_DK, DN_DV), F32)

    def unit_group(grp, carry):
        where, chains = [], []
        for t in range(DN_GROUP):
            idx = grp * DN_GROUP + t
            j = idx // n_units
            n = idx % n_units
            rows = pl.ds(pl.multiple_of(n * DN_UNIT, DN_UNIT), DN_UNIT)
            qc, kc, vc = q_s[j, rows, :], k_s[j, rows, :], v_s[j, rows, :]
            kcb = kc.astype(BF16)
            kk = _bdot_nt(kcb, kcb)
            qk_raw = _bdot_nt(qc, kcb)
            for d in range(2):
                where.append((j, n, rows, d))
                chains.append(dict(q=qc, k=kc, v=vc, kk=kk, qk_raw=qk_raw, beta=bt_s[j, d, rows, :],
                                   g=g_s[j, d, rows, :], backward=(d == 1)))
        results = _dn_units(chains)
        for (j, n, rows, d), (o_loc, q_eff, per_chunk) in zip(where, results):
            qe_s[j, d, rows, :] = q_eff.astype(BF16)
            for half, (theta, psi, egl) in enumerate(per_chunk):
                th_s[j, d, 2 * n + half] = theta.astype(BF16)
                psi_s[j, d, 2 * n + half] = psi
                egl_s[j, d, 2 * n + half] = egl
            if d == 0:
                o_fwd = o_loc
            else:
                oacc_s[j, rows, :] = o_fwd + o_loc
        return carry

    lax.fori_loop(0, hps * n_units // DN_GROUP, unit_group, 0)

    def scan_step(i, carry):
        for j in range(hps):
            for d, n in ((0, i), (1, n_chunks - 1 - i)):
                rows = pl.ds(pl.multiple_of(n * c, c), c)
                s = st_s[j, d]
                sb = s.astype(BF16)
                oacc_s[j, rows, :] += jnp.dot(qe_s[j, d, rows, :], sb, preferred_element_type=F32)
                st_s[j, d] = (egl_s[j, d, n] * s + jnp.dot(th_s[j, d, n], sb, preferred_element_type=F32)
                              + psi_s[j, d, n])
        return carry

    lax.fori_loop(0, n_chunks, scan_step, 0)

    if sfin_ref is not None:
        for j in range(hps):
            for d in range(2):
                sfin_ref[0, d, j] = st_s[j, d]

    def out_rows(r, carry):
        rows = pl.ds(pl.multiple_of(r * ROW_CHUNK, ROW_CHUNK), ROW_CHUNK)
        for j in range(hps):
            cols = slice(j * LANES, (j + 1) * LANES)
            o = _rms(oacc_s[j, rows, :], ng_ref[...]) * _silu(z_ref[rows, cols].astype(F32))
            o_ref[rows, cols] = o.astype(BF16)
        return carry

    lax.fori_loop(0, seq_len // ROW_CHUNK, out_rows, 0)


def _deltanet(p_main, p_small, conv_w, a_log, dt_bias, norm_g, s0, *, n_seq, seq_len, row_blk0, hps, layer):
    has_state = s0 is not None
    n_chunks = seq_len // DN_CHUNK
    width = hps * LANES
    tok = lambda cb: pl.BlockSpec((seq_len, width), lambda b, h: (row_blk0 + b, cb // hps + h))
    cw = lambda cb: pl.BlockSpec((DN_CONV, width), lambda b, h: (0, cb // hps + h))
    in_specs = [tok(CB_QA), tok(CB_KA), tok(CB_VA), tok(CB_ZA),
                pl.BlockSpec((seq_len, LANES), lambda b, h: (row_blk0 + b, 0)),
                cw(0), cw(DN_HEADS), cw(2 * DN_HEADS),
                pl.BlockSpec((1, LANES), lambda b, h: (0, 0)),
                pl.BlockSpec((1, LANES), lambda b, h: (0, 0)),
                pl.BlockSpec((1, DN_DV), lambda b, h: (0, 0))]
    pad8 = lambda t: jnp.pad(t.reshape(1, 2 * DN_HEADS), ((0, 0), (2 * DN_HEADS, LANES - 4 * DN_HEADS)))
    args = [p_main, p_main, p_main, p_main, p_small, conv_w, conv_w, conv_w,
            pad8(a_log), pad8(dt_bias), norm_g.reshape(1, DN_DV)]
    o_spec = pl.BlockSpec((seq_len, width), lambda b, h: (b, h))
    o_shape = jax.ShapeDtypeStruct((n_seq * seq_len, DN_W), BF16)
    if has_state:
        in_specs.append(pl.BlockSpec((1, 1, 2, hps, DN_DK, DN_DV), lambda b, h: (b, layer, 0, h, 0, 0)))
        args.append(s0)
        out_specs, out_shape = o_spec, o_shape
    else:
        out_specs = [o_spec, pl.BlockSpec((1, 2, hps, DN_DK, DN_DV), lambda b, h: (b, 0, h, 0, 0))]
        out_shape = [o_shape, jax.ShapeDtypeStruct((n_seq, 2, DN_HEADS, DN_DK, DN_DV), F32)]
    scratch = [pltpu.VMEM((hps, seq_len, LANES), F32),
               pltpu.VMEM((hps, seq_len, LANES), F32),
               pltpu.VMEM((hps, seq_len, LANES), F32),
               pltpu.VMEM((hps, 2, seq_len, LANES), F32),
               pltpu.VMEM((hps, 2, seq_len, LANES), F32),
               pltpu.VMEM((hps, seq_len, LANES), F32),
               pltpu.VMEM((hps, 2, seq_len, LANES), BF16),
               pltpu.VMEM((hps, 2, n_chunks, DN_DK, DN_DK), BF16),
               pltpu.VMEM((hps, 2, n_chunks, DN_DK, DN_DV), F32),
               pltpu.VMEM((hps, 2, n_chunks, 1, LANES), F32),
               pltpu.VMEM((hps, 2, DN_DK, DN_DV), F32),
               pltpu.VMEM((seq_len + 2 * CONV_PAD, LANES), F32)]
    res = pl.pallas_call(
        functools.partial(_dn_kernel, seq_len=seq_len, has_state=has_state, hps=hps),
        grid=(n_seq, DN_HEADS // hps),
        in_specs=in_specs, out_specs=out_specs, out_shape=out_shape,
        scratch_shapes=scratch,
        compiler_params=_cparams(("arbitrary", "arbitrary")),
        name=f"deltanet_{seq_len}",
    )(*args)
    return (res, None) if has_state else (res[0], res[1])


def _rope(x, cos, sin_signed):
    lane = lax.broadcasted_iota(jnp.int32, x.shape, 1)
    partner = jnp.where((lane % 32) < 16, pltpu.roll(x, LANES - 16, axis=1), pltpu.roll(x, 16, axis=1))
    return x * cos + partner * sin_signed


DA_ROWS = 128


def _exp2_rows(s_parts):
    m = functools.reduce(jnp.maximum, [jnp.max(s, axis=-1, keepdims=True) for s in s_parts])
    return [jnp.exp2(s - m).astype(BF16) for s in s_parts]


def _da_kernel(*refs, layer, latent):
    if latent:
        (q_ref, k_ref, v_ref, lam_ref, sg_ref, cq_ref, sq_ref, ck_ref, sk_ref, ctxk_ref, ctxv_ref,
         o_ref, krot_s, vext_s) = refs
    else:
        q_ref, k_ref, v_ref, lam_ref, sg_ref, o_ref, ko_ref, vo_ref = refs
    with_ones = lambda v: jnp.concatenate([v, jnp.ones(v.shape, BF16)], axis=1)
    lam_init = 0.8 - 0.6 * math.exp(-0.3 * layer)
    lp = lam_ref[...]
    dots = jnp.sum(jnp.concatenate([lp[0:1] * lp[1:2], lp[2:3] * lp[3:4]], axis=0), axis=1, keepdims=True)
    e = jnp.exp(dots)
    lam = e[0:1, :] - e[1:2, :] + lam_init

    q = q_ref[...].astype(F32)
    if latent:
        @pl.when(pl.program_id(2) == 0)
        def _():
            krot_s[...] = _rope(k_ref[...].astype(F32), ck_ref[...], sk_ref[...]).astype(BF16)
            vext_s[...] = with_ones(v_ref[...])
        q = _rope(q, cq_ref[...], sq_ref[...])
        keys = [ctxk_ref[0, 0, 0].astype(BF16), krot_s[...]]
        vals = [with_ones(ctxv_ref[0, 0, 0].astype(BF16)), vext_s[...]]
    else:
        keys = [k_ref[...]]
        vals = [with_ones(v_ref[...])]
        ko_ref[0, 0] = k_ref[...].astype(F32)
        vo_ref[0, 0] = v_ref[...].astype(F32)
    q = q * (DA_DH ** -0.5 * math.log2(math.e))
    tq = q.shape[0]
    lane = lax.broadcasted_iota(jnp.int32, q.shape, 1)
    q12 = jnp.concatenate([jnp.where(lane < DA_DH, q, 0.0), jnp.where(lane >= DA_DH, q, 0.0)], axis=0).astype(BF16)
    groups = [q12[r:r + DA_ROWS] for r in range(0, 2 * tq, DA_ROWS)]
    scores = [[_bdot_nt(qg, kk) for kk in keys] for qg in groups]
    soft = [_exp2_rows(sg) for sg in scores]
    pv = []
    for e_parts in soft:
        acc = None
        for e, vv in zip(e_parts, vals):
            part = _bdot(e, vv)
            acc = part if acc is None else acc + part
        pv.append(acc[:, :DA_DV] * (1.0 / acc[:, DA_DV:]))
    pv = jnp.concatenate(pv, axis=0)
    o = pv[:tq] - lam * pv[tq:]
    o_ref[...] = (_rms(o, sg_ref[...]) * (1.0 - lam_init)).astype(BF16)


def _diff_attention(p_main, lam_p, subln_g, layer, *, n_seq, seq_len, row_blk0, tq, rope=None, ctx=None):
    latent = ctx is not None
    nq = seq_len // tq
    qpb = seq_len // tq
    in_specs = [pl.BlockSpec((tq, LANES), lambda b, h, i: ((row_blk0 + b) * qpb + i, CB_QB + h)),
                pl.BlockSpec((seq_len, LANES), lambda b, h, i: (row_blk0 + b, CB_KB + h)),
                pl.BlockSpec((seq_len, LANES), lambda b, h, i: (row_blk0 + b, CB_VB + h)),
                pl.BlockSpec((4, LANES), lambda b, h, i: (0, 0)),
                pl.BlockSpec((1, DA_DV), lambda b, h, i: (0, 0))]
    args = [p_main, p_main, p_main, jnp.pad(lam_p, ((0, 0), (0, LANES - DA_DH))), subln_g.reshape(1, DA_DV)]
    o_spec = pl.BlockSpec((tq, LANES), lambda b, h, i: (b * qpb + i, h))
    o_shape = jax.ShapeDtypeStruct((n_seq * seq_len, DA_HEADS * DA_DV), BF16)
    scratch = []
    if latent:
        cos, sin_signed = rope
        ctx_k, ctx_v = ctx
        n_ctx = ctx_k.shape[3]
        in_specs += [pl.BlockSpec((tq, LANES), lambda b, h, i: (i, 0)),
                     pl.BlockSpec((tq, LANES), lambda b, h, i: (i, 0)),
                     pl.BlockSpec((seq_len, LANES), lambda b, h, i: (0, 0)),
                     pl.BlockSpec((seq_len, LANES), lambda b, h, i: (0, 0)),
                     pl.BlockSpec((1, 1, 1, n_ctx, DA_DV), lambda b, h, i: (b, layer, h, 0, 0)),
                     pl.BlockSpec((1, 1, 1, n_ctx, DA_DV), lambda b, h, i: (b, layer, h, 0, 0))]
        args += [cos, sin_signed, cos, sin_signed, ctx_k, ctx_v]
        out_specs, out_shape = o_spec, o_shape
        scratch = [pltpu.VMEM((seq_len, LANES), BF16), pltpu.VMEM((seq_len, 2 * DA_DV), BF16)]
    else:
        kv_spec = pl.BlockSpec((1, 1, seq_len, DA_DV), lambda b, h, i: (b, h, 0, 0))
        kv_shape = jax.ShapeDtypeStruct((n_seq, DA_HEADS, seq_len, DA_DV), F32)
        out_specs = [o_spec, kv_spec, kv_spec]
        out_shape = [o_shape, kv_shape, kv_shape]
    res = pl.pallas_call(
        functools.partial(_da_kernel, layer=layer, latent=latent),
        grid=(n_seq, DA_HEADS, nq),
        in_specs=in_specs, out_specs=out_specs, out_shape=out_shape,
        scratch_shapes=scratch,
        compiler_params=_cparams(("arbitrary", "arbitrary", "arbitrary")),
        name=f"diff_attention_{seq_len}",
    )(*args)
    return (res, None, None) if latent else tuple(res)


def _rope_tables(n_tok):
    half = DA_DH // 2
    inv = ROPE_BASE ** (-jnp.arange(0, half, 2, dtype=F32) / half)
    t = jnp.arange(n_tok)
    ang_r = (t // GRID_W).astype(F32)[:, None] * inv
    ang_c = (t % GRID_W).astype(F32)[:, None] * inv
    cos32 = lambda a: jnp.concatenate([jnp.cos(a), jnp.cos(a)], axis=-1)
    sin32 = lambda a: jnp.concatenate([-jnp.sin(a), jnp.sin(a)], axis=-1)
    cos = jnp.concatenate([cos32(ang_r), cos32(ang_c)] * 2, axis=-1)
    sin_signed = jnp.concatenate([sin32(ang_r), sin32(ang_c)] * 2, axis=-1)
    return cos, sin_signed


def _dft_tables(n):
    t_lo = 64
    k = jnp.arange(n, dtype=jnp.int32)[:, None]
    ang = lambda m: ((k * m[None, :]) % (2 * n)).astype(F32) * (math.pi / n)
    a = ang(t_lo * jnp.arange(n // t_lo, dtype=jnp.int32))
    b = ang(jnp.arange(t_lo, dtype=jnp.int32))
    ca, sa, cb, sb = jnp.cos(a)[:, :, None], jnp.sin(a)[:, :, None], jnp.cos(b)[:, None, :], jnp.sin(b)[:, None, :]
    cos_t = (ca * cb - sa * sb).reshape(n, n)
    nsin_t = -(sa * cb + ca * sb).reshape(n, n)
    return cos_t.astype(BF16), nsin_t.astype(BF16)


def _hy_embedding(n):
    j = jnp.arange(n, dtype=F32)
    t = j / (n - 1)
    ang = (2.0 * math.pi * j / n)[:, None] * jnp.linspace(1e-4, HY_BANDS - 1, HY_BANDS, dtype=F32)
    z = jnp.concatenate([t[:, None], jnp.cos(ang), -jnp.sin(ang)], axis=-1)
    half = n // 2
    dist = jnp.abs(j - half) / half
    max_decay = math.log(HY_TARGET) / HY_FAST_DECAY
    min_decay = math.log(HY_TARGET) / HY_SLOW_DECAY
    deltas = jnp.abs(jnp.linspace(min_decay, max_decay, HY_W, dtype=F32))
    return jnp.pad(z, ((0, 0), (0, LANES - HY_EMB))), dist[:, None], deltas[None, :]


def _alt_rows(n):
    t = lax.broadcasted_iota(jnp.int32, (8, n), 1)
    return (1 - 2 * (t % 2)).astype(F32)


def _hy_filter_kernel(z_ref, dist_ref, delta_ref, w1_ref, b1_ref, fr_ref, w2_ref, b2_ref, w3_ref,
                      cos_ref, nsin_ref, hre_ref, him_ref, hny_ref, h_s):
    @pl.when(pl.program_id(1) == 0)
    def _():
        fr = fr_ref[...]
        hdn = jnp.sin(fr * (_dot3(z_ref[...], w1_ref[...]) + b1_ref[...]))
        hdn = jnp.sin(fr * (_dot3(hdn, w2_ref[...]) + b2_ref[...]))
        h = _dot3(hdn, w3_ref[...])
        h = h * jnp.exp(-dist_ref[...] * delta_ref[...])
        h = h / jnp.sum(jnp.abs(h), axis=0, keepdims=True)
        h_s[...] = h.astype(BF16)
        hny_ref[...] = _dot_exact_lhs(_alt_rows(h.shape[0]), h)

    hre_ref[...] = jnp.dot(cos_ref[...], h_s[...], preferred_element_type=F32)
    him_ref[...] = jnp.dot(nsin_ref[...], h_s[...], preferred_element_type=F32)


def _hyena_filter_spectrum(n, emb, tables, w1, b1, freq, w2, b2, w3):
    z, dist, deltas = emb
    pad_o = LANES - HY_ORDER
    w1p = jnp.pad(w1, ((0, LANES - HY_EMB), (0, pad_o)))
    w2p = jnp.pad(w2, ((0, pad_o), (0, pad_o)))
    w3p = jnp.pad(w3, ((0, pad_o), (0, 0)))
    row = lambda t: jnp.pad(t.reshape(1, HY_ORDER), ((0, 0), (0, pad_o)))
    tc = 256
    kt = min(n, 512)
    full = lambda shape: pl.BlockSpec(shape, lambda j, k: (0, 0))
    tab = pl.BlockSpec((kt, n), lambda j, k: (k, 0))
    spec = pl.BlockSpec((kt, tc), lambda j, k: (k, j))
    return pl.pallas_call(
        _hy_filter_kernel,
        grid=(HY_W // tc, n // kt),
        in_specs=[full((n, LANES)), full((n, 1)), pl.BlockSpec((1, tc), lambda j, k: (0, j)),
                  full((LANES, LANES)), full((1, LANES)), full((1, LANES)),
                  full((LANES, LANES)), full((1, LANES)), pl.BlockSpec((LANES, tc), lambda j, k: (0, j)),
                  tab, tab],
        out_specs=[spec, spec, pl.BlockSpec((8, tc), lambda j, k: (0, j))],
        out_shape=[jax.ShapeDtypeStruct((n, HY_W), F32), jax.ShapeDtypeStruct((n, HY_W), F32),
                   jax.ShapeDtypeStruct((8, HY_W), F32)],
        scratch_shapes=[pltpu.VMEM((n, tc), BF16)],
        compiler_params=_cparams(("arbitrary", "arbitrary")),
        name=f"hyena_filter_{n}",
    )(z, dist, deltas, w1p, row(b1), row(freq), w2p, row(b2), w3p, *tables)


def _hy_pre_kernel(x0_ref, x1_ref, v_ref, w0_ref, w1_ref, w2_ref, b0_ref, b1_ref, b2_ref, x0c_ref, gv_ref,
                   pad_s, x1_s, *, seq_len):
    _zero_conv_borders(pad_s)
    for base in range(0, x0_ref.shape[0], seq_len):
        def conv(x_ref, w_ref, b_ref, store):
            _centred_conv(pad_s, lambda r0, rb: x_ref[base + r0:base + r0 + rb, :].astype(F32), store, seq_len,
                          w_ref, HY_SHORT, b_ref[...])

        def store_x0(r0, y):
            x0c_ref[base + r0:base + r0 + y.shape[0], :] = y.astype(BF16)

        def store_x1(r0, y):
            x1_s[r0:r0 + y.shape[0], :] = y

        def store_gv(r0, y):
            gv_ref[base + r0:base + r0 + y.shape[0], :] = (y * x1_s[r0:r0 + y.shape[0], :]).astype(BF16)

        conv(x0_ref, w0_ref, b0_ref, store_x0)
        conv(x1_ref, w1_ref, b1_ref, store_x1)
        conv(v_ref, w2_ref, b2_ref, store_gv)


def _hyena_pre(p_main, conv_w, conv_b):
    outs = []
    for n_seq, seq_len, row_blk0, spb, tc in ((BATCH, SEQ, 0, 4, HY_W),
                                               (DEC_BATCH, DEC_SEQ, N_PROMPT // DEC_SEQ, 1, HY_W // 2)):
        ncb = HY_W // tc
        rows = spb * seq_len
        tok = lambda cb: pl.BlockSpec((rows, tc), lambda b, j, cb=cb: (row_blk0 + b, cb * LANES // tc + j))
        cw = lambda s: pl.BlockSpec((HY_SHORT, tc), lambda b, j, s=s: (0, s * ncb + j))
        cb_ = lambda s: pl.BlockSpec((1, tc), lambda b, j, s=s: (0, s * ncb + j))
        o_spec = pl.BlockSpec((rows, tc), lambda b, j: (b, j))
        o_shape = jax.ShapeDtypeStruct((n_seq * seq_len, HY_W), BF16)
        outs.append(pl.pallas_call(
            functools.partial(_hy_pre_kernel, seq_len=seq_len),
            grid=(n_seq // spb, ncb),
            in_specs=[tok(CB_X0), tok(CB_X1), tok(CB_HV), cw(0), cw(1), cw(2), cb_(0), cb_(1), cb_(2)],
            out_specs=[o_spec, o_spec], out_shape=[o_shape, o_shape],
            scratch_shapes=[pltpu.VMEM((seq_len + 2 * CONV_PAD, tc), F32), pltpu.VMEM((seq_len, tc), F32)],
            compiler_params=_cparams(("arbitrary", "arbitrary")),
            name=f"hyena_pre_{seq_len}",
        )(p_main, p_main, p_main, conv_w, conv_w, conv_w,
          conv_b.reshape(1, -1), conv_b.reshape(1, -1), conv_b.reshape(1, -1)))
    return outs


def _hy_conv_kernel(gv_ref, x0_ref, d_ref, hre_ref, him_ref, hny_ref, cf_ref, sf_ref, ci_ref, si_ref,
                    o_ref, acc_s, *, seq_len, n_seq, kt):
    n = seq_len
    ki = pl.program_id(2)
    nk = pl.num_programs(2)
    hre, him = hre_ref[...], him_ref[...]
    cf, sf, ci, si = cf_ref[...], sf_ref[...], ci_ref[...], si_ref[...]
    freq = lax.broadcasted_iota(jnp.int32, hre.shape, 0) + ki * kt
    wk = jnp.where(freq == 0, 0.5 / n, 1.0 / n)
    quarter = freq % 4
    for b in range(n_seq):
        rows = pl.ds(b * seq_len, seq_len)
        gv = gv_ref[rows, :]
        gre = jnp.dot(cf, gv, preferred_element_type=F32)
        gim = jnp.dot(sf, gv, preferred_element_type=F32)
        yre = (gre * hre - gim * him) * wk
        yim = (gre * him + gim * hre) * wk
        are = jnp.where(quarter == 0, yre, jnp.where(quarter == 1, -yim, jnp.where(quarter == 2, -yre, yim)))
        aim = jnp.where(quarter == 0, yim, jnp.where(quarter == 1, yre, jnp.where(quarter == 2, -yim, -yre)))
        part = _bdot(ci, are) + _bdot(si, aim)

        @pl.when(ki == 0)
        def _():
            g_ny = jnp.dot(_alt_rows(n).astype(BF16), gv, preferred_element_type=F32)[0:1, :]
            y_ny = g_ny * hny_ref[0:1, :] * ((-1.0) ** (n // 2) / (2 * n))
            t = lax.broadcasted_iota(jnp.int32, part.shape, 0)
            acc_s[rows, :] = part + (1 - 2 * (t % 2)).astype(F32) * y_ny

        @pl.when(ki > 0)
        def _():
            acc_s[rows, :] += part

    @pl.when(ki == nk - 1)
    def _():
        y = acc_s[...] + gv_ref[...].astype(F32) * d_ref[...]
        o_ref[...] = (y * x0_ref[...].astype(F32)).astype(BF16)


def _hyena_conv(gv, x0c, d_skip, h_spec, tables, *, n_seq_total, seq_len, n_seq_blk, kt):
    tc = 256
    hre, him, hny = h_spec
    cos_t, nsin_t = tables
    rows = n_seq_blk * seq_len
    grid = (n_seq_total // n_seq_blk, HY_W // tc, seq_len // kt)
    tok = pl.BlockSpec((rows, tc), lambda g, j, k: (g, j))
    hs = pl.BlockSpec((kt, tc), lambda g, j, k: (k, j))
    fwd = pl.BlockSpec((kt, seq_len), lambda g, j, k: (k, 0))
    inv = pl.BlockSpec((seq_len, kt), lambda g, j, k: (0, k))
    return pl.pallas_call(
        functools.partial(_hy_conv_kernel, seq_len=seq_len, n_seq=n_seq_blk, kt=kt),
        grid=grid,
        in_specs=[tok, tok, pl.BlockSpec((1, tc), lambda g, j, k: (0, j)), hs, hs,
                  pl.BlockSpec((8, tc), lambda g, j, k: (0, j)), fwd, fwd, inv, inv],
        out_specs=tok,
        out_shape=jax.ShapeDtypeStruct((n_seq_total * seq_len, HY_W), BF16),
        scratch_shapes=[pltpu.VMEM((rows, tc), F32)],
        compiler_params=_cparams(("arbitrary", "arbitrary", "arbitrary")),
        name=f"hyena_conv_{seq_len}",
    )(gv, x0c, d_skip.reshape(1, HY_W), hre, him, hny, cos_t, nsin_t, cos_t, nsin_t)


MERGE_TM = 512


def _merge_kernel(x_ref, oap_ref, obp_ref, ocp_ref, oas_ref, obs_ref, ocs_ref, ga_ref, gb_ref, gc_ref,
                  wa_ref, wb_ref, wc_ref, wo_ref, mod_ref, y_ref, wa_s, wb_s, wc_s, wo_s):
    i = pl.program_id(0)

    @pl.when(i == 0)
    def _():
        for src, dst in ((wa_ref, wa_s), (wb_ref, wb_s), (wc_ref, wc_s), (wo_ref, wo_s)):
            dst[...] = src[...].astype(BF16)

    is_prompt = i < N_PROMPT // MERGE_TM
    pick = lambda p_ref, s_ref: jnp.where(is_prompt, p_ref[...], s_ref[...])
    sig = lambda r: jax.nn.sigmoid(r[...].astype(F32))
    merged = (sig(ga_ref) * jnp.dot(pick(oap_ref, oas_ref), wa_s[...], preferred_element_type=F32)
              + sig(gb_ref) * jnp.dot(pick(obp_ref, obs_ref), wb_s[...], preferred_element_type=F32)
              + sig(gc_ref) * jnp.dot(pick(ocp_ref, ocs_ref), wc_s[...], preferred_element_type=F32))
    out = _bdot(merged, wo_s[...])
    y_ref[...] = x_ref[...] + mod_ref[0, 2:3, :] * out


def _merge(x, branches_p, branches_s, p_main, mods, w_a, w_b, w_c, w_out, layer):
    tm = MERGE_TM
    npt = N_PROMPT // tm
    tok = lambda w: pl.BlockSpec((tm, w), lambda i: (i, 0))
    tok_p = pl.BlockSpec((tm, DN_W), lambda i: (jnp.minimum(i, npt - 1), 0))
    tok_s = pl.BlockSpec((tm, DN_W), lambda i: (jnp.maximum(i - npt, 0), 0))
    gate = lambda s: pl.BlockSpec((tm, D_MODEL), lambda i, s=s: (i, CB_GATE // 8 + s))
    wfull = lambda r: pl.BlockSpec((None, r, D_MODEL), lambda i: (layer, 0, 0))
    wscr = lambda r: pltpu.VMEM((r, D_MODEL), BF16)
    return pl.pallas_call(
        _merge_kernel,
        grid=(N_TOK // tm,),
        in_specs=[tok(D_MODEL), tok_p, tok_p, tok_p, tok_s, tok_s, tok_s, gate(0), gate(1), gate(2),
                  wfull(DN_W), wfull(DN_W), wfull(HY_W), wfull(D_MODEL),
                  pl.BlockSpec((1, 6, D_MODEL), lambda i: (_mod_row(i, tm), 0, 0))],
        out_specs=tok(D_MODEL),
        out_shape=jax.ShapeDtypeStruct((N_TOK, D_MODEL), F32),
        scratch_shapes=[wscr(DN_W), wscr(DN_W), wscr(HY_W), wscr(D_MODEL)],
        compiler_params=_cparams(("arbitrary",)),
        name="merge_out_projection",
    )(x, *branches_p, *branches_s, p_main, p_main, p_main, w_a, w_b, w_c, w_out, mods)


FFN_TM = 512
FFN_TF = D_FF // 2


def _ffn_kernel(x_ref, g_ref, mod_ref, wg_ref, wu_ref, wo_ref, fg_ref, *rest, final):
    if final:
        yp_ref, ys_ref, h_scr, acc_scr = rest
    else:
        y_ref, h_scr, acc_scr = rest
    f = pl.program_id(1)
    nf = pl.num_programs(1)

    @pl.when(f == 0)
    def _():
        g = g_ref[...]
        sh = mod_ref[0, 3:4, :]
        sc1 = 1.0 + mod_ref[0, 4:5, :]

        def body(r, carry):
            rows = pl.ds(pl.multiple_of(r * ROW_CHUNK, ROW_CHUNK), ROW_CHUNK)
            h_scr[rows, :] = (_rms(x_ref[rows, :], g) * sc1 + sh).astype(BF16)
            return carry

        lax.fori_loop(0, FFN_TM // ROW_CHUNK, body, 0)

    h = h_scr[...]
    gate = jnp.dot(h, wg_ref[...], preferred_element_type=F32)
    up = jnp.dot(h, wu_ref[...], preferred_element_type=F32)
    part = _bdot(_silu(gate) * up, wo_ref[...])

    @pl.when(f == 0)
    def _():
        acc_scr[...] = part

    @pl.when(f > 0)
    def _():
        acc_scr[...] += part

    @pl.when(f == nf - 1)
    def _():
        y = x_ref[...] + mod_ref[0, 5:6, :] * acc_scr[...]
        if final:
            y = _rms(y, fg_ref[...])
            is_prompt = pl.program_id(0) < N_PROMPT // FFN_TM

            @pl.when(is_prompt)
            def _():
                yp_ref[...] = y

            @pl.when(jnp.logical_not(is_prompt))
            def _():
                ys_ref[...] = y
        else:
            y_ref[...] = y


def _ffn(x, norm_g, mods, w_in, w_out, final_g, *, final):
    tm, tf = FFN_TM, FFN_TF
    nf = D_FF // tf
    if final:
        npt = N_PROMPT // tm
        out_specs = [pl.BlockSpec((tm, D_MODEL), lambda i, f: (jnp.minimum(i, npt - 1), 0)),
                     pl.BlockSpec((tm, D_MODEL), lambda i, f: (jnp.maximum(i - npt, 0), 0))]
        out_shape = [jax.ShapeDtypeStruct((N_PROMPT, D_MODEL), F32), jax.ShapeDtypeStruct((N_SAMPLE, D_MODEL), F32)]
    else:
        out_specs = pl.BlockSpec((tm, D_MODEL), lambda i, f: (i, 0))
        out_shape = jax.ShapeDtypeStruct((N_TOK, D_MODEL), F32)
    return pl.pallas_call(
        functools.partial(_ffn_kernel, final=final),
        grid=(N_TOK // tm, nf),
        in_specs=[pl.BlockSpec((tm, D_MODEL), lambda i, f: (i, 0)),
                  pl.BlockSpec((1, D_MODEL), lambda i, f: (0, 0)),
                  pl.BlockSpec((1, 6, D_MODEL), lambda i, f: (_mod_row(i, tm), 0, 0)),
                  pl.BlockSpec((D_MODEL, tf), lambda i, f: (0, f)),
                  pl.BlockSpec((D_MODEL, tf), lambda i, f: (0, nf + f)),
                  pl.BlockSpec((tf, D_MODEL), lambda i, f: (f, 0)),
                  pl.BlockSpec((1, D_MODEL), lambda i, f: (0, 0))],
        out_specs=out_specs,
        out_shape=out_shape,
        scratch_shapes=[pltpu.VMEM((tm, D_MODEL), BF16), pltpu.VMEM((tm, D_MODEL), F32)],
        compiler_params=_cparams(("arbitrary", "arbitrary")),
        name="ffn",
    )(x, norm_g.reshape(1, D_MODEL), mods, w_in, w_in, w_out, final_g.reshape(1, D_MODEL))


def kernel(x_prompt, x_sample, cache_k, cache_v, state_dn, c, c_ctx, norm1_g, norm2_g, w_mod, b_mod,
           w_in, dn_conv_w, dn_a_log, dn_dt_bias, dn_norm_g, da_lambda, da_subln_g, hy_conv_w,
           hy_conv_b, hy_w1, hy_b1, hy_freq, hy_w2, hy_b2, hy_w3, hy_d, w_br_a, w_br_b, w_br_c,
           w_out, w_ffn_in, w_ffn_out, final_g):
    x = jnp.concatenate([x_prompt.reshape(N_PROMPT, D_MODEL), x_sample.reshape(N_SAMPLE, D_MODEL)], axis=0)
    cond8 = jnp.concatenate([c_ctx[None, :], c, jnp.zeros((8 - 1 - DEC_BATCH, D_MODEL), F32)], axis=0)
    rope = _rope_tables(DEC_SEQ)
    tab_p = _dft_tables(SEQ)
    tab_s = _dft_tables(DEC_SEQ)
    emb_p, emb_s = _hy_embedding(SEQ), _hy_embedding(DEC_SEQ)
    sample_blk0 = N_PROMPT // DEC_SEQ

    new_k, new_v, new_s = [], [], []
    for l in range(DEPTH):
        mods = _modulation(cond8, w_mod, b_mod[l], l).reshape(8, 6, D_MODEL)
        w_l = w_in[l]
        w_main = jnp.concatenate([w_l[:, :3 * DN_W], w_l[:, 3 * DN_W + 4 * DN_HEADS:]], axis=1).astype(BF16)
        w_small = jnp.pad(w_l[:, 3 * DN_W:3 * DN_W + 4 * DN_HEADS],
                          ((0, 0), (0, LANES - 4 * DN_HEADS))).astype(BF16)
        p_main, p_small = _in_projection(x, norm1_g[l], mods, w_main, w_small)

        dn_args = (p_main, p_small, dn_conv_w[l], dn_a_log[l], dn_dt_bias[l], dn_norm_g[l])
        oa_p, s_fin = _deltanet(*dn_args, None, n_seq=BATCH, seq_len=SEQ, row_blk0=0, hps=4, layer=l)
        oa_s, _ = _deltanet(*dn_args, state_dn, n_seq=DEC_BATCH, seq_len=DEC_SEQ, row_blk0=sample_blk0,
                            hps=2, layer=l)

        ob_p, k_l, v_l = _diff_attention(p_main, da_lambda[l], da_subln_g[l], l, n_seq=BATCH, seq_len=SEQ,
                                         row_blk0=0, tq=SEQ)
        ob_s, _, _ = _diff_attention(p_main, da_lambda[l], da_subln_g[l], l, n_seq=DEC_BATCH,
                                     seq_len=DEC_SEQ, row_blk0=sample_blk0, tq=256, rope=rope,
                                     ctx=(cache_k, cache_v))

        hy_w = (hy_w1[l], hy_b1[l], hy_freq[l], hy_w2[l], hy_b2[l], hy_w3[l])
        hspec_p = _hyena_filter_spectrum(SEQ, emb_p, tab_p, *hy_w)
        hspec_s = _hyena_filter_spectrum(DEC_SEQ, emb_s, tab_s, *hy_w)
        (x0_p, gv_p), (x0_s, gv_s) = _hyena_pre(p_main, hy_conv_w[l], hy_conv_b[l])
        oc_p = _hyena_conv(gv_p, x0_p, hy_d[l], hspec_p, tab_p, n_seq_total=BATCH, seq_len=SEQ,
                           n_seq_blk=BATCH, kt=SEQ)
        oc_s = _hyena_conv(gv_s, x0_s, hy_d[l], hspec_s, tab_s, n_seq_total=DEC_BATCH, seq_len=DEC_SEQ,
                           n_seq_blk=2, kt=512)

        x = _merge(x, (oa_p, ob_p, oc_p), (oa_s, ob_s, oc_s), p_main, mods,
                   w_br_a, w_br_b, w_br_c, w_out, l)
        x = _ffn(x, norm2_g[l], mods, w_ffn_in[l].astype(BF16), w_ffn_out[l].astype(BF16), final_g,
                 final=(l == DEPTH - 1))
        new_k.append(k_l)
        new_v.append(v_l)
        new_s.append(s_fin)

    y_prompt = x[0].reshape(BATCH, SEQ, D_MODEL)
    y_sample = x[1].reshape(DEC_BATCH, DEC_SEQ, D_MODEL)
    return (y_prompt, y_sample, jnp.stack(new_k, axis=1), jnp.stack(new_v, axis=1), jnp.stack(new_s, axis=1))
```

```python
import functools
import math

import jax
import jax.numpy as jnp
from jax import lax
from jax.experimental import pallas as pl
from jax.experimental.pallas import tpu as pltpu

F32 = jnp.float32
BF16 = jnp.bfloat16

D_MODEL = 1024
BATCH = 16
SEQ = 256
DEPTH = 2
DEC_BATCH = 4
DEC_SEQ = 2048
PAST_LEN = 256
GRID_W = 64
RMS_EPS = 1e-6
DN_HEADS = 4
DN_DK = 128
DN_DV = 128
DN_W = DN_HEADS * DN_DK
DN_CONV = 5
DN_CHUNK = 64
DA_HEADS = 4
DA_DH = 64
DA_DV = 2 * DA_DH
ROPE_BASE = 10000.0
HY_W = 512
HY_SHORT = 3
HY_BANDS = 8
HY_EMB = 1 + 2 * HY_BANDS
HY_ORDER = 64
HY_FAST_DECAY = 0.3
HY_SLOW_DECAY = 1.5
HY_TARGET = 1e-2
D_FF = ((8 * D_MODEL // 3 + 255) // 256) * 256

N_PROMPT = BATCH * SEQ
N_SAMPLE = DEC_BATCH * DEC_SEQ
N_TOK = N_PROMPT + N_SAMPLE
LANES = 128
N_MAIN = 8192
CB_QA, CB_KA, CB_VA, CB_ZA = 0, 4, 8, 12
CB_QB, CB_KB, CB_VB = 16, 20, 24
CB_X0, CB_X1, CB_HV = 28, 32, 36
CB_GATE = 40
VMEM_LIMIT = 56 * 1024 * 1024


def _cparams(sem):
    return pltpu.CompilerParams(dimension_semantics=sem, vmem_limit_bytes=VMEM_LIMIT)


def _bdot(a, b):
    return jnp.dot(a.astype(BF16), b.astype(BF16), preferred_element_type=F32)


def _bdot_nt(a, b):
    return lax.dot_general(a.astype(BF16), b.astype(BF16), (((1,), (1,)), ((), ())),
                           preferred_element_type=F32)


def _bdot_tn(a, b):
    return lax.dot_general(a.astype(BF16), b.astype(BF16), (((0,), (0,)), ((), ())),
                           preferred_element_type=F32)


def _split3(x):
    hi = x.astype(BF16)
    r = x - hi.astype(F32)
    mid = r.astype(BF16)
    lo = (r - mid.astype(F32)).astype(BF16)
    return hi, mid, lo


def _dot_exact_lhs(t, x):
    hi, mid, lo = _split3(x)
    tb = t.astype(BF16)
    d = lambda p: jnp.dot(tb, p, preferred_element_type=F32)
    return d(hi) + d(mid) + d(lo)


def _dot_exact_rhs(x, e):
    hi, mid, lo = _split3(x)
    eb = e.astype(BF16)
    d = lambda p: jnp.dot(p, eb, preferred_element_type=F32)
    return d(hi) + d(mid) + d(lo)


def _dot3(a, b):
    ah = a.astype(BF16)
    al = (a - ah.astype(F32)).astype(BF16)
    bh = b.astype(BF16)
    bl = (b - bh.astype(F32)).astype(BF16)
    d = lambda p, q: jnp.dot(p, q, preferred_element_type=F32)
    return d(ah, bh) + d(ah, bl) + d(al, bh)


def _silu(x):
    return x * jax.nn.sigmoid(x)


def _rms(x, g):
    return x * lax.rsqrt(jnp.mean(x * x, axis=-1, keepdims=True) + RMS_EPS) * g


def _mod_row(i, tm):
    n_prompt_tiles = N_PROMPT // tm
    tiles_per_seq = DEC_SEQ // tm
    return jnp.where(i < n_prompt_tiles, 0, 1 + (i - n_prompt_tiles) // tiles_per_seq)


def _mod_kernel(c_ref, w_ref, b_ref, o_ref):
    o_ref[...] = _bdot(_silu(c_ref[...]), w_ref[...]) + b_ref[...]


def _modulation(cond8, w_mod, b_mod, layer):
    n = 6 * D_MODEL
    tn = 1024
    return pl.pallas_call(
        _mod_kernel,
        grid=(n // tn,),
        in_specs=[pl.BlockSpec((8, D_MODEL), lambda j: (0, 0)),
                  pl.BlockSpec((None, D_MODEL, tn), lambda j: (layer, 0, j)),
                  pl.BlockSpec((1, tn), lambda j: (0, j))],
        out_specs=pl.BlockSpec((8, tn), lambda j: (0, j)),
        out_shape=jax.ShapeDtypeStruct((8, n), F32),
        compiler_params=_cparams(("arbitrary",)),
        name="modulation",
    )(cond8, w_mod, b_mod.reshape(1, n))


IN_TM = 2048
IN_TN = 2048
ROW_CHUNK = 256


def _inproj_kernel(x_ref, g_ref, mod_ref, wm_ref, ws_ref, p_ref, ps_ref, h_scr):
    @pl.when(pl.program_id(1) == 0)
    def _():
        g = g_ref[...]
        sh = mod_ref[0, 0:1, :]
        sc1 = 1.0 + mod_ref[0, 1:2, :]

        def body(r, carry):
            rows = pl.ds(pl.multiple_of(r * ROW_CHUNK, ROW_CHUNK), ROW_CHUNK)
            hb = (_rms(x_ref[rows, :], g) * sc1 + sh).astype(BF16)
            h_scr[rows, :] = hb
            ps_ref[rows, :] = jnp.dot(hb, ws_ref[...], preferred_element_type=F32)
            return carry

        lax.fori_loop(0, IN_TM // ROW_CHUNK, body, 0)

    p_ref[...] = jnp.dot(h_scr[...], wm_ref[...], preferred_element_type=F32).astype(BF16)


def _in_projection(x, norm_g, mods, w_main, w_small):
    grid = (N_TOK // IN_TM, N_MAIN // IN_TN)
    return pl.pallas_call(
        _inproj_kernel,
        grid=grid,
        in_specs=[pl.BlockSpec((IN_TM, D_MODEL), lambda i, j: (i, 0)),
                  pl.BlockSpec((1, D_MODEL), lambda i, j: (0, 0)),
                  pl.BlockSpec((1, 6, D_MODEL), lambda i, j: (_mod_row(i, IN_TM), 0, 0)),
                  pl.BlockSpec((D_MODEL, IN_TN), lambda i, j: (0, j)),
                  pl.BlockSpec((D_MODEL, LANES), lambda i, j: (0, 0))],
        out_specs=[pl.BlockSpec((IN_TM, IN_TN), lambda i, j: (i, j)),
                   pl.BlockSpec((IN_TM, LANES), lambda i, j: (i, 0))],
        out_shape=[jax.ShapeDtypeStruct((N_TOK, N_MAIN), BF16),
                   jax.ShapeDtypeStruct((N_TOK, LANES), F32)],
        scratch_shapes=[pltpu.VMEM((IN_TM, D_MODEL), BF16)],
        compiler_params=_cparams(("arbitrary", "arbitrary")),
        name="in_projection",
    )(x, norm_g.reshape(1, D_MODEL), mods, w_main, w_small)


DN_UNIT = 2 * DN_CHUNK
DN_GROUP = 8


CONV_PAD = 8


def _zero_conv_borders(pad_ref):
    n = pad_ref.shape[0] - 2 * CONV_PAD
    zeros = jnp.zeros((CONV_PAD, pad_ref.shape[1]), F32)
    pad_ref[0:CONV_PAD, :] = zeros
    pad_ref[CONV_PAD + n:, :] = zeros


CONV_ROWS = 512


def _centred_conv(pad_ref, load, store, n, w_ref, n_taps, bias=None):
    rb = min(n, CONV_ROWS)
    half = n_taps // 2
    for r0 in range(0, n, rb):
        pad_ref[CONV_PAD + r0:CONV_PAD + r0 + rb, :] = load(r0, rb)
    for r0 in range(0, n, rb):
        acc = None
        for tap in range(n_taps):
            lo = CONV_PAD + r0 + tap - half
            term = pad_ref[lo:lo + rb, :] * w_ref[tap:tap + 1, :]
            acc = term if acc is None else acc + term
        store(r0, acc if bias is None else acc + bias)


def _l2norm(x):
    return x * lax.rsqrt(jnp.sum(x * x, axis=-1, keepdims=True) + RMS_EPS)


def _softplus(x):
    return jnp.maximum(x, 0.0) + jnp.log1p(jnp.exp(-jnp.abs(x)))


DN_BASE = 16


def _unit_tri_inverse(a_list, ri, ci):
    eye = (ri == ci).astype(F32)
    blk = lambda b: (ri // b) == (ci // b)
    y = [jnp.where(blk(DN_BASE), -a, 0.0) for a in a_list]
    p = [eye + yi for yi in y]
    for _ in range(3):
        y = [_bdot(yi, yi) for yi in y]
        p = [pi + _bdot(yi, pi) for yi, pi in zip(y, p)]
    b = DN_BASE
    while b < DN_CHUNK:
        off_mask = blk(2 * b) & ~blk(b)
        t = [_bdot(jnp.where(off_mask, a, 0.0), pi) for a, pi in zip(a_list, p)]
        p = [pi - _bdot(pi, ti) for pi, ti in zip(p, t)]
        b *= 2
    return p


def _dn_units(chains):
    u, c = DN_UNIT, DN_CHUNK
    ri = lax.broadcasted_iota(jnp.int32, (u, u), 0)
    ci = lax.broadcasted_iota(jnp.int32, (u, u), 1)
    same = (ri // c) == (ci // c)
    eye_mask = ri == ci
    hi_rows = ri >= c
    incl_of = {False: same & (ri >= ci), True: same & (ri <= ci)}
    strict_of = {False: same & (ri > ci), True: same & (ri < ci)}
    incl = [incl_of[ch['backward']] for ch in chains]
    strict = [strict_of[ch['backward']] for ch in chains]
    gc = [_dot_exact_lhs(m.astype(F32), ch['g']) for m, ch in zip(incl, chains)]
    g_tot = [(x[0:1, :], x[c:c + 1, :]) if ch['backward'] else (x[c - 1:c, :], x[u - 1:u, :])
             for x, ch in zip(gc, chains)]
    gc_row = [jnp.sum(jnp.where(eye_mask, x, 0.0), axis=0, keepdims=True) for x in gc]
    dec = [jnp.exp(jnp.where(m, x - xr, -1e30)) for m, x, xr in zip(incl, gc, gc_row)]
    e_gc = [jnp.exp(x) for x in gc]
    a = [jnp.where(m, ch['kk'] * ch['beta'] * d, 0.0) for m, ch, d in zip(strict, chains, dec)]
    qk = [ch['qk_raw'] * d for ch, d in zip(chains, dec)]
    eye = eye_mask.astype(F32)
    r = [p - eye for p in _unit_tri_inverse(a, ri, ci)]
    rhs = [jnp.concatenate([ch['v'] * ch['beta'], ch['k'] * (ch['beta'] * e)], axis=1)
           for ch, e in zip(chains, e_gc)]
    uwb = [(x + _bdot(ri_, x)).astype(BF16) for ri_, x in zip(r, rhs)]
    qkuw = [_bdot(x, y) for x, y in zip(qk, uwb)]
    kd = [ch['k'] * jnp.exp(jnp.where(hi_rows, gt[1], gt[0]) - x) for ch, gt, x in zip(chains, g_tot, gc)]
    tp = [[_bdot_tn(jnp.where(keep, x, 0.0), y) for x, y in zip(kd, uwb)]
          for keep in (~hi_rows, hi_rows)]
    out = []
    for i, ch in enumerate(chains):
        o_local = qkuw[i][:, :DN_DV]
        q_eff = ch['q'] * e_gc[i] - qkuw[i][:, DN_DV:]
        per_chunk = [(-tp[h][i][:, DN_DV:], tp[h][i][:, :DN_DV], jnp.exp(g_tot[i][h])) for h in range(2)]
        out.append((o_local, q_eff, per_chunk))
    return out


def _dn_kernel(*refs, seq_len, has_state, hps):
    if has_state:
        (q_ref, k_ref, v_ref, z_ref, ps_ref, cwq_ref, cwk_ref, cwv_ref, alog_ref, dtb_ref, ng_ref,
         s0_ref, o_ref, q_s, k_s, v_s, bt_s, g_s, oacc_s, qe_s, th_s, psi_s, egl_s, st_s, pad_s) = refs
        sfin_ref = None
    else:
        (q_ref, k_ref, v_ref, z_ref, ps_ref, cwq_ref, cwk_ref, cwv_ref, alog_ref, dtb_ref, ng_ref,
         o_ref, sfin_ref, q_s, k_s, v_s, bt_s, g_s, oacc_s, qe_s, th_s, psi_s, egl_s, st_s, pad_s) = refs
        s0_ref = None
    n_units = seq_len // DN_UNIT
    n_chunks = seq_len // DN_CHUNK
    c = DN_CHUNK
    sel_r = lax.broadcasted_iota(jnp.int32, (LANES, 4 * LANES), 0)
    sel_blk = lax.broadcasted_iota(jnp.int32, (LANES, 4 * LANES), 1) // LANES
    sels = []
    for j in range(hps):
        head = pl.program_id(1) * hps + j
        src_col = head + DN_HEADS * jnp.where(sel_blk == 0, 0, jnp.where(sel_blk == 1, 2, jnp.where(sel_blk == 2, 1, 3)))
        sels.append((sel_r == src_col).astype(BF16))

    def gate_rows(r, carry):
        rows = pl.ds(pl.multiple_of(r * ROW_CHUNK, ROW_CHUNK), ROW_CHUNK)
        ps = ps_ref[rows, :]
        lane = lax.broadcasted_iota(jnp.int32, ps.shape, 1)
        gate_cols = jnp.where(lane < 2 * DN_HEADS, jax.nn.sigmoid(ps),
                              -(jnp.exp(alog_ref[...]) * _softplus(ps + dtb_ref[...])))
        parts = _split3(gate_cols)
        for j in range(hps):
            ext = functools.reduce(lambda a, b: a + b,
                                   [jnp.dot(part, sels[j], preferred_element_type=F32) for part in parts])
            for d in range(2):
                bt_s[j, d, rows, :] = ext[:, (2 * d) * LANES:(2 * d + 1) * LANES]
                g_s[j, d, rows, :] = ext[:, (2 * d + 1) * LANES:(2 * d + 2) * LANES]
        return carry

    lax.fori_loop(0, seq_len // ROW_CHUNK, gate_rows, 0)

    _zero_conv_borders(pad_s)
    for j in range(hps):
        cols = slice(j * LANES, (j + 1) * LANES)
        for x_ref, w_ref, out_s, post in (
                (q_ref, cwq_ref, q_s, lambda y: _l2norm(_silu(y)) * (DN_DK ** -0.5)),
                (k_ref, cwk_ref, k_s, lambda y: _l2norm(_silu(y))),
                (v_ref, cwv_ref, v_s, _silu)):
            def store(r0, y, out_s=out_s, post=post):
                out_s[j, r0:r0 + y.shape[0], :] = post(y)
            _centred_conv(pad_s, lambda r0, rb, x_ref=x_ref: x_ref[r0:r0 + rb, cols].astype(F32), store,
                          seq_len, w_ref.at[:, cols], DN_CONV)
        for d in range(2):
            st_s[j, d] = s0_ref[0, 0, d, j] if has_state else jnp.zeros((DN_DK, DN_DV), F32)

    def unit_group(grp, carry):
        where, chains = [], []
        for t in range(DN_GROUP):
            idx = grp * DN_GROUP + t
            j = idx // n_units
            n = idx % n_units
            rows = pl.ds(pl.multiple_of(n * DN_UNIT, DN_UNIT), DN_UNIT)
            qc, kc, vc = q_s[j, rows, :], k_s[j, rows, :], v_s[j, rows, :]
            kcb = kc.astype(BF16)
            kk = _bdot_nt(kcb, kcb)
            qk_raw = _bdot_nt(qc, kcb)
            for d in range(2):
                where.append((j, n, rows, d))
                chains.append(dict(q=qc, k=kc, v=vc, kk=kk, qk_raw=qk_raw, beta=bt_s[j, d, rows, :],
                                   g=g_s[j, d, rows, :], backward=(d == 1)))
        results = _dn_units(chains)
        for (j, n, rows, d), (o_loc, q_eff, per_chunk) in zip(where, results):
            qe_s[j, d, rows, :] = q_eff.astype(BF16)
            for half, (theta, psi, egl) in enumerate(per_chunk):
                th_s[j, d, 2 * n + half] = theta.astype(BF16)
                psi_s[j, d, 2 * n + half] = psi
                egl_s[j, d, 2 * n + half] = egl
            if d == 0:
                o_fwd = o_loc
            else:
                oacc_s[j, rows, :] = o_fwd + o_loc
        return carry

    lax.fori_loop(0, hps * n_units // DN_GROUP, unit_group, 0)

    def scan_step(i, carry):
        for j in range(hps):
            for d, n in ((0, i), (1, n_chunks - 1 - i)):
                rows = pl.ds(pl.multiple_of(n * c, c), c)
                s = st_s[j, d]
                sb = s.astype(BF16)
                oacc_s[j, rows, :] += jnp.dot(qe_s[j, d, rows, :], sb, preferred_element_type=F32)
                st_s[j, d] = (egl_s[j, d, n] * s + jnp.dot(th_s[j, d, n], sb, preferred_element_type=F32)
                              + psi_s[j, d, n])
        return carry

    lax.fori_loop(0, n_chunks, scan_step, 0)

    if sfin_ref is not None:
        for j in range(hps):
            for d in range(2):
                sfin_ref[0, d, j] = st_s[j, d]

    def out_rows(r, carry):
        rows = pl.ds(pl.multiple_of(r * ROW_CHUNK, ROW_CHUNK), ROW_CHUNK)
        for j in range(hps):
            cols = slice(j * LANES, (j + 1) * LANES)
            o = _rms(oacc_s[j, rows, :], ng_ref[...]) * _silu(z_ref[rows, cols].astype(F32))
            o_ref[rows, cols] = o.astype(BF16)
        return carry

    lax.fori_loop(0, seq_len // ROW_CHUNK, out_rows, 0)


def _deltanet(p_main, p_small, conv_w, a_log, dt_bias, norm_g, s0, *, n_seq, seq_len, row_blk0, hps, layer):
    has_state = s0 is not None
    n_chunks = seq_len // DN_CHUNK
    width = hps * LANES
    tok = lambda cb: pl.BlockSpec((seq_len, width), lambda b, h: (row_blk0 + b, cb // hps + h))
    cw = lambda cb: pl.BlockSpec((DN_CONV, width), lambda b, h: (0, cb // hps + h))
    in_specs = [tok(CB_QA), tok(CB_KA), tok(CB_VA), tok(CB_ZA),
                pl.BlockSpec((seq_len, LANES), lambda b, h: (row_blk0 + b, 0)),
                cw(0), cw(DN_HEADS), cw(2 * DN_HEADS),
                pl.BlockSpec((1, LANES), lambda b, h: (0, 0)),
                pl.BlockSpec((1, LANES), lambda b, h: (0, 0)),
                pl.BlockSpec((1, DN_DV), lambda b, h: (0, 0))]
    pad8 = lambda t: jnp.pad(t.reshape(1, 2 * DN_HEADS), ((0, 0), (2 * DN_HEADS, LANES - 4 * DN_HEADS)))
    args = [p_main, p_main, p_main, p_main, p_small, conv_w, conv_w, conv_w,
            pad8(a_log), pad8(dt_bias), norm_g.reshape(1, DN_DV)]
    o_spec = pl.BlockSpec((seq_len, width), lambda b, h: (b, h))
    o_shape = jax.ShapeDtypeStruct((n_seq * seq_len, DN_W), BF16)
    if has_state:
        in_specs.append(pl.BlockSpec((1, 1, 2, hps, DN_DK, DN_DV), lambda b, h: (b, layer, 0, h, 0, 0)))
        args.append(s0)
        out_specs, out_shape = o_spec, o_shape
    else:
        out_specs = [o_spec, pl.BlockSpec((1, 2, hps, DN_DK, DN_DV), lambda b, h: (b, 0, h, 0, 0))]
        out_shape = [o_shape, jax.ShapeDtypeStruct((n_seq, 2, DN_HEADS, DN_DK, DN_DV), F32)]
    scratch = [pltpu.VMEM((hps, seq_len, LANES), F32),
               pltpu.VMEM((hps, seq_len, LANES), F32),
               pltpu.VMEM((hps, seq_len, LANES), F32),
               pltpu.VMEM((hps, 2, seq_len, LANES), F32),
               pltpu.VMEM((hps, 2, seq_len, LANES), F32),
               pltpu.VMEM((hps, seq_len, LANES), F32),
               pltpu.VMEM((hps, 2, seq_len, LANES), BF16),
               pltpu.VMEM((hps, 2, n_chunks, DN_DK, DN_DK), BF16),
               pltpu.VMEM((hps, 2, n_chunks, DN_DK, DN_DV), F32),
               pltpu.VMEM((hps, 2, n_chunks, 1, LANES), F32),
               pltpu.VMEM((hps, 2, DN_DK, DN_DV), F32),
               pltpu.VMEM((seq_len + 2 * CONV_PAD, LANES), F32)]
    res = pl.pallas_call(
        functools.partial(_dn_kernel, seq_len=seq_len, has_state=has_state, hps=hps),
        grid=(n_seq, DN_HEADS // hps),
        in_specs=in_specs, out_specs=out_specs, out_shape=out_shape,
        scratch_shapes=scratch,
        compiler_params=_cparams(("arbitrary", "arbitrary")),
        name=f"deltanet_{seq_len}",
    )(*args)
    return (res, None) if has_state else (res[0], res[1])


def _rope(x, cos, sin_signed):
    lane = lax.broadcasted_iota(jnp.int32, x.shape, 1)
    partner = jnp.where((lane % 32) < 16, pltpu.roll(x, LANES - 16, axis=1), pltpu.roll(x, 16, axis=1))
    return x * cos + partner * sin_signed


DA_ROWS = 128


def _exp2_rows(s_parts):
    m = functools.reduce(jnp.maximum, [jnp.max(s, axis=-1, keepdims=True) for s in s_parts])
    return [jnp.exp2(s - m).astype(BF16) for s in s_parts]


def _da_kernel(*refs, layer, latent):
    if latent:
        (q_ref, k_ref, v_ref, lam_ref, sg_ref, cq_ref, sq_ref, ck_ref, sk_ref, ctxk_ref, ctxv_ref,
         o_ref, krot_s, vext_s) = refs
    else:
        q_ref, k_ref, v_ref, lam_ref, sg_ref, o_ref, ko_ref, vo_ref = refs
    with_ones = lambda v: jnp.concatenate([v, jnp.ones(v.shape, BF16)], axis=1)
    lam_init = 0.8 - 0.6 * math.exp(-0.3 * layer)
    lp = lam_ref[...]
    dots = jnp.sum(jnp.concatenate([lp[0:1] * lp[1:2], lp[2:3] * lp[3:4]], axis=0), axis=1, keepdims=True)
    e = jnp.exp(dots)
    lam = e[0:1, :] - e[1:2, :] + lam_init

    n_heads = q_ref.shape[1] // LANES
    tq = q_ref.shape[0]
    heads = []
    if latent:
        @pl.when(pl.program_id(2) == 0)
        def _():
            krot_s[...] = _rope(k_ref[...].astype(F32), ck_ref[...], sk_ref[...]).astype(BF16)
            vext_s[...] = with_ones(v_ref[...])
        heads.append((_rope(q_ref[...].astype(F32), cq_ref[...], sq_ref[...]),
                      [ctxk_ref[0, 0, 0].astype(BF16), krot_s[...]],
                      [with_ones(ctxv_ref[0, 0, 0].astype(BF16)), vext_s[...]]))
    else:
        for j in range(n_heads):
            cols = slice(j * LANES, (j + 1) * LANES)
            heads.append((q_ref[:, cols].astype(F32), [k_ref[:, cols]], [with_ones(v_ref[:, cols])]))
            ko_ref[0, j] = k_ref[:, cols].astype(F32)
            vo_ref[0, j] = v_ref[:, cols].astype(F32)
    lane = lax.broadcasted_iota(jnp.int32, (tq, LANES), 1)
    groups = []
    for q, keys, vals in heads:
        q = q * (DA_DH ** -0.5 * math.log2(math.e))
        q12 = jnp.concatenate([jnp.where(lane < DA_DH, q, 0.0), jnp.where(lane >= DA_DH, q, 0.0)],
                              axis=0).astype(BF16)
        groups += [(q12[r:r + DA_ROWS], keys, vals) for r in range(0, 2 * tq, DA_ROWS)]
    scores = [[_bdot_nt(qg, kk) for kk in keys] for qg, keys, _ in groups]
    soft = [_exp2_rows(sg) for sg in scores]
    pv = []
    for e_parts, (_, _, vals) in zip(soft, groups):
        acc = None
        for e, vv in zip(e_parts, vals):
            part = _bdot(e, vv)
            acc = part if acc is None else acc + part
        pv.append(acc[:, :DA_DV] * (1.0 / acc[:, DA_DV:]))
    per_head = 2 * tq // DA_ROWS
    for j in range(len(heads)):
        hp = jnp.concatenate(pv[j * per_head:(j + 1) * per_head], axis=0)
        o = hp[:tq] - lam * hp[tq:]
        o_ref[:, j * LANES:(j + 1) * LANES] = (_rms(o, sg_ref[...]) * (1.0 - lam_init)).astype(BF16)


def _diff_attention(p_main, lam_p, subln_g, layer, *, n_seq, seq_len, row_blk0, tq, rope=None, ctx=None):
    latent = ctx is not None
    hps = 1 if latent else DA_HEADS
    width = hps * LANES
    nq = seq_len // tq
    qpb = seq_len // tq
    in_specs = [pl.BlockSpec((tq, width), lambda b, h, i: ((row_blk0 + b) * qpb + i, CB_QB // hps + h)),
                pl.BlockSpec((seq_len, width), lambda b, h, i: (row_blk0 + b, CB_KB // hps + h)),
                pl.BlockSpec((seq_len, width), lambda b, h, i: (row_blk0 + b, CB_VB // hps + h)),
                pl.BlockSpec((4, LANES), lambda b, h, i: (0, 0)),
                pl.BlockSpec((1, DA_DV), lambda b, h, i: (0, 0))]
    args = [p_main, p_main, p_main, jnp.pad(lam_p, ((0, 0), (0, LANES - DA_DH))), subln_g.reshape(1, DA_DV)]
    o_spec = pl.BlockSpec((tq, width), lambda b, h, i: (b * qpb + i, h))
    o_shape = jax.ShapeDtypeStruct((n_seq * seq_len, DA_HEADS * DA_DV), BF16)
    scratch = []
    if latent:
        cos, sin_signed = rope
        ctx_k, ctx_v = ctx
        n_ctx = ctx_k.shape[3]
        in_specs += [pl.BlockSpec((tq, LANES), lambda b, h, i: (i, 0)),
                     pl.BlockSpec((tq, LANES), lambda b, h, i: (i, 0)),
                     pl.BlockSpec((seq_len, LANES), lambda b, h, i: (0, 0)),
                     pl.BlockSpec((seq_len, LANES), lambda b, h, i: (0, 0)),
                     pl.BlockSpec((1, 1, 1, n_ctx, DA_DV), lambda b, h, i: (b, layer, h, 0, 0)),
                     pl.BlockSpec((1, 1, 1, n_ctx, DA_DV), lambda b, h, i: (b, layer, h, 0, 0))]
        args += [cos, sin_signed, cos, sin_signed, ctx_k, ctx_v]
        out_specs, out_shape = o_spec, o_shape
        scratch = [pltpu.VMEM((seq_len, LANES), BF16), pltpu.VMEM((seq_len, 2 * DA_DV), BF16)]
    else:
        kv_spec = pl.BlockSpec((1, hps, seq_len, DA_DV), lambda b, h, i: (b, h, 0, 0))
        kv_shape = jax.ShapeDtypeStruct((n_seq, DA_HEADS, seq_len, DA_DV), F32)
        out_specs = [o_spec, kv_spec, kv_spec]
        out_shape = [o_shape, kv_shape, kv_shape]
    res = pl.pallas_call(
        functools.partial(_da_kernel, layer=layer, latent=latent),
        grid=(n_seq, DA_HEADS // hps, nq),
        in_specs=in_specs, out_specs=out_specs, out_shape=out_shape,
        scratch_shapes=scratch,
        compiler_params=_cparams(("arbitrary", "arbitrary", "arbitrary")),
        name=f"diff_attention_{seq_len}",
    )(*args)
    return (res, None, None) if latent else tuple(res)


def _rope_tables(n_tok):
    half = DA_DH // 2
    inv = ROPE_BASE ** (-jnp.arange(0, half, 2, dtype=F32) / half)
    t = jnp.arange(n_tok)
    ang_r = (t // GRID_W).astype(F32)[:, None] * inv
    ang_c = (t % GRID_W).astype(F32)[:, None] * inv
    cos32 = lambda a: jnp.concatenate([jnp.cos(a), jnp.cos(a)], axis=-1)
    sin32 = lambda a: jnp.concatenate([-jnp.sin(a), jnp.sin(a)], axis=-1)
    cos = jnp.concatenate([cos32(ang_r), cos32(ang_c)] * 2, axis=-1)
    sin_signed = jnp.concatenate([sin32(ang_r), sin32(ang_c)] * 2, axis=-1)
    return cos, sin_signed


def _dft_tables(n):
    t_lo = 64
    k = jnp.arange(n, dtype=jnp.int32)[:, None]
    ang = lambda m: ((k * m[None, :]) % (2 * n)).astype(F32) * (math.pi / n)
    a = ang(t_lo * jnp.arange(n // t_lo, dtype=jnp.int32))
    b = ang(jnp.arange(t_lo, dtype=jnp.int32))
    ca, sa, cb, sb = jnp.cos(a)[:, :, None], jnp.sin(a)[:, :, None], jnp.cos(b)[:, None, :], jnp.sin(b)[:, None, :]
    cos_t = (ca * cb - sa * sb).reshape(n, n)
    nsin_t = -(sa * cb + ca * sb).reshape(n, n)
    return cos_t.astype(BF16), nsin_t.astype(BF16)


def _hy_embedding(n):
    j = jnp.arange(n, dtype=F32)
    t = j / (n - 1)
    ang = (2.0 * math.pi * j / n)[:, None] * jnp.linspace(1e-4, HY_BANDS - 1, HY_BANDS, dtype=F32)
    z = jnp.concatenate([t[:, None], jnp.cos(ang), -jnp.sin(ang)], axis=-1)
    half = n // 2
    dist = jnp.abs(j - half) / half
    max_decay = math.log(HY_TARGET) / HY_FAST_DECAY
    min_decay = math.log(HY_TARGET) / HY_SLOW_DECAY
    deltas = jnp.abs(jnp.linspace(min_decay, max_decay, HY_W, dtype=F32))
    return jnp.pad(z, ((0, 0), (0, LANES - HY_EMB))), dist[:, None], deltas[None, :]


def _alt_rows(n):
    t = lax.broadcasted_iota(jnp.int32, (8, n), 1)
    return (1 - 2 * (t % 2)).astype(F32)


def _hy_filter_kernel(z_ref, dist_ref, delta_ref, w1_ref, b1_ref, fr_ref, w2_ref, b2_ref, w3_ref,
                      cos_ref, nsin_ref, hre_ref, him_ref, hny_ref, h_s):
    @pl.when(pl.program_id(1) == 0)
    def _():
        fr = fr_ref[...]
        hdn = jnp.sin(fr * (_dot3(z_ref[...], w1_ref[...]) + b1_ref[...]))
        hdn = jnp.sin(fr * (_dot3(hdn, w2_ref[...]) + b2_ref[...]))
        h = _dot3(hdn, w3_ref[...])
        h = h * jnp.exp(-dist_ref[...] * delta_ref[...])
        h = h / jnp.sum(jnp.abs(h), axis=0, keepdims=True)
        h_s[...] = h.astype(BF16)
        hny_ref[...] = _dot_exact_lhs(_alt_rows(h.shape[0]), h)

    hre_ref[...] = jnp.dot(cos_ref[...], h_s[...], preferred_element_type=F32)
    him_ref[...] = jnp.dot(nsin_ref[...], h_s[...], preferred_element_type=F32)


def _hyena_filter_spectrum(n, emb, tables, w1, b1, freq, w2, b2, w3):
    z, dist, deltas = emb
    pad_o = LANES - HY_ORDER
    w1p = jnp.pad(w1, ((0, LANES - HY_EMB), (0, pad_o)))
    w2p = jnp.pad(w2, ((0, pad_o), (0, pad_o)))
    w3p = jnp.pad(w3, ((0, pad_o), (0, 0)))
    row = lambda t: jnp.pad(t.reshape(1, HY_ORDER), ((0, 0), (0, pad_o)))
    tc = HY_W
    kt = min(n, 512)
    full = lambda shape: pl.BlockSpec(shape, lambda j, k: (0, 0))
    tab = pl.BlockSpec((kt, n), lambda j, k: (k, 0))
    spec = pl.BlockSpec((kt, tc), lambda j, k: (k, j))
    return pl.pallas_call(
        _hy_filter_kernel,
        grid=(HY_W // tc, n // kt),
        in_specs=[full((n, LANES)), full((n, 1)), pl.BlockSpec((1, tc), lambda j, k: (0, j)),
                  full((LANES, LANES)), full((1, LANES)), full((1, LANES)),
                  full((LANES, LANES)), full((1, LANES)), pl.BlockSpec((LANES, tc), lambda j, k: (0, j)),
                  tab, tab],
        out_specs=[spec, spec, pl.BlockSpec((8, tc), lambda j, k: (0, j))],
        out_shape=[jax.ShapeDtypeStruct((n, HY_W), F32), jax.ShapeDtypeStruct((n, HY_W), F32),
                   jax.ShapeDtypeStruct((8, HY_W), F32)],
        scratch_shapes=[pltpu.VMEM((n, tc), BF16)],
        compiler_params=_cparams(("arbitrary", "arbitrary")),
        name=f"hyena_filter_{n}",
    )(z, dist, deltas, w1p, row(b1), row(freq), w2p, row(b2), w3p, *tables)


def _hy_pre_kernel(x0_ref, x1_ref, v_ref, w0_ref, w1_ref, w2_ref, b0_ref, b1_ref, b2_ref, x0c_ref, gv_ref,
                   pad_s, x1_s, *, seq_len):
    _zero_conv_borders(pad_s)
    for base in range(0, x0_ref.shape[0], seq_len):
        def conv(x_ref, w_ref, b_ref, store):
            _centred_conv(pad_s, lambda r0, rb: x_ref[base + r0:base + r0 + rb, :].astype(F32), store, seq_len,
                          w_ref, HY_SHORT, b_ref[...])

        def store_x0(r0, y):
            x0c_ref[base + r0:base + r0 + y.shape[0], :] = y.astype(BF16)

        def store_x1(r0, y):
            x1_s[r0:r0 + y.shape[0], :] = y

        def store_gv(r0, y):
            gv_ref[base + r0:base + r0 + y.shape[0], :] = (y * x1_s[r0:r0 + y.shape[0], :]).astype(BF16)

        conv(x0_ref, w0_ref, b0_ref, store_x0)
        conv(x1_ref, w1_ref, b1_ref, store_x1)
        conv(v_ref, w2_ref, b2_ref, store_gv)


def _hyena_pre(p_main, conv_w, conv_b):
    outs = []
    for n_seq, seq_len, row_blk0, spb, tc in ((BATCH, SEQ, 0, 4, HY_W),
                                               (DEC_BATCH, DEC_SEQ, N_PROMPT // DEC_SEQ, 1, HY_W // 2)):
        ncb = HY_W // tc
        rows = spb * seq_len
        tok = lambda cb: pl.BlockSpec((rows, tc), lambda b, j, cb=cb: (row_blk0 + b, cb * LANES // tc + j))
        cw = lambda s: pl.BlockSpec((HY_SHORT, tc), lambda b, j, s=s: (0, s * ncb + j))
        cb_ = lambda s: pl.BlockSpec((1, tc), lambda b, j, s=s: (0, s * ncb + j))
        o_spec = pl.BlockSpec((rows, tc), lambda b, j: (b, j))
        o_shape = jax.ShapeDtypeStruct((n_seq * seq_len, HY_W), BF16)
        outs.append(pl.pallas_call(
            functools.partial(_hy_pre_kernel, seq_len=seq_len),
            grid=(n_seq // spb, ncb),
            in_specs=[tok(CB_X0), tok(CB_X1), tok(CB_HV), cw(0), cw(1), cw(2), cb_(0), cb_(1), cb_(2)],
            out_specs=[o_spec, o_spec], out_shape=[o_shape, o_shape],
            scratch_shapes=[pltpu.VMEM((seq_len + 2 * CONV_PAD, tc), F32), pltpu.VMEM((seq_len, tc), F32)],
            compiler_params=_cparams(("arbitrary", "arbitrary")),
            name=f"hyena_pre_{seq_len}",
        )(p_main, p_main, p_main, conv_w, conv_w, conv_w,
          conv_b.reshape(1, -1), conv_b.reshape(1, -1), conv_b.reshape(1, -1)))
    return outs


def _hy_conv_kernel(gv_ref, x0_ref, d_ref, hre_ref, him_ref, hny_ref, cf_ref, sf_ref, ci_ref, si_ref,
                    o_ref, acc_s, *, seq_len, n_seq, kt):
    n = seq_len
    ki = pl.program_id(2)
    nk = pl.num_programs(2)
    hre, him = hre_ref[...], him_ref[...]
    cf, sf, ci, si = cf_ref[...], sf_ref[...], ci_ref[...], si_ref[...]
    freq = lax.broadcasted_iota(jnp.int32, hre.shape, 0) + ki * kt
    wk = jnp.where(freq == 0, 0.5 / n, 1.0 / n)
    quarter = freq % 4
    for b in range(n_seq):
        rows = pl.ds(b * seq_len, seq_len)
        gv = gv_ref[rows, :]
        gre = jnp.dot(cf, gv, preferred_element_type=F32)
        gim = jnp.dot(sf, gv, preferred_element_type=F32)
        yre = (gre * hre - gim * him) * wk
        yim = (gre * him + gim * hre) * wk
        are = jnp.where(quarter == 0, yre, jnp.where(quarter == 1, -yim, jnp.where(quarter == 2, -yre, yim)))
        aim = jnp.where(quarter == 0, yim, jnp.where(quarter == 1, yre, jnp.where(quarter == 2, -yim, -yre)))
        part = _bdot(ci, are) + _bdot(si, aim)

        @pl.when(ki == 0)
        def _():
            g_ny = jnp.dot(_alt_rows(n).astype(BF16), gv, preferred_element_type=F32)[0:1, :]
            y_ny = g_ny * hny_ref[0:1, :] * ((-1.0) ** (n // 2) / (2 * n))
            t = lax.broadcasted_iota(jnp.int32, part.shape, 0)
            acc_s[rows, :] = part + (1 - 2 * (t % 2)).astype(F32) * y_ny

        @pl.when(ki > 0)
        def _():
            acc_s[rows, :] += part

    @pl.when(ki == nk - 1)
    def _():
        y = acc_s[...] + gv_ref[...].astype(F32) * d_ref[...]
        o_ref[...] = (y * x0_ref[...].astype(F32)).astype(BF16)


def _hyena_conv(gv, x0c, d_skip, h_spec, tables, *, n_seq_total, seq_len, n_seq_blk, kt):
    tc = 256
    hre, him, hny = h_spec
    cos_t, nsin_t = tables
    rows = n_seq_blk * seq_len
    grid = (n_seq_total // n_seq_blk, HY_W // tc, seq_len // kt)
    tok = pl.BlockSpec((rows, tc), lambda g, j, k: (g, j))
    hs = pl.BlockSpec((kt, tc), lambda g, j, k: (k, j))
    fwd = pl.BlockSpec((kt, seq_len), lambda g, j, k: (k, 0))
    inv = pl.BlockSpec((seq_len, kt), lambda g, j, k: (0, k))
    return pl.pallas_call(
        functools.partial(_hy_conv_kernel, seq_len=seq_len, n_seq=n_seq_blk, kt=kt),
        grid=grid,
        in_specs=[tok, tok, pl.BlockSpec((1, tc), lambda g, j, k: (0, j)), hs, hs,
                  pl.BlockSpec((8, tc), lambda g, j, k: (0, j)), fwd, fwd, inv, inv],
        out_specs=tok,
        out_shape=jax.ShapeDtypeStruct((n_seq_total * seq_len, HY_W), BF16),
        scratch_shapes=[pltpu.VMEM((rows, tc), F32)],
        compiler_params=_cparams(("arbitrary", "arbitrary", "arbitrary")),
        name=f"hyena_conv_{seq_len}",
    )(gv, x0c, d_skip.reshape(1, HY_W), hre, him, hny, cos_t, nsin_t, cos_t, nsin_t)


MERGE_TM = 512


def _merge_kernel(x_ref, oap_ref, obp_ref, ocp_ref, oas_ref, obs_ref, ocs_ref, ga_ref, gb_ref, gc_ref,
                  wa_ref, wb_ref, wc_ref, wo_ref, mod_ref, y_ref, wa_s, wb_s, wc_s, wo_s):
    i = pl.program_id(0)

    @pl.when(i == 0)
    def _():
        for src, dst in ((wa_ref, wa_s), (wb_ref, wb_s), (wc_ref, wc_s), (wo_ref, wo_s)):
            dst[...] = src[...].astype(BF16)

    is_prompt = i < N_PROMPT // MERGE_TM
    pick = lambda p_ref, s_ref: jnp.where(is_prompt, p_ref[...], s_ref[...])
    sig = lambda r: jax.nn.sigmoid(r[...].astype(F32))
    merged = (sig(ga_ref) * jnp.dot(pick(oap_ref, oas_ref), wa_s[...], preferred_element_type=F32)
              + sig(gb_ref) * jnp.dot(pick(obp_ref, obs_ref), wb_s[...], preferred_element_type=F32)
              + sig(gc_ref) * jnp.dot(pick(ocp_ref, ocs_ref), wc_s[...], preferred_element_type=F32))
    out = _bdot(merged, wo_s[...])
    y_ref[...] = x_ref[...] + mod_ref[0, 2:3, :] * out


def _merge(x, branches_p, branches_s, p_main, mods, w_a, w_b, w_c, w_out, layer):
    tm = MERGE_TM
    npt = N_PROMPT // tm
    tok = lambda w: pl.BlockSpec((tm, w), lambda i: (i, 0))
    tok_p = pl.BlockSpec((tm, DN_W), lambda i: (jnp.minimum(i, npt - 1), 0))
    tok_s = pl.BlockSpec((tm, DN_W), lambda i: (jnp.maximum(i - npt, 0), 0))
    gate = lambda s: pl.BlockSpec((tm, D_MODEL), lambda i, s=s: (i, CB_GATE // 8 + s))
    wfull = lambda r: pl.BlockSpec((None, r, D_MODEL), lambda i: (layer, 0, 0))
    wscr = lambda r: pltpu.VMEM((r, D_MODEL), BF16)
    return pl.pallas_call(
        _merge_kernel,
        grid=(N_TOK // tm,),
        in_specs=[tok(D_MODEL), tok_p, tok_p, tok_p, tok_s, tok_s, tok_s, gate(0), gate(1), gate(2),
                  wfull(DN_W), wfull(DN_W), wfull(HY_W), wfull(D_MODEL),
                  pl.BlockSpec((1, 6, D_MODEL), lambda i: (_mod_row(i, tm), 0, 0))],
        out_specs=tok(D_MODEL),
        out_shape=jax.ShapeDtypeStruct((N_TOK, D_MODEL), F32),
        scratch_shapes=[wscr(DN_W), wscr(DN_W), wscr(HY_W), wscr(D_MODEL)],
        compiler_params=_cparams(("arbitrary",)),
        name="merge_out_projection",
    )(x, *branches_p, *branches_s, p_main, p_main, p_main, w_a, w_b, w_c, w_out, mods)


FFN_TM = 512
FFN_TF = D_FF // 2


def _ffn_kernel(x_ref, g_ref, mod_ref, wg_ref, wu_ref, wo_ref, fg_ref, *rest, final):
    if final:
        yp_ref, ys_ref, h_scr, acc_scr = rest
    else:
        y_ref, h_scr, acc_scr = rest
    f = pl.program_id(1)
    nf = pl.num_programs(1)

    @pl.when(f == 0)
    def _():
        g = g_ref[...]
        sh = mod_ref[0, 3:4, :]
        sc1 = 1.0 + mod_ref[0, 4:5, :]

        def body(r, carry):
            rows = pl.ds(pl.multiple_of(r * ROW_CHUNK, ROW_CHUNK), ROW_CHUNK)
            h_scr[rows, :] = (_rms(x_ref[rows, :], g) * sc1 + sh).astype(BF16)
            return carry

        lax.fori_loop(0, FFN_TM // ROW_CHUNK, body, 0)

    h = h_scr[...]
    gate = jnp.dot(h, wg_ref[...], preferred_element_type=F32)
    up = jnp.dot(h, wu_ref[...], preferred_element_type=F32)
    part = _bdot(_silu(gate) * up, wo_ref[...])

    @pl.when(f == 0)
    def _():
        acc_scr[...] = part

    @pl.when(f > 0)
    def _():
        acc_scr[...] += part

    @pl.when(f == nf - 1)
    def _():
        y = x_ref[...] + mod_ref[0, 5:6, :] * acc_scr[...]
        if final:
            y = _rms(y, fg_ref[...])
            is_prompt = pl.program_id(0) < N_PROMPT // FFN_TM

            @pl.when(is_prompt)
            def _():
                yp_ref[...] = y

            @pl.when(jnp.logical_not(is_prompt))
            def _():
                ys_ref[...] = y
        else:
            y_ref[...] = y


def _ffn(x, norm_g, mods, w_in, w_out, final_g, *, final):
    tm, tf = FFN_TM, FFN_TF
    nf = D_FF // tf
    if final:
        npt = N_PROMPT // tm
        out_specs = [pl.BlockSpec((tm, D_MODEL), lambda i, f: (jnp.minimum(i, npt - 1), 0)),
                     pl.BlockSpec((tm, D_MODEL), lambda i, f: (jnp.maximum(i - npt, 0), 0))]
        out_shape = [jax.ShapeDtypeStruct((N_PROMPT, D_MODEL), F32), jax.ShapeDtypeStruct((N_SAMPLE, D_MODEL), F32)]
    else:
        out_specs = pl.BlockSpec((tm, D_MODEL), lambda i, f: (i, 0))
        out_shape = jax.ShapeDtypeStruct((N_TOK, D_MODEL), F32)
    return pl.pallas_call(
        functools.partial(_ffn_kernel, final=final),
        grid=(N_TOK // tm, nf),
        in_specs=[pl.BlockSpec((tm, D_MODEL), lambda i, f: (i, 0)),
                  pl.BlockSpec((1, D_MODEL), lambda i, f: (0, 0)),
                  pl.BlockSpec((1, 6, D_MODEL), lambda i, f: (_mod_row(i, tm), 0, 0)),
                  pl.BlockSpec((D_MODEL, tf), lambda i, f: (0, f)),
                  pl.BlockSpec((D_MODEL, tf), lambda i, f: (0, nf + f)),
                  pl.BlockSpec((tf, D_MODEL), lambda i, f: (f, 0)),
                  pl.BlockSpec((1, D_MODEL), lambda i, f: (0, 0))],
        out_specs=out_specs,
        out_shape=out_shape,
        scratch_shapes=[pltpu.VMEM((tm, D_MODEL), BF16), pltpu.VMEM((tm, D_MODEL), F32)],
        compiler_params=_cparams(("arbitrary", "arbitrary")),
        name="ffn",
    )(x, norm_g.reshape(1, D_MODEL), mods, w_in, w_in, w_out, final_g.reshape(1, D_MODEL))


def kernel(x_prompt, x_sample, cache_k, cache_v, state_dn, c, c_ctx, norm1_g, norm2_g, w_mod, b_mod,
           w_in, dn_conv_w, dn_a_log, dn_dt_bias, dn_norm_g, da_lambda, da_subln_g, hy_conv_w,
           hy_conv_b, hy_w1, hy_b1, hy_freq, hy_w2, hy_b2, hy_w3, hy_d, w_br_a, w_br_b, w_br_c,
           w_out, w_ffn_in, w_ffn_out, final_g):
    x = jnp.concatenate([x_prompt.reshape(N_PROMPT, D_MODEL), x_sample.reshape(N_SAMPLE, D_MODEL)], axis=0)
    cond8 = jnp.concatenate([c_ctx[None, :], c, jnp.zeros((8 - 1 - DEC_BATCH, D_MODEL), F32)], axis=0)
    rope = _rope_tables(DEC_SEQ)
    tab_p = _dft_tables(SEQ)
    tab_s = _dft_tables(DEC_SEQ)
    emb_p, emb_s = _hy_embedding(SEQ), _hy_embedding(DEC_SEQ)
    sample_blk0 = N_PROMPT // DEC_SEQ

    new_k, new_v, new_s = [], [], []
    for l in range(DEPTH):
        mods = _modulation(cond8, w_mod, b_mod[l], l).reshape(8, 6, D_MODEL)
        w_l = w_in[l]
        w_main = jnp.concatenate([w_l[:, :3 * DN_W], w_l[:, 3 * DN_W + 4 * DN_HEADS:]], axis=1).astype(BF16)
        w_small = jnp.pad(w_l[:, 3 * DN_W:3 * DN_W + 4 * DN_HEADS],
                          ((0, 0), (0, LANES - 4 * DN_HEADS))).astype(BF16)
        p_main, p_small = _in_projection(x, norm1_g[l], mods, w_main, w_small)

        dn_args = (p_main, p_small, dn_conv_w[l], dn_a_log[l], dn_dt_bias[l], dn_norm_g[l])
        oa_p, s_fin = _deltanet(*dn_args, None, n_seq=BATCH, seq_len=SEQ, row_blk0=0, hps=4, layer=l)
        oa_s, _ = _deltanet(*dn_args, state_dn, n_seq=DEC_BATCH, seq_len=DEC_SEQ, row_blk0=sample_blk0,
                            hps=2, layer=l)

        ob_p, k_l, v_l = _diff_attention(p_main, da_lambda[l], da_subln_g[l], l, n_seq=BATCH, seq_len=SEQ,
                                         row_blk0=0, tq=SEQ)
        ob_s, _, _ = _diff_attention(p_main, da_lambda[l], da_subln_g[l], l, n_seq=DEC_BATCH,
                                     seq_len=DEC_SEQ, row_blk0=sample_blk0, tq=256, rope=rope,
                                     ctx=(cache_k, cache_v))

        hy_w = (hy_w1[l], hy_b1[l], hy_freq[l], hy_w2[l], hy_b2[l], hy_w3[l])
        hspec_p = _hyena_filter_spectrum(SEQ, emb_p, tab_p, *hy_w)
        hspec_s = _hyena_filter_spectrum(DEC_SEQ, emb_s, tab_s, *hy_w)
        (x0_p, gv_p), (x0_s, gv_s) = _hyena_pre(p_main, hy_conv_w[l], hy_conv_b[l])
        oc_p = _hyena_conv(gv_p, x0_p, hy_d[l], hspec_p, tab_p, n_seq_total=BATCH, seq_len=SEQ,
                           n_seq_blk=BATCH, kt=SEQ)
        oc_s = _hyena_conv(gv_s, x0_s, hy_d[l], hspec_s, tab_s, n_seq_total=DEC_BATCH, seq_len=DEC_SEQ,
                           n_seq_blk=2, kt=512)

        x = _merge(x, (oa_p, ob_p, oc_p), (oa_s, ob_s, oc_s), p_main, mods,
                   w_br_a, w_br_b, w_br_c, w_out, l)
        x = _ffn(x, norm2_g[l], mods, w_ffn_in[l].astype(BF16), w_ffn_out[l].astype(BF16), final_g,
                 final=(l == DEPTH - 1))
        new_k.append(k_l)
        new_v.append(v_l)
        new_s.append(s_fin)

    y_prompt = x[0].reshape(BATCH, SEQ, D_MODEL)
    y_sample = x[1].reshape(DEC_BATCH, DEC_SEQ, D_MODEL)
    return (y_prompt, y_sample, jnp.stack(new_k, axis=1), jnp.stack(new_v, axis=1), jnp.stack(new_s, axis=1))
```

```python
import functools
import math

import jax
import jax.numpy as jnp
from jax import lax
from jax.experimental import pallas as pl
from jax.experimental.pallas import tpu as pltpu

F32 = jnp.float32
BF16 = jnp.bfloat16

D_MODEL = 1024
BATCH = 16
SEQ = 256
DEPTH = 2
DEC_BATCH = 4
DEC_SEQ = 2048
PAST_LEN = 256
GRID_W = 64
RMS_EPS = 1e-6
DN_HEADS = 4
DN_DK = 128
DN_DV = 128
DN_W = DN_HEADS * DN_DK
DN_CONV = 5
DN_CHUNK = 64
DA_HEADS = 4
DA_DH = 64
DA_DV = 2 * DA_DH
ROPE_BASE = 10000.0
HY_W = 512
HY_SHORT = 3
HY_BANDS = 8
HY_EMB = 1 + 2 * HY_BANDS
HY_ORDER = 64
HY_FAST_DECAY = 0.3
HY_SLOW_DECAY = 1.5
HY_TARGET = 1e-2
D_FF = ((8 * D_MODEL // 3 + 255) // 256) * 256

N_PROMPT = BATCH * SEQ
N_SAMPLE = DEC_BATCH * DEC_SEQ
N_TOK = N_PROMPT + N_SAMPLE
LANES = 128
N_MAIN = 8192
CB_QA, CB_KA, CB_VA, CB_ZA = 0, 4, 8, 12
CB_QB, CB_KB, CB_VB = 16, 20, 24
CB_X0, CB_X1, CB_HV = 28, 32, 36
CB_GATE = 40
VMEM_LIMIT = 56 * 1024 * 1024


def _cparams(sem):
    return pltpu.CompilerParams(dimension_semantics=sem, vmem_limit_bytes=VMEM_LIMIT)


def _bdot(a, b):
    return jnp.dot(a.astype(BF16), b.astype(BF16), preferred_element_type=F32)


def _bdot_nt(a, b):
    return lax.dot_general(a.astype(BF16), b.astype(BF16), (((1,), (1,)), ((), ())),
                           preferred_element_type=F32)


def _bdot_tn(a, b):
    return lax.dot_general(a.astype(BF16), b.astype(BF16), (((0,), (0,)), ((), ())),
                           preferred_element_type=F32)


def _split3(x):
    hi = x.astype(BF16)
    r = x - hi.astype(F32)
    mid = r.astype(BF16)
    lo = (r - mid.astype(F32)).astype(BF16)
    return hi, mid, lo


def _dot_exact_lhs(t, x):
    hi, mid, lo = _split3(x)
    tb = t.astype(BF16)
    d = lambda p: jnp.dot(tb, p, preferred_element_type=F32)
    return d(hi) + d(mid) + d(lo)


def _dot_exact_rhs(x, e):
    hi, mid, lo = _split3(x)
    eb = e.astype(BF16)
    d = lambda p: jnp.dot(p, eb, preferred_element_type=F32)
    return d(hi) + d(mid) + d(lo)


def _dot3(a, b):
    ah = a.astype(BF16)
    al = (a - ah.astype(F32)).astype(BF16)
    bh = b.astype(BF16)
    bl = (b - bh.astype(F32)).astype(BF16)
    d = lambda p, q: jnp.dot(p, q, preferred_element_type=F32)
    return d(ah, bh) + d(ah, bl) + d(al, bh)


def _silu(x):
    return x * jax.nn.sigmoid(x)


def _rms(x, g):
    return x * lax.rsqrt(jnp.mean(x * x, axis=-1, keepdims=True) + RMS_EPS) * g


def _mod_row(i, tm):
    n_prompt_tiles = N_PROMPT // tm
    tiles_per_seq = DEC_SEQ // tm
    return jnp.where(i < n_prompt_tiles, 0, 1 + (i - n_prompt_tiles) // tiles_per_seq)


def _mod_kernel(c_ref, w_ref, b_ref, o_ref):
    o_ref[...] = _bdot(_silu(c_ref[...]), w_ref[...]) + b_ref[...]


def _modulation(cond8, w_mod, b_mod, layer):
    n = 6 * D_MODEL
    tn = 1024
    return pl.pallas_call(
        _mod_kernel,
        grid=(n // tn,),
        in_specs=[pl.BlockSpec((8, D_MODEL), lambda j: (0, 0)),
                  pl.BlockSpec((None, D_MODEL, tn), lambda j: (layer, 0, j)),
                  pl.BlockSpec((1, tn), lambda j: (0, j))],
        out_specs=pl.BlockSpec((8, tn), lambda j: (0, j)),
        out_shape=jax.ShapeDtypeStruct((8, n), F32),
        compiler_params=_cparams(("arbitrary",)),
        name="modulation",
    )(cond8, w_mod, b_mod.reshape(1, n))


IN_TM = 2048
IN_TN = 2048
ROW_CHUNK = 256


def _inproj_kernel(x_ref, g_ref, mod_ref, wm_ref, ws_ref, p_ref, ps_ref, h_scr):
    @pl.when(pl.program_id(1) == 0)
    def _():
        g = g_ref[...]
        sh = mod_ref[0, 0:1, :]
        sc1 = 1.0 + mod_ref[0, 1:2, :]

        def body(r, carry):
            rows = pl.ds(pl.multiple_of(r * ROW_CHUNK, ROW_CHUNK), ROW_CHUNK)
            hb = (_rms(x_ref[rows, :], g) * sc1 + sh).astype(BF16)
            h_scr[rows, :] = hb
            ps_ref[rows, :] = jnp.dot(hb, ws_ref[...], preferred_element_type=F32)
            return carry

        lax.fori_loop(0, IN_TM // ROW_CHUNK, body, 0)

    p_ref[...] = jnp.dot(h_scr[...], wm_ref[...], preferred_element_type=F32).astype(BF16)


N_GATE_COLS = 4 * DN_HEADS
PACK_ROWS = 128
PACK_COLS = 512


def _pack_w_in_kernel(w_ref, main_ref, small_ref):
    lo = 3 * DN_W
    main_ref[:, :lo] = w_ref[:, :lo].astype(BF16)
    for c in range(lo, N_MAIN, PACK_COLS):
        main_ref[:, c:c + PACK_COLS] = w_ref[:, c + N_GATE_COLS:c + N_GATE_COLS + PACK_COLS].astype(BF16)
    gate = w_ref[:, lo:lo + LANES]
    lane = lax.broadcasted_iota(jnp.int32, gate.shape, 1)
    small_ref[...] = jnp.where(lane < N_GATE_COLS, gate, 0.0).astype(BF16)


def _pack_w_in(w_in):
    n_in = w_in.shape[2]
    return pl.pallas_call(
        _pack_w_in_kernel,
        grid=(DEPTH, D_MODEL // PACK_ROWS),
        in_specs=[pl.BlockSpec((None, PACK_ROWS, n_in), lambda l, r: (l, r, 0))],
        out_specs=[pl.BlockSpec((None, PACK_ROWS, N_MAIN), lambda l, r: (l, r, 0)),
                   pl.BlockSpec((None, PACK_ROWS, LANES), lambda l, r: (l, r, 0))],
        out_shape=[jax.ShapeDtypeStruct((DEPTH, D_MODEL, N_MAIN), BF16),
                   jax.ShapeDtypeStruct((DEPTH, D_MODEL, LANES), BF16)],
        compiler_params=_cparams(("arbitrary", "arbitrary")),
        name="pack_w_in",
    )(w_in)


def _in_projection(x, norm_g, mods, w_main, w_small, layer):
    grid = (N_TOK // IN_TM, N_MAIN // IN_TN)
    return pl.pallas_call(
        _inproj_kernel,
        grid=grid,
        in_specs=[pl.BlockSpec((IN_TM, D_MODEL), lambda i, j: (i, 0)),
                  pl.BlockSpec((1, D_MODEL), lambda i, j: (0, 0)),
                  pl.BlockSpec((1, 6, D_MODEL), lambda i, j: (_mod_row(i, IN_TM), 0, 0)),
                  pl.BlockSpec((None, D_MODEL, IN_TN), lambda i, j: (layer, 0, j)),
                  pl.BlockSpec((None, D_MODEL, LANES), lambda i, j: (layer, 0, 0))],
        out_specs=[pl.BlockSpec((IN_TM, IN_TN), lambda i, j: (i, j)),
                   pl.BlockSpec((IN_TM, LANES), lambda i, j: (i, 0))],
        out_shape=[jax.ShapeDtypeStruct((N_TOK, N_MAIN), BF16),
                   jax.ShapeDtypeStruct((N_TOK, LANES), F32)],
        scratch_shapes=[pltpu.VMEM((IN_TM, D_MODEL), BF16)],
        compiler_params=_cparams(("arbitrary", "arbitrary")),
        name="in_projection",
    )(x, norm_g.reshape(1, D_MODEL), mods, w_main, w_small)


DN_UNIT = 2 * DN_CHUNK
DN_GROUP = 8


CONV_PAD = 8


def _zero_conv_borders(pad_ref):
    n = pad_ref.shape[0] - 2 * CONV_PAD
    zeros = jnp.zeros((CONV_PAD, pad_ref.shape[1]), F32)
    pad_ref[0:CONV_PAD, :] = zeros
    pad_ref[CONV_PAD + n:, :] = zeros


CONV_ROWS = 512


def _centred_conv(pad_ref, load, store, n, w_ref, n_taps, bias=None):
    rb = min(n, CONV_ROWS)
    half = n_taps // 2
    for r0 in range(0, n, rb):
        pad_ref[CONV_PAD + r0:CONV_PAD + r0 + rb, :] = load(r0, rb)
    for r0 in range(0, n, rb):
        acc = None
        for tap in range(n_taps):
            lo = CONV_PAD + r0 + tap - half
            term = pad_ref[lo:lo + rb, :] * w_ref[tap:tap + 1, :]
            acc = term if acc is None else acc + term
        store(r0, acc if bias is None else acc + bias)


def _l2norm(x):
    return x * lax.rsqrt(jnp.sum(x * x, axis=-1, keepdims=True) + RMS_EPS)


def _softplus(x):
    return jnp.maximum(x, 0.0) + jnp.log1p(jnp.exp(-jnp.abs(x)))


DN_BASE = 16


def _unit_tri_inverse(a_list, ri, ci):
    eye = (ri == ci).astype(F32)
    blk = lambda b: (ri // b) == (ci // b)
    y = [jnp.where(blk(DN_BASE), -a, 0.0) for a in a_list]
    p = [eye + yi for yi in y]
    for _ in range(3):
        y = [_bdot(yi, yi) for yi in y]
        p = [pi + _bdot(yi, pi) for yi, pi in zip(y, p)]
    b = DN_BASE
    while b < DN_CHUNK:
        off_mask = blk(2 * b) & ~blk(b)
        t = [_bdot(jnp.where(off_mask, a, 0.0), pi) for a, pi in zip(a_list, p)]
        p = [pi - _bdot(pi, ti) for pi, ti in zip(p, t)]
        b *= 2
    return p


def _dn_units(chains):
    u, c = DN_UNIT, DN_CHUNK
    ri = lax.broadcasted_iota(jnp.int32, (u, u), 0)
    ci = lax.broadcasted_iota(jnp.int32, (u, u), 1)
    same = (ri // c) == (ci // c)
    eye_mask = ri == ci
    hi_rows = ri >= c
    incl_of = {False: same & (ri >= ci), True: same & (ri <= ci)}
    strict_of = {False: same & (ri > ci), True: same & (ri < ci)}
    incl = [incl_of[ch['backward']] for ch in chains]
    strict = [strict_of[ch['backward']] for ch in chains]
    gc = [_dot_exact_lhs(m.astype(F32), ch['g']) for m, ch in zip(incl, chains)]
    g_tot = [(x[0:1, :], x[c:c + 1, :]) if ch['backward'] else (x[c - 1:c, :], x[u - 1:u, :])
             for x, ch in zip(gc, chains)]
    gc_row = [jnp.sum(jnp.where(eye_mask, x, 0.0), axis=0, keepdims=True) for x in gc]
    dec = [jnp.exp(jnp.where(m, x - xr, -1e30)) for m, x, xr in zip(incl, gc, gc_row)]
    e_gc = [jnp.exp(x) for x in gc]
    a = [jnp.where(m, ch['kk'] * ch['beta'] * d, 0.0) for m, ch, d in zip(strict, chains, dec)]
    qk = [ch['qk_raw'] * d for ch, d in zip(chains, dec)]
    eye = eye_mask.astype(F32)
    r = [p - eye for p in _unit_tri_inverse(a, ri, ci)]
    rhs = [jnp.concatenate([ch['v'] * ch['beta'], ch['k'] * (ch['beta'] * e)], axis=1)
           for ch, e in zip(chains, e_gc)]
    uwb = [(x + _bdot(ri_, x)).astype(BF16) for ri_, x in zip(r, rhs)]
    qkuw = [_bdot(x, y) for x, y in zip(qk, uwb)]
    kd = [ch['k'] * jnp.exp(jnp.where(hi_rows, gt[1], gt[0]) - x) for ch, gt, x in zip(chains, g_tot, gc)]
    tp = [[_bdot_tn(jnp.where(keep, x, 0.0), y) for x, y in zip(kd, uwb)]
          for keep in (~hi_rows, hi_rows)]
    out = []
    for i, ch in enumerate(chains):
        o_local = qkuw[i][:, :DN_DV]
        q_eff = ch['q'] * e_gc[i] - qkuw[i][:, DN_DV:]
        per_chunk = [(-tp[h][i][:, DN_DV:], tp[h][i][:, :DN_DV], jnp.exp(g_tot[i][h])) for h in range(2)]
        out.append((o_local, q_eff, per_chunk))
    return out


def _dn_kernel(*refs, seq_len, has_state, hps):
    if has_state:
        (q_ref, k_ref, v_ref, z_ref, ps_ref, cwq_ref, cwk_ref, cwv_ref, alog_ref, dtb_ref, ng_ref,
         s0_ref, o_ref, q_s, k_s, v_s, bt_s, g_s, oacc_s, qe_s, th_s, psi_s, egl_s, st_s, pad_s) = refs
        sfin_ref = None
    else:
        (q_ref, k_ref, v_ref, z_ref, ps_ref, cwq_ref, cwk_ref, cwv_ref, alog_ref, dtb_ref, ng_ref,
         o_ref, sfin_ref, q_s, k_s, v_s, bt_s, g_s, oacc_s, qe_s, th_s, psi_s, egl_s, st_s, pad_s) = refs
        s0_ref = None
    n_units = seq_len // DN_UNIT
    n_chunks = seq_len // DN_CHUNK
    c = DN_CHUNK
    sel_r = lax.broadcasted_iota(jnp.int32, (LANES, 4 * LANES), 0)
    sel_blk = lax.broadcasted_iota(jnp.int32, (LANES, 4 * LANES), 1) // LANES
    sels = []
    for j in range(hps):
        head = pl.program_id(1) * hps + j
        src_col = head + DN_HEADS * jnp.where(sel_blk == 0, 0, jnp.where(sel_blk == 1, 2, jnp.where(sel_blk == 2, 1, 3)))
        sels.append((sel_r == src_col).astype(BF16))

    def gate_rows(r, carry):
        rows = pl.ds(pl.multiple_of(r * ROW_CHUNK, ROW_CHUNK), ROW_CHUNK)
        ps = ps_ref[rows, :]
        lane = lax.broadcasted_iota(jnp.int32, ps.shape, 1)
        gate_cols = jnp.where(lane < 2 * DN_HEADS, jax.nn.sigmoid(ps),
                              -(jnp.exp(alog_ref[...]) * _softplus(ps + dtb_ref[...])))
        parts = _split3(gate_cols)
        for j in range(hps):
            ext = functools.reduce(lambda a, b: a + b,
                                   [jnp.dot(part, sels[j], preferred_element_type=F32) for part in parts])
            for d in range(2):
                bt_s[j, d, rows, :] = ext[:, (2 * d) * LANES:(2 * d + 1) * LANES]
                g_s[j, d, rows, :] = ext[:, (2 * d + 1) * LANES:(2 * d + 2) * LANES]
        return carry

    lax.fori_loop(0, seq_len // ROW_CHUNK, gate_rows, 0)

    _zero_conv_borders(pad_s)
    for j in range(hps):
        cols = slice(j * LANES, (j + 1) * LANES)
        for x_ref, w_ref, out_s, post in (
                (q_ref, cwq_ref, q_s, lambda y: _l2norm(_silu(y)) * (DN_DK ** -0.5)),
                (k_ref, cwk_ref, k_s, lambda y: _l2norm(_silu(y))),
                (v_ref, cwv_ref, v_s, _silu)):
            def store(r0, y, out_s=out_s, post=post):
                out_s[j, r0:r0 + y.shape[0], :] = post(y)
            _centred_conv(pad_s, lambda r0, rb, x_ref=x_ref: x_ref[r0:r0 + rb, cols].astype(F32), store,
                          seq_len, w_ref.at[:, cols], DN_CONV)
        for d in range(2):
            st_s[j, d] = s0_ref[0, 0, d, j] if has_state else jnp.zeros((DN_DK, DN_DV), F32)

    def unit_group(grp, carry):
        where, chains = [], []
        for t in range(DN_GROUP):
            idx = grp * DN_GROUP + t
            j = idx // n_units
            n = idx % n_units
            rows = pl.ds(pl.multiple_of(n * DN_UNIT, DN_UNIT), DN_UNIT)
            qc, kc, vc = q_s[j, rows, :], k_s[j, rows, :], v_s[j, rows, :]
            kcb = kc.astype(BF16)
            kk = _bdot_nt(kcb, kcb)
            qk_raw = _bdot_nt(qc, kcb)
            for d in range(2):
                where.append((j, n, rows, d))
                chains.append(dict(q=qc, k=kc, v=vc, kk=kk, qk_raw=qk_raw, beta=bt_s[j, d, rows, :],
                                   g=g_s[j, d, rows, :], backward=(d == 1)))
        results = _dn_units(chains)
        for (j, n, rows, d), (o_loc, q_eff, per_chunk) in zip(where, results):
            qe_s[j, d, rows, :] = q_eff.astype(BF16)
            for half, (theta, psi, egl) in enumerate(per_chunk):
                th_s[j, d, 2 * n + half] = theta.astype(BF16)
                psi_s[j, d, 2 * n + half] = psi
                egl_s[j, d, 2 * n + half] = egl
            if d == 0:
                o_fwd = o_loc
            else:
                oacc_s[j, rows, :] = o_fwd + o_loc
        return carry

    lax.fori_loop(0, hps * n_units // DN_GROUP, unit_group, 0)

    def scan_step(i, carry):
        for j in range(hps):
            for d, n in ((0, i), (1, n_chunks - 1 - i)):
                rows = pl.ds(pl.multiple_of(n * c, c), c)
                s = st_s[j, d]
                sb = s.astype(BF16)
                oacc_s[j, rows, :] += jnp.dot(qe_s[j, d, rows, :], sb, preferred_element_type=F32)
                st_s[j, d] = (egl_s[j, d, n] * s + jnp.dot(th_s[j, d, n], sb, preferred_element_type=F32)
                              + psi_s[j, d, n])
        return carry

    lax.fori_loop(0, n_chunks, scan_step, 0)

    if sfin_ref is not None:
        for j in range(hps):
            for d in range(2):
                sfin_ref[0, d, j] = st_s[j, d]

    def out_rows(r, carry):
        rows = pl.ds(pl.multiple_of(r * ROW_CHUNK, ROW_CHUNK), ROW_CHUNK)
        for j in range(hps):
            cols = slice(j * LANES, (j + 1) * LANES)
            o = _rms(oacc_s[j, rows, :], ng_ref[...]) * _silu(z_ref[rows, cols].astype(F32))
            o_ref[rows, cols] = o.astype(BF16)
        return carry

    lax.fori_loop(0, seq_len // ROW_CHUNK, out_rows, 0)


def _deltanet(p_main, p_small, conv_w, a_log, dt_bias, norm_g, s0, *, n_seq, seq_len, row_blk0, hps, layer):
    has_state = s0 is not None
    n_chunks = seq_len // DN_CHUNK
    width = hps * LANES
    tok = lambda cb: pl.BlockSpec((seq_len, width), lambda b, h: (row_blk0 + b, cb // hps + h))
    cw = lambda cb: pl.BlockSpec((DN_CONV, width), lambda b, h: (0, cb // hps + h))
    in_specs = [tok(CB_QA), tok(CB_KA), tok(CB_VA), tok(CB_ZA),
                pl.BlockSpec((seq_len, LANES), lambda b, h: (row_blk0 + b, 0)),
                cw(0), cw(DN_HEADS), cw(2 * DN_HEADS),
                pl.BlockSpec((1, LANES), lambda b, h: (0, 0)),
                pl.BlockSpec((1, LANES), lambda b, h: (0, 0)),
                pl.BlockSpec((1, DN_DV), lambda b, h: (0, 0))]
    pad8 = lambda t: jnp.pad(t.reshape(1, 2 * DN_HEADS), ((0, 0), (2 * DN_HEADS, LANES - 4 * DN_HEADS)))
    args = [p_main, p_main, p_main, p_main, p_small, conv_w, conv_w, conv_w,
            pad8(a_log), pad8(dt_bias), norm_g.reshape(1, DN_DV)]
    o_spec = pl.BlockSpec((seq_len, width), lambda b, h: (b, h))
    o_shape = jax.ShapeDtypeStruct((n_seq * seq_len, DN_W), BF16)
    if has_state:
        in_specs.append(pl.BlockSpec((1, 1, 2, hps, DN_DK, DN_DV), lambda b, h: (b, layer, 0, h, 0, 0)))
        args.append(s0)
        out_specs, out_shape = o_spec, o_shape
    else:
        out_specs = [o_spec, pl.BlockSpec((1, 2, hps, DN_DK, DN_DV), lambda b, h: (b, 0, h, 0, 0))]
        out_shape = [o_shape, jax.ShapeDtypeStruct((n_seq, 2, DN_HEADS, DN_DK, DN_DV), F32)]
    scratch = [pltpu.VMEM((hps, seq_len, LANES), F32),
               pltpu.VMEM((hps, seq_len, LANES), F32),
               pltpu.VMEM((hps, seq_len, LANES), F32),
               pltpu.VMEM((hps, 2, seq_len, LANES), F32),
               pltpu.VMEM((hps, 2, seq_len, LANES), F32),
               pltpu.VMEM((hps, seq_len, LANES), F32),
               pltpu.VMEM((hps, 2, seq_len, LANES), BF16),
               pltpu.VMEM((hps, 2, n_chunks, DN_DK, DN_DK), BF16),
               pltpu.VMEM((hps, 2, n_chunks, DN_DK, DN_DV), F32),
               pltpu.VMEM((hps, 2, n_chunks, 1, LANES), F32),
               pltpu.VMEM((hps, 2, DN_DK, DN_DV), F32),
               pltpu.VMEM((seq_len + 2 * CONV_PAD, LANES), F32)]
    res = pl.pallas_call(
        functools.partial(_dn_kernel, seq_len=seq_len, has_state=has_state, hps=hps),
        grid=(n_seq, DN_HEADS // hps),
        in_specs=in_specs, out_specs=out_specs, out_shape=out_shape,
        scratch_shapes=scratch,
        compiler_params=_cparams(("arbitrary", "arbitrary")),
        name=f"deltanet_{seq_len}",
    )(*args)
    return (res, None) if has_state else (res[0], res[1])


def _rope(x, cos, sin_signed):
    lane = lax.broadcasted_iota(jnp.int32, x.shape, 1)
    partner = jnp.where((lane % 32) < 16, pltpu.roll(x, LANES - 16, axis=1), pltpu.roll(x, 16, axis=1))
    return x * cos + partner * sin_signed


DA_ROWS = 128


def _exp2_rows(s_parts):
    m = functools.reduce(jnp.maximum, [jnp.max(s, axis=-1, keepdims=True) for s in s_parts])
    return [jnp.exp2(s - m).astype(BF16) for s in s_parts]


def _da_kernel(*refs, layer, latent):
    if latent:
        (q_ref, k_ref, v_ref, lam_ref, sg_ref, cq_ref, sq_ref, ck_ref, sk_ref, ctxk_ref, ctxv_ref,
         o_ref, krot_s, vext_s) = refs
    else:
        q_ref, k_ref, v_ref, lam_ref, sg_ref, o_ref, ko_ref, vo_ref = refs
    with_ones = lambda v: jnp.concatenate([v, jnp.ones(v.shape, BF16)], axis=1)
    lam_init = 0.8 - 0.6 * math.exp(-0.3 * layer)
    lp = lam_ref[...]
    dots = jnp.sum(jnp.concatenate([lp[0:1] * lp[1:2], lp[2:3] * lp[3:4]], axis=0), axis=1, keepdims=True)
    e = jnp.exp(dots)
    lam = e[0:1, :] - e[1:2, :] + lam_init

    n_heads = q_ref.shape[1] // LANES
    tq = q_ref.shape[0]
    heads = []
    if latent:
        @pl.when(pl.program_id(2) == 0)
        def _():
            krot_s[...] = _rope(k_ref[...].astype(F32), ck_ref[...], sk_ref[...]).astype(BF16)
            vext_s[...] = with_ones(v_ref[...])
        heads.append((_rope(q_ref[...].astype(F32), cq_ref[...], sq_ref[...]),
                      [ctxk_ref[0, 0, 0].astype(BF16), krot_s[...]],
                      [with_ones(ctxv_ref[0, 0, 0].astype(BF16)), vext_s[...]]))
    else:
        for j in range(n_heads):
            cols = slice(j * LANES, (j + 1) * LANES)
            heads.append((q_ref[:, cols].astype(F32), [k_ref[:, cols]], [with_ones(v_ref[:, cols])]))
            ko_ref[0, j] = k_ref[:, cols].astype(F32)
            vo_ref[0, j] = v_ref[:, cols].astype(F32)
    lane = lax.broadcasted_iota(jnp.int32, (tq, LANES), 1)
    groups = []
    for q, keys, vals in heads:
        q = q * (DA_DH ** -0.5 * math.log2(math.e))
        q12 = jnp.concatenate([jnp.where(lane < DA_DH, q, 0.0), jnp.where(lane >= DA_DH, q, 0.0)],
                              axis=0).astype(BF16)
        groups += [(q12[r:r + DA_ROWS], keys, vals) for r in range(0, 2 * tq, DA_ROWS)]
    scores = [[_bdot_nt(qg, kk) for kk in keys] for qg, keys, _ in groups]
    soft = [_exp2_rows(sg) for sg in scores]
    pv = []
    for e_parts, (_, _, vals) in zip(soft, groups):
        acc = None
        for e, vv in zip(e_parts, vals):
            part = _bdot(e, vv)
            acc = part if acc is None else acc + part
        pv.append(acc[:, :DA_DV] * (1.0 / acc[:, DA_DV:]))
    per_head = 2 * tq // DA_ROWS
    for j in range(len(heads)):
        hp = jnp.concatenate(pv[j * per_head:(j + 1) * per_head], axis=0)
        o = hp[:tq] - lam * hp[tq:]
        o_ref[:, j * LANES:(j + 1) * LANES] = (_rms(o, sg_ref[...]) * (1.0 - lam_init)).astype(BF16)


def _diff_attention(p_main, lam_p, subln_g, layer, *, n_seq, seq_len, row_blk0, tq, rope=None, ctx=None):
    latent = ctx is not None
    hps = 1 if latent else DA_HEADS
    width = hps * LANES
    nq = seq_len // tq
    qpb = seq_len // tq
    in_specs = [pl.BlockSpec((tq, width), lambda b, h, i: ((row_blk0 + b) * qpb + i, CB_QB // hps + h)),
                pl.BlockSpec((seq_len, width), lambda b, h, i: (row_blk0 + b, CB_KB // hps + h)),
                pl.BlockSpec((seq_len, width), lambda b, h, i: (row_blk0 + b, CB_VB // hps + h)),
                pl.BlockSpec((4, LANES), lambda b, h, i: (0, 0)),
                pl.BlockSpec((1, DA_DV), lambda b, h, i: (0, 0))]
    args = [p_main, p_main, p_main, jnp.pad(lam_p, ((0, 0), (0, LANES - DA_DH))), subln_g.reshape(1, DA_DV)]
    o_spec = pl.BlockSpec((tq, width), lambda b, h, i: (b * qpb + i, h))
    o_shape = jax.ShapeDtypeStruct((n_seq * seq_len, DA_HEADS * DA_DV), BF16)
    scratch = []
    if latent:
        cos, sin_signed = rope
        ctx_k, ctx_v = ctx
        n_ctx = ctx_k.shape[3]
        in_specs += [pl.BlockSpec((tq, LANES), lambda b, h, i: (i, 0)),
                     pl.BlockSpec((tq, LANES), lambda b, h, i: (i, 0)),
                     pl.BlockSpec((seq_len, LANES), lambda b, h, i: (0, 0)),
                     pl.BlockSpec((seq_len, LANES), lambda b, h, i: (0, 0)),
                     pl.BlockSpec((1, 1, 1, n_ctx, DA_DV), lambda b, h, i: (b, layer, h, 0, 0)),
                     pl.BlockSpec((1, 1, 1, n_ctx, DA_DV), lambda b, h, i: (b, layer, h, 0, 0))]
        args += [cos, sin_signed, cos, sin_signed, ctx_k, ctx_v]
        out_specs, out_shape = o_spec, o_shape
        scratch = [pltpu.VMEM((seq_len, LANES), BF16), pltpu.VMEM((seq_len, 2 * DA_DV), BF16)]
    else:
        kv_spec = pl.BlockSpec((1, hps, seq_len, DA_DV), lambda b, h, i: (b, h, 0, 0))
        kv_shape = jax.ShapeDtypeStruct((n_seq, DA_HEADS, seq_len, DA_DV), F32)
        out_specs = [o_spec, kv_spec, kv_spec]
        out_shape = [o_shape, kv_shape, kv_shape]
    res = pl.pallas_call(
        functools.partial(_da_kernel, layer=layer, latent=latent),
        grid=(n_seq, DA_HEADS // hps, nq),
        in_specs=in_specs, out_specs=out_specs, out_shape=out_shape,
        scratch_shapes=scratch,
        compiler_params=_cparams(("arbitrary", "arbitrary", "arbitrary")),
        name=f"diff_attention_{seq_len}",
    )(*args)
    return (res, None, None) if latent else tuple(res)


def _rope_tables(n_tok):
    half = DA_DH // 2
    inv = ROPE_BASE ** (-jnp.arange(0, half, 2, dtype=F32) / half)
    t = jnp.arange(n_tok)
    ang_r = (t // GRID_W).astype(F32)[:, None] * inv
    ang_c = (t % GRID_W).astype(F32)[:, None] * inv
    cos32 = lambda a: jnp.concatenate([jnp.cos(a), jnp.cos(a)], axis=-1)
    sin32 = lambda a: jnp.concatenate([-jnp.sin(a), jnp.sin(a)], axis=-1)
    cos = jnp.concatenate([cos32(ang_r), cos32(ang_c)] * 2, axis=-1)
    sin_signed = jnp.concatenate([sin32(ang_r), sin32(ang_c)] * 2, axis=-1)
    return cos, sin_signed


def _dft_tables(n):
    t_lo = 64
    k = jnp.arange(n, dtype=jnp.int32)[:, None]
    ang = lambda m: ((k * m[None, :]) % (2 * n)).astype(F32) * (math.pi / n)
    a = ang(t_lo * jnp.arange(n // t_lo, dtype=jnp.int32))
    b = ang(jnp.arange(t_lo, dtype=jnp.int32))
    ca, sa, cb, sb = jnp.cos(a)[:, :, None], jnp.sin(a)[:, :, None], jnp.cos(b)[:, None, :], jnp.sin(b)[:, None, :]
    cos_t = (ca * cb - sa * sb).reshape(n, n)
    nsin_t = -(sa * cb + ca * sb).reshape(n, n)
    return cos_t.astype(BF16), nsin_t.astype(BF16)


def _hy_embedding(n):
    j = jnp.arange(n, dtype=F32)
    t = j / (n - 1)
    ang = (2.0 * math.pi * j / n)[:, None] * jnp.linspace(1e-4, HY_BANDS - 1, HY_BANDS, dtype=F32)
    z = jnp.concatenate([t[:, None], jnp.cos(ang), -jnp.sin(ang)], axis=-1)
    half = n // 2
    dist = jnp.abs(j - half) / half
    max_decay = math.log(HY_TARGET) / HY_FAST_DECAY
    min_decay = math.log(HY_TARGET) / HY_SLOW_DECAY
    deltas = jnp.abs(jnp.linspace(min_decay, max_decay, HY_W, dtype=F32))
    return jnp.pad(z, ((0, 0), (0, LANES - HY_EMB))), dist[:, None], deltas[None, :]


def _alt_rows(n):
    t = lax.broadcasted_iota(jnp.int32, (8, n), 1)
    return (1 - 2 * (t % 2)).astype(F32)


def _hy_filter_kernel(z_ref, dist_ref, delta_ref, w1_ref, b1_ref, fr_ref, w2_ref, b2_ref, w3_ref,
                      cos_ref, nsin_ref, hre_ref, him_ref, hny_ref, h_s):
    @pl.when(pl.program_id(1) == 0)
    def _():
        fr = fr_ref[...]
        hdn = jnp.sin(fr * (_dot3(z_ref[...], w1_ref[...]) + b1_ref[...]))
        hdn = jnp.sin(fr * (_dot3(hdn, w2_ref[...]) + b2_ref[...]))
        h = _dot3(hdn, w3_ref[...])
        h = h * jnp.exp(-dist_ref[...] * delta_ref[...])
        h = h / jnp.sum(jnp.abs(h), axis=0, keepdims=True)
        h_s[...] = h.astype(BF16)
        hny_ref[...] = _dot_exact_lhs(_alt_rows(h.shape[0]), h)

    hre_ref[...] = jnp.dot(cos_ref[...], h_s[...], preferred_element_type=F32)
    him_ref[...] = jnp.dot(nsin_ref[...], h_s[...], preferred_element_type=F32)


def _hyena_filter_spectrum(n, emb, tables, w1, b1, freq, w2, b2, w3):
    z, dist, deltas = emb
    pad_o = LANES - HY_ORDER
    w1p = jnp.pad(w1, ((0, LANES - HY_EMB), (0, pad_o)))
    w2p = jnp.pad(w2, ((0, pad_o), (0, pad_o)))
    w3p = jnp.pad(w3, ((0, pad_o), (0, 0)))
    row = lambda t: jnp.pad(t.reshape(1, HY_ORDER), ((0, 0), (0, pad_o)))
    tc = HY_W
    kt = min(n, 512)
    full = lambda shape: pl.BlockSpec(shape, lambda j, k: (0, 0))
    tab = pl.BlockSpec((kt, n), lambda j, k: (k, 0))
    spec = pl.BlockSpec((kt, tc), lambda j, k: (k, j))
    return pl.pallas_call(
        _hy_filter_kernel,
        grid=(HY_W // tc, n // kt),
        in_specs=[full((n, LANES)), full((n, 1)), pl.BlockSpec((1, tc), lambda j, k: (0, j)),
                  full((LANES, LANES)), full((1, LANES)), full((1, LANES)),
                  full((LANES, LANES)), full((1, LANES)), pl.BlockSpec((LANES, tc), lambda j, k: (0, j)),
                  tab, tab],
        out_specs=[spec, spec, pl.BlockSpec((8, tc), lambda j, k: (0, j))],
        out_shape=[jax.ShapeDtypeStruct((n, HY_W), F32), jax.ShapeDtypeStruct((n, HY_W), F32),
                   jax.ShapeDtypeStruct((8, HY_W), F32)],
        scratch_shapes=[pltpu.VMEM((n, tc), BF16)],
        compiler_params=_cparams(("arbitrary", "arbitrary")),
        name=f"hyena_filter_{n}",
    )(z, dist, deltas, w1p, row(b1), row(freq), w2p, row(b2), w3p, *tables)


def _hy_pre_kernel(x0_ref, x1_ref, v_ref, w0_ref, w1_ref, w2_ref, b0_ref, b1_ref, b2_ref, x0c_ref, gv_ref,
                   pad_s, x1_s, *, seq_len):
    _zero_conv_borders(pad_s)
    for base in range(0, x0_ref.shape[0], seq_len):
        def conv(x_ref, w_ref, b_ref, store):
            _centred_conv(pad_s, lambda r0, rb: x_ref[base + r0:base + r0 + rb, :].astype(F32), store, seq_len,
                          w_ref, HY_SHORT, b_ref[...])

        def store_x0(r0, y):
            x0c_ref[base + r0:base + r0 + y.shape[0], :] = y.astype(BF16)

        def store_x1(r0, y):
            x1_s[r0:r0 + y.shape[0], :] = y

        def store_gv(r0, y):
            gv_ref[base + r0:base + r0 + y.shape[0], :] = (y * x1_s[r0:r0 + y.shape[0], :]).astype(BF16)

        conv(x0_ref, w0_ref, b0_ref, store_x0)
        conv(x1_ref, w1_ref, b1_ref, store_x1)
        conv(v_ref, w2_ref, b2_ref, store_gv)


def _hyena_pre(p_main, conv_w, conv_b):
    outs = []
    for n_seq, seq_len, row_blk0, spb, tc in ((BATCH, SEQ, 0, 4, HY_W),
                                               (DEC_BATCH, DEC_SEQ, N_PROMPT // DEC_SEQ, 1, HY_W // 2)):
        ncb = HY_W // tc
        rows = spb * seq_len
        tok = lambda cb: pl.BlockSpec((rows, tc), lambda b, j, cb=cb: (row_blk0 + b, cb * LANES // tc + j))
        cw = lambda s: pl.BlockSpec((HY_SHORT, tc), lambda b, j, s=s: (0, s * ncb + j))
        cb_ = lambda s: pl.BlockSpec((1, tc), lambda b, j, s=s: (0, s * ncb + j))
        o_spec = pl.BlockSpec((rows, tc), lambda b, j: (b, j))
        o_shape = jax.ShapeDtypeStruct((n_seq * seq_len, HY_W), BF16)
        outs.append(pl.pallas_call(
            functools.partial(_hy_pre_kernel, seq_len=seq_len),
            grid=(n_seq // spb, ncb),
            in_specs=[tok(CB_X0), tok(CB_X1), tok(CB_HV), cw(0), cw(1), cw(2), cb_(0), cb_(1), cb_(2)],
            out_specs=[o_spec, o_spec], out_shape=[o_shape, o_shape],
            scratch_shapes=[pltpu.VMEM((seq_len + 2 * CONV_PAD, tc), F32), pltpu.VMEM((seq_len, tc), F32)],
            compiler_params=_cparams(("arbitrary", "arbitrary")),
            name=f"hyena_pre_{seq_len}",
        )(p_main, p_main, p_main, conv_w, conv_w, conv_w,
          conv_b.reshape(1, -1), conv_b.reshape(1, -1), conv_b.reshape(1, -1)))
    return outs


HY_FREQ_BLOCK = 512


def _hy_conv_kernel(gv_ref, x0_ref, d_ref, hre_ref, him_ref, hny_ref, cos_ref, nsin_ref, o_ref, *,
                    seq_len, n_seq):
    n = seq_len
    fb = min(n, HY_FREQ_BLOCK)
    chains = [(b, f0) for b in range(n_seq) for f0 in range(0, n, fb)]
    gvs = [gv_ref[pl.ds(b * seq_len, seq_len), :] for b in range(n_seq)]
    gre = [jnp.dot(cos_ref[f0:f0 + fb, :], gvs[b], preferred_element_type=F32) for b, f0 in chains]
    gim = [jnp.dot(nsin_ref[f0:f0 + fb, :], gvs[b], preferred_element_type=F32) for b, f0 in chains]
    spec = []
    for (b, f0), gr, gi in zip(chains, gre, gim):
        hre, him = hre_ref[f0:f0 + fb, :], him_ref[f0:f0 + fb, :]
        freq = lax.broadcasted_iota(jnp.int32, hre.shape, 0) + f0
        wk = jnp.where(freq == 0, 0.5 / n, 1.0 / n)
        q4 = freq % 4
        yre = (gr * hre - gi * him) * wk
        yim = (gr * him + gi * hre) * wk
        spec.append((jnp.where(q4 == 0, yre, jnp.where(q4 == 1, -yim, jnp.where(q4 == 2, -yre, yim))),
                     jnp.where(q4 == 0, yim, jnp.where(q4 == 1, yre, jnp.where(q4 == 2, -yim, -yre)))))
    inv = [_bdot(cos_ref[:, f0:f0 + fb], are) + _bdot(nsin_ref[:, f0:f0 + fb], aim)
           for (b, f0), (are, aim) in zip(chains, spec)]
    t = lax.broadcasted_iota(jnp.int32, (n, gvs[0].shape[1]), 0)
    alt = (1 - 2 * (t % 2)).astype(F32)
    for b in range(n_seq):
        rows = pl.ds(b * seq_len, seq_len)
        gvf = gvs[b].astype(F32)
        g_ny = jnp.sum(gvf * alt, axis=0, keepdims=True)
        y_ny = g_ny * hny_ref[0:1, :] * ((-1.0) ** (n // 2) / (2 * n))
        y = functools.reduce(lambda x, z: x + z, [p for (cb, _), p in zip(chains, inv) if cb == b])
        y = y + alt * y_ny + gvf * d_ref[...]
        o_ref[rows, :] = (y * x0_ref[rows, :].astype(F32)).astype(BF16)


def _hyena_conv(gv, x0c, d_skip, h_spec, tables, *, n_seq_total, seq_len, n_seq_blk):
    tc = 256
    hre, him, hny = h_spec
    cos_t, nsin_t = tables
    rows = n_seq_blk * seq_len
    grid = (HY_W // tc, n_seq_total // n_seq_blk)
    tok = pl.BlockSpec((rows, tc), lambda j, g: (g, j))
    hs = pl.BlockSpec((seq_len, tc), lambda j, g: (0, j))
    table = pl.BlockSpec((seq_len, seq_len), lambda j, g: (0, 0), pipeline_mode=pl.Buffered(1))
    return pl.pallas_call(
        functools.partial(_hy_conv_kernel, seq_len=seq_len, n_seq=n_seq_blk),
        grid=grid,
        in_specs=[tok, tok, pl.BlockSpec((1, tc), lambda j, g: (0, j)), hs, hs,
                  pl.BlockSpec((8, tc), lambda j, g: (0, j)), table, table],
        out_specs=tok,
        out_shape=jax.ShapeDtypeStruct((n_seq_total * seq_len, HY_W), BF16),
        compiler_params=_cparams(("arbitrary", "arbitrary")),
        name=f"hyena_conv_{seq_len}",
    )(gv, x0c, d_skip.reshape(1, HY_W), hre, him, hny, cos_t, nsin_t)


MERGE_TM = 512


def _merge_kernel(x_ref, oap_ref, obp_ref, ocp_ref, oas_ref, obs_ref, ocs_ref, ga_ref, gb_ref, gc_ref,
                  wa_ref, wb_ref, wc_ref, wo_ref, mod_ref, y_ref, wa_s, wb_s, wc_s, wo_s):
    i = pl.program_id(0)

    @pl.when(i == 0)
    def _():
        for src, dst in ((wa_ref, wa_s), (wb_ref, wb_s), (wc_ref, wc_s), (wo_ref, wo_s)):
            dst[...] = src[...].astype(BF16)

    is_prompt = i < N_PROMPT // MERGE_TM
    pick = lambda p_ref, s_ref: jnp.where(is_prompt, p_ref[...], s_ref[...])
    sig = lambda r: jax.nn.sigmoid(r[...].astype(F32))
    merged = (sig(ga_ref) * jnp.dot(pick(oap_ref, oas_ref), wa_s[...], preferred_element_type=F32)
              + sig(gb_ref) * jnp.dot(pick(obp_ref, obs_ref), wb_s[...], preferred_element_type=F32)
              + sig(gc_ref) * jnp.dot(pick(ocp_ref, ocs_ref), wc_s[...], preferred_element_type=F32))
    out = _bdot(merged, wo_s[...])
    y_ref[...] = x_ref[...] + mod_ref[0, 2:3, :] * out


def _merge(x, branches_p, branches_s, p_main, mods, w_a, w_b, w_c, w_out, layer):
    tm = MERGE_TM
    npt = N_PROMPT // tm
    tok = lambda w: pl.BlockSpec((tm, w), lambda i: (i, 0))
    tok_p = pl.BlockSpec((tm, DN_W), lambda i: (jnp.minimum(i, npt - 1), 0))
    tok_s = pl.BlockSpec((tm, DN_W), lambda i: (jnp.maximum(i - npt, 0), 0))
    gate = lambda s: pl.BlockSpec((tm, D_MODEL), lambda i, s=s: (i, CB_GATE // 8 + s))
    wfull = lambda r: pl.BlockSpec((None, r, D_MODEL), lambda i: (layer, 0, 0))
    wscr = lambda r: pltpu.VMEM((r, D_MODEL), BF16)
    return pl.pallas_call(
        _merge_kernel,
        grid=(N_TOK // tm,),
        in_specs=[tok(D_MODEL), tok_p, tok_p, tok_p, tok_s, tok_s, tok_s, gate(0), gate(1), gate(2),
                  wfull(DN_W), wfull(DN_W), wfull(HY_W), wfull(D_MODEL),
                  pl.BlockSpec((1, 6, D_MODEL), lambda i: (_mod_row(i, tm), 0, 0))],
        out_specs=tok(D_MODEL),
        out_shape=jax.ShapeDtypeStruct((N_TOK, D_MODEL), F32),
        scratch_shapes=[wscr(DN_W), wscr(DN_W), wscr(HY_W), wscr(D_MODEL)],
        compiler_params=_cparams(("arbitrary",)),
        name="merge_out_projection",
    )(x, *branches_p, *branches_s, p_main, p_main, p_main, w_a, w_b, w_c, w_out, mods)


FFN_TM = 512
FFN_TF = D_FF // 2


def _ffn_kernel(x_ref, g_ref, mod_ref, wg_ref, wu_ref, wo_ref, fg_ref, *rest, final):
    if final:
        yp_ref, ys_ref, h_scr, acc_scr = rest
    else:
        y_ref, h_scr, acc_scr = rest
    f = pl.program_id(1)
    nf = pl.num_programs(1)

    @pl.when(f == 0)
    def _():
        g = g_ref[...]
        sh = mod_ref[0, 3:4, :]
        sc1 = 1.0 + mod_ref[0, 4:5, :]

        def body(r, carry):
            rows = pl.ds(pl.multiple_of(r * ROW_CHUNK, ROW_CHUNK), ROW_CHUNK)
            h_scr[rows, :] = (_rms(x_ref[rows, :], g) * sc1 + sh).astype(BF16)
            return carry

        lax.fori_loop(0, FFN_TM // ROW_CHUNK, body, 0)

    h = h_scr[...]
    gate = jnp.dot(h, wg_ref[...], preferred_element_type=F32)
    up = jnp.dot(h, wu_ref[...], preferred_element_type=F32)
    part = _bdot(_silu(gate) * up, wo_ref[...])

    @pl.when(f == 0)
    def _():
        acc_scr[...] = part

    @pl.when(f > 0)
    def _():
        acc_scr[...] += part

    @pl.when(f == nf - 1)
    def _():
        y = x_ref[...] + mod_ref[0, 5:6, :] * acc_scr[...]
        if final:
            y = _rms(y, fg_ref[...])
            is_prompt = pl.program_id(0) < N_PROMPT // FFN_TM

            @pl.when(is_prompt)
            def _():
                yp_ref[...] = y

            @pl.when(jnp.logical_not(is_prompt))
            def _():
                ys_ref[...] = y
        else:
            y_ref[...] = y


def _ffn(x, norm_g, mods, w_in, w_out, final_g, layer, *, final):
    tm, tf = FFN_TM, FFN_TF
    nf = D_FF // tf
    if final:
        npt = N_PROMPT // tm
        out_specs = [pl.BlockSpec((tm, D_MODEL), lambda i, f: (jnp.minimum(i, npt - 1), 0)),
                     pl.BlockSpec((tm, D_MODEL), lambda i, f: (jnp.maximum(i - npt, 0), 0))]
        out_shape = [jax.ShapeDtypeStruct((N_PROMPT, D_MODEL), F32), jax.ShapeDtypeStruct((N_SAMPLE, D_MODEL), F32)]
    else:
        out_specs = pl.BlockSpec((tm, D_MODEL), lambda i, f: (i, 0))
        out_shape = jax.ShapeDtypeStruct((N_TOK, D_MODEL), F32)
    return pl.pallas_call(
        functools.partial(_ffn_kernel, final=final),
        grid=(N_TOK // tm, nf),
        in_specs=[pl.BlockSpec((tm, D_MODEL), lambda i, f: (i, 0)),
                  pl.BlockSpec((1, D_MODEL), lambda i, f: (0, 0)),
                  pl.BlockSpec((1, 6, D_MODEL), lambda i, f: (_mod_row(i, tm), 0, 0)),
                  pl.BlockSpec((None, D_MODEL, tf), lambda i, f: (layer, 0, f)),
                  pl.BlockSpec((None, D_MODEL, tf), lambda i, f: (layer, 0, nf + f)),
                  pl.BlockSpec((None, tf, D_MODEL), lambda i, f: (layer, f, 0)),
                  pl.BlockSpec((1, D_MODEL), lambda i, f: (0, 0))],
        out_specs=out_specs,
        out_shape=out_shape,
        scratch_shapes=[pltpu.VMEM((tm, D_MODEL), BF16), pltpu.VMEM((tm, D_MODEL), F32)],
        compiler_params=_cparams(("arbitrary", "arbitrary")),
        name="ffn",
    )(x, norm_g.reshape(1, D_MODEL), mods, w_in, w_in, w_out, final_g.reshape(1, D_MODEL))


def kernel(x_prompt, x_sample, cache_k, cache_v, state_dn, c, c_ctx, norm1_g, norm2_g, w_mod, b_mod,
           w_in, dn_conv_w, dn_a_log, dn_dt_bias, dn_norm_g, da_lambda, da_subln_g, hy_conv_w,
           hy_conv_b, hy_w1, hy_b1, hy_freq, hy_w2, hy_b2, hy_w3, hy_d, w_br_a, w_br_b, w_br_c,
           w_out, w_ffn_in, w_ffn_out, final_g):
    x = jnp.concatenate([x_prompt.reshape(N_PROMPT, D_MODEL), x_sample.reshape(N_SAMPLE, D_MODEL)], axis=0)
    cond8 = jnp.concatenate([c_ctx[None, :], c, jnp.zeros((8 - 1 - DEC_BATCH, D_MODEL), F32)], axis=0)
    rope = _rope_tables(DEC_SEQ)
    tab_p = _dft_tables(SEQ)
    tab_s = _dft_tables(DEC_SEQ)
    emb_p, emb_s = _hy_embedding(SEQ), _hy_embedding(DEC_SEQ)
    sample_blk0 = N_PROMPT // DEC_SEQ
    w_main, w_small = _pack_w_in(w_in)
    w_ffn_in_b, w_ffn_out_b = w_ffn_in.astype(BF16), w_ffn_out.astype(BF16)

    new_k, new_v, new_s = [], [], []
    for l in range(DEPTH):
        mods = _modulation(cond8, w_mod, b_mod[l], l).reshape(8, 6, D_MODEL)
        p_main, p_small = _in_projection(x, norm1_g[l], mods, w_main, w_small, l)

        dn_args = (p_main, p_small, dn_conv_w[l], dn_a_log[l], dn_dt_bias[l], dn_norm_g[l])
        oa_p, s_fin = _deltanet(*dn_args, None, n_seq=BATCH, seq_len=SEQ, row_blk0=0, hps=4, layer=l)
        oa_s, _ = _deltanet(*dn_args, state_dn, n_seq=DEC_BATCH, seq_len=DEC_SEQ, row_blk0=sample_blk0,
                            hps=2, layer=l)

        ob_p, k_l, v_l = _diff_attention(p_main, da_lambda[l], da_subln_g[l], l, n_seq=BATCH, seq_len=SEQ,
                                         row_blk0=0, tq=SEQ)
        ob_s, _, _ = _diff_attention(p_main, da_lambda[l], da_subln_g[l], l, n_seq=DEC_BATCH,
                                     seq_len=DEC_SEQ, row_blk0=sample_blk0, tq=1024, rope=rope,
                                     ctx=(cache_k, cache_v))

        hy_w = (hy_w1[l], hy_b1[l], hy_freq[l], hy_w2[l], hy_b2[l], hy_w3[l])
        hspec_p = _hyena_filter_spectrum(SEQ, emb_p, tab_p, *hy_w)
        hspec_s = _hyena_filter_spectrum(DEC_SEQ, emb_s, tab_s, *hy_w)
        (x0_p, gv_p), (x0_s, gv_s) = _hyena_pre(p_main, hy_conv_w[l], hy_conv_b[l])
        oc_p = _hyena_conv(gv_p, x0_p, hy_d[l], hspec_p, tab_p, n_seq_total=BATCH, seq_len=SEQ,
                           n_seq_blk=BATCH)
        oc_s = _hyena_conv(gv_s, x0_s, hy_d[l], hspec_s, tab_s, n_seq_total=DEC_BATCH, seq_len=DEC_SEQ,
                           n_seq_blk=1)

        x = _merge(x, (oa_p, ob_p, oc_p), (oa_s, ob_s, oc_s), p_main, mods,
                   w_br_a, w_br_b, w_br_c, w_out, l)
        x = _ffn(x, norm2_g[l], mods, w_ffn_in_b, w_ffn_out_b, final_g, l, final=(l == DEPTH - 1))
        new_k.append(k_l)
        new_v.append(v_l)
        new_s.append(s_fin)

    y_prompt = x[0].reshape(BATCH, SEQ, D_MODEL)
    y_sample = x[1].reshape(DEC_BATCH, DEC_SEQ, D_MODEL)
    return (y_prompt, y_sample, jnp.stack(new_k, axis=1), jnp.stack(new_v, axis=1), jnp.stack(new_s, axis=1))
```

```python
import functools
import math

import jax
import jax.numpy as jnp
from jax import lax
from jax.experimental import pallas as pl
from jax.experimental.pallas import tpu as pltpu

F32 = jnp.float32
BF16 = jnp.bfloat16

D_MODEL = 1024
BATCH = 16
SEQ = 256
DEPTH = 2
DEC_BATCH = 4
DEC_SEQ = 2048
PAST_LEN = 256
GRID_W = 64
RMS_EPS = 1e-6
DN_HEADS = 4
DN_DK = 128
DN_DV = 128
DN_W = DN_HEADS * DN_DK
DN_CONV = 5
DN_CHUNK = 64
DA_HEADS = 4
DA_DH = 64
DA_DV = 2 * DA_DH
ROPE_BASE = 10000.0
HY_W = 512
HY_SHORT = 3
HY_BANDS = 8
HY_EMB = 1 + 2 * HY_BANDS
HY_ORDER = 64
HY_FAST_DECAY = 0.3
HY_SLOW_DECAY = 1.5
HY_TARGET = 1e-2
D_FF = ((8 * D_MODEL // 3 + 255) // 256) * 256

N_PROMPT = BATCH * SEQ
N_SAMPLE = DEC_BATCH * DEC_SEQ
N_TOK = N_PROMPT + N_SAMPLE
LANES = 128
N_MAIN = 8192
CB_QA, CB_KA, CB_VA, CB_ZA = 0, 4, 8, 12
CB_QB, CB_KB, CB_VB = 16, 20, 24
CB_X0, CB_X1, CB_HV = 28, 32, 36
CB_GATE = 40
VMEM_LIMIT = 56 * 1024 * 1024


def _cparams(sem):
    return pltpu.CompilerParams(dimension_semantics=sem, vmem_limit_bytes=VMEM_LIMIT)


def _bdot(a, b):
    return jnp.dot(a.astype(BF16), b.astype(BF16), preferred_element_type=F32)


def _bdot_nt(a, b):
    return lax.dot_general(a.astype(BF16), b.astype(BF16), (((1,), (1,)), ((), ())),
                           preferred_element_type=F32)


def _bdot_tn(a, b):
    return lax.dot_general(a.astype(BF16), b.astype(BF16), (((0,), (0,)), ((), ())),
                           preferred_element_type=F32)


def _split3(x):
    hi = x.astype(BF16)
    r = x - hi.astype(F32)
    mid = r.astype(BF16)
    lo = (r - mid.astype(F32)).astype(BF16)
    return hi, mid, lo


def _dot_exact_lhs(t, x):
    hi, mid, lo = _split3(x)
    tb = t.astype(BF16)
    d = lambda p: jnp.dot(tb, p, preferred_element_type=F32)
    return d(hi) + d(mid) + d(lo)


def _dot_exact_rhs(x, e):
    hi, mid, lo = _split3(x)
    eb = e.astype(BF16)
    d = lambda p: jnp.dot(p, eb, preferred_element_type=F32)
    return d(hi) + d(mid) + d(lo)


def _dot3(a, b):
    ah = a.astype(BF16)
    al = (a - ah.astype(F32)).astype(BF16)
    bh = b.astype(BF16)
    bl = (b - bh.astype(F32)).astype(BF16)
    d = lambda p, q: jnp.dot(p, q, preferred_element_type=F32)
    return d(ah, bh) + d(ah, bl) + d(al, bh)


def _silu(x):
    return x * jax.nn.sigmoid(x)


def _rms(x, g):
    return x * lax.rsqrt(jnp.mean(x * x, axis=-1, keepdims=True) + RMS_EPS) * g


def _mod_row(i, tm):
    n_prompt_tiles = N_PROMPT // tm
    tiles_per_seq = DEC_SEQ // tm
    return jnp.where(i < n_prompt_tiles, 0, 1 + (i - n_prompt_tiles) // tiles_per_seq)


JOIN_ROWS = 1024


def _join_kernel(xp_ref, xs_ref, o_ref):
    o_ref[...] = jnp.where(pl.program_id(0) < N_PROMPT // JOIN_ROWS, xp_ref[...], xs_ref[...])


def _join_streams(xp, xs):
    npt = N_PROMPT // JOIN_ROWS
    return pl.pallas_call(
        _join_kernel,
        grid=(N_TOK // JOIN_ROWS,),
        in_specs=[pl.BlockSpec((JOIN_ROWS, D_MODEL), lambda i: (jnp.minimum(i, npt - 1), 0)),
                  pl.BlockSpec((JOIN_ROWS, D_MODEL), lambda i: (jnp.maximum(i - npt, 0), 0))],
        out_specs=pl.BlockSpec((JOIN_ROWS, D_MODEL), lambda i: (i, 0)),
        out_shape=jax.ShapeDtypeStruct((N_TOK, D_MODEL), F32),
        compiler_params=_cparams(("arbitrary",)),
        name="join_streams",
    )(xp, xs)


def _mod_kernel(c_ref, w_ref, b_ref, o_ref):
    o_ref[...] = _bdot(_silu(c_ref[...]), w_ref[...]) + b_ref[...]


def _modulation(cond8, w_mod, b_mod, layer):
    n = 6 * D_MODEL
    tn = 1024
    return pl.pallas_call(
        _mod_kernel,
        grid=(n // tn,),
        in_specs=[pl.BlockSpec((8, D_MODEL), lambda j: (0, 0)),
                  pl.BlockSpec((None, D_MODEL, tn), lambda j: (layer, 0, j)),
                  pl.BlockSpec((1, tn), lambda j: (0, j))],
        out_specs=pl.BlockSpec((8, tn), lambda j: (0, j)),
        out_shape=jax.ShapeDtypeStruct((8, n), F32),
        compiler_params=_cparams(("arbitrary",)),
        name="modulation",
    )(cond8, w_mod, b_mod.reshape(1, n))


IN_TM = 2048
IN_TN = 2048
ROW_CHUNK = 256


def _inproj_kernel(x_ref, g_ref, mod_ref, wm_ref, ws_ref, p_ref, ps_ref, h_scr):
    @pl.when(pl.program_id(1) == 0)
    def _():
        g = g_ref[...]
        sh = mod_ref[0, 0:1, :]
        sc1 = 1.0 + mod_ref[0, 1:2, :]

        def body(r, carry):
            rows = pl.ds(pl.multiple_of(r * ROW_CHUNK, ROW_CHUNK), ROW_CHUNK)
            hb = (_rms(x_ref[rows, :], g) * sc1 + sh).astype(BF16)
            h_scr[rows, :] = hb
            ps_ref[rows, :] = jnp.dot(hb, ws_ref[...], preferred_element_type=F32)
            return carry

        lax.fori_loop(0, IN_TM // ROW_CHUNK, body, 0)

    p_ref[...] = jnp.dot(h_scr[...], wm_ref[...], preferred_element_type=F32).astype(BF16)


N_GATE_COLS = 4 * DN_HEADS
PACK_ROWS = 128
PACK_COLS = 512


def _pack_w_in_kernel(w_ref, main_ref, small_ref):
    lo = 3 * DN_W
    main_ref[:, :lo] = w_ref[:, :lo].astype(BF16)
    for c in range(lo, N_MAIN, PACK_COLS):
        main_ref[:, c:c + PACK_COLS] = w_ref[:, c + N_GATE_COLS:c + N_GATE_COLS + PACK_COLS].astype(BF16)
    gate = w_ref[:, lo:lo + LANES]
    lane = lax.broadcasted_iota(jnp.int32, gate.shape, 1)
    small_ref[...] = jnp.where(lane < N_GATE_COLS, gate, 0.0).astype(BF16)


def _pack_w_in(w_in):
    n_in = w_in.shape[2]
    return pl.pallas_call(
        _pack_w_in_kernel,
        grid=(DEPTH, D_MODEL // PACK_ROWS),
        in_specs=[pl.BlockSpec((None, PACK_ROWS, n_in), lambda l, r: (l, r, 0))],
        out_specs=[pl.BlockSpec((None, PACK_ROWS, N_MAIN), lambda l, r: (l, r, 0)),
                   pl.BlockSpec((None, PACK_ROWS, LANES), lambda l, r: (l, r, 0))],
        out_shape=[jax.ShapeDtypeStruct((DEPTH, D_MODEL, N_MAIN), BF16),
                   jax.ShapeDtypeStruct((DEPTH, D_MODEL, LANES), BF16)],
        compiler_params=_cparams(("arbitrary", "arbitrary")),
        name="pack_w_in",
    )(w_in)


def _in_projection(x, norm_g, mods, w_main, w_small, layer):
    grid = (N_TOK // IN_TM, N_MAIN // IN_TN)
    return pl.pallas_call(
        _inproj_kernel,
        grid=grid,
        in_specs=[pl.BlockSpec((IN_TM, D_MODEL), lambda i, j: (i, 0)),
                  pl.BlockSpec((1, D_MODEL), lambda i, j: (0, 0)),
                  pl.BlockSpec((1, 6, D_MODEL), lambda i, j: (_mod_row(i, IN_TM), 0, 0)),
                  pl.BlockSpec((None, D_MODEL, IN_TN), lambda i, j: (layer, 0, j)),
                  pl.BlockSpec((None, D_MODEL, LANES), lambda i, j: (layer, 0, 0))],
        out_specs=[pl.BlockSpec((IN_TM, IN_TN), lambda i, j: (i, j)),
                   pl.BlockSpec((IN_TM, LANES), lambda i, j: (i, 0))],
        out_shape=[jax.ShapeDtypeStruct((N_TOK, N_MAIN), BF16),
                   jax.ShapeDtypeStruct((N_TOK, LANES), F32)],
        scratch_shapes=[pltpu.VMEM((IN_TM, D_MODEL), BF16)],
        compiler_params=_cparams(("arbitrary", "arbitrary")),
        name="in_projection",
    )(x, norm_g.reshape(1, D_MODEL), mods, w_main, w_small)


DN_UNIT = 2 * DN_CHUNK
DN_GROUP = 8


CONV_PAD = 8


def _zero_conv_borders(pad_ref):
    n = pad_ref.shape[0] - 2 * CONV_PAD
    zeros = jnp.zeros((CONV_PAD, pad_ref.shape[1]), F32)
    pad_ref[0:CONV_PAD, :] = zeros
    pad_ref[CONV_PAD + n:, :] = zeros


CONV_ROWS = 512


def _centred_conv(pad_ref, load, store, n, w_ref, n_taps, bias=None):
    rb = min(n, CONV_ROWS)
    half = n_taps // 2
    for r0 in range(0, n, rb):
        pad_ref[CONV_PAD + r0:CONV_PAD + r0 + rb, :] = load(r0, rb)
    for r0 in range(0, n, rb):
        acc = None
        for tap in range(n_taps):
            lo = CONV_PAD + r0 + tap - half
            term = pad_ref[lo:lo + rb, :] * w_ref[tap:tap + 1, :]
            acc = term if acc is None else acc + term
        store(r0, acc if bias is None else acc + bias)


def _l2norm(x):
    return x * lax.rsqrt(jnp.sum(x * x, axis=-1, keepdims=True) + RMS_EPS)


def _softplus(x):
    return jnp.maximum(x, 0.0) + jnp.log1p(jnp.exp(-jnp.abs(x)))


DN_BASE = 16


def _unit_tri_inverse(a_list, ri, ci):
    eye = (ri == ci).astype(F32)
    blk = lambda b: (ri // b) == (ci // b)
    y = [jnp.where(blk(DN_BASE), -a, 0.0) for a in a_list]
    p = [eye + yi for yi in y]
    for _ in range(3):
        y = [_bdot(yi, yi) for yi in y]
        p = [pi + _bdot(yi, pi) for yi, pi in zip(y, p)]
    b = DN_BASE
    while b < DN_CHUNK:
        off_mask = blk(2 * b) & ~blk(b)
        t = [_bdot(jnp.where(off_mask, a, 0.0), pi) for a, pi in zip(a_list, p)]
        p = [pi - _bdot(pi, ti) for pi, ti in zip(p, t)]
        b *= 2
    return p


def _dn_units(chains):
    u, c = DN_UNIT, DN_CHUNK
    ri = lax.broadcasted_iota(jnp.int32, (u, u), 0)
    ci = lax.broadcasted_iota(jnp.int32, (u, u), 1)
    same = (ri // c) == (ci // c)
    eye_mask = ri == ci
    hi_rows = ri >= c
    incl_of = {False: same & (ri >= ci), True: same & (ri <= ci)}
    strict_of = {False: same & (ri > ci), True: same & (ri < ci)}
    incl = [incl_of[ch['backward']] for ch in chains]
    strict = [strict_of[ch['backward']] for ch in chains]
    gc = [_dot_exact_lhs(m.astype(F32), ch['g']) for m, ch in zip(incl, chains)]
    g_tot = [(x[0:1, :], x[c:c + 1, :]) if ch['backward'] else (x[c - 1:c, :], x[u - 1:u, :])
             for x, ch in zip(gc, chains)]
    gc_row = [jnp.sum(jnp.where(eye_mask, x, 0.0), axis=0, keepdims=True) for x in gc]
    dec = [jnp.exp(jnp.where(m, x - xr, -1e30)) for m, x, xr in zip(incl, gc, gc_row)]
    e_gc = [jnp.exp(x) for x in gc]
    a = [jnp.where(m, ch['kk'] * ch['beta'] * d, 0.0) for m, ch, d in zip(strict, chains, dec)]
    qk = [ch['qk_raw'] * d for ch, d in zip(chains, dec)]
    eye = eye_mask.astype(F32)
    r = [p - eye for p in _unit_tri_inverse(a, ri, ci)]
    rhs = [jnp.concatenate([ch['v'] * ch['beta'], ch['k'] * (ch['beta'] * e)], axis=1)
           for ch, e in zip(chains, e_gc)]
    uwb = [(x + _bdot(ri_, x)).astype(BF16) for ri_, x in zip(r, rhs)]
    qkuw = [_bdot(x, y) for x, y in zip(qk, uwb)]
    kd = [ch['k'] * jnp.exp(jnp.where(hi_rows, gt[1], gt[0]) - x) for ch, gt, x in zip(chains, g_tot, gc)]
    tp = [[_bdot_tn(jnp.where(keep, x, 0.0), y) for x, y in zip(kd, uwb)]
          for keep in (~hi_rows, hi_rows)]
    out = []
    for i, ch in enumerate(chains):
        o_local = qkuw[i][:, :DN_DV]
        q_eff = ch['q'] * e_gc[i] - qkuw[i][:, DN_DV:]
        per_chunk = [(-tp[h][i][:, DN_DV:], tp[h][i][:, :DN_DV], jnp.exp(g_tot[i][h])) for h in range(2)]
        out.append((o_local, q_eff, per_chunk))
    return out


def _dn_kernel(*refs, seq_len, has_state, hps):
    if has_state:
        (q_ref, k_ref, v_ref, z_ref, ps_ref, cwq_ref, cwk_ref, cwv_ref, alog_ref, dtb_ref, ng_ref,
         s0_ref, o_ref, q_s, k_s, v_s, bt_s, g_s, oacc_s, qe_s, th_s, psi_s, egl_s, st_s, pad_s) = refs
        sfin_ref = None
    else:
        (q_ref, k_ref, v_ref, z_ref, ps_ref, cwq_ref, cwk_ref, cwv_ref, alog_ref, dtb_ref, ng_ref,
         o_ref, sfin_ref, q_s, k_s, v_s, bt_s, g_s, oacc_s, qe_s, th_s, psi_s, egl_s, st_s, pad_s) = refs
        s0_ref = None
    n_units = seq_len // DN_UNIT
    n_chunks = seq_len // DN_CHUNK
    c = DN_CHUNK
    sel_r = lax.broadcasted_iota(jnp.int32, (LANES, 4 * LANES), 0)
    sel_blk = lax.broadcasted_iota(jnp.int32, (LANES, 4 * LANES), 1) // LANES
    sels = []
    for j in range(hps):
        head = pl.program_id(1) * hps + j
        src_col = head + DN_HEADS * jnp.where(sel_blk == 0, 0, jnp.where(sel_blk == 1, 2, jnp.where(sel_blk == 2, 1, 3)))
        sels.append((sel_r == src_col).astype(BF16))

    def gate_rows(r, carry):
        rows = pl.ds(pl.multiple_of(r * ROW_CHUNK, ROW_CHUNK), ROW_CHUNK)
        ps = ps_ref[rows, :]
        lane = lax.broadcasted_iota(jnp.int32, ps.shape, 1)
        gate_cols = jnp.where(lane < 2 * DN_HEADS, jax.nn.sigmoid(ps),
                              -(jnp.exp(alog_ref[...]) * _softplus(ps + dtb_ref[...])))
        parts = _split3(gate_cols)
        for j in range(hps):
            ext = functools.reduce(lambda a, b: a + b,
                                   [jnp.dot(part, sels[j], preferred_element_type=F32) for part in parts])
            for d in range(2):
                bt_s[j, d, rows, :] = ext[:, (2 * d) * LANES:(2 * d + 1) * LANES]
                g_s[j, d, rows, :] = ext[:, (2 * d + 1) * LANES:(2 * d + 2) * LANES]
        return carry

    lax.fori_loop(0, seq_len // ROW_CHUNK, gate_rows, 0)

    _zero_conv_borders(pad_s)
    for j in range(hps):
        cols = slice(j * LANES, (j + 1) * LANES)
        for x_ref, w_ref, out_s, post in (
                (q_ref, cwq_ref, q_s, lambda y: _l2norm(_silu(y)) * (DN_DK ** -0.5)),
                (k_ref, cwk_ref, k_s, lambda y: _l2norm(_silu(y))),
                (v_ref, cwv_ref, v_s, _silu)):
            def store(r0, y, out_s=out_s, post=post):
                out_s[j, r0:r0 + y.shape[0], :] = post(y)
            _centred_conv(pad_s, lambda r0, rb, x_ref=x_ref: x_ref[r0:r0 + rb, cols].astype(F32), store,
                          seq_len, w_ref.at[:, cols], DN_CONV)
        for d in range(2):
            st_s[j, d] = s0_ref[0, 0, d, j] if has_state else jnp.zeros((DN_DK, DN_DV), F32)

    def unit_group(grp, carry):
        where, chains = [], []
        for t in range(DN_GROUP):
            idx = grp * DN_GROUP + t
            j = idx // n_units
            n = idx % n_units
            rows = pl.ds(pl.multiple_of(n * DN_UNIT, DN_UNIT), DN_UNIT)
            qc, kc, vc = q_s[j, rows, :], k_s[j, rows, :], v_s[j, rows, :]
            kcb = kc.astype(BF16)
            kk = _bdot_nt(kcb, kcb)
            qk_raw = _bdot_nt(qc, kcb)
            for d in range(2):
                where.append((j, n, rows, d))
                chains.append(dict(q=qc, k=kc, v=vc, kk=kk, qk_raw=qk_raw, beta=bt_s[j, d, rows, :],
                                   g=g_s[j, d, rows, :], backward=(d == 1)))
        results = _dn_units(chains)
        for (j, n, rows, d), (o_loc, q_eff, per_chunk) in zip(where, results):
            qe_s[j, d, rows, :] = q_eff.astype(BF16)
            for half, (theta, psi, egl) in enumerate(per_chunk):
                th_s[j, d, 2 * n + half] = theta.astype(BF16)
                psi_s[j, d, 2 * n + half] = psi
                egl_s[j, d, 2 * n + half] = egl
            if d == 0:
                o_fwd = o_loc
            else:
                oacc_s[j, rows, :] = o_fwd + o_loc
        return carry

    lax.fori_loop(0, hps * n_units // DN_GROUP, unit_group, 0)

    def scan_step(i, carry):
        for j in range(hps):
            for d, n in ((0, i), (1, n_chunks - 1 - i)):
                rows = pl.ds(pl.multiple_of(n * c, c), c)
                s = st_s[j, d]
                sb = s.astype(BF16)
                oacc_s[j, rows, :] += jnp.dot(qe_s[j, d, rows, :], sb, preferred_element_type=F32)
                st_s[j, d] = (egl_s[j, d, n] * s + jnp.dot(th_s[j, d, n], sb, preferred_element_type=F32)
                              + psi_s[j, d, n])
        return carry

    lax.fori_loop(0, n_chunks, scan_step, 0)

    if sfin_ref is not None:
        for j in range(hps):
            for d in range(2):
                sfin_ref[0, d, j] = st_s[j, d]

    def out_rows(r, carry):
        rows = pl.ds(pl.multiple_of(r * ROW_CHUNK, ROW_CHUNK), ROW_CHUNK)
        for j in range(hps):
            cols = slice(j * LANES, (j + 1) * LANES)
            o = _rms(oacc_s[j, rows, :], ng_ref[...]) * _silu(z_ref[rows, cols].astype(F32))
            o_ref[rows, cols] = o.astype(BF16)
        return carry

    lax.fori_loop(0, seq_len // ROW_CHUNK, out_rows, 0)


def _deltanet(p_main, p_small, conv_w, a_log, dt_bias, norm_g, s0, *, n_seq, seq_len, row_blk0, hps, layer):
    has_state = s0 is not None
    n_chunks = seq_len // DN_CHUNK
    width = hps * LANES
    tok = lambda cb: pl.BlockSpec((seq_len, width), lambda b, h: (row_blk0 + b, cb // hps + h))
    cw = lambda cb: pl.BlockSpec((DN_CONV, width), lambda b, h: (0, cb // hps + h))
    in_specs = [tok(CB_QA), tok(CB_KA), tok(CB_VA), tok(CB_ZA),
                pl.BlockSpec((seq_len, LANES), lambda b, h: (row_blk0 + b, 0)),
                cw(0), cw(DN_HEADS), cw(2 * DN_HEADS),
                pl.BlockSpec((1, LANES), lambda b, h: (0, 0)),
                pl.BlockSpec((1, LANES), lambda b, h: (0, 0)),
                pl.BlockSpec((1, DN_DV), lambda b, h: (0, 0))]
    pad8 = lambda t: jnp.pad(t.reshape(1, 2 * DN_HEADS), ((0, 0), (2 * DN_HEADS, LANES - 4 * DN_HEADS)))
    args = [p_main, p_main, p_main, p_main, p_small, conv_w, conv_w, conv_w,
            pad8(a_log), pad8(dt_bias), norm_g.reshape(1, DN_DV)]
    o_spec = pl.BlockSpec((seq_len, width), lambda b, h: (b, h))
    o_shape = jax.ShapeDtypeStruct((n_seq * seq_len, DN_W), BF16)
    if has_state:
        in_specs.append(pl.BlockSpec((1, 1, 2, hps, DN_DK, DN_DV), lambda b, h: (b, layer, 0, h, 0, 0)))
        args.append(s0)
        out_specs, out_shape = o_spec, o_shape
    else:
        out_specs = [o_spec, pl.BlockSpec((1, 2, hps, DN_DK, DN_DV), lambda b, h: (b, 0, h, 0, 0))]
        out_shape = [o_shape, jax.ShapeDtypeStruct((n_seq, 2, DN_HEADS, DN_DK, DN_DV), F32)]
    scratch = [pltpu.VMEM((hps, seq_len, LANES), F32),
               pltpu.VMEM((hps, seq_len, LANES), F32),
               pltpu.VMEM((hps, seq_len, LANES), F32),
               pltpu.VMEM((hps, 2, seq_len, LANES), F32),
               pltpu.VMEM((hps, 2, seq_len, LANES), F32),
               pltpu.VMEM((hps, seq_len, LANES), F32),
               pltpu.VMEM((hps, 2, seq_len, LANES), BF16),
               pltpu.VMEM((hps, 2, n_chunks, DN_DK, DN_DK), BF16),
               pltpu.VMEM((hps, 2, n_chunks, DN_DK, DN_DV), F32),
               pltpu.VMEM((hps, 2, n_chunks, 1, LANES), F32),
               pltpu.VMEM((hps, 2, DN_DK, DN_DV), F32),
               pltpu.VMEM((seq_len + 2 * CONV_PAD, LANES), F32)]
    res = pl.pallas_call(
        functools.partial(_dn_kernel, seq_len=seq_len, has_state=has_state, hps=hps),
        grid=(n_seq, DN_HEADS // hps),
        in_specs=in_specs, out_specs=out_specs, out_shape=out_shape,
        scratch_shapes=scratch,
        compiler_params=_cparams(("arbitrary", "arbitrary")),
        name=f"deltanet_{seq_len}",
    )(*args)
    return (res, None) if has_state else (res[0], res[1])


def _rope(x, cos, sin_signed):
    lane = lax.broadcasted_iota(jnp.int32, x.shape, 1)
    partner = jnp.where((lane % 32) < 16, pltpu.roll(x, LANES - 16, axis=1), pltpu.roll(x, 16, axis=1))
    return x * cos + partner * sin_signed


DA_ROWS = 128


def _exp2_rows(s_parts):
    m = functools.reduce(jnp.maximum, [jnp.max(s, axis=-1, keepdims=True) for s in s_parts])
    return [jnp.exp2(s - m).astype(BF16) for s in s_parts]


def _da_kernel(*refs, layer, latent):
    if latent:
        (q_ref, k_ref, v_ref, lam_ref, sg_ref, cq_ref, sq_ref, ck_ref, sk_ref, ctxk_ref, ctxv_ref,
         o_ref, krot_s, vext_s) = refs
    else:
        q_ref, k_ref, v_ref, lam_ref, sg_ref, o_ref, ko_ref, vo_ref = refs
    with_ones = lambda v: jnp.concatenate([v, jnp.ones(v.shape, BF16)], axis=1)
    lam_init = 0.8 - 0.6 * math.exp(-0.3 * layer)
    lp = lam_ref[...]
    dots = jnp.sum(jnp.concatenate([lp[0:1] * lp[1:2], lp[2:3] * lp[3:4]], axis=0), axis=1, keepdims=True)
    e = jnp.exp(dots)
    lam = e[0:1, :] - e[1:2, :] + lam_init

    n_heads = q_ref.shape[1] // LANES
    tq = q_ref.shape[0]
    heads = []
    if latent:
        @pl.when(pl.program_id(2) == 0)
        def _():
            krot_s[...] = _rope(k_ref[...].astype(F32), ck_ref[...], sk_ref[...]).astype(BF16)
            vext_s[...] = with_ones(v_ref[...])
        heads.append((_rope(q_ref[...].astype(F32), cq_ref[...], sq_ref[...]),
                      [ctxk_ref[0, 0, 0].astype(BF16), krot_s[...]],
                      [with_ones(ctxv_ref[0, 0, 0].astype(BF16)), vext_s[...]]))
    else:
        for j in range(n_heads):
            cols = slice(j * LANES, (j + 1) * LANES)
            heads.append((q_ref[:, cols].astype(F32), [k_ref[:, cols]], [with_ones(v_ref[:, cols])]))
            ko_ref[0, j] = k_ref[:, cols].astype(F32)
            vo_ref[0, j] = v_ref[:, cols].astype(F32)
    lane = lax.broadcasted_iota(jnp.int32, (tq, LANES), 1)
    groups = []
    for q, keys, vals in heads:
        q = q * (DA_DH ** -0.5 * math.log2(math.e))
        q12 = jnp.concatenate([jnp.where(lane < DA_DH, q, 0.0), jnp.where(lane >= DA_DH, q, 0.0)],
                              axis=0).astype(BF16)
        groups += [(q12[r:r + DA_ROWS], keys, vals) for r in range(0, 2 * tq, DA_ROWS)]
    scores = [[_bdot_nt(qg, kk) for kk in keys] for qg, keys, _ in groups]
    soft = [_exp2_rows(sg) for sg in scores]
    pv = []
    for e_parts, (_, _, vals) in zip(soft, groups):
        acc = None
        for e, vv in zip(e_parts, vals):
            part = _bdot(e, vv)
            acc = part if acc is None else acc + part
        pv.append(acc[:, :DA_DV] * (1.0 / acc[:, DA_DV:]))
    per_head = 2 * tq // DA_ROWS
    for j in range(len(heads)):
        hp = jnp.concatenate(pv[j * per_head:(j + 1) * per_head], axis=0)
        o = hp[:tq] - lam * hp[tq:]
        o_ref[:, j * LANES:(j + 1) * LANES] = (_rms(o, sg_ref[...]) * (1.0 - lam_init)).astype(BF16)


def _diff_attention(p_main, lam_p, subln_g, layer, *, n_seq, seq_len, row_blk0, tq, rope=None, ctx=None):
    latent = ctx is not None
    hps = 1 if latent else DA_HEADS
    width = hps * LANES
    nq = seq_len // tq
    qpb = seq_len // tq
    in_specs = [pl.BlockSpec((tq, width), lambda b, h, i: ((row_blk0 + b) * qpb + i, CB_QB // hps + h)),
                pl.BlockSpec((seq_len, width), lambda b, h, i: (row_blk0 + b, CB_KB // hps + h)),
                pl.BlockSpec((seq_len, width), lambda b, h, i: (row_blk0 + b, CB_VB // hps + h)),
                pl.BlockSpec((4, LANES), lambda b, h, i: (0, 0)),
                pl.BlockSpec((1, DA_DV), lambda b, h, i: (0, 0))]
    args = [p_main, p_main, p_main, jnp.pad(lam_p, ((0, 0), (0, LANES - DA_DH))), subln_g.reshape(1, DA_DV)]
    o_spec = pl.BlockSpec((tq, width), lambda b, h, i: (b * qpb + i, h))
    o_shape = jax.ShapeDtypeStruct((n_seq * seq_len, DA_HEADS * DA_DV), BF16)
    scratch = []
    if latent:
        cos, sin_signed = rope
        ctx_k, ctx_v = ctx
        n_ctx = ctx_k.shape[3]
        in_specs += [pl.BlockSpec((tq, LANES), lambda b, h, i: (i, 0)),
                     pl.BlockSpec((tq, LANES), lambda b, h, i: (i, 0)),
                     pl.BlockSpec((seq_len, LANES), lambda b, h, i: (0, 0)),
                     pl.BlockSpec((seq_len, LANES), lambda b, h, i: (0, 0)),
                     pl.BlockSpec((1, 1, 1, n_ctx, DA_DV), lambda b, h, i: (b, layer, h, 0, 0)),
                     pl.BlockSpec((1, 1, 1, n_ctx, DA_DV), lambda b, h, i: (b, layer, h, 0, 0))]
        args += [cos, sin_signed, cos, sin_signed, ctx_k, ctx_v]
        out_specs, out_shape = o_spec, o_shape
        scratch = [pltpu.VMEM((seq_len, LANES), BF16), pltpu.VMEM((seq_len, 2 * DA_DV), BF16)]
    else:
        kv_spec = pl.BlockSpec((1, hps, seq_len, DA_DV), lambda b, h, i: (b, h, 0, 0))
        kv_shape = jax.ShapeDtypeStruct((n_seq, DA_HEADS, seq_len, DA_DV), F32)
        out_specs = [o_spec, kv_spec, kv_spec]
        out_shape = [o_shape, kv_shape, kv_shape]
    res = pl.pallas_call(
        functools.partial(_da_kernel, layer=layer, latent=latent),
        grid=(n_seq, DA_HEADS // hps, nq),
        in_specs=in_specs, out_specs=out_specs, out_shape=out_shape,
        scratch_shapes=scratch,
        compiler_params=_cparams(("arbitrary", "arbitrary", "arbitrary")),
        name=f"diff_attention_{seq_len}",
    )(*args)
    return (res, None, None) if latent else tuple(res)


def _rope_tables(n_tok):
    half = DA_DH // 2
    inv = ROPE_BASE ** (-jnp.arange(0, half, 2, dtype=F32) / half)
    t = jnp.arange(n_tok)
    ang_r = (t // GRID_W).astype(F32)[:, None] * inv
    ang_c = (t % GRID_W).astype(F32)[:, None] * inv
    cos32 = lambda a: jnp.concatenate([jnp.cos(a), jnp.cos(a)], axis=-1)
    sin32 = lambda a: jnp.concatenate([-jnp.sin(a), jnp.sin(a)], axis=-1)
    cos = jnp.concatenate([cos32(ang_r), cos32(ang_c)] * 2, axis=-1)
    sin_signed = jnp.concatenate([sin32(ang_r), sin32(ang_c)] * 2, axis=-1)
    return cos, sin_signed


def _dft_tables(n):
    t_lo = 64
    k = jnp.arange(n, dtype=jnp.int32)[:, None]
    ang = lambda m: ((k * m[None, :]) % (2 * n)).astype(F32) * (math.pi / n)
    a = ang(t_lo * jnp.arange(n // t_lo, dtype=jnp.int32))
    b = ang(jnp.arange(t_lo, dtype=jnp.int32))
    ca, sa, cb, sb = jnp.cos(a)[:, :, None], jnp.sin(a)[:, :, None], jnp.cos(b)[:, None, :], jnp.sin(b)[:, None, :]
    cos_t = (ca * cb - sa * sb).reshape(n, n)
    nsin_t = -(sa * cb + ca * sb).reshape(n, n)
    return cos_t.astype(BF16), nsin_t.astype(BF16)


def _hy_embedding(n):
    j = jnp.arange(n, dtype=F32)
    t = j / (n - 1)
    ang = (2.0 * math.pi * j / n)[:, None] * jnp.linspace(1e-4, HY_BANDS - 1, HY_BANDS, dtype=F32)
    z = jnp.concatenate([t[:, None], jnp.cos(ang), -jnp.sin(ang)], axis=-1)
    half = n // 2
    dist = jnp.abs(j - half) / half
    max_decay = math.log(HY_TARGET) / HY_FAST_DECAY
    min_decay = math.log(HY_TARGET) / HY_SLOW_DECAY
    deltas = jnp.abs(jnp.linspace(min_decay, max_decay, HY_W, dtype=F32))
    return jnp.pad(z, ((0, 0), (0, LANES - HY_EMB))), dist[:, None], deltas[None, :]


def _alt_rows(n):
    t = lax.broadcasted_iota(jnp.int32, (8, n), 1)
    return (1 - 2 * (t % 2)).astype(F32)


def _hy_filter_kernel(z_ref, dist_ref, delta_ref, w1_ref, b1_ref, fr_ref, w2_ref, b2_ref, w3_ref,
                      cos_ref, nsin_ref, hre_ref, him_ref, hny_ref, h_s):
    @pl.when(pl.program_id(1) == 0)
    def _():
        fr = fr_ref[...]
        hdn = jnp.sin(fr * (_dot3(z_ref[...], w1_ref[...]) + b1_ref[...]))
        hdn = jnp.sin(fr * (_dot3(hdn, w2_ref[...]) + b2_ref[...]))
        h = _dot3(hdn, w3_ref[...])
        h = h * jnp.exp(-dist_ref[...] * delta_ref[...])
        h = h / jnp.sum(jnp.abs(h), axis=0, keepdims=True)
        h_s[...] = h.astype(BF16)
        hny_ref[...] = _dot_exact_lhs(_alt_rows(h.shape[0]), h)

    hre_ref[...] = jnp.dot(cos_ref[...], h_s[...], preferred_element_type=F32)
    him_ref[...] = jnp.dot(nsin_ref[...], h_s[...], preferred_element_type=F32)


def _hyena_filter_spectrum(n, emb, tables, w1, b1, freq, w2, b2, w3):
    z, dist, deltas = emb
    pad_o = LANES - HY_ORDER
    w1p = jnp.pad(w1, ((0, LANES - HY_EMB), (0, pad_o)))
    w2p = jnp.pad(w2, ((0, pad_o), (0, pad_o)))
    w3p = jnp.pad(w3, ((0, pad_o), (0, 0)))
    row = lambda t: jnp.pad(t.reshape(1, HY_ORDER), ((0, 0), (0, pad_o)))
    tc = HY_W
    kt = min(n, 512)
    full = lambda shape: pl.BlockSpec(shape, lambda j, k: (0, 0))
    tab = pl.BlockSpec((kt, n), lambda j, k: (k, 0))
    spec = pl.BlockSpec((kt, tc), lambda j, k: (k, j))
    return pl.pallas_call(
        _hy_filter_kernel,
        grid=(HY_W // tc, n // kt),
        in_specs=[full((n, LANES)), full((n, 1)), pl.BlockSpec((1, tc), lambda j, k: (0, j)),
                  full((LANES, LANES)), full((1, LANES)), full((1, LANES)),
                  full((LANES, LANES)), full((1, LANES)), pl.BlockSpec((LANES, tc), lambda j, k: (0, j)),
                  tab, tab],
        out_specs=[spec, spec, pl.BlockSpec((8, tc), lambda j, k: (0, j))],
        out_shape=[jax.ShapeDtypeStruct((n, HY_W), F32), jax.ShapeDtypeStruct((n, HY_W), F32),
                   jax.ShapeDtypeStruct((8, HY_W), F32)],
        scratch_shapes=[pltpu.VMEM((n, tc), BF16)],
        compiler_params=_cparams(("arbitrary", "arbitrary")),
        name=f"hyena_filter_{n}",
    )(z, dist, deltas, w1p, row(b1), row(freq), w2p, row(b2), w3p, *tables)


def _hy_pre_kernel(x0_ref, x1_ref, v_ref, w0_ref, w1_ref, w2_ref, b0_ref, b1_ref, b2_ref, x0c_ref, gv_ref,
                   pad_s, x1_s, *, seq_len):
    _zero_conv_borders(pad_s)
    for base in range(0, x0_ref.shape[0], seq_len):
        def conv(x_ref, w_ref, b_ref, store):
            _centred_conv(pad_s, lambda r0, rb: x_ref[base + r0:base + r0 + rb, :].astype(F32), store, seq_len,
                          w_ref, HY_SHORT, b_ref[...])

        def store_x0(r0, y):
            x0c_ref[base + r0:base + r0 + y.shape[0], :] = y.astype(BF16)

        def store_x1(r0, y):
            x1_s[r0:r0 + y.shape[0], :] = y

        def store_gv(r0, y):
            gv_ref[base + r0:base + r0 + y.shape[0], :] = (y * x1_s[r0:r0 + y.shape[0], :]).astype(BF16)

        conv(x0_ref, w0_ref, b0_ref, store_x0)
        conv(x1_ref, w1_ref, b1_ref, store_x1)
        conv(v_ref, w2_ref, b2_ref, store_gv)


def _hyena_pre(p_main, conv_w, conv_b):
    outs = []
    for n_seq, seq_len, row_blk0, spb, tc in ((BATCH, SEQ, 0, 4, HY_W),
                                               (DEC_BATCH, DEC_SEQ, N_PROMPT // DEC_SEQ, 1, HY_W // 2)):
        ncb = HY_W // tc
        rows = spb * seq_len
        tok = lambda cb: pl.BlockSpec((rows, tc), lambda b, j, cb=cb: (row_blk0 + b, cb * LANES // tc + j))
        cw = lambda s: pl.BlockSpec((HY_SHORT, tc), lambda b, j, s=s: (0, s * ncb + j))
        cb_ = lambda s: pl.BlockSpec((1, tc), lambda b, j, s=s: (0, s * ncb + j))
        o_spec = pl.BlockSpec((rows, tc), lambda b, j: (b, j))
        o_shape = jax.ShapeDtypeStruct((n_seq * seq_len, HY_W), BF16)
        outs.append(pl.pallas_call(
            functools.partial(_hy_pre_kernel, seq_len=seq_len),
            grid=(n_seq // spb, ncb),
            in_specs=[tok(CB_X0), tok(CB_X1), tok(CB_HV), cw(0), cw(1), cw(2), cb_(0), cb_(1), cb_(2)],
            out_specs=[o_spec, o_spec], out_shape=[o_shape, o_shape],
            scratch_shapes=[pltpu.VMEM((seq_len + 2 * CONV_PAD, tc), F32), pltpu.VMEM((seq_len, tc), F32)],
            compiler_params=_cparams(("arbitrary", "arbitrary")),
            name=f"hyena_pre_{seq_len}",
        )(p_main, p_main, p_main, conv_w, conv_w, conv_w,
          conv_b.reshape(1, -1), conv_b.reshape(1, -1), conv_b.reshape(1, -1)))
    return outs


HY_FREQ_BLOCK = 512


def _hy_conv_kernel(gv_ref, x0_ref, d_ref, hre_ref, him_ref, hny_ref, cos_ref, nsin_ref, o_ref, *,
                    seq_len, n_seq):
    n = seq_len
    fb = min(n, HY_FREQ_BLOCK)
    chains = [(b, f0) for b in range(n_seq) for f0 in range(0, n, fb)]
    gvs = [gv_ref[pl.ds(b * seq_len, seq_len), :] for b in range(n_seq)]
    gre = [jnp.dot(cos_ref[f0:f0 + fb, :], gvs[b], preferred_element_type=F32) for b, f0 in chains]
    gim = [jnp.dot(nsin_ref[f0:f0 + fb, :], gvs[b], preferred_element_type=F32) for b, f0 in chains]
    spec = []
    for (b, f0), gr, gi in zip(chains, gre, gim):
        hre, him = hre_ref[f0:f0 + fb, :], him_ref[f0:f0 + fb, :]
        freq = lax.broadcasted_iota(jnp.int32, hre.shape, 0) + f0
        wk = jnp.where(freq == 0, 0.5 / n, 1.0 / n)
        q4 = freq % 4
        yre = (gr * hre - gi * him) * wk
        yim = (gr * him + gi * hre) * wk
        spec.append((jnp.where(q4 == 0, yre, jnp.where(q4 == 1, -yim, jnp.where(q4 == 2, -yre, yim))),
                     jnp.where(q4 == 0, yim, jnp.where(q4 == 1, yre, jnp.where(q4 == 2, -yim, -yre)))))
    inv = [_bdot(cos_ref[:, f0:f0 + fb], are) + _bdot(nsin_ref[:, f0:f0 + fb], aim)
           for (b, f0), (are, aim) in zip(chains, spec)]
    t = lax.broadcasted_iota(jnp.int32, (n, gvs[0].shape[1]), 0)
    alt = (1 - 2 * (t % 2)).astype(F32)
    for b in range(n_seq):
        rows = pl.ds(b * seq_len, seq_len)
        gvf = gvs[b].astype(F32)
        g_ny = jnp.sum(gvf * alt, axis=0, keepdims=True)
        y_ny = g_ny * hny_ref[0:1, :] * ((-1.0) ** (n // 2) / (2 * n))
        y = functools.reduce(lambda x, z: x + z, [p for (cb, _), p in zip(chains, inv) if cb == b])
        y = y + alt * y_ny + gvf * d_ref[...]
        o_ref[rows, :] = (y * x0_ref[rows, :].astype(F32)).astype(BF16)


def _hyena_conv(gv, x0c, d_skip, h_spec, tables, *, n_seq_total, seq_len, n_seq_blk):
    tc = 256
    hre, him, hny = h_spec
    cos_t, nsin_t = tables
    rows = n_seq_blk * seq_len
    grid = (HY_W // tc, n_seq_total // n_seq_blk)
    tok = pl.BlockSpec((rows, tc), lambda j, g: (g, j))
    hs = pl.BlockSpec((seq_len, tc), lambda j, g: (0, j))
    table = pl.BlockSpec((seq_len, seq_len), lambda j, g: (0, 0), pipeline_mode=pl.Buffered(1))
    return pl.pallas_call(
        functools.partial(_hy_conv_kernel, seq_len=seq_len, n_seq=n_seq_blk),
        grid=grid,
        in_specs=[tok, tok, pl.BlockSpec((1, tc), lambda j, g: (0, j)), hs, hs,
                  pl.BlockSpec((8, tc), lambda j, g: (0, j)), table, table],
        out_specs=tok,
        out_shape=jax.ShapeDtypeStruct((n_seq_total * seq_len, HY_W), BF16),
        compiler_params=_cparams(("arbitrary", "arbitrary")),
        name=f"hyena_conv_{seq_len}",
    )(gv, x0c, d_skip.reshape(1, HY_W), hre, him, hny, cos_t, nsin_t)


MERGE_TM = 512


def _merge_kernel(x_ref, oap_ref, obp_ref, ocp_ref, oas_ref, obs_ref, ocs_ref, ga_ref, gb_ref, gc_ref,
                  wa_ref, wb_ref, wc_ref, wo_ref, mod_ref, y_ref, wa_s, wb_s, wc_s, wo_s):
    i = pl.program_id(0)

    @pl.when(i == 0)
    def _():
        for src, dst in ((wa_ref, wa_s), (wb_ref, wb_s), (wc_ref, wc_s), (wo_ref, wo_s)):
            dst[...] = src[...].astype(BF16)

    is_prompt = i < N_PROMPT // MERGE_TM
    pick = lambda p_ref, s_ref: jnp.where(is_prompt, p_ref[...], s_ref[...])
    sig = lambda r: jax.nn.sigmoid(r[...].astype(F32))
    merged = (sig(ga_ref) * jnp.dot(pick(oap_ref, oas_ref), wa_s[...], preferred_element_type=F32)
              + sig(gb_ref) * jnp.dot(pick(obp_ref, obs_ref), wb_s[...], preferred_element_type=F32)
              + sig(gc_ref) * jnp.dot(pick(ocp_ref, ocs_ref), wc_s[...], preferred_element_type=F32))
    out = _bdot(merged, wo_s[...])
    y_ref[...] = x_ref[...] + mod_ref[0, 2:3, :] * out


def _merge(x, branches_p, branches_s, p_main, mods, w_a, w_b, w_c, w_out, layer):
    tm = MERGE_TM
    npt = N_PROMPT // tm
    tok = lambda w: pl.BlockSpec((tm, w), lambda i: (i, 0))
    tok_p = pl.BlockSpec((tm, DN_W), lambda i: (jnp.minimum(i, npt - 1), 0))
    tok_s = pl.BlockSpec((tm, DN_W), lambda i: (jnp.maximum(i - npt, 0), 0))
    gate = lambda s: pl.BlockSpec((tm, D_MODEL), lambda i, s=s: (i, CB_GATE // 8 + s))
    wfull = lambda r: pl.BlockSpec((None, r, D_MODEL), lambda i: (layer, 0, 0))
    wscr = lambda r: pltpu.VMEM((r, D_MODEL), BF16)
    return pl.pallas_call(
        _merge_kernel,
        grid=(N_TOK // tm,),
        in_specs=[tok(D_MODEL), tok_p, tok_p, tok_p, tok_s, tok_s, tok_s, gate(0), gate(1), gate(2),
                  wfull(DN_W), wfull(DN_W), wfull(HY_W), wfull(D_MODEL),
                  pl.BlockSpec((1, 6, D_MODEL), lambda i: (_mod_row(i, tm), 0, 0))],
        out_specs=tok(D_MODEL),
        out_shape=jax.ShapeDtypeStruct((N_TOK, D_MODEL), F32),
        scratch_shapes=[wscr(DN_W), wscr(DN_W), wscr(HY_W), wscr(D_MODEL)],
        compiler_params=_cparams(("arbitrary",)),
        name="merge_out_projection",
    )(x, *branches_p, *branches_s, p_main, p_main, p_main, w_a, w_b, w_c, w_out, mods)


FFN_TM = 512
FFN_TF = D_FF // 2


def _ffn_kernel(x_ref, g_ref, mod_ref, wg_ref, wu_ref, wo_ref, fg_ref, *rest, final):
    if final:
        yp_ref, ys_ref, h_scr, acc_scr = rest
    else:
        y_ref, h_scr, acc_scr = rest
    f = pl.program_id(1)
    nf = pl.num_programs(1)

    @pl.when(f == 0)
    def _():
        g = g_ref[...]
        sh = mod_ref[0, 3:4, :]
        sc1 = 1.0 + mod_ref[0, 4:5, :]

        def body(r, carry):
            rows = pl.ds(pl.multiple_of(r * ROW_CHUNK, ROW_CHUNK), ROW_CHUNK)
            h_scr[rows, :] = (_rms(x_ref[rows, :], g) * sc1 + sh).astype(BF16)
            return carry

        lax.fori_loop(0, FFN_TM // ROW_CHUNK, body, 0)

    h = h_scr[...]
    gate = jnp.dot(h, wg_ref[...], preferred_element_type=F32)
    up = jnp.dot(h, wu_ref[...], preferred_element_type=F32)
    part = _bdot(_silu(gate) * up, wo_ref[...])

    @pl.when(f == 0)
    def _():
        acc_scr[...] = part

    @pl.when(f > 0)
    def _():
        acc_scr[...] += part

    @pl.when(f == nf - 1)
    def _():
        y = x_ref[...] + mod_ref[0, 5:6, :] * acc_scr[...]
        if final:
            y = _rms(y, fg_ref[...])
            is_prompt = pl.program_id(0) < N_PROMPT // FFN_TM

            @pl.when(is_prompt)
            def _():
                yp_ref[...] = y

            @pl.when(jnp.logical_not(is_prompt))
            def _():
                ys_ref[...] = y
        else:
            y_ref[...] = y


def _ffn(x, norm_g, mods, w_in, w_out, final_g, layer, *, final):
    tm, tf = FFN_TM, FFN_TF
    nf = D_FF // tf
    if final:
        npt = N_PROMPT // tm
        out_specs = [pl.BlockSpec((tm, D_MODEL), lambda i, f: (jnp.minimum(i, npt - 1), 0)),
                     pl.BlockSpec((tm, D_MODEL), lambda i, f: (jnp.maximum(i - npt, 0), 0))]
        out_shape = [jax.ShapeDtypeStruct((N_PROMPT, D_MODEL), F32), jax.ShapeDtypeStruct((N_SAMPLE, D_MODEL), F32)]
    else:
        out_specs = pl.BlockSpec((tm, D_MODEL), lambda i, f: (i, 0))
        out_shape = jax.ShapeDtypeStruct((N_TOK, D_MODEL), F32)
    return pl.pallas_call(
        functools.partial(_ffn_kernel, final=final),
        grid=(N_TOK // tm, nf),
        in_specs=[pl.BlockSpec((tm, D_MODEL), lambda i, f: (i, 0)),
                  pl.BlockSpec((1, D_MODEL), lambda i, f: (0, 0)),
                  pl.BlockSpec((1, 6, D_MODEL), lambda i, f: (_mod_row(i, tm), 0, 0)),
                  pl.BlockSpec((None, D_MODEL, tf), lambda i, f: (layer, 0, f)),
                  pl.BlockSpec((None, D_MODEL, tf), lambda i, f: (layer, 0, nf + f)),
                  pl.BlockSpec((None, tf, D_MODEL), lambda i, f: (layer, f, 0)),
                  pl.BlockSpec((1, D_MODEL), lambda i, f: (0, 0))],
        out_specs=out_specs,
        out_shape=out_shape,
        scratch_shapes=[pltpu.VMEM((tm, D_MODEL), BF16), pltpu.VMEM((tm, D_MODEL), F32)],
        compiler_params=_cparams(("arbitrary", "arbitrary")),
        name="ffn",
    )(x, norm_g.reshape(1, D_MODEL), mods, w_in, w_in, w_out, final_g.reshape(1, D_MODEL))


def kernel(x_prompt, x_sample, cache_k, cache_v, state_dn, c, c_ctx, norm1_g, norm2_g, w_mod, b_mod,
           w_in, dn_conv_w, dn_a_log, dn_dt_bias, dn_norm_g, da_lambda, da_subln_g, hy_conv_w,
           hy_conv_b, hy_w1, hy_b1, hy_freq, hy_w2, hy_b2, hy_w3, hy_d, w_br_a, w_br_b, w_br_c,
           w_out, w_ffn_in, w_ffn_out, final_g):
    x = _join_streams(x_prompt.reshape(N_PROMPT, D_MODEL), x_sample.reshape(N_SAMPLE, D_MODEL))
    cond8 = jnp.concatenate([c_ctx[None, :], c, jnp.zeros((8 - 1 - DEC_BATCH, D_MODEL), F32)], axis=0)
    rope = _rope_tables(DEC_SEQ)
    tab_p = _dft_tables(SEQ)
    tab_s = _dft_tables(DEC_SEQ)
    emb_p, emb_s = _hy_embedding(SEQ), _hy_embedding(DEC_SEQ)
    sample_blk0 = N_PROMPT // DEC_SEQ
    w_main, w_small = _pack_w_in(w_in)
    w_ffn_in_b, w_ffn_out_b = w_ffn_in.astype(BF16), w_ffn_out.astype(BF16)

    new_k, new_v, new_s = [], [], []
    for l in range(DEPTH):
        mods = _modulation(cond8, w_mod, b_mod[l], l).reshape(8, 6, D_MODEL)
        p_main, p_small = _in_projection(x, norm1_g[l], mods, w_main, w_small, l)

        dn_args = (p_main, p_small, dn_conv_w[l], dn_a_log[l], dn_dt_bias[l], dn_norm_g[l])
        oa_p, s_fin = _deltanet(*dn_args, None, n_seq=BATCH, seq_len=SEQ, row_blk0=0, hps=4, layer=l)
        oa_s, _ = _deltanet(*dn_args, state_dn, n_seq=DEC_BATCH, seq_len=DEC_SEQ, row_blk0=sample_blk0,
                            hps=2, layer=l)

        ob_p, k_l, v_l = _diff_attention(p_main, da_lambda[l], da_subln_g[l], l, n_seq=BATCH, seq_len=SEQ,
                                         row_blk0=0, tq=SEQ)
        ob_s, _, _ = _diff_attention(p_main, da_lambda[l], da_subln_g[l], l, n_seq=DEC_BATCH,
                                     seq_len=DEC_SEQ, row_blk0=sample_blk0, tq=1024, rope=rope,
                                     ctx=(cache_k, cache_v))

        hy_w = (hy_w1[l], hy_b1[l], hy_freq[l], hy_w2[l], hy_b2[l], hy_w3[l])
        hspec_p = _hyena_filter_spectrum(SEQ, emb_p, tab_p, *hy_w)
        hspec_s = _hyena_filter_spectrum(DEC_SEQ, emb_s, tab_s, *hy_w)
        (x0_p, gv_p), (x0_s, gv_s) = _hyena_pre(p_main, hy_conv_w[l], hy_conv_b[l])
        oc_p = _hyena_conv(gv_p, x0_p, hy_d[l], hspec_p, tab_p, n_seq_total=BATCH, seq_len=SEQ,
                           n_seq_blk=BATCH)
        oc_s = _hyena_conv(gv_s, x0_s, hy_d[l], hspec_s, tab_s, n_seq_total=DEC_BATCH, seq_len=DEC_SEQ,
                           n_seq_blk=1)

        x = _merge(x, (oa_p, ob_p, oc_p), (oa_s, ob_s, oc_s), p_main, mods,
                   w_br_a, w_br_b, w_br_c, w_out, l)
        x = _ffn(x, norm2_g[l], mods, w_ffn_in_b, w_ffn_out_b, final_g, l, final=(l == DEPTH - 1))
        new_k.append(k_l)
        new_v.append(v_l)
        new_s.append(s_fin)

    y_prompt = x[0].reshape(BATCH, SEQ, D_MODEL)
    y_sample = x[1].reshape(DEC_BATCH, DEC_SEQ, D_MODEL)
    return (y_prompt, y_sample, jnp.stack(new_k, axis=1), jnp.stack(new_v, axis=1), jnp.stack(new_s, axis=1))
```

```python
import functools
import math

import jax
import jax.numpy as jnp
from jax import lax
from jax.experimental import pallas as pl
from jax.experimental.pallas import tpu as pltpu

F32 = jnp.float32
BF16 = jnp.bfloat16

D_MODEL = 1024
BATCH = 16
SEQ = 256
DEPTH = 2
DEC_BATCH = 4
DEC_SEQ = 2048
PAST_LEN = 256
GRID_W = 64
RMS_EPS = 1e-6
DN_HEADS = 4
DN_DK = 128
DN_DV = 128
DN_W = DN_HEADS * DN_DK
DN_CONV = 5
DN_CHUNK = 64
DA_HEADS = 4
DA_DH = 64
DA_DV = 2 * DA_DH
ROPE_BASE = 10000.0
HY_W = 512
HY_SHORT = 3
HY_BANDS = 8
HY_EMB = 1 + 2 * HY_BANDS
HY_ORDER = 64
HY_FAST_DECAY = 0.3
HY_SLOW_DECAY = 1.5
HY_TARGET = 1e-2
D_FF = ((8 * D_MODEL // 3 + 255) // 256) * 256

N_PROMPT = BATCH * SEQ
N_SAMPLE = DEC_BATCH * DEC_SEQ
N_TOK = N_PROMPT + N_SAMPLE
LANES = 128
N_MAIN = 8192
CB_QA, CB_KA, CB_VA, CB_ZA = 0, 4, 8, 12
CB_QB, CB_KB, CB_VB = 16, 20, 24
CB_X0, CB_X1, CB_HV = 28, 32, 36
CB_GATE = 40
VMEM_LIMIT = 56 * 1024 * 1024


def _cparams(sem):
    return pltpu.CompilerParams(dimension_semantics=sem, vmem_limit_bytes=VMEM_LIMIT)


def _bdot(a, b):
    return jnp.dot(a.astype(BF16), b.astype(BF16), preferred_element_type=F32)


def _bdot_nt(a, b):
    return lax.dot_general(a.astype(BF16), b.astype(BF16), (((1,), (1,)), ((), ())),
                           preferred_element_type=F32)


def _bdot_tn(a, b):
    return lax.dot_general(a.astype(BF16), b.astype(BF16), (((0,), (0,)), ((), ())),
                           preferred_element_type=F32)


def _split3(x):
    hi = x.astype(BF16)
    r = x - hi.astype(F32)
    mid = r.astype(BF16)
    lo = (r - mid.astype(F32)).astype(BF16)
    return hi, mid, lo


def _dot_exact_lhs(t, x):
    hi, mid, lo = _split3(x)
    tb = t.astype(BF16)
    d = lambda p: jnp.dot(tb, p, preferred_element_type=F32)
    return d(hi) + d(mid) + d(lo)


def _dot_exact_rhs(x, e):
    hi, mid, lo = _split3(x)
    eb = e.astype(BF16)
    d = lambda p: jnp.dot(p, eb, preferred_element_type=F32)
    return d(hi) + d(mid) + d(lo)


def _dot3(a, b):
    ah = a.astype(BF16)
    al = (a - ah.astype(F32)).astype(BF16)
    bh = b.astype(BF16)
    bl = (b - bh.astype(F32)).astype(BF16)
    d = lambda p, q: jnp.dot(p, q, preferred_element_type=F32)
    return d(ah, bh) + d(ah, bl) + d(al, bh)


def _silu(x):
    return x * jax.nn.sigmoid(x)


def _rms(x, g):
    return x * lax.rsqrt(jnp.mean(x * x, axis=-1, keepdims=True) + RMS_EPS) * g


def _mod_row(i, tm):
    n_prompt_tiles = N_PROMPT // tm
    tiles_per_seq = DEC_SEQ // tm
    return jnp.where(i < n_prompt_tiles, 0, 1 + (i - n_prompt_tiles) // tiles_per_seq)


def _mod_kernel(c_ref, w_ref, b_ref, o_ref):
    o_ref[...] = _bdot(_silu(c_ref[...]), w_ref[...]) + b_ref[...]


def _modulation(cond8, w_mod, b_mod, layer):
    n = 6 * D_MODEL
    tn = 1024
    return pl.pallas_call(
        _mod_kernel,
        grid=(n // tn,),
        in_specs=[pl.BlockSpec((8, D_MODEL), lambda j: (0, 0)),
                  pl.BlockSpec((None, D_MODEL, tn), lambda j: (layer, 0, j)),
                  pl.BlockSpec((1, tn), lambda j: (0, j))],
        out_specs=pl.BlockSpec((8, tn), lambda j: (0, j)),
        out_shape=jax.ShapeDtypeStruct((8, n), F32),
        compiler_params=_cparams(("arbitrary",)),
        name="modulation",
    )(cond8, w_mod, b_mod.reshape(1, n))


IN_TM = 2048
IN_TN = 2048
ROW_CHUNK = 256


def _inproj_kernel(x_ref, g_ref, mod_ref, wm_ref, ws_ref, p_ref, ps_ref, h_scr):
    @pl.when(pl.program_id(1) == 0)
    def _():
        g = g_ref[...]
        sh = mod_ref[0, 0:1, :]
        sc1 = 1.0 + mod_ref[0, 1:2, :]

        def body(r, carry):
            rows = pl.ds(pl.multiple_of(r * ROW_CHUNK, ROW_CHUNK), ROW_CHUNK)
            hb = (_rms(x_ref[rows, :], g) * sc1 + sh).astype(BF16)
            h_scr[rows, :] = hb
            ps_ref[rows, :] = jnp.dot(hb, ws_ref[...], preferred_element_type=F32)
            return carry

        lax.fori_loop(0, IN_TM // ROW_CHUNK, body, 0)

    p_ref[...] = jnp.dot(h_scr[...], wm_ref[...], preferred_element_type=F32).astype(BF16)


N_GATE_COLS = 4 * DN_HEADS
PACK_ROWS = 128
PACK_COLS = 512


def _pack_w_in_kernel(w_ref, main_ref, small_ref):
    lo = 3 * DN_W
    main_ref[:, :lo] = w_ref[:, :lo].astype(BF16)
    for c in range(lo, N_MAIN, PACK_COLS):
        main_ref[:, c:c + PACK_COLS] = w_ref[:, c + N_GATE_COLS:c + N_GATE_COLS + PACK_COLS].astype(BF16)
    gate = w_ref[:, lo:lo + LANES]
    lane = lax.broadcasted_iota(jnp.int32, gate.shape, 1)
    small_ref[...] = jnp.where(lane < N_GATE_COLS, gate, 0.0).astype(BF16)


def _pack_w_in(w_in):
    n_in = w_in.shape[2]
    return pl.pallas_call(
        _pack_w_in_kernel,
        grid=(DEPTH, D_MODEL // PACK_ROWS),
        in_specs=[pl.BlockSpec((None, PACK_ROWS, n_in), lambda l, r: (l, r, 0))],
        out_specs=[pl.BlockSpec((None, PACK_ROWS, N_MAIN), lambda l, r: (l, r, 0)),
                   pl.BlockSpec((None, PACK_ROWS, LANES), lambda l, r: (l, r, 0))],
        out_shape=[jax.ShapeDtypeStruct((DEPTH, D_MODEL, N_MAIN), BF16),
                   jax.ShapeDtypeStruct((DEPTH, D_MODEL, LANES), BF16)],
        compiler_params=_cparams(("arbitrary", "arbitrary")),
        name="pack_w_in",
    )(w_in)


def _in_projection(x, norm_g, mods, w_main, w_small, layer):
    grid = (N_TOK // IN_TM, N_MAIN // IN_TN)
    return pl.pallas_call(
        _inproj_kernel,
        grid=grid,
        in_specs=[pl.BlockSpec((IN_TM, D_MODEL), lambda i, j: (i, 0)),
                  pl.BlockSpec((1, D_MODEL), lambda i, j: (0, 0)),
                  pl.BlockSpec((1, 6, D_MODEL), lambda i, j: (_mod_row(i, IN_TM), 0, 0)),
                  pl.BlockSpec((None, D_MODEL, IN_TN), lambda i, j: (layer, 0, j)),
                  pl.BlockSpec((None, D_MODEL, LANES), lambda i, j: (layer, 0, 0))],
        out_specs=[pl.BlockSpec((IN_TM, IN_TN), lambda i, j: (i, j)),
                   pl.BlockSpec((IN_TM, LANES), lambda i, j: (i, 0))],
        out_shape=[jax.ShapeDtypeStruct((N_TOK, N_MAIN), BF16),
                   jax.ShapeDtypeStruct((N_TOK, LANES), F32)],
        scratch_shapes=[pltpu.VMEM((IN_TM, D_MODEL), BF16)],
        compiler_params=_cparams(("arbitrary", "arbitrary")),
        name="in_projection",
    )(x, norm_g.reshape(1, D_MODEL), mods, w_main, w_small)


DN_UNIT = 2 * DN_CHUNK
DN_GROUP = 8


CONV_PAD = 8


def _zero_conv_borders(pad_ref):
    n = pad_ref.shape[0] - 2 * CONV_PAD
    zeros = jnp.zeros((CONV_PAD, pad_ref.shape[1]), F32)
    pad_ref[0:CONV_PAD, :] = zeros
    pad_ref[CONV_PAD + n:, :] = zeros


CONV_ROWS = 512


def _centred_conv(pad_ref, load, store, n, w_ref, n_taps, bias=None):
    rb = min(n, CONV_ROWS)
    half = n_taps // 2
    for r0 in range(0, n, rb):
        pad_ref[CONV_PAD + r0:CONV_PAD + r0 + rb, :] = load(r0, rb)
    for r0 in range(0, n, rb):
        acc = None
        for tap in range(n_taps):
            lo = CONV_PAD + r0 + tap - half
            term = pad_ref[lo:lo + rb, :] * w_ref[tap:tap + 1, :]
            acc = term if acc is None else acc + term
        store(r0, acc if bias is None else acc + bias)


def _l2norm(x):
    return x * lax.rsqrt(jnp.sum(x * x, axis=-1, keepdims=True) + RMS_EPS)


def _softplus(x):
    return jnp.maximum(x, 0.0) + jnp.log1p(jnp.exp(-jnp.abs(x)))


DN_BASE = 16


def _unit_tri_inverse(a_list, ri, ci):
    eye = (ri == ci).astype(F32)
    blk = lambda b: (ri // b) == (ci // b)
    y = [jnp.where(blk(DN_BASE), -a, 0.0) for a in a_list]
    p = [eye + yi for yi in y]
    for _ in range(3):
        y = [_bdot(yi, yi) for yi in y]
        p = [pi + _bdot(yi, pi) for yi, pi in zip(y, p)]
    b = DN_BASE
    while b < DN_CHUNK:
        off_mask = blk(2 * b) & ~blk(b)
        t = [_bdot(jnp.where(off_mask, a, 0.0), pi) for a, pi in zip(a_list, p)]
        p = [pi - _bdot(pi, ti) for pi, ti in zip(p, t)]
        b *= 2
    return p


def _dn_units(chains):
    u, c = DN_UNIT, DN_CHUNK
    ri = lax.broadcasted_iota(jnp.int32, (u, u), 0)
    ci = lax.broadcasted_iota(jnp.int32, (u, u), 1)
    same = (ri // c) == (ci // c)
    eye_mask = ri == ci
    hi_rows = ri >= c
    incl_of = {False: same & (ri >= ci), True: same & (ri <= ci)}
    strict_of = {False: same & (ri > ci), True: same & (ri < ci)}
    incl = [incl_of[ch['backward']] for ch in chains]
    strict = [strict_of[ch['backward']] for ch in chains]
    gc = [_dot_exact_lhs(m.astype(F32), ch['g']) for m, ch in zip(incl, chains)]
    g_tot = [(x[0:1, :], x[c:c + 1, :]) if ch['backward'] else (x[c - 1:c, :], x[u - 1:u, :])
             for x, ch in zip(gc, chains)]
    gc_row = [jnp.sum(jnp.where(eye_mask, x, 0.0), axis=0, keepdims=True) for x in gc]
    dec = [jnp.exp(jnp.where(m, x - xr, -1e30)) for m, x, xr in zip(incl, gc, gc_row)]
    e_gc = [jnp.exp(x) for x in gc]
    a = [jnp.where(m, ch['kk'] * ch['beta'] * d, 0.0) for m, ch, d in zip(strict, chains, dec)]
    qk = [ch['qk_raw'] * d for ch, d in zip(chains, dec)]
    eye = eye_mask.astype(F32)
    r = [p - eye for p in _unit_tri_inverse(a, ri, ci)]
    rhs = [jnp.concatenate([ch['v'] * ch['beta'], ch['k'] * (ch['beta'] * e)], axis=1)
           for ch, e in zip(chains, e_gc)]
    uwb = [(x + _bdot(ri_, x)).astype(BF16) for ri_, x in zip(r, rhs)]
    qkuw = [_bdot(x, y) for x, y in zip(qk, uwb)]
    kd = [ch['k'] * jnp.exp(jnp.where(hi_rows, gt[1], gt[0]) - x) for ch, gt, x in zip(chains, g_tot, gc)]
    tp = [[_bdot_tn(jnp.where(keep, x, 0.0), y) for x, y in zip(kd, uwb)]
          for keep in (~hi_rows, hi_rows)]
    out = []
    for i, ch in enumerate(chains):
        o_local = qkuw[i][:, :DN_DV]
        q_eff = ch['q'] * e_gc[i] - qkuw[i][:, DN_DV:]
        per_chunk = [(-tp[h][i][:, DN_DV:], tp[h][i][:, :DN_DV], jnp.exp(g_tot[i][h])) for h in range(2)]
        out.append((o_local, q_eff, per_chunk))
    return out


def _dn_kernel(*refs, seq_len, has_state, hps, layer):
    if has_state:
        (q_ref, k_ref, v_ref, z_ref, ps_ref, cwq_ref, cwk_ref, cwv_ref, alog_ref, dtb_ref, ng_ref,
         s0_ref, o_ref, q_s, k_s, v_s, bt_s, g_s, oacc_s, qe_s, th_s, psi_s, egl_s, st_s, pad_s) = refs
        sfin_ref = None
    else:
        refs = refs[:11] + refs[(11 if layer == 0 else 12):]
        (q_ref, k_ref, v_ref, z_ref, ps_ref, cwq_ref, cwk_ref, cwv_ref, alog_ref, dtb_ref, ng_ref,
         o_ref, sfin_ref, q_s, k_s, v_s, bt_s, g_s, oacc_s, qe_s, th_s, psi_s, egl_s, st_s, pad_s) = refs
        s0_ref = None
    n_units = seq_len // DN_UNIT
    n_chunks = seq_len // DN_CHUNK
    c = DN_CHUNK
    sel_r = lax.broadcasted_iota(jnp.int32, (LANES, 4 * LANES), 0)
    sel_blk = lax.broadcasted_iota(jnp.int32, (LANES, 4 * LANES), 1) // LANES
    sels = []
    for j in range(hps):
        head = pl.program_id(1) * hps + j
        src_col = head + DN_HEADS * jnp.where(sel_blk == 0, 0, jnp.where(sel_blk == 1, 2, jnp.where(sel_blk == 2, 1, 3)))
        sels.append((sel_r == src_col).astype(BF16))

    def gate_rows(r, carry):
        rows = pl.ds(pl.multiple_of(r * ROW_CHUNK, ROW_CHUNK), ROW_CHUNK)
        ps = ps_ref[rows, :]
        lane = lax.broadcasted_iota(jnp.int32, ps.shape, 1)
        gate_cols = jnp.where(lane < 2 * DN_HEADS, jax.nn.sigmoid(ps),
                              -(jnp.exp(alog_ref[...]) * _softplus(ps + dtb_ref[...])))
        parts = _split3(gate_cols)
        for j in range(hps):
            ext = functools.reduce(lambda a, b: a + b,
                                   [jnp.dot(part, sels[j], preferred_element_type=F32) for part in parts])
            for d in range(2):
                bt_s[j, d, rows, :] = ext[:, (2 * d) * LANES:(2 * d + 1) * LANES]
                g_s[j, d, rows, :] = ext[:, (2 * d + 1) * LANES:(2 * d + 2) * LANES]
        return carry

    lax.fori_loop(0, seq_len // ROW_CHUNK, gate_rows, 0)

    _zero_conv_borders(pad_s)
    for j in range(hps):
        cols = slice(j * LANES, (j + 1) * LANES)
        for x_ref, w_ref, out_s, post in (
                (q_ref, cwq_ref, q_s, lambda y: _l2norm(_silu(y)) * (DN_DK ** -0.5)),
                (k_ref, cwk_ref, k_s, lambda y: _l2norm(_silu(y))),
                (v_ref, cwv_ref, v_s, _silu)):
            def store(r0, y, out_s=out_s, post=post):
                out_s[j, r0:r0 + y.shape[0], :] = post(y)
            _centred_conv(pad_s, lambda r0, rb, x_ref=x_ref: x_ref[r0:r0 + rb, cols].astype(F32), store,
                          seq_len, w_ref.at[:, cols], DN_CONV)
        for d in range(2):
            st_s[j, d] = s0_ref[0, 0, d, j] if has_state else jnp.zeros((DN_DK, DN_DV), F32)

    def unit_group(grp, carry):
        where, chains = [], []
        for t in range(DN_GROUP):
            idx = grp * DN_GROUP + t
            j = idx // n_units
            n = idx % n_units
            rows = pl.ds(pl.multiple_of(n * DN_UNIT, DN_UNIT), DN_UNIT)
            qc, kc, vc = q_s[j, rows, :], k_s[j, rows, :], v_s[j, rows, :]
            kcb = kc.astype(BF16)
            kk = _bdot_nt(kcb, kcb)
            qk_raw = _bdot_nt(qc, kcb)
            for d in range(2):
                where.append((j, n, rows, d))
                chains.append(dict(q=qc, k=kc, v=vc, kk=kk, qk_raw=qk_raw, beta=bt_s[j, d, rows, :],
                                   g=g_s[j, d, rows, :], backward=(d == 1)))
        results = _dn_units(chains)
        for (j, n, rows, d), (o_loc, q_eff, per_chunk) in zip(where, results):
            qe_s[j, d, rows, :] = q_eff.astype(BF16)
            for half, (theta, psi, egl) in enumerate(per_chunk):
                th_s[j, d, 2 * n + half] = theta.astype(BF16)
                psi_s[j, d, 2 * n + half] = psi
                egl_s[j, d, 2 * n + half] = egl
            if d == 0:
                o_fwd = o_loc
            else:
                oacc_s[j, rows, :] = o_fwd + o_loc
        return carry

    lax.fori_loop(0, hps * n_units // DN_GROUP, unit_group, 0)

    def scan_step(i, carry):
        for j in range(hps):
            for d, n in ((0, i), (1, n_chunks - 1 - i)):
                rows = pl.ds(pl.multiple_of(n * c, c), c)
                s = st_s[j, d]
                sb = s.astype(BF16)
                oacc_s[j, rows, :] += jnp.dot(qe_s[j, d, rows, :], sb, preferred_element_type=F32)
                st_s[j, d] = (egl_s[j, d, n] * s + jnp.dot(th_s[j, d, n], sb, preferred_element_type=F32)
                              + psi_s[j, d, n])
        return carry

    lax.fori_loop(0, n_chunks, scan_step, 0)

    if sfin_ref is not None:
        for j in range(hps):
            for d in range(2):
                sfin_ref[0, 0, d, j] = st_s[j, d]
                if layer == 0:
                    for later in range(1, DEPTH):
                        sfin_ref[0, later, d, j] = jnp.zeros((DN_DK, DN_DV), F32)

    def out_rows(r, carry):
        rows = pl.ds(pl.multiple_of(r * ROW_CHUNK, ROW_CHUNK), ROW_CHUNK)
        for j in range(hps):
            cols = slice(j * LANES, (j + 1) * LANES)
            o = _rms(oacc_s[j, rows, :], ng_ref[...]) * _silu(z_ref[rows, cols].astype(F32))
            o_ref[rows, cols] = o.astype(BF16)
        return carry

    lax.fori_loop(0, seq_len // ROW_CHUNK, out_rows, 0)


def _deltanet(p_main, p_small, conv_w, a_log, dt_bias, norm_g, s0, *, n_seq, seq_len, row_blk0, hps, layer,
              states_so_far=None):
    has_state = s0 is not None
    n_chunks = seq_len // DN_CHUNK
    width = hps * LANES
    tok = lambda cb: pl.BlockSpec((seq_len, width), lambda b, h: (row_blk0 + b, cb // hps + h))
    cw = lambda cb: pl.BlockSpec((DN_CONV, width), lambda b, h: (0, cb // hps + h))
    in_specs = [tok(CB_QA), tok(CB_KA), tok(CB_VA), tok(CB_ZA),
                pl.BlockSpec((seq_len, LANES), lambda b, h: (row_blk0 + b, 0)),
                cw(0), cw(DN_HEADS), cw(2 * DN_HEADS),
                pl.BlockSpec((1, LANES), lambda b, h: (0, 0)),
                pl.BlockSpec((1, LANES), lambda b, h: (0, 0)),
                pl.BlockSpec((1, DN_DV), lambda b, h: (0, 0))]
    pad8 = lambda t: jnp.pad(t.reshape(1, 2 * DN_HEADS), ((0, 0), (2 * DN_HEADS, LANES - 4 * DN_HEADS)))
    args = [p_main, p_main, p_main, p_main, p_small, conv_w, conv_w, conv_w,
            pad8(a_log), pad8(dt_bias), norm_g.reshape(1, DN_DV)]
    o_spec = pl.BlockSpec((seq_len, width), lambda b, h: (b, h))
    o_shape = jax.ShapeDtypeStruct((n_seq * seq_len, DN_W), BF16)
    aliases = {}
    if has_state:
        in_specs.append(pl.BlockSpec((1, 1, 2, hps, DN_DK, DN_DV), lambda b, h: (b, layer, 0, h, 0, 0)))
        args.append(s0)
        out_specs, out_shape = o_spec, o_shape
    else:
        if layer == 0:
            s_spec = pl.BlockSpec((1, DEPTH, 2, hps, DN_DK, DN_DV), lambda b, h: (b, 0, 0, h, 0, 0))
        else:
            s_spec = pl.BlockSpec((1, 1, 2, hps, DN_DK, DN_DV), lambda b, h: (b, layer, 0, h, 0, 0))
            in_specs.append(pl.BlockSpec(memory_space=pl.ANY))
            args.append(states_so_far)
            aliases = {len(args) - 1: 1}
        out_specs = [o_spec, s_spec]
        out_shape = [o_shape, jax.ShapeDtypeStruct((n_seq, DEPTH, 2, DN_HEADS, DN_DK, DN_DV), F32)]
    scratch = [pltpu.VMEM((hps, seq_len, LANES), F32),
               pltpu.VMEM((hps, seq_len, LANES), F32),
               pltpu.VMEM((hps, seq_len, LANES), F32),
               pltpu.VMEM((hps, 2, seq_len, LANES), F32),
               pltpu.VMEM((hps, 2, seq_len, LANES), F32),
               pltpu.VMEM((hps, seq_len, LANES), F32),
               pltpu.VMEM((hps, 2, seq_len, LANES), BF16),
               pltpu.VMEM((hps, 2, n_chunks, DN_DK, DN_DK), BF16),
               pltpu.VMEM((hps, 2, n_chunks, DN_DK, DN_DV), F32),
               pltpu.VMEM((hps, 2, n_chunks, 1, LANES), F32),
               pltpu.VMEM((hps, 2, DN_DK, DN_DV), F32),
               pltpu.VMEM((seq_len + 2 * CONV_PAD, LANES), F32)]
    res = pl.pallas_call(
        functools.partial(_dn_kernel, seq_len=seq_len, has_state=has_state, hps=hps, layer=layer),
        grid=(n_seq, DN_HEADS // hps),
        in_specs=in_specs, out_specs=out_specs, out_shape=out_shape,
        input_output_aliases=aliases,
        scratch_shapes=scratch,
        compiler_params=_cparams(("arbitrary", "arbitrary")),
        name=f"deltanet_{seq_len}",
    )(*args)
    return (res, None) if has_state else (res[0], res[1])


def _rope(x, cos, sin_signed):
    lane = lax.broadcasted_iota(jnp.int32, x.shape, 1)
    partner = jnp.where((lane % 32) < 16, pltpu.roll(x, LANES - 16, axis=1), pltpu.roll(x, 16, axis=1))
    return x * cos + partner * sin_signed


DA_ROWS = 128


def _exp2_rows(s_parts):
    m = functools.reduce(jnp.maximum, [jnp.max(s, axis=-1, keepdims=True) for s in s_parts])
    return [jnp.exp2(s - m).astype(BF16) for s in s_parts]


def _da_kernel(*refs, layer, latent):
    if latent:
        (q_ref, k_ref, v_ref, lam_ref, sg_ref, cq_ref, sq_ref, ck_ref, sk_ref, ctxk_ref, ctxv_ref,
         o_ref, krot_s, vext_s) = refs
    else:
        refs = refs[:5] + refs[(5 if layer == 0 else 7):]
        q_ref, k_ref, v_ref, lam_ref, sg_ref, o_ref, ko_ref, vo_ref = refs
    with_ones = lambda v: jnp.concatenate([v, jnp.ones(v.shape, BF16)], axis=1)
    lam_init = 0.8 - 0.6 * math.exp(-0.3 * layer)
    lp = lam_ref[...]
    dots = jnp.sum(jnp.concatenate([lp[0:1] * lp[1:2], lp[2:3] * lp[3:4]], axis=0), axis=1, keepdims=True)
    e = jnp.exp(dots)
    lam = e[0:1, :] - e[1:2, :] + lam_init

    n_heads = q_ref.shape[1] // LANES
    tq = q_ref.shape[0]
    heads = []
    if latent:
        @pl.when(pl.program_id(2) == 0)
        def _():
            krot_s[...] = _rope(k_ref[...].astype(F32), ck_ref[...], sk_ref[...]).astype(BF16)
            vext_s[...] = with_ones(v_ref[...])
        heads.append((_rope(q_ref[...].astype(F32), cq_ref[...], sq_ref[...]),
                      [ctxk_ref[0, 0, 0].astype(BF16), krot_s[...]],
                      [with_ones(ctxv_ref[0, 0, 0].astype(BF16)), vext_s[...]]))
    else:
        for j in range(n_heads):
            cols = slice(j * LANES, (j + 1) * LANES)
            heads.append((q_ref[:, cols].astype(F32), [k_ref[:, cols]], [with_ones(v_ref[:, cols])]))
            ko_ref[0, 0, j] = k_ref[:, cols].astype(F32)
            vo_ref[0, 0, j] = v_ref[:, cols].astype(F32)
            if layer == 0:
                for later in range(1, DEPTH):
                    ko_ref[0, later, j] = jnp.zeros((tq, LANES), F32)
                    vo_ref[0, later, j] = jnp.zeros((tq, LANES), F32)
    lane = lax.broadcasted_iota(jnp.int32, (tq, LANES), 1)
    groups = []
    for q, keys, vals in heads:
        q = q * (DA_DH ** -0.5 * math.log2(math.e))
        q12 = jnp.concatenate([jnp.where(lane < DA_DH, q, 0.0), jnp.where(lane >= DA_DH, q, 0.0)],
                              axis=0).astype(BF16)
        groups += [(q12[r:r + DA_ROWS], keys, vals) for r in range(0, 2 * tq, DA_ROWS)]
    scores = [[_bdot_nt(qg, kk) for kk in keys] for qg, keys, _ in groups]
    soft = [_exp2_rows(sg) for sg in scores]
    pv = []
    for e_parts, (_, _, vals) in zip(soft, groups):
        acc = None
        for e, vv in zip(e_parts, vals):
            part = _bdot(e, vv)
            acc = part if acc is None else acc + part
        pv.append(acc[:, :DA_DV] * (1.0 / acc[:, DA_DV:]))
    per_head = 2 * tq // DA_ROWS
    for j in range(len(heads)):
        hp = jnp.concatenate(pv[j * per_head:(j + 1) * per_head], axis=0)
        o = hp[:tq] - lam * hp[tq:]
        o_ref[:, j * LANES:(j + 1) * LANES] = (_rms(o, sg_ref[...]) * (1.0 - lam_init)).astype(BF16)


def _diff_attention(p_main, lam_p, subln_g, layer, *, n_seq, seq_len, row_blk0, tq, rope=None, ctx=None,
                    caches_so_far=None):
    latent = ctx is not None
    aliases = {}
    hps = 1 if latent else DA_HEADS
    width = hps * LANES
    nq = seq_len // tq
    qpb = seq_len // tq
    in_specs = [pl.BlockSpec((tq, width), lambda b, h, i: ((row_blk0 + b) * qpb + i, CB_QB // hps + h)),
                pl.BlockSpec((seq_len, width), lambda b, h, i: (row_blk0 + b, CB_KB // hps + h)),
                pl.BlockSpec((seq_len, width), lambda b, h, i: (row_blk0 + b, CB_VB // hps + h)),
                pl.BlockSpec((4, LANES), lambda b, h, i: (0, 0)),
                pl.BlockSpec((1, DA_DV), lambda b, h, i: (0, 0))]
    args = [p_main, p_main, p_main, jnp.pad(lam_p, ((0, 0), (0, LANES - DA_DH))), subln_g.reshape(1, DA_DV)]
    o_spec = pl.BlockSpec((tq, width), lambda b, h, i: (b * qpb + i, h))
    o_shape = jax.ShapeDtypeStruct((n_seq * seq_len, DA_HEADS * DA_DV), BF16)
    scratch = []
    if latent:
        cos, sin_signed = rope
        ctx_k, ctx_v = ctx
        n_ctx = ctx_k.shape[3]
        in_specs += [pl.BlockSpec((tq, LANES), lambda b, h, i: (i, 0)),
                     pl.BlockSpec((tq, LANES), lambda b, h, i: (i, 0)),
                     pl.BlockSpec((seq_len, LANES), lambda b, h, i: (0, 0)),
                     pl.BlockSpec((seq_len, LANES), lambda b, h, i: (0, 0)),
                     pl.BlockSpec((1, 1, 1, n_ctx, DA_DV), lambda b, h, i: (b, layer, h, 0, 0)),
                     pl.BlockSpec((1, 1, 1, n_ctx, DA_DV), lambda b, h, i: (b, layer, h, 0, 0))]
        args += [cos, sin_signed, cos, sin_signed, ctx_k, ctx_v]
        out_specs, out_shape = o_spec, o_shape
        scratch = [pltpu.VMEM((seq_len, LANES), BF16), pltpu.VMEM((seq_len, 2 * DA_DV), BF16)]
    else:
        if layer == 0:
            kv_spec = pl.BlockSpec((1, DEPTH, hps, seq_len, DA_DV), lambda b, h, i: (b, 0, h, 0, 0))
        else:
            kv_spec = pl.BlockSpec((1, 1, hps, seq_len, DA_DV), lambda b, h, i: (b, layer, h, 0, 0))
            in_specs += [pl.BlockSpec(memory_space=pl.ANY), pl.BlockSpec(memory_space=pl.ANY)]
            args += list(caches_so_far)
            aliases = {len(args) - 2: 1, len(args) - 1: 2}
        kv_shape = jax.ShapeDtypeStruct((n_seq, DEPTH, DA_HEADS, seq_len, DA_DV), F32)
        out_specs = [o_spec, kv_spec, kv_spec]
        out_shape = [o_shape, kv_shape, kv_shape]
    res = pl.pallas_call(
        functools.partial(_da_kernel, layer=layer, latent=latent),
        grid=(n_seq, DA_HEADS // hps, nq),
        in_specs=in_specs, out_specs=out_specs, out_shape=out_shape,
        input_output_aliases=aliases,
        scratch_shapes=scratch,
        compiler_params=_cparams(("arbitrary", "arbitrary", "arbitrary")),
        name=f"diff_attention_{seq_len}",
    )(*args)
    return (res, None, None) if latent else tuple(res)


def _rope_tables(n_tok):
    half = DA_DH // 2
    inv = ROPE_BASE ** (-jnp.arange(0, half, 2, dtype=F32) / half)
    t = jnp.arange(n_tok)
    ang_r = (t // GRID_W).astype(F32)[:, None] * inv
    ang_c = (t % GRID_W).astype(F32)[:, None] * inv
    cos32 = lambda a: jnp.concatenate([jnp.cos(a), jnp.cos(a)], axis=-1)
    sin32 = lambda a: jnp.concatenate([-jnp.sin(a), jnp.sin(a)], axis=-1)
    cos = jnp.concatenate([cos32(ang_r), cos32(ang_c)] * 2, axis=-1)
    sin_signed = jnp.concatenate([sin32(ang_r), sin32(ang_c)] * 2, axis=-1)
    return cos, sin_signed


def _dft_tables(n):
    t_lo = 64
    k = jnp.arange(n, dtype=jnp.int32)[:, None]
    ang = lambda m: ((k * m[None, :]) % (2 * n)).astype(F32) * (math.pi / n)
    a = ang(t_lo * jnp.arange(n // t_lo, dtype=jnp.int32))
    b = ang(jnp.arange(t_lo, dtype=jnp.int32))
    ca, sa, cb, sb = jnp.cos(a)[:, :, None], jnp.sin(a)[:, :, None], jnp.cos(b)[:, None, :], jnp.sin(b)[:, None, :]
    cos_t = (ca * cb - sa * sb).reshape(n, n)
    nsin_t = -(sa * cb + ca * sb).reshape(n, n)
    return cos_t.astype(BF16), nsin_t.astype(BF16)


def _hy_embedding(n):
    j = jnp.arange(n, dtype=F32)
    t = j / (n - 1)
    ang = (2.0 * math.pi * j / n)[:, None] * jnp.linspace(1e-4, HY_BANDS - 1, HY_BANDS, dtype=F32)
    z = jnp.concatenate([t[:, None], jnp.cos(ang), -jnp.sin(ang)], axis=-1)
    half = n // 2
    dist = jnp.abs(j - half) / half
    max_decay = math.log(HY_TARGET) / HY_FAST_DECAY
    min_decay = math.log(HY_TARGET) / HY_SLOW_DECAY
    deltas = jnp.abs(jnp.linspace(min_decay, max_decay, HY_W, dtype=F32))
    return jnp.pad(z, ((0, 0), (0, LANES - HY_EMB))), dist[:, None], deltas[None, :]


def _alt_rows(n):
    t = lax.broadcasted_iota(jnp.int32, (8, n), 1)
    return (1 - 2 * (t % 2)).astype(F32)


def _hy_filter_kernel(z_ref, dist_ref, delta_ref, w1_ref, b1_ref, fr_ref, w2_ref, b2_ref, w3_ref,
                      cos_ref, nsin_ref, hre_ref, him_ref, hny_ref, h_s):
    @pl.when(pl.program_id(1) == 0)
    def _():
        fr = fr_ref[...]
        hdn = jnp.sin(fr * (_dot3(z_ref[...], w1_ref[...]) + b1_ref[...]))
        hdn = jnp.sin(fr * (_dot3(hdn, w2_ref[...]) + b2_ref[...]))
        h = _dot3(hdn, w3_ref[...])
        h = h * jnp.exp(-dist_ref[...] * delta_ref[...])
        h = h / jnp.sum(jnp.abs(h), axis=0, keepdims=True)
        h_s[...] = h.astype(BF16)
        hny_ref[...] = _dot_exact_lhs(_alt_rows(h.shape[0]), h)

    hre_ref[...] = jnp.dot(cos_ref[...], h_s[...], preferred_element_type=F32)
    him_ref[...] = jnp.dot(nsin_ref[...], h_s[...], preferred_element_type=F32)


def _hyena_filter_spectrum(n, emb, tables, w1, b1, freq, w2, b2, w3):
    z, dist, deltas = emb
    pad_o = LANES - HY_ORDER
    w1p = jnp.pad(w1, ((0, LANES - HY_EMB), (0, pad_o)))
    w2p = jnp.pad(w2, ((0, pad_o), (0, pad_o)))
    w3p = jnp.pad(w3, ((0, pad_o), (0, 0)))
    row = lambda t: jnp.pad(t.reshape(1, HY_ORDER), ((0, 0), (0, pad_o)))
    tc = HY_W
    kt = min(n, 512)
    full = lambda shape: pl.BlockSpec(shape, lambda j, k: (0, 0))
    tab = pl.BlockSpec((kt, n), lambda j, k: (k, 0))
    spec = pl.BlockSpec((kt, tc), lambda j, k: (k, j))
    return pl.pallas_call(
        _hy_filter_kernel,
        grid=(HY_W // tc, n // kt),
        in_specs=[full((n, LANES)), full((n, 1)), pl.BlockSpec((1, tc), lambda j, k: (0, j)),
                  full((LANES, LANES)), full((1, LANES)), full((1, LANES)),
                  full((LANES, LANES)), full((1, LANES)), pl.BlockSpec((LANES, tc), lambda j, k: (0, j)),
                  tab, tab],
        out_specs=[spec, spec, pl.BlockSpec((8, tc), lambda j, k: (0, j))],
        out_shape=[jax.ShapeDtypeStruct((n, HY_W), F32), jax.ShapeDtypeStruct((n, HY_W), F32),
                   jax.ShapeDtypeStruct((8, HY_W), F32)],
        scratch_shapes=[pltpu.VMEM((n, tc), BF16)],
        compiler_params=_cparams(("arbitrary", "arbitrary")),
        name=f"hyena_filter_{n}",
    )(z, dist, deltas, w1p, row(b1), row(freq), w2p, row(b2), w3p, *tables)


def _hy_pre_kernel(x0_ref, x1_ref, v_ref, w0_ref, w1_ref, w2_ref, b0_ref, b1_ref, b2_ref, x0c_ref, gv_ref,
                   pad_s, x1_s, *, seq_len):
    _zero_conv_borders(pad_s)
    for base in range(0, x0_ref.shape[0], seq_len):
        def conv(x_ref, w_ref, b_ref, store):
            _centred_conv(pad_s, lambda r0, rb: x_ref[base + r0:base + r0 + rb, :].astype(F32), store, seq_len,
                          w_ref, HY_SHORT, b_ref[...])

        def store_x0(r0, y):
            x0c_ref[base + r0:base + r0 + y.shape[0], :] = y.astype(BF16)

        def store_x1(r0, y):
            x1_s[r0:r0 + y.shape[0], :] = y

        def store_gv(r0, y):
            gv_ref[base + r0:base + r0 + y.shape[0], :] = (y * x1_s[r0:r0 + y.shape[0], :]).astype(BF16)

        conv(x0_ref, w0_ref, b0_ref, store_x0)
        conv(x1_ref, w1_ref, b1_ref, store_x1)
        conv(v_ref, w2_ref, b2_ref, store_gv)


def _hyena_pre(p_main, conv_w, conv_b):
    outs = []
    for n_seq, seq_len, row_blk0, spb, tc in ((BATCH, SEQ, 0, 4, HY_W),
                                               (DEC_BATCH, DEC_SEQ, N_PROMPT // DEC_SEQ, 1, HY_W // 2)):
        ncb = HY_W // tc
        rows = spb * seq_len
        tok = lambda cb: pl.BlockSpec((rows, tc), lambda b, j, cb=cb: (row_blk0 + b, cb * LANES // tc + j))
        cw = lambda s: pl.BlockSpec((HY_SHORT, tc), lambda b, j, s=s: (0, s * ncb + j))
        cb_ = lambda s: pl.BlockSpec((1, tc), lambda b, j, s=s: (0, s * ncb + j))
        o_spec = pl.BlockSpec((rows, tc), lambda b, j: (b, j))
        o_shape = jax.ShapeDtypeStruct((n_seq * seq_len, HY_W), BF16)
        outs.append(pl.pallas_call(
            functools.partial(_hy_pre_kernel, seq_len=seq_len),
            grid=(n_seq // spb, ncb),
            in_specs=[tok(CB_X0), tok(CB_X1), tok(CB_HV), cw(0), cw(1), cw(2), cb_(0), cb_(1), cb_(2)],
            out_specs=[o_spec, o_spec], out_shape=[o_shape, o_shape],
            scratch_shapes=[pltpu.VMEM((seq_len + 2 * CONV_PAD, tc), F32), pltpu.VMEM((seq_len, tc), F32)],
            compiler_params=_cparams(("arbitrary", "arbitrary")),
            name=f"hyena_pre_{seq_len}",
        )(p_main, p_main, p_main, conv_w, conv_w, conv_w,
          conv_b.reshape(1, -1), conv_b.reshape(1, -1), conv_b.reshape(1, -1)))
    return outs


HY_FREQ_BLOCK = 512


def _hy_conv_kernel(gv_ref, x0_ref, d_ref, hre_ref, him_ref, hny_ref, cos_ref, nsin_ref, o_ref, *,
                    seq_len, n_seq):
    n = seq_len
    fb = min(n, HY_FREQ_BLOCK)
    chains = [(b, f0) for b in range(n_seq) for f0 in range(0, n, fb)]
    gvs = [gv_ref[pl.ds(b * seq_len, seq_len), :] for b in range(n_seq)]
    gre = [jnp.dot(cos_ref[f0:f0 + fb, :], gvs[b], preferred_element_type=F32) for b, f0 in chains]
    gim = [jnp.dot(nsin_ref[f0:f0 + fb, :], gvs[b], preferred_element_type=F32) for b, f0 in chains]
    spec = []
    for (b, f0), gr, gi in zip(chains, gre, gim):
        hre, him = hre_ref[f0:f0 + fb, :], him_ref[f0:f0 + fb, :]
        freq = lax.broadcasted_iota(jnp.int32, hre.shape, 0) + f0
        wk = jnp.where(freq == 0, 0.5 / n, 1.0 / n)
        q4 = freq % 4
        yre = (gr * hre - gi * him) * wk
        yim = (gr * him + gi * hre) * wk
        spec.append((jnp.where(q4 == 0, yre, jnp.where(q4 == 1, -yim, jnp.where(q4 == 2, -yre, yim))),
                     jnp.where(q4 == 0, yim, jnp.where(q4 == 1, yre, jnp.where(q4 == 2, -yim, -yre)))))
    inv = [_bdot(cos_ref[:, f0:f0 + fb], are) + _bdot(nsin_ref[:, f0:f0 + fb], aim)
           for (b, f0), (are, aim) in zip(chains, spec)]
    t = lax.broadcasted_iota(jnp.int32, (n, gvs[0].shape[1]), 0)
    alt = (1 - 2 * (t % 2)).astype(F32)
    for b in range(n_seq):
        rows = pl.ds(b * seq_len, seq_len)
        gvf = gvs[b].astype(F32)
        g_ny = jnp.sum(gvf * alt, axis=0, keepdims=True)
        y_ny = g_ny * hny_ref[0:1, :] * ((-1.0) ** (n // 2) / (2 * n))
        y = functools.reduce(lambda x, z: x + z, [p for (cb, _), p in zip(chains, inv) if cb == b])
        y = y + alt * y_ny + gvf * d_ref[...]
        o_ref[rows, :] = (y * x0_ref[rows, :].astype(F32)).astype(BF16)


def _hyena_conv(gv, x0c, d_skip, h_spec, tables, *, n_seq_total, seq_len, n_seq_blk):
    tc = 256
    hre, him, hny = h_spec
    cos_t, nsin_t = tables
    rows = n_seq_blk * seq_len
    grid = (HY_W // tc, n_seq_total // n_seq_blk)
    tok = pl.BlockSpec((rows, tc), lambda j, g: (g, j))
    hs = pl.BlockSpec((seq_len, tc), lambda j, g: (0, j))
    table = pl.BlockSpec((seq_len, seq_len), lambda j, g: (0, 0), pipeline_mode=pl.Buffered(1))
    return pl.pallas_call(
        functools.partial(_hy_conv_kernel, seq_len=seq_len, n_seq=n_seq_blk),
        grid=grid,
        in_specs=[tok, tok, pl.BlockSpec((1, tc), lambda j, g: (0, j)), hs, hs,
                  pl.BlockSpec((8, tc), lambda j, g: (0, j)), table, table],
        out_specs=tok,
        out_shape=jax.ShapeDtypeStruct((n_seq_total * seq_len, HY_W), BF16),
        compiler_params=_cparams(("arbitrary", "arbitrary")),
        name=f"hyena_conv_{seq_len}",
    )(gv, x0c, d_skip.reshape(1, HY_W), hre, him, hny, cos_t, nsin_t)


MERGE_TM = 512


def _merge_kernel(x_ref, oap_ref, obp_ref, ocp_ref, oas_ref, obs_ref, ocs_ref, ga_ref, gb_ref, gc_ref,
                  wa_ref, wb_ref, wc_ref, wo_ref, mod_ref, y_ref, wa_s, wb_s, wc_s, wo_s):
    i = pl.program_id(0)

    @pl.when(i == 0)
    def _():
        for src, dst in ((wa_ref, wa_s), (wb_ref, wb_s), (wc_ref, wc_s), (wo_ref, wo_s)):
            dst[...] = src[...].astype(BF16)

    is_prompt = i < N_PROMPT // MERGE_TM
    pick = lambda p_ref, s_ref: jnp.where(is_prompt, p_ref[...], s_ref[...])
    sig = lambda r: jax.nn.sigmoid(r[...].astype(F32))
    merged = (sig(ga_ref) * jnp.dot(pick(oap_ref, oas_ref), wa_s[...], preferred_element_type=F32)
              + sig(gb_ref) * jnp.dot(pick(obp_ref, obs_ref), wb_s[...], preferred_element_type=F32)
              + sig(gc_ref) * jnp.dot(pick(ocp_ref, ocs_ref), wc_s[...], preferred_element_type=F32))
    out = _bdot(merged, wo_s[...])
    y_ref[...] = x_ref[...] + mod_ref[0, 2:3, :] * out


def _merge(x, branches_p, branches_s, p_main, mods, w_a, w_b, w_c, w_out, layer):
    tm = MERGE_TM
    npt = N_PROMPT // tm
    tok = lambda w: pl.BlockSpec((tm, w), lambda i: (i, 0))
    tok_p = pl.BlockSpec((tm, DN_W), lambda i: (jnp.minimum(i, npt - 1), 0))
    tok_s = pl.BlockSpec((tm, DN_W), lambda i: (jnp.maximum(i - npt, 0), 0))
    gate = lambda s: pl.BlockSpec((tm, D_MODEL), lambda i, s=s: (i, CB_GATE // 8 + s))
    wfull = lambda r: pl.BlockSpec((None, r, D_MODEL), lambda i: (layer, 0, 0))
    wscr = lambda r: pltpu.VMEM((r, D_MODEL), BF16)
    return pl.pallas_call(
        _merge_kernel,
        grid=(N_TOK // tm,),
        in_specs=[tok(D_MODEL), tok_p, tok_p, tok_p, tok_s, tok_s, tok_s, gate(0), gate(1), gate(2),
                  wfull(DN_W), wfull(DN_W), wfull(HY_W), wfull(D_MODEL),
                  pl.BlockSpec((1, 6, D_MODEL), lambda i: (_mod_row(i, tm), 0, 0))],
        out_specs=tok(D_MODEL),
        out_shape=jax.ShapeDtypeStruct((N_TOK, D_MODEL), F32),
        scratch_shapes=[wscr(DN_W), wscr(DN_W), wscr(HY_W), wscr(D_MODEL)],
        compiler_params=_cparams(("arbitrary",)),
        name="merge_out_projection",
    )(x, *branches_p, *branches_s, p_main, p_main, p_main, w_a, w_b, w_c, w_out, mods)


FFN_TM = 512
FFN_TF = D_FF // 2


def _ffn_kernel(x_ref, g_ref, mod_ref, wg_ref, wu_ref, wo_ref, fg_ref, *rest, final):
    if final:
        yp_ref, ys_ref, h_scr, acc_scr = rest
    else:
        y_ref, h_scr, acc_scr = rest
    f = pl.program_id(1)
    nf = pl.num_programs(1)

    @pl.when(f == 0)
    def _():
        g = g_ref[...]
        sh = mod_ref[0, 3:4, :]
        sc1 = 1.0 + mod_ref[0, 4:5, :]

        def body(r, carry):
            rows = pl.ds(pl.multiple_of(r * ROW_CHUNK, ROW_CHUNK), ROW_CHUNK)
            h_scr[rows, :] = (_rms(x_ref[rows, :], g) * sc1 + sh).astype(BF16)
            return carry

        lax.fori_loop(0, FFN_TM // ROW_CHUNK, body, 0)

    h = h_scr[...]
    gate = jnp.dot(h, wg_ref[...], preferred_element_type=F32)
    up = jnp.dot(h, wu_ref[...], preferred_element_type=F32)
    part = _bdot(_silu(gate) * up, wo_ref[...])

    @pl.when(f == 0)
    def _():
        acc_scr[...] = part

    @pl.when(f > 0)
    def _():
        acc_scr[...] += part

    @pl.when(f == nf - 1)
    def _():
        y = x_ref[...] + mod_ref[0, 5:6, :] * acc_scr[...]
        if final:
            y = _rms(y, fg_ref[...])
            is_prompt = pl.program_id(0) < N_PROMPT // FFN_TM

            @pl.when(is_prompt)
            def _():
                yp_ref[...] = y

            @pl.when(jnp.logical_not(is_prompt))
            def _():
                ys_ref[...] = y
        else:
            y_ref[...] = y


def _ffn(x, norm_g, mods, w_in, w_out, final_g, layer, *, final):
    tm, tf = FFN_TM, FFN_TF
    nf = D_FF // tf
    if final:
        npt = N_PROMPT // tm
        out_specs = [pl.BlockSpec((tm, D_MODEL), lambda i, f: (jnp.minimum(i, npt - 1), 0)),
                     pl.BlockSpec((tm, D_MODEL), lambda i, f: (jnp.maximum(i - npt, 0), 0))]
        out_shape = [jax.ShapeDtypeStruct((N_PROMPT, D_MODEL), F32), jax.ShapeDtypeStruct((N_SAMPLE, D_MODEL), F32)]
    else:
        out_specs = pl.BlockSpec((tm, D_MODEL), lambda i, f: (i, 0))
        out_shape = jax.ShapeDtypeStruct((N_TOK, D_MODEL), F32)
    return pl.pallas_call(
        functools.partial(_ffn_kernel, final=final),
        grid=(N_TOK // tm, nf),
        in_specs=[pl.BlockSpec((tm, D_MODEL), lambda i, f: (i, 0)),
                  pl.BlockSpec((1, D_MODEL), lambda i, f: (0, 0)),
                  pl.BlockSpec((1, 6, D_MODEL), lambda i, f: (_mod_row(i, tm), 0, 0)),
                  pl.BlockSpec((None, D_MODEL, tf), lambda i, f: (layer, 0, f)),
                  pl.BlockSpec((None, D_MODEL, tf), lambda i, f: (layer, 0, nf + f)),
                  pl.BlockSpec((None, tf, D_MODEL), lambda i, f: (layer, f, 0)),
                  pl.BlockSpec((1, D_MODEL), lambda i, f: (0, 0))],
        out_specs=out_specs,
        out_shape=out_shape,
        scratch_shapes=[pltpu.VMEM((tm, D_MODEL), BF16), pltpu.VMEM((tm, D_MODEL), F32)],
        compiler_params=_cparams(("arbitrary", "arbitrary")),
        name="ffn",
    )(x, norm_g.reshape(1, D_MODEL), mods, w_in, w_in, w_out, final_g.reshape(1, D_MODEL))


def kernel(x_prompt, x_sample, cache_k, cache_v, state_dn, c, c_ctx, norm1_g, norm2_g, w_mod, b_mod,
           w_in, dn_conv_w, dn_a_log, dn_dt_bias, dn_norm_g, da_lambda, da_subln_g, hy_conv_w,
           hy_conv_b, hy_w1, hy_b1, hy_freq, hy_w2, hy_b2, hy_w3, hy_d, w_br_a, w_br_b, w_br_c,
           w_out, w_ffn_in, w_ffn_out, final_g):
    x = jnp.concatenate([x_prompt.reshape(N_PROMPT, D_MODEL), x_sample.reshape(N_SAMPLE, D_MODEL)], axis=0)
    cond8 = jnp.concatenate([c_ctx[None, :], c, jnp.zeros((8 - 1 - DEC_BATCH, D_MODEL), F32)], axis=0)
    rope = _rope_tables(DEC_SEQ)
    tab_p = _dft_tables(SEQ)
    tab_s = _dft_tables(DEC_SEQ)
    emb_p, emb_s = _hy_embedding(SEQ), _hy_embedding(DEC_SEQ)
    sample_blk0 = N_PROMPT // DEC_SEQ
    w_main, w_small = _pack_w_in(w_in)
    w_ffn_in_b, w_ffn_out_b = w_ffn_in.astype(BF16), w_ffn_out.astype(BF16)

    new_k = new_v = new_s = None
    for l in range(DEPTH):
        mods = _modulation(cond8, w_mod, b_mod[l], l).reshape(8, 6, D_MODEL)
        p_main, p_small = _in_projection(x, norm1_g[l], mods, w_main, w_small, l)

        dn_args = (p_main, p_small, dn_conv_w[l], dn_a_log[l], dn_dt_bias[l], dn_norm_g[l])
        oa_p, new_s = _deltanet(*dn_args, None, n_seq=BATCH, seq_len=SEQ, row_blk0=0, hps=4, layer=l,
                                states_so_far=new_s)
        oa_s, _ = _deltanet(*dn_args, state_dn, n_seq=DEC_BATCH, seq_len=DEC_SEQ, row_blk0=sample_blk0,
                            hps=2, layer=l)

        ob_p, new_k, new_v = _diff_attention(p_main, da_lambda[l], da_subln_g[l], l, n_seq=BATCH, seq_len=SEQ,
                                             row_blk0=0, tq=SEQ, caches_so_far=(new_k, new_v))
        ob_s, _, _ = _diff_attention(p_main, da_lambda[l], da_subln_g[l], l, n_seq=DEC_BATCH,
                                     seq_len=DEC_SEQ, row_blk0=sample_blk0, tq=1024, rope=rope,
                                     ctx=(cache_k, cache_v))

        hy_w = (hy_w1[l], hy_b1[l], hy_freq[l], hy_w2[l], hy_b2[l], hy_w3[l])
        hspec_p = _hyena_filter_spectrum(SEQ, emb_p, tab_p, *hy_w)
        hspec_s = _hyena_filter_spectrum(DEC_SEQ, emb_s, tab_s, *hy_w)
        (x0_p, gv_p), (x0_s, gv_s) = _hyena_pre(p_main, hy_conv_w[l], hy_conv_b[l])
        oc_p = _hyena_conv(gv_p, x0_p, hy_d[l], hspec_p, tab_p, n_seq_total=BATCH, seq_len=SEQ,
                           n_seq_blk=BATCH)
        oc_s = _hyena_conv(gv_s, x0_s, hy_d[l], hspec_s, tab_s, n_seq_total=DEC_BATCH, seq_len=DEC_SEQ,
                           n_seq_blk=1)

        x = _merge(x, (oa_p, ob_p, oc_p), (oa_s, ob_s, oc_s), p_main, mods,
                   w_br_a, w_br_b, w_br_c, w_out, l)
        x = _ffn(x, norm2_g[l], mods, w_ffn_in_b, w_ffn_out_b, final_g, l, final=(l == DEPTH - 1))

    y_prompt = x[0].reshape(BATCH, SEQ, D_MODEL)
    y_sample = x[1].reshape(DEC_BATCH, DEC_SEQ, D_MODEL)
    return (y_prompt, y_sample, new_k, new_v, new_s)
```

```python
import functools
import math

import jax
import jax.numpy as jnp
from jax import lax
from jax.experimental import pallas as pl
from jax.experimental.pallas import tpu as pltpu

F32 = jnp.float32
BF16 = jnp.bfloat16

D_MODEL = 1024
BATCH = 16
SEQ = 256
DEPTH = 2
DEC_BATCH = 4
DEC_SEQ = 2048
PAST_LEN = 256
GRID_W = 64
RMS_EPS = 1e-6
DN_HEADS = 4
DN_DK = 128
DN_DV = 128
DN_W = DN_HEADS * DN_DK
DN_CONV = 5
DN_CHUNK = 64
DA_HEADS = 4
DA_DH = 64
DA_DV = 2 * DA_DH
ROPE_BASE = 10000.0
HY_W = 512
HY_SHORT = 3
HY_BANDS = 8
HY_EMB = 1 + 2 * HY_BANDS
HY_ORDER = 64
HY_FAST_DECAY = 0.3
HY_SLOW_DECAY = 1.5
HY_TARGET = 1e-2
D_FF = ((8 * D_MODEL // 3 + 255) // 256) * 256

N_PROMPT = BATCH * SEQ
N_SAMPLE = DEC_BATCH * DEC_SEQ
N_TOK = N_PROMPT + N_SAMPLE
LANES = 128
N_MAIN = 8192
CB_QA, CB_KA, CB_VA, CB_ZA = 0, 4, 8, 12
CB_QB, CB_KB, CB_VB = 16, 20, 24
CB_X0, CB_X1, CB_HV = 28, 32, 36
CB_GATE = 40
VMEM_LIMIT = 56 * 1024 * 1024


def _cparams(sem):
    return pltpu.CompilerParams(dimension_semantics=sem, vmem_limit_bytes=VMEM_LIMIT)


def _bdot(a, b):
    return jnp.dot(a.astype(BF16), b.astype(BF16), preferred_element_type=F32)


def _bdot_nt(a, b):
    return lax.dot_general(a.astype(BF16), b.astype(BF16), (((1,), (1,)), ((), ())),
                           preferred_element_type=F32)


def _bdot_tn(a, b):
    return lax.dot_general(a.astype(BF16), b.astype(BF16), (((0,), (0,)), ((), ())),
                           preferred_element_type=F32)


def _split3(x):
    hi = x.astype(BF16)
    r = x - hi.astype(F32)
    mid = r.astype(BF16)
    lo = (r - mid.astype(F32)).astype(BF16)
    return hi, mid, lo


def _dot_exact_lhs(t, x):
    hi, mid, lo = _split3(x)
    tb = t.astype(BF16)
    d = lambda p: jnp.dot(tb, p, preferred_element_type=F32)
    return d(hi) + d(mid) + d(lo)


def _dot_exact_rhs(x, e):
    hi, mid, lo = _split3(x)
    eb = e.astype(BF16)
    d = lambda p: jnp.dot(p, eb, preferred_element_type=F32)
    return d(hi) + d(mid) + d(lo)


def _dot3(a, b):
    ah = a.astype(BF16)
    al = (a - ah.astype(F32)).astype(BF16)
    bh = b.astype(BF16)
    bl = (b - bh.astype(F32)).astype(BF16)
    d = lambda p, q: jnp.dot(p, q, preferred_element_type=F32)
    return d(ah, bh) + d(ah, bl) + d(al, bh)


def _silu(x):
    return x * jax.nn.sigmoid(x)


def _rms(x, g):
    return x * lax.rsqrt(jnp.mean(x * x, axis=-1, keepdims=True) + RMS_EPS) * g


def _mod_row(i, tm):
    n_prompt_tiles = N_PROMPT // tm
    tiles_per_seq = DEC_SEQ // tm
    return jnp.where(i < n_prompt_tiles, 0, 1 + (i - n_prompt_tiles) // tiles_per_seq)


def _mod_kernel(c_ref, w_ref, b_ref, o_ref):
    o_ref[...] = _bdot(_silu(c_ref[...]), w_ref[...]) + b_ref[...]


def _modulation(cond8, w_mod, b_mod, layer):
    n = 6 * D_MODEL
    tn = 1024
    return pl.pallas_call(
        _mod_kernel,
        grid=(n // tn,),
        in_specs=[pl.BlockSpec((8, D_MODEL), lambda j: (0, 0)),
                  pl.BlockSpec((None, D_MODEL, tn), lambda j: (layer, 0, j)),
                  pl.BlockSpec((1, tn), lambda j: (0, j))],
        out_specs=pl.BlockSpec((8, tn), lambda j: (0, j)),
        out_shape=jax.ShapeDtypeStruct((8, n), F32),
        compiler_params=_cparams(("arbitrary",)),
        name="modulation",
    )(cond8, w_mod, b_mod.reshape(1, n))


IN_TM = 2048
IN_TN = 2048
ROW_CHUNK = 256


def _inproj_kernel(x_ref, g_ref, mod_ref, wm_ref, ws_ref, p_ref, ps_ref, h_scr):
    @pl.when(pl.program_id(1) == 0)
    def _():
        g = g_ref[...]
        sh = mod_ref[0, 0:1, :]
        sc1 = 1.0 + mod_ref[0, 1:2, :]

        def body(r, carry):
            rows = pl.ds(pl.multiple_of(r * ROW_CHUNK, ROW_CHUNK), ROW_CHUNK)
            hb = (_rms(x_ref[rows, :], g) * sc1 + sh).astype(BF16)
            h_scr[rows, :] = hb
            ps_ref[rows, :] = jnp.dot(hb, ws_ref[...], preferred_element_type=F32)
            return carry

        lax.fori_loop(0, IN_TM // ROW_CHUNK, body, 0)

    p_ref[...] = jnp.dot(h_scr[...], wm_ref[...], preferred_element_type=F32).astype(BF16)


N_GATE_COLS = 4 * DN_HEADS
PACK_ROWS = 128
PACK_COLS = 512


def _pack_w_in_kernel(w_ref, main_ref, small_ref):
    lo = 3 * DN_W
    main_ref[:, :lo] = w_ref[:, :lo].astype(BF16)
    for c in range(lo, N_MAIN, PACK_COLS):
        main_ref[:, c:c + PACK_COLS] = w_ref[:, c + N_GATE_COLS:c + N_GATE_COLS + PACK_COLS].astype(BF16)
    gate = w_ref[:, lo:lo + LANES]
    lane = lax.broadcasted_iota(jnp.int32, gate.shape, 1)
    small_ref[...] = jnp.where(lane < N_GATE_COLS, gate, 0.0).astype(BF16)


def _pack_w_in(w_in):
    n_in = w_in.shape[2]
    return pl.pallas_call(
        _pack_w_in_kernel,
        grid=(DEPTH, D_MODEL // PACK_ROWS),
        in_specs=[pl.BlockSpec((None, PACK_ROWS, n_in), lambda l, r: (l, r, 0))],
        out_specs=[pl.BlockSpec((None, PACK_ROWS, N_MAIN), lambda l, r: (l, r, 0)),
                   pl.BlockSpec((None, PACK_ROWS, LANES), lambda l, r: (l, r, 0))],
        out_shape=[jax.ShapeDtypeStruct((DEPTH, D_MODEL, N_MAIN), BF16),
                   jax.ShapeDtypeStruct((DEPTH, D_MODEL, LANES), BF16)],
        compiler_params=_cparams(("arbitrary", "arbitrary")),
        name="pack_w_in",
    )(w_in)


def _in_projection(x, norm_g, mods, w_main, w_small, layer):
    grid = (N_TOK // IN_TM, N_MAIN // IN_TN)
    return pl.pallas_call(
        _inproj_kernel,
        grid=grid,
        in_specs=[pl.BlockSpec((IN_TM, D_MODEL), lambda i, j: (i, 0)),
                  pl.BlockSpec((1, D_MODEL), lambda i, j: (0, 0)),
                  pl.BlockSpec((1, 6, D_MODEL), lambda i, j: (_mod_row(i, IN_TM), 0, 0)),
                  pl.BlockSpec((None, D_MODEL, IN_TN), lambda i, j: (layer, 0, j)),
                  pl.BlockSpec((None, D_MODEL, LANES), lambda i, j: (layer, 0, 0))],
        out_specs=[pl.BlockSpec((IN_TM, IN_TN), lambda i, j: (i, j)),
                   pl.BlockSpec((IN_TM, LANES), lambda i, j: (i, 0))],
        out_shape=[jax.ShapeDtypeStruct((N_TOK, N_MAIN), BF16),
                   jax.ShapeDtypeStruct((N_TOK, LANES), F32)],
        scratch_shapes=[pltpu.VMEM((IN_TM, D_MODEL), BF16)],
        compiler_params=_cparams(("arbitrary", "arbitrary")),
        name="in_projection",
    )(x, norm_g.reshape(1, D_MODEL), mods, w_main, w_small)


DN_UNIT = 2 * DN_CHUNK
DN_GROUP = 8


CONV_PAD = 8


def _zero_conv_borders(pad_ref):
    n = pad_ref.shape[0] - 2 * CONV_PAD
    zeros = jnp.zeros((CONV_PAD, pad_ref.shape[1]), F32)
    pad_ref[0:CONV_PAD, :] = zeros
    pad_ref[CONV_PAD + n:, :] = zeros


CONV_ROWS = 512


def _centred_conv(pad_ref, load, store, n, w_ref, n_taps, bias=None):
    rb = min(n, CONV_ROWS)
    half = n_taps // 2
    for r0 in range(0, n, rb):
        pad_ref[CONV_PAD + r0:CONV_PAD + r0 + rb, :] = load(r0, rb)
    for r0 in range(0, n, rb):
        acc = None
        for tap in range(n_taps):
            lo = CONV_PAD + r0 + tap - half
            term = pad_ref[lo:lo + rb, :] * w_ref[tap:tap + 1, :]
            acc = term if acc is None else acc + term
        store(r0, acc if bias is None else acc + bias)


def _l2norm(x):
    return x * lax.rsqrt(jnp.sum(x * x, axis=-1, keepdims=True) + RMS_EPS)


def _softplus(x):
    return jnp.maximum(x, 0.0) + jnp.log1p(jnp.exp(-jnp.abs(x)))


DN_BASE = 16


def _unit_tri_inverse(a_list, ri, ci):
    eye = (ri == ci).astype(F32)
    blk = lambda b: (ri // b) == (ci // b)
    y = [jnp.where(blk(DN_BASE), -a, 0.0) for a in a_list]
    p = [eye + yi for yi in y]
    for _ in range(3):
        y = [_bdot(yi, yi) for yi in y]
        p = [pi + _bdot(yi, pi) for yi, pi in zip(y, p)]
    b = DN_BASE
    while b < DN_CHUNK:
        off_mask = blk(2 * b) & ~blk(b)
        t = [_bdot(jnp.where(off_mask, a, 0.0), pi) for a, pi in zip(a_list, p)]
        p = [pi - _bdot(pi, ti) for pi, ti in zip(p, t)]
        b *= 2
    return p


def _dn_units(chains):
    u, c = DN_UNIT, DN_CHUNK
    ri = lax.broadcasted_iota(jnp.int32, (u, u), 0)
    ci = lax.broadcasted_iota(jnp.int32, (u, u), 1)
    same = (ri // c) == (ci // c)
    eye_mask = ri == ci
    hi_rows = ri >= c
    incl_of = {False: same & (ri >= ci), True: same & (ri <= ci)}
    strict_of = {False: same & (ri > ci), True: same & (ri < ci)}
    incl = [incl_of[ch['backward']] for ch in chains]
    strict = [strict_of[ch['backward']] for ch in chains]
    gc = [_dot_exact_lhs(m.astype(F32), ch['g']) for m, ch in zip(incl, chains)]
    g_tot = [(x[0:1, :], x[c:c + 1, :]) if ch['backward'] else (x[c - 1:c, :], x[u - 1:u, :])
             for x, ch in zip(gc, chains)]
    gc_row = [jnp.sum(jnp.where(eye_mask, x, 0.0), axis=0, keepdims=True) for x in gc]
    dec = [jnp.exp(jnp.where(m, x - xr, -1e30)) for m, x, xr in zip(incl, gc, gc_row)]
    e_gc = [jnp.exp(x) for x in gc]
    a = [jnp.where(m, ch['kk'] * ch['beta'] * d, 0.0) for m, ch, d in zip(strict, chains, dec)]
    qk = [ch['qk_raw'] * d for ch, d in zip(chains, dec)]
    eye = eye_mask.astype(F32)
    r = [p - eye for p in _unit_tri_inverse(a, ri, ci)]
    rhs = [jnp.concatenate([ch['v'] * ch['beta'], ch['k'] * (ch['beta'] * e)], axis=1)
           for ch, e in zip(chains, e_gc)]
    uwb = [(x + _bdot(ri_, x)).astype(BF16) for ri_, x in zip(r, rhs)]
    qkuw = [_bdot(x, y) for x, y in zip(qk, uwb)]
    kd = [ch['k'] * jnp.exp(jnp.where(hi_rows, gt[1], gt[0]) - x) for ch, gt, x in zip(chains, g_tot, gc)]
    tp = [[_bdot_tn(jnp.where(keep, x, 0.0), y) for x, y in zip(kd, uwb)]
          for keep in (~hi_rows, hi_rows)]
    out = []
    for i, ch in enumerate(chains):
        o_local = qkuw[i][:, :DN_DV]
        q_eff = ch['q'] * e_gc[i] - qkuw[i][:, DN_DV:]
        per_chunk = [(-tp[h][i][:, DN_DV:], tp[h][i][:, :DN_DV], jnp.exp(g_tot[i][h])) for h in range(2)]
        out.append((o_local, q_eff, per_chunk))
    return out


def _dn_kernel(*refs, seq_len, has_state, hps, layer):
    if has_state:
        (q_ref, k_ref, v_ref, z_ref, ps_ref, cwq_ref, cwk_ref, cwv_ref, alog_ref, dtb_ref, ng_ref,
         s0_ref, o_ref, q_s, k_s, v_s, bt_s, g_s, oacc_s, qe_s, th_s, psi_s, egl_s, st_s, pad_s) = refs
        sfin_ref = None
    else:
        refs = refs[:11] + refs[(11 if layer == 0 else 12):]
        (q_ref, k_ref, v_ref, z_ref, ps_ref, cwq_ref, cwk_ref, cwv_ref, alog_ref, dtb_ref, ng_ref,
         o_ref, sfin_ref, q_s, k_s, v_s, bt_s, g_s, oacc_s, qe_s, th_s, psi_s, egl_s, st_s, pad_s) = refs
        s0_ref = None
    n_units = seq_len // DN_UNIT
    n_chunks = seq_len // DN_CHUNK
    c = DN_CHUNK
    sel_r = lax.broadcasted_iota(jnp.int32, (LANES, 4 * LANES), 0)
    sel_blk = lax.broadcasted_iota(jnp.int32, (LANES, 4 * LANES), 1) // LANES
    sels = []
    for j in range(hps):
        head = pl.program_id(1) * hps + j
        src_col = head + DN_HEADS * jnp.where(sel_blk == 0, 0, jnp.where(sel_blk == 1, 2, jnp.where(sel_blk == 2, 1, 3)))
        sels.append((sel_r == src_col).astype(BF16))

    def gate_rows(r, carry):
        rows = pl.ds(pl.multiple_of(r * ROW_CHUNK, ROW_CHUNK), ROW_CHUNK)
        ps = ps_ref[rows, :]
        lane = lax.broadcasted_iota(jnp.int32, ps.shape, 1)
        gate_cols = jnp.where(lane < 2 * DN_HEADS, jax.nn.sigmoid(ps),
                              -(jnp.exp(alog_ref[...]) * _softplus(ps + dtb_ref[...])))
        parts = _split3(gate_cols)
        for j in range(hps):
            ext = functools.reduce(lambda a, b: a + b,
                                   [jnp.dot(part, sels[j], preferred_element_type=F32) for part in parts])
            for d in range(2):
                bt_s[j, d, rows, :] = ext[:, (2 * d) * LANES:(2 * d + 1) * LANES]
                g_s[j, d, rows, :] = ext[:, (2 * d + 1) * LANES:(2 * d + 2) * LANES]
        return carry

    lax.fori_loop(0, seq_len // ROW_CHUNK, gate_rows, 0)

    _zero_conv_borders(pad_s)
    for j in range(hps):
        cols = slice(j * LANES, (j + 1) * LANES)
        for x_ref, w_ref, out_s, post in (
                (q_ref, cwq_ref, q_s, lambda y: _l2norm(_silu(y)) * (DN_DK ** -0.5)),
                (k_ref, cwk_ref, k_s, lambda y: _l2norm(_silu(y))),
                (v_ref, cwv_ref, v_s, _silu)):
            def store(r0, y, out_s=out_s, post=post):
                out_s[j, r0:r0 + y.shape[0], :] = post(y)
            _centred_conv(pad_s, lambda r0, rb, x_ref=x_ref: x_ref[r0:r0 + rb, cols].astype(F32), store,
                          seq_len, w_ref.at[:, cols], DN_CONV)
        for d in range(2):
            st_s[j, d] = s0_ref[0, 0, d, j] if has_state else jnp.zeros((DN_DK, DN_DV), F32)

    def unit_group(grp, carry):
        where, chains = [], []
        for t in range(DN_GROUP):
            idx = grp * DN_GROUP + t
            j = idx // n_units
            n = idx % n_units
            rows = pl.ds(pl.multiple_of(n * DN_UNIT, DN_UNIT), DN_UNIT)
            qc, kc, vc = q_s[j, rows, :], k_s[j, rows, :], v_s[j, rows, :]
            kcb = kc.astype(BF16)
            kk = _bdot_nt(kcb, kcb)
            qk_raw = _bdot_nt(qc, kcb)
            for d in range(2):
                where.append((j, n, rows, d))
                chains.append(dict(q=qc, k=kc, v=vc, kk=kk, qk_raw=qk_raw, beta=bt_s[j, d, rows, :],
                                   g=g_s[j, d, rows, :], backward=(d == 1)))
        results = _dn_units(chains)
        for (j, n, rows, d), (o_loc, q_eff, per_chunk) in zip(where, results):
            qe_s[j, d, rows, :] = q_eff.astype(BF16)
            for half, (theta, psi, egl) in enumerate(per_chunk):
                th_s[j, d, 2 * n + half] = theta.astype(BF16)
                psi_s[j, d, 2 * n + half] = psi
                egl_s[j, d, 2 * n + half] = egl
            if d == 0:
                o_fwd = o_loc
            else:
                oacc_s[j, rows, :] = o_fwd + o_loc
        return carry

    lax.fori_loop(0, hps * n_units // DN_GROUP, unit_group, 0)

    def scan_step(i, carry):
        for j in range(hps):
            for d, n in ((0, i), (1, n_chunks - 1 - i)):
                rows = pl.ds(pl.multiple_of(n * c, c), c)
                s = st_s[j, d]
                sb = s.astype(BF16)
                oacc_s[j, rows, :] += jnp.dot(qe_s[j, d, rows, :], sb, preferred_element_type=F32)
                st_s[j, d] = (egl_s[j, d, n] * s + jnp.dot(th_s[j, d, n], sb, preferred_element_type=F32)
                              + psi_s[j, d, n])
        return carry

    lax.fori_loop(0, n_chunks, scan_step, 0, unroll=min(n_chunks, 8))

    if sfin_ref is not None:
        for j in range(hps):
            for d in range(2):
                sfin_ref[0, 0, d, j] = st_s[j, d]
                if layer == 0:
                    for later in range(1, DEPTH):
                        sfin_ref[0, later, d, j] = jnp.zeros((DN_DK, DN_DV), F32)

    def out_rows(r, carry):
        rows = pl.ds(pl.multiple_of(r * ROW_CHUNK, ROW_CHUNK), ROW_CHUNK)
        for j in range(hps):
            cols = slice(j * LANES, (j + 1) * LANES)
            o = _rms(oacc_s[j, rows, :], ng_ref[...]) * _silu(z_ref[rows, cols].astype(F32))
            o_ref[rows, cols] = o.astype(BF16)
        return carry

    lax.fori_loop(0, seq_len // ROW_CHUNK, out_rows, 0)


def _deltanet(p_main, p_small, conv_w, a_log, dt_bias, norm_g, s0, *, n_seq, seq_len, row_blk0, hps, layer,
              states_so_far=None):
    has_state = s0 is not None
    n_chunks = seq_len // DN_CHUNK
    width = hps * LANES
    tok = lambda cb: pl.BlockSpec((seq_len, width), lambda b, h: (row_blk0 + b, cb // hps + h))
    cw = lambda cb: pl.BlockSpec((DN_CONV, width), lambda b, h: (0, cb // hps + h))
    in_specs = [tok(CB_QA), tok(CB_KA), tok(CB_VA), tok(CB_ZA),
                pl.BlockSpec((seq_len, LANES), lambda b, h: (row_blk0 + b, 0)),
                cw(0), cw(DN_HEADS), cw(2 * DN_HEADS),
                pl.BlockSpec((1, LANES), lambda b, h: (0, 0)),
                pl.BlockSpec((1, LANES), lambda b, h: (0, 0)),
                pl.BlockSpec((1, DN_DV), lambda b, h: (0, 0))]
    pad8 = lambda t: jnp.pad(t.reshape(1, 2 * DN_HEADS), ((0, 0), (2 * DN_HEADS, LANES - 4 * DN_HEADS)))
    args = [p_main, p_main, p_main, p_main, p_small, conv_w, conv_w, conv_w,
            pad8(a_log), pad8(dt_bias), norm_g.reshape(1, DN_DV)]
    o_spec = pl.BlockSpec((seq_len, width), lambda b, h: (b, h))
    o_shape = jax.ShapeDtypeStruct((n_seq * seq_len, DN_W), BF16)
    aliases = {}
    if has_state:
        in_specs.append(pl.BlockSpec((1, 1, 2, hps, DN_DK, DN_DV), lambda b, h: (b, layer, 0, h, 0, 0)))
        args.append(s0)
        out_specs, out_shape = o_spec, o_shape
    else:
        if layer == 0:
            s_spec = pl.BlockSpec((1, DEPTH, 2, hps, DN_DK, DN_DV), lambda b, h: (b, 0, 0, h, 0, 0))
        else:
            s_spec = pl.BlockSpec((1, 1, 2, hps, DN_DK, DN_DV), lambda b, h: (b, layer, 0, h, 0, 0))
            in_specs.append(pl.BlockSpec(memory_space=pl.ANY))
            args.append(states_so_far)
            aliases = {len(args) - 1: 1}
        out_specs = [o_spec, s_spec]
        out_shape = [o_shape, jax.ShapeDtypeStruct((n_seq, DEPTH, 2, DN_HEADS, DN_DK, DN_DV), F32)]
    scratch = [pltpu.VMEM((hps, seq_len, LANES), F32),
               pltpu.VMEM((hps, seq_len, LANES), F32),
               pltpu.VMEM((hps, seq_len, LANES), F32),
               pltpu.VMEM((hps, 2, seq_len, LANES), F32),
               pltpu.VMEM((hps, 2, seq_len, LANES), F32),
               pltpu.VMEM((hps, seq_len, LANES), F32),
               pltpu.VMEM((hps, 2, seq_len, LANES), BF16),
               pltpu.VMEM((hps, 2, n_chunks, DN_DK, DN_DK), BF16),
               pltpu.VMEM((hps, 2, n_chunks, DN_DK, DN_DV), F32),
               pltpu.VMEM((hps, 2, n_chunks, 1, LANES), F32),
               pltpu.VMEM((hps, 2, DN_DK, DN_DV), F32),
               pltpu.VMEM((seq_len + 2 * CONV_PAD, LANES), F32)]
    res = pl.pallas_call(
        functools.partial(_dn_kernel, seq_len=seq_len, has_state=has_state, hps=hps, layer=layer),
        grid=(n_seq, DN_HEADS // hps),
        in_specs=in_specs, out_specs=out_specs, out_shape=out_shape,
        input_output_aliases=aliases,
        scratch_shapes=scratch,
        compiler_params=_cparams(("arbitrary", "arbitrary")),
        name=f"deltanet_{seq_len}",
    )(*args)
    return (res, None) if has_state else (res[0], res[1])


def _rope(x, cos, sin_signed):
    lane = lax.broadcasted_iota(jnp.int32, x.shape, 1)
    partner = jnp.where((lane % 32) < 16, pltpu.roll(x, LANES - 16, axis=1), pltpu.roll(x, 16, axis=1))
    return x * cos + partner * sin_signed


DA_ROWS = 128


def _exp2_rows(s_parts):
    m = functools.reduce(jnp.maximum, [jnp.max(s, axis=-1, keepdims=True) for s in s_parts])
    return [jnp.exp2(s - m).astype(BF16) for s in s_parts]


def _da_kernel(*refs, layer, latent):
    if latent:
        (q_ref, k_ref, v_ref, lam_ref, sg_ref, cq_ref, sq_ref, ck_ref, sk_ref, ctxk_ref, ctxv_ref,
         o_ref, krot_s, vext_s) = refs
    else:
        refs = refs[:5] + refs[(5 if layer == 0 else 7):]
        q_ref, k_ref, v_ref, lam_ref, sg_ref, o_ref, ko_ref, vo_ref = refs
    with_ones = lambda v: jnp.concatenate([v, jnp.ones(v.shape, BF16)], axis=1)
    lam_init = 0.8 - 0.6 * math.exp(-0.3 * layer)
    lp = lam_ref[...]
    dots = jnp.sum(jnp.concatenate([lp[0:1] * lp[1:2], lp[2:3] * lp[3:4]], axis=0), axis=1, keepdims=True)
    e = jnp.exp(dots)
    lam = e[0:1, :] - e[1:2, :] + lam_init

    n_heads = q_ref.shape[1] // LANES
    tq = q_ref.shape[0]
    heads = []
    if latent:
        @pl.when(pl.program_id(2) == 0)
        def _():
            krot_s[...] = _rope(k_ref[...].astype(F32), ck_ref[...], sk_ref[...]).astype(BF16)
            vext_s[...] = with_ones(v_ref[...])
        heads.append((_rope(q_ref[...].astype(F32), cq_ref[...], sq_ref[...]),
                      [ctxk_ref[0, 0, 0].astype(BF16), krot_s[...]],
                      [with_ones(ctxv_ref[0, 0, 0].astype(BF16)), vext_s[...]]))
    else:
        for j in range(n_heads):
            cols = slice(j * LANES, (j + 1) * LANES)
            heads.append((q_ref[:, cols].astype(F32), [k_ref[:, cols]], [with_ones(v_ref[:, cols])]))
            ko_ref[0, 0, j] = k_ref[:, cols].astype(F32)
            vo_ref[0, 0, j] = v_ref[:, cols].astype(F32)
            if layer == 0:
                for later in range(1, DEPTH):
                    ko_ref[0, later, j] = jnp.zeros((tq, LANES), F32)
                    vo_ref[0, later, j] = jnp.zeros((tq, LANES), F32)
    lane = lax.broadcasted_iota(jnp.int32, (tq, LANES), 1)
    groups = []
    for q, keys, vals in heads:
        q = q * (DA_DH ** -0.5 * math.log2(math.e))
        q12 = jnp.concatenate([jnp.where(lane < DA_DH, q, 0.0), jnp.where(lane >= DA_DH, q, 0.0)],
                              axis=0).astype(BF16)
        groups += [(q12[r:r + DA_ROWS], keys, vals) for r in range(0, 2 * tq, DA_ROWS)]
    scores = [[_bdot_nt(qg, kk) for kk in keys] for qg, keys, _ in groups]
    soft = [_exp2_rows(sg) for sg in scores]
    pv = []
    for e_parts, (_, _, vals) in zip(soft, groups):
        acc = None
        for e, vv in zip(e_parts, vals):
            part = _bdot(e, vv)
            acc = part if acc is None else acc + part
        pv.append(acc[:, :DA_DV] * (1.0 / acc[:, DA_DV:]))
    per_head = 2 * tq // DA_ROWS
    for j in range(len(heads)):
        hp = jnp.concatenate(pv[j * per_head:(j + 1) * per_head], axis=0)
        o = hp[:tq] - lam * hp[tq:]
        o_ref[:, j * LANES:(j + 1) * LANES] = (_rms(o, sg_ref[...]) * (1.0 - lam_init)).astype(BF16)


def _diff_attention(p_main, lam_p, subln_g, layer, *, n_seq, seq_len, row_blk0, tq, rope=None, ctx=None,
                    caches_so_far=None):
    latent = ctx is not None
    aliases = {}
    hps = 1 if latent else DA_HEADS
    width = hps * LANES
    nq = seq_len // tq
    qpb = seq_len // tq
    in_specs = [pl.BlockSpec((tq, width), lambda b, h, i: ((row_blk0 + b) * qpb + i, CB_QB // hps + h)),
                pl.BlockSpec((seq_len, width), lambda b, h, i: (row_blk0 + b, CB_KB // hps + h)),
                pl.BlockSpec((seq_len, width), lambda b, h, i: (row_blk0 + b, CB_VB // hps + h)),
                pl.BlockSpec((4, LANES), lambda b, h, i: (0, 0)),
                pl.BlockSpec((1, DA_DV), lambda b, h, i: (0, 0))]
    args = [p_main, p_main, p_main, jnp.pad(lam_p, ((0, 0), (0, LANES - DA_DH))), subln_g.reshape(1, DA_DV)]
    o_spec = pl.BlockSpec((tq, width), lambda b, h, i: (b * qpb + i, h))
    o_shape = jax.ShapeDtypeStruct((n_seq * seq_len, DA_HEADS * DA_DV), BF16)
    scratch = []
    if latent:
        cos, sin_signed = rope
        ctx_k, ctx_v = ctx
        n_ctx = ctx_k.shape[3]
        in_specs += [pl.BlockSpec((tq, LANES), lambda b, h, i: (i, 0)),
                     pl.BlockSpec((tq, LANES), lambda b, h, i: (i, 0)),
                     pl.BlockSpec((seq_len, LANES), lambda b, h, i: (0, 0)),
                     pl.BlockSpec((seq_len, LANES), lambda b, h, i: (0, 0)),
                     pl.BlockSpec((1, 1, 1, n_ctx, DA_DV), lambda b, h, i: (b, layer, h, 0, 0)),
                     pl.BlockSpec((1, 1, 1, n_ctx, DA_DV), lambda b, h, i: (b, layer, h, 0, 0))]
        args += [cos, sin_signed, cos, sin_signed, ctx_k, ctx_v]
        out_specs, out_shape = o_spec, o_shape
        scratch = [pltpu.VMEM((seq_len, LANES), BF16), pltpu.VMEM((seq_len, 2 * DA_DV), BF16)]
    else:
        if layer == 0:
            kv_spec = pl.BlockSpec((1, DEPTH, hps, seq_len, DA_DV), lambda b, h, i: (b, 0, h, 0, 0))
        else:
            kv_spec = pl.BlockSpec((1, 1, hps, seq_len, DA_DV), lambda b, h, i: (b, layer, h, 0, 0))
            in_specs += [pl.BlockSpec(memory_space=pl.ANY), pl.BlockSpec(memory_space=pl.ANY)]
            args += list(caches_so_far)
            aliases = {len(args) - 2: 1, len(args) - 1: 2}
        kv_shape = jax.ShapeDtypeStruct((n_seq, DEPTH, DA_HEADS, seq_len, DA_DV), F32)
        out_specs = [o_spec, kv_spec, kv_spec]
        out_shape = [o_shape, kv_shape, kv_shape]
    res = pl.pallas_call(
        functools.partial(_da_kernel, layer=layer, latent=latent),
        grid=(n_seq, DA_HEADS // hps, nq),
        in_specs=in_specs, out_specs=out_specs, out_shape=out_shape,
        input_output_aliases=aliases,
        scratch_shapes=scratch,
        compiler_params=_cparams(("arbitrary", "arbitrary", "arbitrary")),
        name=f"diff_attention_{seq_len}",
    )(*args)
    return (res, None, None) if latent else tuple(res)


def _rope_tables(n_tok):
    half = DA_DH // 2
    inv = ROPE_BASE ** (-jnp.arange(0, half, 2, dtype=F32) / half)
    t = jnp.arange(n_tok)
    ang_r = (t // GRID_W).astype(F32)[:, None] * inv
    ang_c = (t % GRID_W).astype(F32)[:, None] * inv
    cos32 = lambda a: jnp.concatenate([jnp.cos(a), jnp.cos(a)], axis=-1)
    sin32 = lambda a: jnp.concatenate([-jnp.sin(a), jnp.sin(a)], axis=-1)
    cos = jnp.concatenate([cos32(ang_r), cos32(ang_c)] * 2, axis=-1)
    sin_signed = jnp.concatenate([sin32(ang_r), sin32(ang_c)] * 2, axis=-1)
    return cos, sin_signed


def _dft_tables(n):
    t_lo = 64
    k = jnp.arange(n, dtype=jnp.int32)[:, None]
    ang = lambda m: ((k * m[None, :]) % (2 * n)).astype(F32) * (math.pi / n)
    a = ang(t_lo * jnp.arange(n // t_lo, dtype=jnp.int32))
    b = ang(jnp.arange(t_lo, dtype=jnp.int32))
    ca, sa, cb, sb = jnp.cos(a)[:, :, None], jnp.sin(a)[:, :, None], jnp.cos(b)[:, None, :], jnp.sin(b)[:, None, :]
    cos_t = (ca * cb - sa * sb).reshape(n, n)
    nsin_t = -(sa * cb + ca * sb).reshape(n, n)
    return cos_t.astype(BF16), nsin_t.astype(BF16)


def _hy_embedding(n):
    j = jnp.arange(n, dtype=F32)
    t = j / (n - 1)
    ang = (2.0 * math.pi * j / n)[:, None] * jnp.linspace(1e-4, HY_BANDS - 1, HY_BANDS, dtype=F32)
    z = jnp.concatenate([t[:, None], jnp.cos(ang), -jnp.sin(ang)], axis=-1)
    half = n // 2
    dist = jnp.abs(j - half) / half
    max_decay = math.log(HY_TARGET) / HY_FAST_DECAY
    min_decay = math.log(HY_TARGET) / HY_SLOW_DECAY
    deltas = jnp.abs(jnp.linspace(min_decay, max_decay, HY_W, dtype=F32))
    return jnp.pad(z, ((0, 0), (0, LANES - HY_EMB))), dist[:, None], deltas[None, :]


def _alt_rows(n):
    t = lax.broadcasted_iota(jnp.int32, (8, n), 1)
    return (1 - 2 * (t % 2)).astype(F32)


def _hy_filter_kernel(z_ref, dist_ref, delta_ref, w1_ref, b1_ref, fr_ref, w2_ref, b2_ref, w3_ref,
                      cos_ref, nsin_ref, hre_ref, him_ref, hny_ref, h_s):
    @pl.when(pl.program_id(1) == 0)
    def _():
        fr = fr_ref[...]
        hdn = jnp.sin(fr * (_dot3(z_ref[...], w1_ref[...]) + b1_ref[...]))
        hdn = jnp.sin(fr * (_dot3(hdn, w2_ref[...]) + b2_ref[...]))
        h = _dot3(hdn, w3_ref[...])
        h = h * jnp.exp(-dist_ref[...] * delta_ref[...])
        h = h / jnp.sum(jnp.abs(h), axis=0, keepdims=True)
        h_s[...] = h.astype(BF16)
        hny_ref[...] = _dot_exact_lhs(_alt_rows(h.shape[0]), h)

    hre_ref[...] = jnp.dot(cos_ref[...], h_s[...], preferred_element_type=F32)
    him_ref[...] = jnp.dot(nsin_ref[...], h_s[...], preferred_element_type=F32)


def _hyena_filter_spectrum(n, emb, tables, w1, b1, freq, w2, b2, w3):
    z, dist, deltas = emb
    pad_o = LANES - HY_ORDER
    w1p = jnp.pad(w1, ((0, LANES - HY_EMB), (0, pad_o)))
    w2p = jnp.pad(w2, ((0, pad_o), (0, pad_o)))
    w3p = jnp.pad(w3, ((0, pad_o), (0, 0)))
    row = lambda t: jnp.pad(t.reshape(1, HY_ORDER), ((0, 0), (0, pad_o)))
    tc = HY_W
    kt = min(n, 512)
    full = lambda shape: pl.BlockSpec(shape, lambda j, k: (0, 0))
    tab = pl.BlockSpec((kt, n), lambda j, k: (k, 0))
    spec = pl.BlockSpec((kt, tc), lambda j, k: (k, j))
    return pl.pallas_call(
        _hy_filter_kernel,
        grid=(HY_W // tc, n // kt),
        in_specs=[full((n, LANES)), full((n, 1)), pl.BlockSpec((1, tc), lambda j, k: (0, j)),
                  full((LANES, LANES)), full((1, LANES)), full((1, LANES)),
                  full((LANES, LANES)), full((1, LANES)), pl.BlockSpec((LANES, tc), lambda j, k: (0, j)),
                  tab, tab],
        out_specs=[spec, spec, pl.BlockSpec((8, tc), lambda j, k: (0, j))],
        out_shape=[jax.ShapeDtypeStruct((n, HY_W), F32), jax.ShapeDtypeStruct((n, HY_W), F32),
                   jax.ShapeDtypeStruct((8, HY_W), F32)],
        scratch_shapes=[pltpu.VMEM((n, tc), BF16)],
        compiler_params=_cparams(("arbitrary", "arbitrary")),
        name=f"hyena_filter_{n}",
    )(z, dist, deltas, w1p, row(b1), row(freq), w2p, row(b2), w3p, *tables)


def _hy_pre_kernel(x0_ref, x1_ref, v_ref, w0_ref, w1_ref, w2_ref, b0_ref, b1_ref, b2_ref, x0c_ref, gv_ref,
                   pad_s, x1_s, *, seq_len):
    _zero_conv_borders(pad_s)
    for base in range(0, x0_ref.shape[0], seq_len):
        def conv(x_ref, w_ref, b_ref, store):
            _centred_conv(pad_s, lambda r0, rb: x_ref[base + r0:base + r0 + rb, :].astype(F32), store, seq_len,
                          w_ref, HY_SHORT, b_ref[...])

        def store_x0(r0, y):
            x0c_ref[base + r0:base + r0 + y.shape[0], :] = y.astype(BF16)

        def store_x1(r0, y):
            x1_s[r0:r0 + y.shape[0], :] = y

        def store_gv(r0, y):
            gv_ref[base + r0:base + r0 + y.shape[0], :] = (y * x1_s[r0:r0 + y.shape[0], :]).astype(BF16)

        conv(x0_ref, w0_ref, b0_ref, store_x0)
        conv(x1_ref, w1_ref, b1_ref, store_x1)
        conv(v_ref, w2_ref, b2_ref, store_gv)


def _hyena_pre(p_main, conv_w, conv_b):
    outs = []
    for n_seq, seq_len, row_blk0, spb, tc in ((BATCH, SEQ, 0, 4, HY_W),
                                               (DEC_BATCH, DEC_SEQ, N_PROMPT // DEC_SEQ, 1, HY_W // 2)):
        ncb = HY_W // tc
        rows = spb * seq_len
        tok = lambda cb: pl.BlockSpec((rows, tc), lambda b, j, cb=cb: (row_blk0 + b, cb * LANES // tc + j))
        cw = lambda s: pl.BlockSpec((HY_SHORT, tc), lambda b, j, s=s: (0, s * ncb + j))
        cb_ = lambda s: pl.BlockSpec((1, tc), lambda b, j, s=s: (0, s * ncb + j))
        o_spec = pl.BlockSpec((rows, tc), lambda b, j: (b, j))
        o_shape = jax.ShapeDtypeStruct((n_seq * seq_len, HY_W), BF16)
        outs.append(pl.pallas_call(
            functools.partial(_hy_pre_kernel, seq_len=seq_len),
            grid=(n_seq // spb, ncb),
            in_specs=[tok(CB_X0), tok(CB_X1), tok(CB_HV), cw(0), cw(1), cw(2), cb_(0), cb_(1), cb_(2)],
            out_specs=[o_spec, o_spec], out_shape=[o_shape, o_shape],
            scratch_shapes=[pltpu.VMEM((seq_len + 2 * CONV_PAD, tc), F32), pltpu.VMEM((seq_len, tc), F32)],
            compiler_params=_cparams(("arbitrary", "arbitrary")),
            name=f"hyena_pre_{seq_len}",
        )(p_main, p_main, p_main, conv_w, conv_w, conv_w,
          conv_b.reshape(1, -1), conv_b.reshape(1, -1), conv_b.reshape(1, -1)))
    return outs


HY_FREQ_BLOCK = 512


def _hy_conv_kernel(gv_ref, x0_ref, d_ref, hre_ref, him_ref, hny_ref, cos_ref, nsin_ref, o_ref, *,
                    seq_len, n_seq):
    n = seq_len
    fb = min(n, HY_FREQ_BLOCK)
    chains = [(b, f0) for b in range(n_seq) for f0 in range(0, n, fb)]
    gvs = [gv_ref[pl.ds(b * seq_len, seq_len), :] for b in range(n_seq)]
    gre = [jnp.dot(cos_ref[f0:f0 + fb, :], gvs[b], preferred_element_type=F32) for b, f0 in chains]
    gim = [jnp.dot(nsin_ref[f0:f0 + fb, :], gvs[b], preferred_element_type=F32) for b, f0 in chains]
    spec = []
    for (b, f0), gr, gi in zip(chains, gre, gim):
        hre, him = hre_ref[f0:f0 + fb, :], him_ref[f0:f0 + fb, :]
        freq = lax.broadcasted_iota(jnp.int32, hre.shape, 0) + f0
        wk = jnp.where(freq == 0, 0.5 / n, 1.0 / n)
        q4 = freq % 4
        yre = (gr * hre - gi * him) * wk
        yim = (gr * him + gi * hre) * wk
        spec.append((jnp.where(q4 == 0, yre, jnp.where(q4 == 1, -yim, jnp.where(q4 == 2, -yre, yim))),
                     jnp.where(q4 == 0, yim, jnp.where(q4 == 1, yre, jnp.where(q4 == 2, -yim, -yre)))))
    inv = [_bdot(cos_ref[:, f0:f0 + fb], are) + _bdot(nsin_ref[:, f0:f0 + fb], aim)
           for (b, f0), (are, aim) in zip(chains, spec)]
    t = lax.broadcasted_iota(jnp.int32, (n, gvs[0].shape[1]), 0)
    alt = (1 - 2 * (t % 2)).astype(F32)
    for b in range(n_seq):
        rows = pl.ds(b * seq_len, seq_len)
        gvf = gvs[b].astype(F32)
        g_ny = jnp.sum(gvf * alt, axis=0, keepdims=True)
        y_ny = g_ny * hny_ref[0:1, :] * ((-1.0) ** (n // 2) / (2 * n))
        y = functools.reduce(lambda x, z: x + z, [p for (cb, _), p in zip(chains, inv) if cb == b])
        y = y + alt * y_ny + gvf * d_ref[...]
        o_ref[rows, :] = (y * x0_ref[rows, :].astype(F32)).astype(BF16)


def _hyena_conv(gv, x0c, d_skip, h_spec, tables, *, n_seq_total, seq_len, n_seq_blk):
    tc = 256
    hre, him, hny = h_spec
    cos_t, nsin_t = tables
    rows = n_seq_blk * seq_len
    grid = (HY_W // tc, n_seq_total // n_seq_blk)
    tok = pl.BlockSpec((rows, tc), lambda j, g: (g, j))
    hs = pl.BlockSpec((seq_len, tc), lambda j, g: (0, j))
    table = pl.BlockSpec((seq_len, seq_len), lambda j, g: (0, 0), pipeline_mode=pl.Buffered(1))
    return pl.pallas_call(
        functools.partial(_hy_conv_kernel, seq_len=seq_len, n_seq=n_seq_blk),
        grid=grid,
        in_specs=[tok, tok, pl.BlockSpec((1, tc), lambda j, g: (0, j)), hs, hs,
                  pl.BlockSpec((8, tc), lambda j, g: (0, j)), table, table],
        out_specs=tok,
        out_shape=jax.ShapeDtypeStruct((n_seq_total * seq_len, HY_W), BF16),
        compiler_params=_cparams(("arbitrary", "arbitrary")),
        name=f"hyena_conv_{seq_len}",
    )(gv, x0c, d_skip.reshape(1, HY_W), hre, him, hny, cos_t, nsin_t)


MERGE_TM = 512


def _merge_kernel(x_ref, oap_ref, obp_ref, ocp_ref, oas_ref, obs_ref, ocs_ref, ga_ref, gb_ref, gc_ref,
                  wa_ref, wb_ref, wc_ref, wo_ref, mod_ref, y_ref, wa_s, wb_s, wc_s, wo_s):
    i = pl.program_id(0)

    @pl.when(i == 0)
    def _():
        for src, dst in ((wa_ref, wa_s), (wb_ref, wb_s), (wc_ref, wc_s), (wo_ref, wo_s)):
            dst[...] = src[...].astype(BF16)

    is_prompt = i < N_PROMPT // MERGE_TM
    pick = lambda p_ref, s_ref: jnp.where(is_prompt, p_ref[...], s_ref[...])
    sig = lambda r: jax.nn.sigmoid(r[...].astype(F32))
    merged = (sig(ga_ref) * jnp.dot(pick(oap_ref, oas_ref), wa_s[...], preferred_element_type=F32)
              + sig(gb_ref) * jnp.dot(pick(obp_ref, obs_ref), wb_s[...], preferred_element_type=F32)
              + sig(gc_ref) * jnp.dot(pick(ocp_ref, ocs_ref), wc_s[...], preferred_element_type=F32))
    out = _bdot(merged, wo_s[...])
    y_ref[...] = x_ref[...] + mod_ref[0, 2:3, :] * out


def _merge(x, branches_p, branches_s, p_main, mods, w_a, w_b, w_c, w_out, layer):
    tm = MERGE_TM
    npt = N_PROMPT // tm
    tok = lambda w: pl.BlockSpec((tm, w), lambda i: (i, 0))
    tok_p = pl.BlockSpec((tm, DN_W), lambda i: (jnp.minimum(i, npt - 1), 0))
    tok_s = pl.BlockSpec((tm, DN_W), lambda i: (jnp.maximum(i - npt, 0), 0))
    gate = lambda s: pl.BlockSpec((tm, D_MODEL), lambda i, s=s: (i, CB_GATE // 8 + s))
    wfull = lambda r: pl.BlockSpec((None, r, D_MODEL), lambda i: (layer, 0, 0))
    wscr = lambda r: pltpu.VMEM((r, D_MODEL), BF16)
    return pl.pallas_call(
        _merge_kernel,
        grid=(N_TOK // tm,),
        in_specs=[tok(D_MODEL), tok_p, tok_p, tok_p, tok_s, tok_s, tok_s, gate(0), gate(1), gate(2),
                  wfull(DN_W), wfull(DN_W), wfull(HY_W), wfull(D_MODEL),
                  pl.BlockSpec((1, 6, D_MODEL), lambda i: (_mod_row(i, tm), 0, 0))],
        out_specs=tok(D_MODEL),
        out_shape=jax.ShapeDtypeStruct((N_TOK, D_MODEL), F32),
        scratch_shapes=[wscr(DN_W), wscr(DN_W), wscr(HY_W), wscr(D_MODEL)],
        compiler_params=_cparams(("arbitrary",)),
        name="merge_out_projection",
    )(x, *branches_p, *branches_s, p_main, p_main, p_main, w_a, w_b, w_c, w_out, mods)


FFN_TM = 512
FFN_TF = D_FF // 2


def _ffn_kernel(x_ref, g_ref, mod_ref, wg_ref, wu_ref, wo_ref, fg_ref, *rest, final):
    if final:
        yp_ref, ys_ref, h_scr, acc_scr = rest
    else:
        y_ref, h_scr, acc_scr = rest
    f = pl.program_id(1)
    nf = pl.num_programs(1)

    @pl.when(f == 0)
    def _():
        g = g_ref[...]
        sh = mod_ref[0, 3:4, :]
        sc1 = 1.0 + mod_ref[0, 4:5, :]

        def body(r, carry):
            rows = pl.ds(pl.multiple_of(r * ROW_CHUNK, ROW_CHUNK), ROW_CHUNK)
            h_scr[rows, :] = (_rms(x_ref[rows, :], g) * sc1 + sh).astype(BF16)
            return carry

        lax.fori_loop(0, FFN_TM // ROW_CHUNK, body, 0)

    h = h_scr[...]
    gate = jnp.dot(h, wg_ref[...], preferred_element_type=F32)
    up = jnp.dot(h, wu_ref[...], preferred_element_type=F32)
    part = _bdot(_silu(gate) * up, wo_ref[...])

    @pl.when(f == 0)
    def _():
        acc_scr[...] = part

    @pl.when(f > 0)
    def _():
        acc_scr[...] += part

    @pl.when(f == nf - 1)
    def _():
        y = x_ref[...] + mod_ref[0, 5:6, :] * acc_scr[...]
        if final:
            y = _rms(y, fg_ref[...])
            is_prompt = pl.program_id(0) < N_PROMPT // FFN_TM

            @pl.when(is_prompt)
            def _():
                yp_ref[...] = y

            @pl.when(jnp.logical_not(is_prompt))
            def _():
                ys_ref[...] = y
        else:
            y_ref[...] = y


def _ffn(x, norm_g, mods, w_in, w_out, final_g, layer, *, final):
    tm, tf = FFN_TM, FFN_TF
    nf = D_FF // tf
    if final:
        npt = N_PROMPT // tm
        out_specs = [pl.BlockSpec((tm, D_MODEL), lambda i, f: (jnp.minimum(i, npt - 1), 0)),
                     pl.BlockSpec((tm, D_MODEL), lambda i, f: (jnp.maximum(i - npt, 0), 0))]
        out_shape = [jax.ShapeDtypeStruct((N_PROMPT, D_MODEL), F32), jax.ShapeDtypeStruct((N_SAMPLE, D_MODEL), F32)]
    else:
        out_specs = pl.BlockSpec((tm, D_MODEL), lambda i, f: (i, 0))
        out_shape = jax.ShapeDtypeStruct((N_TOK, D_MODEL), F32)
    return pl.pallas_call(
        functools.partial(_ffn_kernel, final=final),
        grid=(N_TOK // tm, nf),
        in_specs=[pl.BlockSpec((tm, D_MODEL), lambda i, f: (i, 0)),
                  pl.BlockSpec((1, D_MODEL), lambda i, f: (0, 0)),
                  pl.BlockSpec((1, 6, D_MODEL), lambda i, f: (_mod_row(i, tm), 0, 0)),
                  pl.BlockSpec((None, D_MODEL, tf), lambda i, f: (layer, 0, f)),
                  pl.BlockSpec((None, D_MODEL, tf), lambda i, f: (layer, 0, nf + f)),
                  pl.BlockSpec((None, tf, D_MODEL), lambda i, f: (layer, f, 0)),
                  pl.BlockSpec((1, D_MODEL), lambda i, f: (0, 0))],
        out_specs=out_specs,
        out_shape=out_shape,
        scratch_shapes=[pltpu.VMEM((tm, D_MODEL), BF16), pltpu.VMEM((tm, D_MODEL), F32)],
        compiler_params=_cparams(("arbitrary", "arbitrary")),
        name="ffn",
    )(x, norm_g.reshape(1, D_MODEL), mods, w_in, w_in, w_out, final_g.reshape(1, D_MODEL))


def kernel(x_prompt, x_sample, cache_k, cache_v, state_dn, c, c_ctx, norm1_g, norm2_g, w_mod, b_mod,
           w_in, dn_conv_w, dn_a_log, dn_dt_bias, dn_norm_g, da_lambda, da_subln_g, hy_conv_w,
           hy_conv_b, hy_w1, hy_b1, hy_freq, hy_w2, hy_b2, hy_w3, hy_d, w_br_a, w_br_b, w_br_c,
           w_out, w_ffn_in, w_ffn_out, final_g):
    x = jnp.concatenate([x_prompt.reshape(N_PROMPT, D_MODEL), x_sample.reshape(N_SAMPLE, D_MODEL)], axis=0)
    cond8 = jnp.concatenate([c_ctx[None, :], c, jnp.zeros((8 - 1 - DEC_BATCH, D_MODEL), F32)], axis=0)
    rope = _rope_tables(DEC_SEQ)
    tab_p = _dft_tables(SEQ)
    tab_s = _dft_tables(DEC_SEQ)
    emb_p, emb_s = _hy_embedding(SEQ), _hy_embedding(DEC_SEQ)
    sample_blk0 = N_PROMPT // DEC_SEQ
    w_main, w_small = _pack_w_in(w_in)
    w_ffn_in_b, w_ffn_out_b = w_ffn_in.astype(BF16), w_ffn_out.astype(BF16)

    new_k = new_v = new_s = None
    for l in range(DEPTH):
        mods = _modulation(cond8, w_mod, b_mod[l], l).reshape(8, 6, D_MODEL)
        p_main, p_small = _in_projection(x, norm1_g[l], mods, w_main, w_small, l)

        dn_args = (p_main, p_small, dn_conv_w[l], dn_a_log[l], dn_dt_bias[l], dn_norm_g[l])
        oa_p, new_s = _deltanet(*dn_args, None, n_seq=BATCH, seq_len=SEQ, row_blk0=0, hps=4, layer=l,
                                states_so_far=new_s)
        oa_s, _ = _deltanet(*dn_args, state_dn, n_seq=DEC_BATCH, seq_len=DEC_SEQ, row_blk0=sample_blk0,
                            hps=2, layer=l)

        ob_p, new_k, new_v = _diff_attention(p_main, da_lambda[l], da_subln_g[l], l, n_seq=BATCH, seq_len=SEQ,
                                             row_blk0=0, tq=SEQ, caches_so_far=(new_k, new_v))
        ob_s, _, _ = _diff_attention(p_main, da_lambda[l], da_subln_g[l], l, n_seq=DEC_BATCH,
                                     seq_len=DEC_SEQ, row_blk0=sample_blk0, tq=1024, rope=rope,
                                     ctx=(cache_k, cache_v))

        hy_w = (hy_w1[l], hy_b1[l], hy_freq[l], hy_w2[l], hy_b2[l], hy_w3[l])
        hspec_p = _hyena_filter_spectrum(SEQ, emb_p, tab_p, *hy_w)
        hspec_s = _hyena_filter_spectrum(DEC_SEQ, emb_s, tab_s, *hy_w)
        (x0_p, gv_p), (x0_s, gv_s) = _hyena_pre(p_main, hy_conv_w[l], hy_conv_b[l])
        oc_p = _hyena_conv(gv_p, x0_p, hy_d[l], hspec_p, tab_p, n_seq_total=BATCH, seq_len=SEQ,
                           n_seq_blk=BATCH)
        oc_s = _hyena_conv(gv_s, x0_s, hy_d[l], hspec_s, tab_s, n_seq_total=DEC_BATCH, seq_len=DEC_SEQ,
                           n_seq_blk=1)

        x = _merge(x, (oa_p, ob_p, oc_p), (oa_s, ob_s, oc_s), p_main, mods,
                   w_br_a, w_br_b, w_br_c, w_out, l)
        x = _ffn(x, norm2_g[l], mods, w_ffn_in_b, w_ffn_out_b, final_g, l, final=(l == DEPTH - 1))

    y_prompt = x[0].reshape(BATCH, SEQ, D_MODEL)
    y_sample = x[1].reshape(DEC_BATCH, DEC_SEQ, D_MODEL)
    return (y_prompt, y_sample, new_k, new_v, new_s)
```

```python
import functools
import math

import jax
import jax.numpy as jnp
from jax import lax
from jax.experimental import pallas as pl
from jax.experimental.pallas import tpu as pltpu

F32 = jnp.float32
BF16 = jnp.bfloat16

D_MODEL = 1024
BATCH = 16
SEQ = 256
DEPTH = 2
DEC_BATCH = 4
DEC_SEQ = 2048
PAST_LEN = 256
GRID_W = 64
RMS_EPS = 1e-6
DN_HEADS = 4
DN_DK = 128
DN_DV = 128
DN_W = DN_HEADS * DN_DK
DN_CONV = 5
DN_CHUNK = 64
DA_HEADS = 4
DA_DH = 64
DA_DV = 2 * DA_DH
ROPE_BASE = 10000.0
HY_W = 512
HY_SHORT = 3
HY_BANDS = 8
HY_EMB = 1 + 2 * HY_BANDS
HY_ORDER = 64
HY_FAST_DECAY = 0.3
HY_SLOW_DECAY = 1.5
HY_TARGET = 1e-2
D_FF = ((8 * D_MODEL // 3 + 255) // 256) * 256

N_PROMPT = BATCH * SEQ
N_SAMPLE = DEC_BATCH * DEC_SEQ
N_TOK = N_PROMPT + N_SAMPLE
LANES = 128
N_MAIN = 8192
CB_QA, CB_KA, CB_VA, CB_ZA = 0, 4, 8, 12
CB_QB, CB_KB, CB_VB = 16, 20, 24
CB_X0, CB_X1, CB_HV = 28, 32, 36
CB_GATE = 40
VMEM_LIMIT = 56 * 1024 * 1024


def _cparams(sem):
    return pltpu.CompilerParams(dimension_semantics=sem, vmem_limit_bytes=VMEM_LIMIT)


def _bdot(a, b):
    return jnp.dot(a.astype(BF16), b.astype(BF16), preferred_element_type=F32)


def _bdot_nt(a, b):
    return lax.dot_general(a.astype(BF16), b.astype(BF16), (((1,), (1,)), ((), ())),
                           preferred_element_type=F32)


def _bdot_tn(a, b):
    return lax.dot_general(a.astype(BF16), b.astype(BF16), (((0,), (0,)), ((), ())),
                           preferred_element_type=F32)


def _split3(x):
    hi = x.astype(BF16)
    r = x - hi.astype(F32)
    mid = r.astype(BF16)
    lo = (r - mid.astype(F32)).astype(BF16)
    return hi, mid, lo


def _dot_exact_lhs(t, x):
    hi, mid, lo = _split3(x)
    tb = t.astype(BF16)
    d = lambda p: jnp.dot(tb, p, preferred_element_type=F32)
    return d(hi) + d(mid) + d(lo)


def _dot_exact_rhs(x, e):
    hi, mid, lo = _split3(x)
    eb = e.astype(BF16)
    d = lambda p: jnp.dot(p, eb, preferred_element_type=F32)
    return d(hi) + d(mid) + d(lo)


def _dot3(a, b):
    ah = a.astype(BF16)
    al = (a - ah.astype(F32)).astype(BF16)
    bh = b.astype(BF16)
    bl = (b - bh.astype(F32)).astype(BF16)
    d = lambda p, q: jnp.dot(p, q, preferred_element_type=F32)
    return d(ah, bh) + d(ah, bl) + d(al, bh)


def _silu(x):
    return x * jax.nn.sigmoid(x)


def _rms(x, g):
    return x * lax.rsqrt(jnp.mean(x * x, axis=-1, keepdims=True) + RMS_EPS) * g


def _mod_row(i, tm):
    n_prompt_tiles = N_PROMPT // tm
    tiles_per_seq = DEC_SEQ // tm
    return jnp.where(i < n_prompt_tiles, 0, 1 + (i - n_prompt_tiles) // tiles_per_seq)


def _mod_kernel(c_ref, w_ref, b_ref, o_ref):
    o_ref[...] = _bdot(_silu(c_ref[...]), w_ref[...]) + b_ref[...]


def _modulation(cond8, w_mod, b_mod, layer):
    n = 6 * D_MODEL
    tn = 1024
    return pl.pallas_call(
        _mod_kernel,
        grid=(n // tn,),
        in_specs=[pl.BlockSpec((8, D_MODEL), lambda j: (0, 0)),
                  pl.BlockSpec((None, D_MODEL, tn), lambda j: (layer, 0, j)),
                  pl.BlockSpec((1, tn), lambda j: (0, j))],
        out_specs=pl.BlockSpec((8, tn), lambda j: (0, j)),
        out_shape=jax.ShapeDtypeStruct((8, n), F32),
        compiler_params=_cparams(("arbitrary",)),
        name="modulation",
    )(cond8, w_mod, b_mod.reshape(1, n))


IN_TM = 2048
IN_TN = 2048
ROW_CHUNK = 256


def _inproj_kernel(x_ref, g_ref, mod_ref, wm_ref, ws_ref, p_ref, ps_ref, h_scr):
    @pl.when(pl.program_id(1) == 0)
    def _():
        g = g_ref[...]
        sh = mod_ref[0, 0:1, :]
        sc1 = 1.0 + mod_ref[0, 1:2, :]

        def body(r, carry):
            rows = pl.ds(pl.multiple_of(r * ROW_CHUNK, ROW_CHUNK), ROW_CHUNK)
            hb = (_rms(x_ref[rows, :], g) * sc1 + sh).astype(BF16)
            h_scr[rows, :] = hb
            ps_ref[rows, :] = jnp.dot(hb, ws_ref[...], preferred_element_type=F32)
            return carry

        lax.fori_loop(0, IN_TM // ROW_CHUNK, body, 0)

    p_ref[...] = jnp.dot(h_scr[...], wm_ref[...], preferred_element_type=F32).astype(BF16)


N_GATE_COLS = 4 * DN_HEADS
PACK_ROWS = 128
PACK_COLS = 512


def _pack_w_in_kernel(w_ref, main_ref, small_ref):
    lo = 3 * DN_W
    main_ref[:, :lo] = w_ref[:, :lo].astype(BF16)
    for c in range(lo, N_MAIN, PACK_COLS):
        main_ref[:, c:c + PACK_COLS] = w_ref[:, c + N_GATE_COLS:c + N_GATE_COLS + PACK_COLS].astype(BF16)
    gate = w_ref[:, lo:lo + LANES]
    lane = lax.broadcasted_iota(jnp.int32, gate.shape, 1)
    small_ref[...] = jnp.where(lane < N_GATE_COLS, gate, 0.0).astype(BF16)


def _pack_w_in(w_in):
    n_in = w_in.shape[2]
    return pl.pallas_call(
        _pack_w_in_kernel,
        grid=(DEPTH, D_MODEL // PACK_ROWS),
        in_specs=[pl.BlockSpec((None, PACK_ROWS, n_in), lambda l, r: (l, r, 0))],
        out_specs=[pl.BlockSpec((None, PACK_ROWS, N_MAIN), lambda l, r: (l, r, 0)),
                   pl.BlockSpec((None, PACK_ROWS, LANES), lambda l, r: (l, r, 0))],
        out_shape=[jax.ShapeDtypeStruct((DEPTH, D_MODEL, N_MAIN), BF16),
                   jax.ShapeDtypeStruct((DEPTH, D_MODEL, LANES), BF16)],
        compiler_params=_cparams(("arbitrary", "arbitrary")),
        name="pack_w_in",
    )(w_in)


def _in_projection(x, norm_g, mods, w_main, w_small, layer):
    grid = (N_TOK // IN_TM, N_MAIN // IN_TN)
    return pl.pallas_call(
        _inproj_kernel,
        grid=grid,
        in_specs=[pl.BlockSpec((IN_TM, D_MODEL), lambda i, j: (i, 0)),
                  pl.BlockSpec((1, D_MODEL), lambda i, j: (0, 0)),
                  pl.BlockSpec((1, 6, D_MODEL), lambda i, j: (_mod_row(i, IN_TM), 0, 0)),
                  pl.BlockSpec((None, D_MODEL, IN_TN), lambda i, j: (layer, 0, j)),
                  pl.BlockSpec((None, D_MODEL, LANES), lambda i, j: (layer, 0, 0))],
        out_specs=[pl.BlockSpec((IN_TM, IN_TN), lambda i, j: (i, j)),
                   pl.BlockSpec((IN_TM, LANES), lambda i, j: (i, 0))],
        out_shape=[jax.ShapeDtypeStruct((N_TOK, N_MAIN), BF16),
                   jax.ShapeDtypeStruct((N_TOK, LANES), F32)],
        scratch_shapes=[pltpu.VMEM((IN_TM, D_MODEL), BF16)],
        compiler_params=_cparams(("arbitrary", "arbitrary")),
        name="in_projection",
    )(x, norm_g.reshape(1, D_MODEL), mods, w_main, w_small)


DN_UNIT = 2 * DN_CHUNK
DN_GROUP = 8


CONV_PAD = 8


def _zero_conv_borders(pad_ref):
    n = pad_ref.shape[0] - 2 * CONV_PAD
    zeros = jnp.zeros((CONV_PAD, pad_ref.shape[1]), F32)
    pad_ref[0:CONV_PAD, :] = zeros
    pad_ref[CONV_PAD + n:, :] = zeros


CONV_ROWS = 512


def _centred_conv(pad_ref, load, store, n, w_ref, n_taps, bias=None):
    rb = min(n, CONV_ROWS)
    half = n_taps // 2
    for r0 in range(0, n, rb):
        pad_ref[CONV_PAD + r0:CONV_PAD + r0 + rb, :] = load(r0, rb)
    for r0 in range(0, n, rb):
        acc = None
        for tap in range(n_taps):
            lo = CONV_PAD + r0 + tap - half
            term = pad_ref[lo:lo + rb, :] * w_ref[tap:tap + 1, :]
            acc = term if acc is None else acc + term
        store(r0, acc if bias is None else acc + bias)


def _l2norm(x):
    return x * lax.rsqrt(jnp.sum(x * x, axis=-1, keepdims=True) + RMS_EPS)


def _softplus(x):
    return jnp.maximum(x, 0.0) + jnp.log1p(jnp.exp(-jnp.abs(x)))


DN_BASE = 16


def _unit_tri_inverse(a_list, ri, ci):
    eye = (ri == ci).astype(F32)
    blk = lambda b: (ri // b) == (ci // b)
    y = [jnp.where(blk(DN_BASE), -a, 0.0) for a in a_list]
    p = [eye + yi for yi in y]
    for _ in range(3):
        y = [_bdot(yi, yi) for yi in y]
        p = [pi + _bdot(yi, pi) for yi, pi in zip(y, p)]
    b = DN_BASE
    while b < DN_CHUNK:
        off_mask = blk(2 * b) & ~blk(b)
        t = [_bdot(jnp.where(off_mask, a, 0.0), pi) for a, pi in zip(a_list, p)]
        p = [pi - _bdot(pi, ti) for pi, ti in zip(p, t)]
        b *= 2
    return p


def _dn_units(chains):
    u, c = DN_UNIT, DN_CHUNK
    ri = lax.broadcasted_iota(jnp.int32, (u, u), 0)
    ci = lax.broadcasted_iota(jnp.int32, (u, u), 1)
    same = (ri // c) == (ci // c)
    eye_mask = ri == ci
    hi_rows = ri >= c
    incl_of = {False: same & (ri >= ci), True: same & (ri <= ci)}
    strict_of = {False: same & (ri > ci), True: same & (ri < ci)}
    incl = [incl_of[ch['backward']] for ch in chains]
    strict = [strict_of[ch['backward']] for ch in chains]
    gc = [_dot_exact_lhs(m.astype(F32), ch['g']) for m, ch in zip(incl, chains)]
    g_tot = [(x[0:1, :], x[c:c + 1, :]) if ch['backward'] else (x[c - 1:c, :], x[u - 1:u, :])
             for x, ch in zip(gc, chains)]
    gc_row = [jnp.sum(jnp.where(eye_mask, x, 0.0), axis=0, keepdims=True) for x in gc]
    dec = [jnp.exp(jnp.where(m, x - xr, -1e30)) for m, x, xr in zip(incl, gc, gc_row)]
    e_gc = [jnp.exp(x) for x in gc]
    a = [jnp.where(m, ch['kk'] * ch['beta'] * d, 0.0) for m, ch, d in zip(strict, chains, dec)]
    qk = [ch['qk_raw'] * d for ch, d in zip(chains, dec)]
    eye = eye_mask.astype(F32)
    r = [p - eye for p in _unit_tri_inverse(a, ri, ci)]
    rhs = [jnp.concatenate([ch['v'] * ch['beta'], ch['k'] * (ch['beta'] * e)], axis=1)
           for ch, e in zip(chains, e_gc)]
    uwb = [(x + _bdot(ri_, x)).astype(BF16) for ri_, x in zip(r, rhs)]
    qkuw = [_bdot(x, y) for x, y in zip(qk, uwb)]
    kd = [ch['k'] * jnp.exp(jnp.where(hi_rows, gt[1], gt[0]) - x) for ch, gt, x in zip(chains, g_tot, gc)]
    tp = [[_bdot_tn(jnp.where(keep, x, 0.0), y) for x, y in zip(kd, uwb)]
          for keep in (~hi_rows, hi_rows)]
    out = []
    for i, ch in enumerate(chains):
        o_local = qkuw[i][:, :DN_DV]
        q_eff = ch['q'] * e_gc[i] - qkuw[i][:, DN_DV:]
        per_chunk = [(-tp[h][i][:, DN_DV:], tp[h][i][:, :DN_DV], jnp.exp(g_tot[i][h])) for h in range(2)]
        out.append((o_local, q_eff, per_chunk))
    return out


def _dn_kernel(*refs, seq_len, has_state, hps, layer):
    if has_state:
        (q_ref, k_ref, v_ref, z_ref, ps_ref, cwq_ref, cwk_ref, cwv_ref, alog_ref, dtb_ref, ng_ref,
         s0_ref, o_ref, q_s, k_s, v_s, bt_s, g_s, oacc_s, qe_s, th_s, psi_s, egl_s, st_s, pad_s) = refs
        sfin_ref = None
    else:
        refs = refs[:11] + refs[(11 if layer == 0 else 12):]
        (q_ref, k_ref, v_ref, z_ref, ps_ref, cwq_ref, cwk_ref, cwv_ref, alog_ref, dtb_ref, ng_ref,
         o_ref, sfin_ref, q_s, k_s, v_s, bt_s, g_s, oacc_s, qe_s, th_s, psi_s, egl_s, st_s, pad_s) = refs
        s0_ref = None
    n_units = seq_len // DN_UNIT
    n_chunks = seq_len // DN_CHUNK
    c = DN_CHUNK
    sel_r = lax.broadcasted_iota(jnp.int32, (LANES, 4 * LANES), 0)
    sel_blk = lax.broadcasted_iota(jnp.int32, (LANES, 4 * LANES), 1) // LANES
    sels = []
    for j in range(hps):
        head = pl.program_id(1) * hps + j
        src_col = head + DN_HEADS * jnp.where(sel_blk == 0, 0, jnp.where(sel_blk == 1, 2, jnp.where(sel_blk == 2, 1, 3)))
        sels.append((sel_r == src_col).astype(BF16))

    def gate_rows(r, carry):
        rows = pl.ds(pl.multiple_of(r * ROW_CHUNK, ROW_CHUNK), ROW_CHUNK)
        ps = ps_ref[rows, :]
        lane = lax.broadcasted_iota(jnp.int32, ps.shape, 1)
        gate_cols = jnp.where(lane < 2 * DN_HEADS, jax.nn.sigmoid(ps),
                              -(jnp.exp(alog_ref[...]) * _softplus(ps + dtb_ref[...])))
        parts = _split3(gate_cols)
        for j in range(hps):
            ext = functools.reduce(lambda a, b: a + b,
                                   [jnp.dot(part, sels[j], preferred_element_type=F32) for part in parts])
            for d in range(2):
                bt_s[j, d, rows, :] = ext[:, (2 * d) * LANES:(2 * d + 1) * LANES]
                g_s[j, d, rows, :] = ext[:, (2 * d + 1) * LANES:(2 * d + 2) * LANES]
        return carry

    lax.fori_loop(0, seq_len // ROW_CHUNK, gate_rows, 0, unroll=min(seq_len // ROW_CHUNK, 2))

    _zero_conv_borders(pad_s)
    for j in range(hps):
        cols = slice(j * LANES, (j + 1) * LANES)
        for x_ref, w_ref, out_s, post in (
                (q_ref, cwq_ref, q_s, lambda y: _l2norm(_silu(y)) * (DN_DK ** -0.5)),
                (k_ref, cwk_ref, k_s, lambda y: _l2norm(_silu(y))),
                (v_ref, cwv_ref, v_s, _silu)):
            def store(r0, y, out_s=out_s, post=post):
                out_s[j, r0:r0 + y.shape[0], :] = post(y)
            _centred_conv(pad_s, lambda r0, rb, x_ref=x_ref: x_ref[r0:r0 + rb, cols].astype(F32), store,
                          seq_len, w_ref.at[:, cols], DN_CONV)
        for d in range(2):
            st_s[j, d] = s0_ref[0, 0, d, j] if has_state else jnp.zeros((DN_DK, DN_DV), F32)

    def unit_group(grp, carry):
        where, chains = [], []
        for t in range(DN_GROUP):
            idx = grp * DN_GROUP + t
            j = idx // n_units
            n = idx % n_units
            rows = pl.ds(pl.multiple_of(n * DN_UNIT, DN_UNIT), DN_UNIT)
            qc, kc, vc = q_s[j, rows, :], k_s[j, rows, :], v_s[j, rows, :]
            kcb = kc.astype(BF16)
            kk = _bdot_nt(kcb, kcb)
            qk_raw = _bdot_nt(qc, kcb)
            for d in range(2):
                where.append((j, n, rows, d))
                chains.append(dict(q=qc, k=kc, v=vc, kk=kk, qk_raw=qk_raw, beta=bt_s[j, d, rows, :],
                                   g=g_s[j, d, rows, :], backward=(d == 1)))
        results = _dn_units(chains)
        for (j, n, rows, d), (o_loc, q_eff, per_chunk) in zip(where, results):
            qe_s[j, d, rows, :] = q_eff.astype(BF16)
            for half, (theta, psi, egl) in enumerate(per_chunk):
                th_s[j, d, 2 * n + half] = theta.astype(BF16)
                psi_s[j, d, 2 * n + half] = psi
                egl_s[j, d, 2 * n + half] = egl
            if d == 0:
                o_fwd = o_loc
            else:
                oacc_s[j, rows, :] = o_fwd + o_loc
        return carry

    lax.fori_loop(0, hps * n_units // DN_GROUP, unit_group, 0)

    def scan_step(i, carry):
        for j in range(hps):
            for d, n in ((0, i), (1, n_chunks - 1 - i)):
                rows = pl.ds(pl.multiple_of(n * c, c), c)
                s = st_s[j, d]
                sb = s.astype(BF16)
                oacc_s[j, rows, :] += jnp.dot(qe_s[j, d, rows, :], sb, preferred_element_type=F32)
                st_s[j, d] = (egl_s[j, d, n] * s + jnp.dot(th_s[j, d, n], sb, preferred_element_type=F32)
                              + psi_s[j, d, n])
        return carry

    lax.fori_loop(0, n_chunks, scan_step, 0, unroll=min(n_chunks, 8))

    if sfin_ref is not None:
        for j in range(hps):
            for d in range(2):
                sfin_ref[0, 0, d, j] = st_s[j, d]
                if layer == 0:
                    for later in range(1, DEPTH):
                        sfin_ref[0, later, d, j] = jnp.zeros((DN_DK, DN_DV), F32)

    def out_rows(r, carry):
        rows = pl.ds(pl.multiple_of(r * ROW_CHUNK, ROW_CHUNK), ROW_CHUNK)
        for j in range(hps):
            cols = slice(j * LANES, (j + 1) * LANES)
            o = _rms(oacc_s[j, rows, :], ng_ref[...]) * _silu(z_ref[rows, cols].astype(F32))
            o_ref[rows, cols] = o.astype(BF16)
        return carry

    lax.fori_loop(0, seq_len // ROW_CHUNK, out_rows, 0, unroll=min(seq_len // ROW_CHUNK, 4))


def _deltanet(p_main, p_small, conv_w, a_log, dt_bias, norm_g, s0, *, n_seq, seq_len, row_blk0, hps, layer,
              states_so_far=None):
    has_state = s0 is not None
    n_chunks = seq_len // DN_CHUNK
    width = hps * LANES
    tok = lambda cb: pl.BlockSpec((seq_len, width), lambda b, h: (row_blk0 + b, cb // hps + h))
    cw = lambda cb: pl.BlockSpec((DN_CONV, width), lambda b, h: (0, cb // hps + h))
    in_specs = [tok(CB_QA), tok(CB_KA), tok(CB_VA), tok(CB_ZA),
                pl.BlockSpec((seq_len, LANES), lambda b, h: (row_blk0 + b, 0)),
                cw(0), cw(DN_HEADS), cw(2 * DN_HEADS),
                pl.BlockSpec((1, LANES), lambda b, h: (0, 0)),
                pl.BlockSpec((1, LANES), lambda b, h: (0, 0)),
                pl.BlockSpec((1, DN_DV), lambda b, h: (0, 0))]
    pad8 = lambda t: jnp.pad(t.reshape(1, 2 * DN_HEADS), ((0, 0), (2 * DN_HEADS, LANES - 4 * DN_HEADS)))
    args = [p_main, p_main, p_main, p_main, p_small, conv_w, conv_w, conv_w,
            pad8(a_log), pad8(dt_bias), norm_g.reshape(1, DN_DV)]
    o_spec = pl.BlockSpec((seq_len, width), lambda b, h: (b, h))
    o_shape = jax.ShapeDtypeStruct((n_seq * seq_len, DN_W), BF16)
    aliases = {}
    if has_state:
        in_specs.append(pl.BlockSpec((1, 1, 2, hps, DN_DK, DN_DV), lambda b, h: (b, layer, 0, h, 0, 0)))
        args.append(s0)
        out_specs, out_shape = o_spec, o_shape
    else:
        if layer == 0:
            s_spec = pl.BlockSpec((1, DEPTH, 2, hps, DN_DK, DN_DV), lambda b, h: (b, 0, 0, h, 0, 0))
        else:
            s_spec = pl.BlockSpec((1, 1, 2, hps, DN_DK, DN_DV), lambda b, h: (b, layer, 0, h, 0, 0))
            in_specs.append(pl.BlockSpec(memory_space=pl.ANY))
            args.append(states_so_far)
            aliases = {len(args) - 1: 1}
        out_specs = [o_spec, s_spec]
        out_shape = [o_shape, jax.ShapeDtypeStruct((n_seq, DEPTH, 2, DN_HEADS, DN_DK, DN_DV), F32)]
    scratch = [pltpu.VMEM((hps, seq_len, LANES), F32),
               pltpu.VMEM((hps, seq_len, LANES), F32),
               pltpu.VMEM((hps, seq_len, LANES), F32),
               pltpu.VMEM((hps, 2, seq_len, LANES), F32),
               pltpu.VMEM((hps, 2, seq_len, LANES), F32),
               pltpu.VMEM((hps, seq_len, LANES), F32),
               pltpu.VMEM((hps, 2, seq_len, LANES), BF16),
               pltpu.VMEM((hps, 2, n_chunks, DN_DK, DN_DK), BF16),
               pltpu.VMEM((hps, 2, n_chunks, DN_DK, DN_DV), F32),
               pltpu.VMEM((hps, 2, n_chunks, 1, LANES), F32),
               pltpu.VMEM((hps, 2, DN_DK, DN_DV), F32),
               pltpu.VMEM((seq_len + 2 * CONV_PAD, LANES), F32)]
    res = pl.pallas_call(
        functools.partial(_dn_kernel, seq_len=seq_len, has_state=has_state, hps=hps, layer=layer),
        grid=(n_seq, DN_HEADS // hps),
        in_specs=in_specs, out_specs=out_specs, out_shape=out_shape,
        input_output_aliases=aliases,
        scratch_shapes=scratch,
        compiler_params=_cparams(("arbitrary", "arbitrary")),
        name=f"deltanet_{seq_len}",
    )(*args)
    return (res, None) if has_state else (res[0], res[1])


def _rope(x, cos, sin_signed):
    lane = lax.broadcasted_iota(jnp.int32, x.shape, 1)
    partner = jnp.where((lane % 32) < 16, pltpu.roll(x, LANES - 16, axis=1), pltpu.roll(x, 16, axis=1))
    return x * cos + partner * sin_signed


DA_ROWS = 128


def _exp2_rows(s_parts):
    m = functools.reduce(jnp.maximum, [jnp.max(s, axis=-1, keepdims=True) for s in s_parts])
    return [jnp.exp2(s - m).astype(BF16) for s in s_parts]


def _da_kernel(*refs, layer, latent):
    if latent:
        (q_ref, k_ref, v_ref, lam_ref, sg_ref, cq_ref, sq_ref, ck_ref, sk_ref, ctxk_ref, ctxv_ref,
         o_ref, krot_s, vext_s) = refs
    else:
        refs = refs[:5] + refs[(5 if layer == 0 else 7):]
        q_ref, k_ref, v_ref, lam_ref, sg_ref, o_ref, ko_ref, vo_ref = refs
    with_ones = lambda v: jnp.concatenate([v, jnp.ones(v.shape, BF16)], axis=1)
    lam_init = 0.8 - 0.6 * math.exp(-0.3 * layer)
    lp = lam_ref[...]
    dots = jnp.sum(jnp.concatenate([lp[0:1] * lp[1:2], lp[2:3] * lp[3:4]], axis=0), axis=1, keepdims=True)
    e = jnp.exp(dots)
    lam = e[0:1, :] - e[1:2, :] + lam_init

    n_heads = q_ref.shape[1] // LANES
    tq = q_ref.shape[0]
    heads = []
    if latent:
        @pl.when(pl.program_id(2) == 0)
        def _():
            krot_s[...] = _rope(k_ref[...].astype(F32), ck_ref[...], sk_ref[...]).astype(BF16)
            vext_s[...] = with_ones(v_ref[...])
        heads.append((_rope(q_ref[...].astype(F32), cq_ref[...], sq_ref[...]),
                      [ctxk_ref[0, 0, 0].astype(BF16), krot_s[...]],
                      [with_ones(ctxv_ref[0, 0, 0].astype(BF16)), vext_s[...]]))
    else:
        for j in range(n_heads):
            cols = slice(j * LANES, (j + 1) * LANES)
            heads.append((q_ref[:, cols].astype(F32), [k_ref[:, cols]], [with_ones(v_ref[:, cols])]))
            ko_ref[0, 0, j] = k_ref[:, cols].astype(F32)
            vo_ref[0, 0, j] = v_ref[:, cols].astype(F32)
            if layer == 0:
                for later in range(1, DEPTH):
                    ko_ref[0, later, j] = jnp.zeros((tq, LANES), F32)
                    vo_ref[0, later, j] = jnp.zeros((tq, LANES), F32)
    lane = lax.broadcasted_iota(jnp.int32, (tq, LANES), 1)
    groups = []
    for q, keys, vals in heads:
        q = q * (DA_DH ** -0.5 * math.log2(math.e))
        q12 = jnp.concatenate([jnp.where(lane < DA_DH, q, 0.0), jnp.where(lane >= DA_DH, q, 0.0)],
                              axis=0).astype(BF16)
        groups += [(q12[r:r + DA_ROWS], keys, vals) for r in range(0, 2 * tq, DA_ROWS)]
    scores = [[_bdot_nt(qg, kk) for kk in keys] for qg, keys, _ in groups]
    soft = [_exp2_rows(sg) for sg in scores]
    pv = []
    for e_parts, (_, _, vals) in zip(soft, groups):
        acc = None
        for e, vv in zip(e_parts, vals):
            part = _bdot(e, vv)
            acc = part if acc is None else acc + part
        pv.append(acc[:, :DA_DV] * (1.0 / acc[:, DA_DV:]))
    per_head = 2 * tq // DA_ROWS
    for j in range(len(heads)):
        hp = jnp.concatenate(pv[j * per_head:(j + 1) * per_head], axis=0)
        o = hp[:tq] - lam * hp[tq:]
        o_ref[:, j * LANES:(j + 1) * LANES] = (_rms(o, sg_ref[...]) * (1.0 - lam_init)).astype(BF16)


def _diff_attention(p_main, lam_p, subln_g, layer, *, n_seq, seq_len, row_blk0, tq, rope=None, ctx=None,
                    caches_so_far=None):
    latent = ctx is not None
    aliases = {}
    hps = 1 if latent else DA_HEADS
    width = hps * LANES
    nq = seq_len // tq
    qpb = seq_len // tq
    in_specs = [pl.BlockSpec((tq, width), lambda b, h, i: ((row_blk0 + b) * qpb + i, CB_QB // hps + h)),
                pl.BlockSpec((seq_len, width), lambda b, h, i: (row_blk0 + b, CB_KB // hps + h)),
                pl.BlockSpec((seq_len, width), lambda b, h, i: (row_blk0 + b, CB_VB // hps + h)),
                pl.BlockSpec((4, LANES), lambda b, h, i: (0, 0)),
                pl.BlockSpec((1, DA_DV), lambda b, h, i: (0, 0))]
    args = [p_main, p_main, p_main, jnp.pad(lam_p, ((0, 0), (0, LANES - DA_DH))), subln_g.reshape(1, DA_DV)]
    o_spec = pl.BlockSpec((tq, width), lambda b, h, i: (b * qpb + i, h))
    o_shape = jax.ShapeDtypeStruct((n_seq * seq_len, DA_HEADS * DA_DV), BF16)
    scratch = []
    if latent:
        cos, sin_signed = rope
        ctx_k, ctx_v = ctx
        n_ctx = ctx_k.shape[3]
        in_specs += [pl.BlockSpec((tq, LANES), lambda b, h, i: (i, 0)),
                     pl.BlockSpec((tq, LANES), lambda b, h, i: (i, 0)),
                     pl.BlockSpec((seq_len, LANES), lambda b, h, i: (0, 0)),
                     pl.BlockSpec((seq_len, LANES), lambda b, h, i: (0, 0)),
                     pl.BlockSpec((1, 1, 1, n_ctx, DA_DV), lambda b, h, i: (b, layer, h, 0, 0)),
                     pl.BlockSpec((1, 1, 1, n_ctx, DA_DV), lambda b, h, i: (b, layer, h, 0, 0))]
        args += [cos, sin_signed, cos, sin_signed, ctx_k, ctx_v]
        out_specs, out_shape = o_spec, o_shape
        scratch = [pltpu.VMEM((seq_len, LANES), BF16), pltpu.VMEM((seq_len, 2 * DA_DV), BF16)]
    else:
        if layer == 0:
            kv_spec = pl.BlockSpec((1, DEPTH, hps, seq_len, DA_DV), lambda b, h, i: (b, 0, h, 0, 0))
        else:
            kv_spec = pl.BlockSpec((1, 1, hps, seq_len, DA_DV), lambda b, h, i: (b, layer, h, 0, 0))
            in_specs += [pl.BlockSpec(memory_space=pl.ANY), pl.BlockSpec(memory_space=pl.ANY)]
            args += list(caches_so_far)
            aliases = {len(args) - 2: 1, len(args) - 1: 2}
        kv_shape = jax.ShapeDtypeStruct((n_seq, DEPTH, DA_HEADS, seq_len, DA_DV), F32)
        out_specs = [o_spec, kv_spec, kv_spec]
        out_shape = [o_shape, kv_shape, kv_shape]
    res = pl.pallas_call(
        functools.partial(_da_kernel, layer=layer, latent=latent),
        grid=(n_seq, DA_HEADS // hps, nq),
        in_specs=in_specs, out_specs=out_specs, out_shape=out_shape,
        input_output_aliases=aliases,
        scratch_shapes=scratch,
        compiler_params=_cparams(("arbitrary", "arbitrary", "arbitrary")),
        name=f"diff_attention_{seq_len}",
    )(*args)
    return (res, None, None) if latent else tuple(res)


def _rope_tables(n_tok):
    half = DA_DH // 2
    inv = ROPE_BASE ** (-jnp.arange(0, half, 2, dtype=F32) / half)
    t = jnp.arange(n_tok)
    ang_r = (t // GRID_W).astype(F32)[:, None] * inv
    ang_c = (t % GRID_W).astype(F32)[:, None] * inv
    cos32 = lambda a: jnp.concatenate([jnp.cos(a), jnp.cos(a)], axis=-1)
    sin32 = lambda a: jnp.concatenate([-jnp.sin(a), jnp.sin(a)], axis=-1)
    cos = jnp.concatenate([cos32(ang_r), cos32(ang_c)] * 2, axis=-1)
    sin_signed = jnp.concatenate([sin32(ang_r), sin32(ang_c)] * 2, axis=-1)
    return cos, sin_signed


def _dft_tables(n):
    t_lo = 64
    k = jnp.arange(n, dtype=jnp.int32)[:, None]
    ang = lambda m: ((k * m[None, :]) % (2 * n)).astype(F32) * (math.pi / n)
    a = ang(t_lo * jnp.arange(n // t_lo, dtype=jnp.int32))
    b = ang(jnp.arange(t_lo, dtype=jnp.int32))
    ca, sa, cb, sb = jnp.cos(a)[:, :, None], jnp.sin(a)[:, :, None], jnp.cos(b)[:, None, :], jnp.sin(b)[:, None, :]
    cos_t = (ca * cb - sa * sb).reshape(n, n)
    nsin_t = -(sa * cb + ca * sb).reshape(n, n)
    return cos_t.astype(BF16), nsin_t.astype(BF16)


def _hy_embedding(n):
    j = jnp.arange(n, dtype=F32)
    t = j / (n - 1)
    ang = (2.0 * math.pi * j / n)[:, None] * jnp.linspace(1e-4, HY_BANDS - 1, HY_BANDS, dtype=F32)
    z = jnp.concatenate([t[:, None], jnp.cos(ang), -jnp.sin(ang)], axis=-1)
    half = n // 2
    dist = jnp.abs(j - half) / half
    max_decay = math.log(HY_TARGET) / HY_FAST_DECAY
    min_decay = math.log(HY_TARGET) / HY_SLOW_DECAY
    deltas = jnp.abs(jnp.linspace(min_decay, max_decay, HY_W, dtype=F32))
    return jnp.pad(z, ((0, 0), (0, LANES - HY_EMB))), dist[:, None], deltas[None, :]


def _alt_rows(n):
    t = lax.broadcasted_iota(jnp.int32, (8, n), 1)
    return (1 - 2 * (t % 2)).astype(F32)


def _hy_filter_kernel(z_ref, dist_ref, delta_ref, w1_ref, b1_ref, fr_ref, w2_ref, b2_ref, w3_ref,
                      cos_ref, nsin_ref, hre_ref, him_ref, hny_ref, h_s):
    @pl.when(pl.program_id(1) == 0)
    def _():
        fr = fr_ref[...]
        hdn = jnp.sin(fr * (_dot3(z_ref[...], w1_ref[...]) + b1_ref[...]))
        hdn = jnp.sin(fr * (_dot3(hdn, w2_ref[...]) + b2_ref[...]))
        h = _dot3(hdn, w3_ref[...])
        h = h * jnp.exp(-dist_ref[...] * delta_ref[...])
        h = h / jnp.sum(jnp.abs(h), axis=0, keepdims=True)
        h_s[...] = h.astype(BF16)
        hny_ref[...] = _dot_exact_lhs(_alt_rows(h.shape[0]), h)

    hre_ref[...] = jnp.dot(cos_ref[...], h_s[...], preferred_element_type=F32)
    him_ref[...] = jnp.dot(nsin_ref[...], h_s[...], preferred_element_type=F32)


def _hyena_filter_spectrum(n, emb, tables, w1, b1, freq, w2, b2, w3):
    z, dist, deltas = emb
    pad_o = LANES - HY_ORDER
    w1p = jnp.pad(w1, ((0, LANES - HY_EMB), (0, pad_o)))
    w2p = jnp.pad(w2, ((0, pad_o), (0, pad_o)))
    w3p = jnp.pad(w3, ((0, pad_o), (0, 0)))
    row = lambda t: jnp.pad(t.reshape(1, HY_ORDER), ((0, 0), (0, pad_o)))
    tc = HY_W
    kt = min(n, 512)
    full = lambda shape: pl.BlockSpec(shape, lambda j, k: (0, 0))
    tab = pl.BlockSpec((kt, n), lambda j, k: (k, 0))
    spec = pl.BlockSpec((kt, tc), lambda j, k: (k, j))
    return pl.pallas_call(
        _hy_filter_kernel,
        grid=(HY_W // tc, n // kt),
        in_specs=[full((n, LANES)), full((n, 1)), pl.BlockSpec((1, tc), lambda j, k: (0, j)),
                  full((LANES, LANES)), full((1, LANES)), full((1, LANES)),
                  full((LANES, LANES)), full((1, LANES)), pl.BlockSpec((LANES, tc), lambda j, k: (0, j)),
                  tab, tab],
        out_specs=[spec, spec, pl.BlockSpec((8, tc), lambda j, k: (0, j))],
        out_shape=[jax.ShapeDtypeStruct((n, HY_W), F32), jax.ShapeDtypeStruct((n, HY_W), F32),
                   jax.ShapeDtypeStruct((8, HY_W), F32)],
        scratch_shapes=[pltpu.VMEM((n, tc), BF16)],
        compiler_params=_cparams(("arbitrary", "arbitrary")),
        name=f"hyena_filter_{n}",
    )(z, dist, deltas, w1p, row(b1), row(freq), w2p, row(b2), w3p, *tables)


def _hy_pre_kernel(x0_ref, x1_ref, v_ref, w0_ref, w1_ref, w2_ref, b0_ref, b1_ref, b2_ref, x0c_ref, gv_ref,
                   pad_s, x1_s, *, seq_len):
    _zero_conv_borders(pad_s)
    for base in range(0, x0_ref.shape[0], seq_len):
        def conv(x_ref, w_ref, b_ref, store):
            _centred_conv(pad_s, lambda r0, rb: x_ref[base + r0:base + r0 + rb, :].astype(F32), store, seq_len,
                          w_ref, HY_SHORT, b_ref[...])

        def store_x0(r0, y):
            x0c_ref[base + r0:base + r0 + y.shape[0], :] = y.astype(BF16)

        def store_x1(r0, y):
            x1_s[r0:r0 + y.shape[0], :] = y

        def store_gv(r0, y):
            gv_ref[base + r0:base + r0 + y.shape[0], :] = (y * x1_s[r0:r0 + y.shape[0], :]).astype(BF16)

        conv(x0_ref, w0_ref, b0_ref, store_x0)
        conv(x1_ref, w1_ref, b1_ref, store_x1)
        conv(v_ref, w2_ref, b2_ref, store_gv)


def _hyena_pre(p_main, conv_w, conv_b):
    outs = []
    for n_seq, seq_len, row_blk0, spb, tc in ((BATCH, SEQ, 0, 4, HY_W),
                                               (DEC_BATCH, DEC_SEQ, N_PROMPT // DEC_SEQ, 1, HY_W // 2)):
        ncb = HY_W // tc
        rows = spb * seq_len
        tok = lambda cb: pl.BlockSpec((rows, tc), lambda b, j, cb=cb: (row_blk0 + b, cb * LANES // tc + j))
        cw = lambda s: pl.BlockSpec((HY_SHORT, tc), lambda b, j, s=s: (0, s * ncb + j))
        cb_ = lambda s: pl.BlockSpec((1, tc), lambda b, j, s=s: (0, s * ncb + j))
        o_spec = pl.BlockSpec((rows, tc), lambda b, j: (b, j))
        o_shape = jax.ShapeDtypeStruct((n_seq * seq_len, HY_W), BF16)
        outs.append(pl.pallas_call(
            functools.partial(_hy_pre_kernel, seq_len=seq_len),
            grid=(n_seq // spb, ncb),
            in_specs=[tok(CB_X0), tok(CB_X1), tok(CB_HV), cw(0), cw(1), cw(2), cb_(0), cb_(1), cb_(2)],
            out_specs=[o_spec, o_spec], out_shape=[o_shape, o_shape],
            scratch_shapes=[pltpu.VMEM((seq_len + 2 * CONV_PAD, tc), F32), pltpu.VMEM((seq_len, tc), F32)],
            compiler_params=_cparams(("arbitrary", "arbitrary")),
            name=f"hyena_pre_{seq_len}",
        )(p_main, p_main, p_main, conv_w, conv_w, conv_w,
          conv_b.reshape(1, -1), conv_b.reshape(1, -1), conv_b.reshape(1, -1)))
    return outs


HY_FREQ_BLOCK = 512


def _hy_conv_kernel(gv_ref, x0_ref, d_ref, hre_ref, him_ref, hny_ref, cos_ref, nsin_ref, o_ref, *,
                    seq_len, n_seq):
    n = seq_len
    fb = min(n, HY_FREQ_BLOCK)
    chains = [(b, f0) for b in range(n_seq) for f0 in range(0, n, fb)]
    gvs = [gv_ref[pl.ds(b * seq_len, seq_len), :] for b in range(n_seq)]
    gre = [jnp.dot(cos_ref[f0:f0 + fb, :], gvs[b], preferred_element_type=F32) for b, f0 in chains]
    gim = [jnp.dot(nsin_ref[f0:f0 + fb, :], gvs[b], preferred_element_type=F32) for b, f0 in chains]
    spec = []
    for (b, f0), gr, gi in zip(chains, gre, gim):
        hre, him = hre_ref[f0:f0 + fb, :], him_ref[f0:f0 + fb, :]
        freq = lax.broadcasted_iota(jnp.int32, hre.shape, 0) + f0
        wk = jnp.where(freq == 0, 0.5 / n, 1.0 / n)
        q4 = freq % 4
        yre = (gr * hre - gi * him) * wk
        yim = (gr * him + gi * hre) * wk
        spec.append((jnp.where(q4 == 0, yre, jnp.where(q4 == 1, -yim, jnp.where(q4 == 2, -yre, yim))),
                     jnp.where(q4 == 0, yim, jnp.where(q4 == 1, yre, jnp.where(q4 == 2, -yim, -yre)))))
    inv = [_bdot(cos_ref[:, f0:f0 + fb], are) + _bdot(nsin_ref[:, f0:f0 + fb], aim)
           for (b, f0), (are, aim) in zip(chains, spec)]
    t = lax.broadcasted_iota(jnp.int32, (n, gvs[0].shape[1]), 0)
    alt = (1 - 2 * (t % 2)).astype(F32)
    for b in range(n_seq):
        rows = pl.ds(b * seq_len, seq_len)
        gvf = gvs[b].astype(F32)
        g_ny = jnp.sum(gvf * alt, axis=0, keepdims=True)
        y_ny = g_ny * hny_ref[0:1, :] * ((-1.0) ** (n // 2) / (2 * n))
        y = functools.reduce(lambda x, z: x + z, [p for (cb, _), p in zip(chains, inv) if cb == b])
        y = y + alt * y_ny + gvf * d_ref[...]
        o_ref[rows, :] = (y * x0_ref[rows, :].astype(F32)).astype(BF16)


def _hyena_conv(gv, x0c, d_skip, h_spec, tables, *, n_seq_total, seq_len, n_seq_blk):
    tc = 256
    hre, him, hny = h_spec
    cos_t, nsin_t = tables
    rows = n_seq_blk * seq_len
    grid = (HY_W // tc, n_seq_total // n_seq_blk)
    tok = pl.BlockSpec((rows, tc), lambda j, g: (g, j))
    hs = pl.BlockSpec((seq_len, tc), lambda j, g: (0, j))
    table = pl.BlockSpec((seq_len, seq_len), lambda j, g: (0, 0), pipeline_mode=pl.Buffered(1))
    return pl.pallas_call(
        functools.partial(_hy_conv_kernel, seq_len=seq_len, n_seq=n_seq_blk),
        grid=grid,
        in_specs=[tok, tok, pl.BlockSpec((1, tc), lambda j, g: (0, j)), hs, hs,
                  pl.BlockSpec((8, tc), lambda j, g: (0, j)), table, table],
        out_specs=tok,
        out_shape=jax.ShapeDtypeStruct((n_seq_total * seq_len, HY_W), BF16),
        compiler_params=_cparams(("arbitrary", "arbitrary")),
        name=f"hyena_conv_{seq_len}",
    )(gv, x0c, d_skip.reshape(1, HY_W), hre, him, hny, cos_t, nsin_t)


MERGE_TM = 512


def _merge_kernel(x_ref, oap_ref, obp_ref, ocp_ref, oas_ref, obs_ref, ocs_ref, ga_ref, gb_ref, gc_ref,
                  wa_ref, wb_ref, wc_ref, wo_ref, mod_ref, y_ref, wa_s, wb_s, wc_s, wo_s):
    i = pl.program_id(0)

    @pl.when(i == 0)
    def _():
        for src, dst in ((wa_ref, wa_s), (wb_ref, wb_s), (wc_ref, wc_s), (wo_ref, wo_s)):
            dst[...] = src[...].astype(BF16)

    is_prompt = i < N_PROMPT // MERGE_TM
    pick = lambda p_ref, s_ref: jnp.where(is_prompt, p_ref[...], s_ref[...])
    sig = lambda r: jax.nn.sigmoid(r[...].astype(F32))
    merged = (sig(ga_ref) * jnp.dot(pick(oap_ref, oas_ref), wa_s[...], preferred_element_type=F32)
              + sig(gb_ref) * jnp.dot(pick(obp_ref, obs_ref), wb_s[...], preferred_element_type=F32)
              + sig(gc_ref) * jnp.dot(pick(ocp_ref, ocs_ref), wc_s[...], preferred_element_type=F32))
    out = _bdot(merged, wo_s[...])
    y_ref[...] = x_ref[...] + mod_ref[0, 2:3, :] * out


def _merge(x, branches_p, branches_s, p_main, mods, w_a, w_b, w_c, w_out, layer):
    tm = MERGE_TM
    npt = N_PROMPT // tm
    tok = lambda w: pl.BlockSpec((tm, w), lambda i: (i, 0))
    tok_p = pl.BlockSpec((tm, DN_W), lambda i: (jnp.minimum(i, npt - 1), 0))
    tok_s = pl.BlockSpec((tm, DN_W), lambda i: (jnp.maximum(i - npt, 0), 0))
    gate = lambda s: pl.BlockSpec((tm, D_MODEL), lambda i, s=s: (i, CB_GATE // 8 + s))
    wfull = lambda r: pl.BlockSpec((None, r, D_MODEL), lambda i: (layer, 0, 0))
    wscr = lambda r: pltpu.VMEM((r, D_MODEL), BF16)
    return pl.pallas_call(
        _merge_kernel,
        grid=(N_TOK // tm,),
        in_specs=[tok(D_MODEL), tok_p, tok_p, tok_p, tok_s, tok_s, tok_s, gate(0), gate(1), gate(2),
                  wfull(DN_W), wfull(DN_W), wfull(HY_W), wfull(D_MODEL),
                  pl.BlockSpec((1, 6, D_MODEL), lambda i: (_mod_row(i, tm), 0, 0))],
        out_specs=tok(D_MODEL),
        out_shape=jax.ShapeDtypeStruct((N_TOK, D_MODEL), F32),
        scratch_shapes=[wscr(DN_W), wscr(DN_W), wscr(HY_W), wscr(D_MODEL)],
        compiler_params=_cparams(("arbitrary",)),
        name="merge_out_projection",
    )(x, *branches_p, *branches_s, p_main, p_main, p_main, w_a, w_b, w_c, w_out, mods)


FFN_TM = 512
FFN_TF = D_FF // 2


def _ffn_kernel(x_ref, g_ref, mod_ref, wg_ref, wu_ref, wo_ref, fg_ref, *rest, final):
    if final:
        yp_ref, ys_ref, h_scr, acc_scr = rest
    else:
        y_ref, h_scr, acc_scr = rest
    f = pl.program_id(1)
    nf = pl.num_programs(1)

    @pl.when(f == 0)
    def _():
        g = g_ref[...]
        sh = mod_ref[0, 3:4, :]
        sc1 = 1.0 + mod_ref[0, 4:5, :]

        def body(r, carry):
            rows = pl.ds(pl.multiple_of(r * ROW_CHUNK, ROW_CHUNK), ROW_CHUNK)
            h_scr[rows, :] = (_rms(x_ref[rows, :], g) * sc1 + sh).astype(BF16)
            return carry

        lax.fori_loop(0, FFN_TM // ROW_CHUNK, body, 0)

    h = h_scr[...]
    gate = jnp.dot(h, wg_ref[...], preferred_element_type=F32)
    up = jnp.dot(h, wu_ref[...], preferred_element_type=F32)
    part = _bdot(_silu(gate) * up, wo_ref[...])

    @pl.when(f == 0)
    def _():
        acc_scr[...] = part

    @pl.when(f > 0)
    def _():
        acc_scr[...] += part

    @pl.when(f == nf - 1)
    def _():
        y = x_ref[...] + mod_ref[0, 5:6, :] * acc_scr[...]
        if final:
            y = _rms(y, fg_ref[...])
            is_prompt = pl.program_id(0) < N_PROMPT // FFN_TM

            @pl.when(is_prompt)
            def _():
                yp_ref[...] = y

            @pl.when(jnp.logical_not(is_prompt))
            def _():
                ys_ref[...] = y
        else:
            y_ref[...] = y


def _ffn(x, norm_g, mods, w_in, w_out, final_g, layer, *, final):
    tm, tf = FFN_TM, FFN_TF
    nf = D_FF // tf
    if final:
        npt = N_PROMPT // tm
        out_specs = [pl.BlockSpec((tm, D_MODEL), lambda i, f: (jnp.minimum(i, npt - 1), 0)),
                     pl.BlockSpec((tm, D_MODEL), lambda i, f: (jnp.maximum(i - npt, 0), 0))]
        out_shape = [jax.ShapeDtypeStruct((N_PROMPT, D_MODEL), F32), jax.ShapeDtypeStruct((N_SAMPLE, D_MODEL), F32)]
    else:
        out_specs = pl.BlockSpec((tm, D_MODEL), lambda i, f: (i, 0))
        out_shape = jax.ShapeDtypeStruct((N_TOK, D_MODEL), F32)
    return pl.pallas_call(
        functools.partial(_ffn_kernel, final=final),
        grid=(N_TOK // tm, nf),
        in_specs=[pl.BlockSpec((tm, D_MODEL), lambda i, f: (i, 0)),
                  pl.BlockSpec((1, D_MODEL), lambda i, f: (0, 0)),
                  pl.BlockSpec((1, 6, D_MODEL), lambda i, f: (_mod_row(i, tm), 0, 0)),
                  pl.BlockSpec((None, D_MODEL, tf), lambda i, f: (layer, 0, f)),
                  pl.BlockSpec((None, D_MODEL, tf), lambda i, f: (layer, 0, nf + f)),
                  pl.BlockSpec((None, tf, D_MODEL), lambda i, f: (layer, f, 0)),
                  pl.BlockSpec((1, D_MODEL), lambda i, f: (0, 0))],
        out_specs=out_specs,
        out_shape=out_shape,
        scratch_shapes=[pltpu.VMEM((tm, D_MODEL), BF16), pltpu.VMEM((tm, D_MODEL), F32)],
        compiler_params=_cparams(("arbitrary", "arbitrary")),
        name="ffn",
    )(x, norm_g.reshape(1, D_MODEL), mods, w_in, w_in, w_out, final_g.reshape(1, D_MODEL))


def kernel(x_prompt, x_sample, cache_k, cache_v, state_dn, c, c_ctx, norm1_g, norm2_g, w_mod, b_mod,
           w_in, dn_conv_w, dn_a_log, dn_dt_bias, dn_norm_g, da_lambda, da_subln_g, hy_conv_w,
           hy_conv_b, hy_w1, hy_b1, hy_freq, hy_w2, hy_b2, hy_w3, hy_d, w_br_a, w_br_b, w_br_c,
           w_out, w_ffn_in, w_ffn_out, final_g):
    x = jnp.concatenate([x_prompt.reshape(N_PROMPT, D_MODEL), x_sample.reshape(N_SAMPLE, D_MODEL)], axis=0)
    cond8 = jnp.concatenate([c_ctx[None, :], c, jnp.zeros((8 - 1 - DEC_BATCH, D_MODEL), F32)], axis=0)
    rope = _rope_tables(DEC_SEQ)
    tab_p = _dft_tables(SEQ)
    tab_s = _dft_tables(DEC_SEQ)
    emb_p, emb_s = _hy_embedding(SEQ), _hy_embedding(DEC_SEQ)
    sample_blk0 = N_PROMPT // DEC_SEQ
    w_main, w_small = _pack_w_in(w_in)
    w_ffn_in_b, w_ffn_out_b = w_ffn_in.astype(BF16), w_ffn_out.astype(BF16)

    new_k = new_v = new_s = None
    for l in range(DEPTH):
        mods = _modulation(cond8, w_mod, b_mod[l], l).reshape(8, 6, D_MODEL)
        p_main, p_small = _in_projection(x, norm1_g[l], mods, w_main, w_small, l)

        dn_args = (p_main, p_small, dn_conv_w[l], dn_a_log[l], dn_dt_bias[l], dn_norm_g[l])
        oa_p, new_s = _deltanet(*dn_args, None, n_seq=BATCH, seq_len=SEQ, row_blk0=0, hps=4, layer=l,
                                states_so_far=new_s)
        oa_s, _ = _deltanet(*dn_args, state_dn, n_seq=DEC_BATCH, seq_len=DEC_SEQ, row_blk0=sample_blk0,
                            hps=2, layer=l)

        ob_p, new_k, new_v = _diff_attention(p_main, da_lambda[l], da_subln_g[l], l, n_seq=BATCH, seq_len=SEQ,
                                             row_blk0=0, tq=SEQ, caches_so_far=(new_k, new_v))
        ob_s, _, _ = _diff_attention(p_main, da_lambda[l], da_subln_g[l], l, n_seq=DEC_BATCH,
                                     seq_len=DEC_SEQ, row_blk0=sample_blk0, tq=1024, rope=rope,
                                     ctx=(cache_k, cache_v))

        hy_w = (hy_w1[l], hy_b1[l], hy_freq[l], hy_w2[l], hy_b2[l], hy_w3[l])
        hspec_p = _hyena_filter_spectrum(SEQ, emb_p, tab_p, *hy_w)
        hspec_s = _hyena_filter_spectrum(DEC_SEQ, emb_s, tab_s, *hy_w)
        (x0_p, gv_p), (x0_s, gv_s) = _hyena_pre(p_main, hy_conv_w[l], hy_conv_b[l])
        oc_p = _hyena_conv(gv_p, x0_p, hy_d[l], hspec_p, tab_p, n_seq_total=BATCH, seq_len=SEQ,
                           n_seq_blk=BATCH)
        oc_s = _hyena_conv(gv_s, x0_s, hy_d[l], hspec_s, tab_s, n_seq_total=DEC_BATCH, seq_len=DEC_SEQ,
                           n_seq_blk=1)

        x = _merge(x, (oa_p, ob_p, oc_p), (oa_s, ob_s, oc_s), p_main, mods,
                   w_br_a, w_br_b, w_br_c, w_out, l)
        x = _ffn(x, norm2_g[l], mods, w_ffn_in_b, w_ffn_out_b, final_g, l, final=(l == DEPTH - 1))

    y_prompt = x[0].reshape(BATCH, SEQ, D_MODEL)
    y_sample = x[1].reshape(DEC_BATCH, DEC_SEQ, D_MODEL)
    return (y_prompt, y_sample, new_k, new_v, new_s)
```

```python
import functools
import math

import jax
import jax.numpy as jnp
from jax import lax
from jax.experimental import pallas as pl
from jax.experimental.pallas import tpu as pltpu

F32 = jnp.float32
BF16 = jnp.bfloat16

D_MODEL = 1024
BATCH = 16
SEQ = 256
DEPTH = 2
DEC_BATCH = 4
DEC_SEQ = 2048
PAST_LEN = 256
GRID_W = 64
RMS_EPS = 1e-6
DN_HEADS = 4
DN_DK = 128
DN_DV = 128
DN_W = DN_HEADS * DN_DK
DN_CONV = 5
DN_CHUNK = 64
DA_HEADS = 4
DA_DH = 64
DA_DV = 2 * DA_DH
ROPE_BASE = 10000.0
HY_W = 512
HY_SHORT = 3
HY_BANDS = 8
HY_EMB = 1 + 2 * HY_BANDS
HY_ORDER = 64
HY_FAST_DECAY = 0.3
HY_SLOW_DECAY = 1.5
HY_TARGET = 1e-2
D_FF = ((8 * D_MODEL // 3 + 255) // 256) * 256

N_PROMPT = BATCH * SEQ
N_SAMPLE = DEC_BATCH * DEC_SEQ
N_TOK = N_PROMPT + N_SAMPLE
LANES = 128
N_MAIN = 8192
CB_QA, CB_KA, CB_VA, CB_ZA = 0, 4, 8, 12
CB_QB, CB_KB, CB_VB = 16, 20, 24
CB_X0, CB_X1, CB_HV = 28, 32, 36
CB_GATE = 40
VMEM_LIMIT = 56 * 1024 * 1024


def _cparams(sem):
    return pltpu.CompilerParams(dimension_semantics=sem, vmem_limit_bytes=VMEM_LIMIT)


def _bdot(a, b):
    return jnp.dot(a.astype(BF16), b.astype(BF16), preferred_element_type=F32)


def _bdot_nt(a, b):
    return lax.dot_general(a.astype(BF16), b.astype(BF16), (((1,), (1,)), ((), ())),
                           preferred_element_type=F32)


def _bdot_tn(a, b):
    return lax.dot_general(a.astype(BF16), b.astype(BF16), (((0,), (0,)), ((), ())),
                           preferred_element_type=F32)


def _split3(x):
    hi = x.astype(BF16)
    r = x - hi.astype(F32)
    mid = r.astype(BF16)
    lo = (r - mid.astype(F32)).astype(BF16)
    return hi, mid, lo


def _dot_exact_lhs(t, x):
    hi, mid, lo = _split3(x)
    tb = t.astype(BF16)
    d = lambda p: jnp.dot(tb, p, preferred_element_type=F32)
    return d(hi) + d(mid) + d(lo)


def _dot_exact_rhs(x, e):
    hi, mid, lo = _split3(x)
    eb = e.astype(BF16)
    d = lambda p: jnp.dot(p, eb, preferred_element_type=F32)
    return d(hi) + d(mid) + d(lo)


def _dot3(a, b):
    ah = a.astype(BF16)
    al = (a - ah.astype(F32)).astype(BF16)
    bh = b.astype(BF16)
    bl = (b - bh.astype(F32)).astype(BF16)
    d = lambda p, q: jnp.dot(p, q, preferred_element_type=F32)
    return d(ah, bh) + d(ah, bl) + d(al, bh)


def _silu(x):
    return x * jax.nn.sigmoid(x)


def _rms(x, g):
    return x * lax.rsqrt(jnp.mean(x * x, axis=-1, keepdims=True) + RMS_EPS) * g


def _mod_row(i, tm):
    n_prompt_tiles = N_PROMPT // tm
    tiles_per_seq = DEC_SEQ // tm
    return jnp.where(i < n_prompt_tiles, 0, 1 + (i - n_prompt_tiles) // tiles_per_seq)


def _mod_kernel(c_ref, w_ref, b_ref, o_ref):
    o_ref[...] = _bdot(_silu(c_ref[...]), w_ref[...]) + b_ref[...]


def _modulation(cond8, w_mod, b_mod, layer):
    n = 6 * D_MODEL
    tn = 1024
    return pl.pallas_call(
        _mod_kernel,
        grid=(n // tn,),
        in_specs=[pl.BlockSpec((8, D_MODEL), lambda j: (0, 0)),
                  pl.BlockSpec((None, D_MODEL, tn), lambda j: (layer, 0, j)),
                  pl.BlockSpec((1, tn), lambda j: (0, j))],
        out_specs=pl.BlockSpec((8, tn), lambda j: (0, j)),
        out_shape=jax.ShapeDtypeStruct((8, n), F32),
        compiler_params=_cparams(("arbitrary",)),
        name="modulation",
    )(cond8, w_mod, b_mod.reshape(1, n))


IN_TM = 2048
IN_TN = 2048
ROW_CHUNK = 256


def _inproj_kernel(x_ref, g_ref, mod_ref, wm_ref, ws_ref, p_ref, ps_ref, h_scr):
    @pl.when(pl.program_id(1) == 0)
    def _():
        g = g_ref[...]
        sh = mod_ref[0, 0:1, :]
        sc1 = 1.0 + mod_ref[0, 1:2, :]

        def body(r, carry):
            rows = pl.ds(pl.multiple_of(r * ROW_CHUNK, ROW_CHUNK), ROW_CHUNK)
            hb = (_rms(x_ref[rows, :], g) * sc1 + sh).astype(BF16)
            h_scr[rows, :] = hb
            ps_ref[rows, :] = jnp.dot(hb, ws_ref[...], preferred_element_type=F32)
            return carry

        lax.fori_loop(0, IN_TM // ROW_CHUNK, body, 0)

    p_ref[...] = jnp.dot(h_scr[...], wm_ref[...], preferred_element_type=F32).astype(BF16)


N_GATE_COLS = 4 * DN_HEADS
PACK_ROWS = 128
PACK_COLS = 512


def _pack_w_in_kernel(w_ref, main_ref, small_ref):
    lo = 3 * DN_W
    main_ref[:, :lo] = w_ref[:, :lo].astype(BF16)
    for c in range(lo, N_MAIN, PACK_COLS):
        main_ref[:, c:c + PACK_COLS] = w_ref[:, c + N_GATE_COLS:c + N_GATE_COLS + PACK_COLS].astype(BF16)
    gate = w_ref[:, lo:lo + LANES]
    lane = lax.broadcasted_iota(jnp.int32, gate.shape, 1)
    small_ref[...] = jnp.where(lane < N_GATE_COLS, gate, 0.0).astype(BF16)


def _pack_w_in(w_in):
    n_in = w_in.shape[2]
    return pl.pallas_call(
        _pack_w_in_kernel,
        grid=(DEPTH, D_MODEL // PACK_ROWS),
        in_specs=[pl.BlockSpec((None, PACK_ROWS, n_in), lambda l, r: (l, r, 0))],
        out_specs=[pl.BlockSpec((None, PACK_ROWS, N_MAIN), lambda l, r: (l, r, 0)),
                   pl.BlockSpec((None, PACK_ROWS, LANES), lambda l, r: (l, r, 0))],
        out_shape=[jax.ShapeDtypeStruct((DEPTH, D_MODEL, N_MAIN), BF16),
                   jax.ShapeDtypeStruct((DEPTH, D_MODEL, LANES), BF16)],
        compiler_params=_cparams(("arbitrary", "arbitrary")),
        name="pack_w_in",
    )(w_in)


def _in_projection(x, norm_g, mods, w_main, w_small, layer):
    grid = (N_TOK // IN_TM, N_MAIN // IN_TN)
    return pl.pallas_call(
        _inproj_kernel,
        grid=grid,
        in_specs=[pl.BlockSpec((IN_TM, D_MODEL), lambda i, j: (i, 0)),
                  pl.BlockSpec((1, D_MODEL), lambda i, j: (0, 0)),
                  pl.BlockSpec((1, 6, D_MODEL), lambda i, j: (_mod_row(i, IN_TM), 0, 0)),
                  pl.BlockSpec((None, D_MODEL, IN_TN), lambda i, j: (layer, 0, j)),
                  pl.BlockSpec((None, D_MODEL, LANES), lambda i, j: (layer, 0, 0))],
        out_specs=[pl.BlockSpec((IN_TM, IN_TN), lambda i, j: (i, j)),
                   pl.BlockSpec((IN_TM, LANES), lambda i, j: (i, 0))],
        out_shape=[jax.ShapeDtypeStruct((N_TOK, N_MAIN), BF16),
                   jax.ShapeDtypeStruct((N_TOK, LANES), F32)],
        scratch_shapes=[pltpu.VMEM((IN_TM, D_MODEL), BF16)],
        compiler_params=_cparams(("arbitrary", "arbitrary")),
        name="in_projection",
    )(x, norm_g.reshape(1, D_MODEL), mods, w_main, w_small)


DN_UNIT = 2 * DN_CHUNK
DN_GROUP = 8


CONV_PAD = 8


def _zero_conv_borders(pad_ref):
    n = pad_ref.shape[0] - 2 * CONV_PAD
    zeros = jnp.zeros((CONV_PAD, pad_ref.shape[1]), F32)
    pad_ref[0:CONV_PAD, :] = zeros
    pad_ref[CONV_PAD + n:, :] = zeros


CONV_ROWS = 512


def _centred_conv(pad_ref, load, store, n, w_ref, n_taps, bias=None):
    rb = min(n, CONV_ROWS)
    half = n_taps // 2
    for r0 in range(0, n, rb):
        pad_ref[CONV_PAD + r0:CONV_PAD + r0 + rb, :] = load(r0, rb)
    for r0 in range(0, n, rb):
        acc = None
        for tap in range(n_taps):
            lo = CONV_PAD + r0 + tap - half
            term = pad_ref[lo:lo + rb, :] * w_ref[tap:tap + 1, :]
            acc = term if acc is None else acc + term
        store(r0, acc if bias is None else acc + bias)


def _l2norm(x):
    return x * lax.rsqrt(jnp.sum(x * x, axis=-1, keepdims=True) + RMS_EPS)


def _softplus(x):
    return jnp.maximum(x, 0.0) + jnp.log1p(jnp.exp(-jnp.abs(x)))


DN_BASE = 16


def _unit_tri_inverse(a_list, ri, ci):
    eye = (ri == ci).astype(F32)
    blk = lambda b: (ri // b) == (ci // b)
    y = [jnp.where(blk(DN_BASE), -a, 0.0) for a in a_list]
    p = [eye + yi for yi in y]
    for _ in range(3):
        y = [_bdot(yi, yi) for yi in y]
        p = [pi + _bdot(yi, pi) for yi, pi in zip(y, p)]
    b = DN_BASE
    while b < DN_CHUNK:
        off_mask = blk(2 * b) & ~blk(b)
        t = [_bdot(jnp.where(off_mask, a, 0.0), pi) for a, pi in zip(a_list, p)]
        p = [pi - _bdot(pi, ti) for pi, ti in zip(p, t)]
        b *= 2
    return p


def _dn_units(chains):
    u, c = DN_UNIT, DN_CHUNK
    ri = lax.broadcasted_iota(jnp.int32, (u, u), 0)
    ci = lax.broadcasted_iota(jnp.int32, (u, u), 1)
    same = (ri // c) == (ci // c)
    eye_mask = ri == ci
    hi_rows = ri >= c
    incl_of = {False: same & (ri >= ci), True: same & (ri <= ci)}
    strict_of = {False: same & (ri > ci), True: same & (ri < ci)}
    incl = [incl_of[ch['backward']] for ch in chains]
    strict = [strict_of[ch['backward']] for ch in chains]
    gc = [_dot_exact_lhs(m.astype(F32), ch['g']) for m, ch in zip(incl, chains)]
    g_tot = [(x[0:1, :], x[c:c + 1, :]) if ch['backward'] else (x[c - 1:c, :], x[u - 1:u, :])
             for x, ch in zip(gc, chains)]
    gc_row = [jnp.sum(jnp.where(eye_mask, x, 0.0), axis=0, keepdims=True) for x in gc]
    dec = [jnp.exp(jnp.where(m, x - xr, -1e30)) for m, x, xr in zip(incl, gc, gc_row)]
    e_gc = [jnp.exp(x) for x in gc]
    a = [jnp.where(m, ch['kk'] * ch['beta'] * d, 0.0) for m, ch, d in zip(strict, chains, dec)]
    qk = [ch['qk_raw'] * d for ch, d in zip(chains, dec)]
    eye = eye_mask.astype(F32)
    r = [p - eye for p in _unit_tri_inverse(a, ri, ci)]
    rhs = [jnp.concatenate([ch['v'] * ch['beta'], ch['k'] * (ch['beta'] * e)], axis=1)
           for ch, e in zip(chains, e_gc)]
    uwb = [(x + _bdot(ri_, x)).astype(BF16) for ri_, x in zip(r, rhs)]
    qkuw = [_bdot(x, y) for x, y in zip(qk, uwb)]
    kd = [ch['k'] * jnp.exp(jnp.where(hi_rows, gt[1], gt[0]) - x) for ch, gt, x in zip(chains, g_tot, gc)]
    tp = [[_bdot_tn(jnp.where(keep, x, 0.0), y) for x, y in zip(kd, uwb)]
          for keep in (~hi_rows, hi_rows)]
    out = []
    for i, ch in enumerate(chains):
        o_local = qkuw[i][:, :DN_DV]
        q_eff = ch['q'] * e_gc[i] - qkuw[i][:, DN_DV:]
        per_chunk = [(-tp[h][i][:, DN_DV:], tp[h][i][:, :DN_DV], jnp.exp(g_tot[i][h])) for h in range(2)]
        out.append((o_local, q_eff, per_chunk))
    return out


def _dn_kernel(*refs, seq_len, has_state, hps, layer):
    if has_state:
        (q_ref, k_ref, v_ref, z_ref, ps_ref, cwq_ref, cwk_ref, cwv_ref, alog_ref, dtb_ref, ng_ref,
         s0_ref, o_ref, q_s, k_s, v_s, bt_s, g_s, oacc_s, qe_s, th_s, psi_s, egl_s, st_s, pad_s) = refs
        sfin_ref = None
    else:
        refs = refs[:11] + refs[(11 if layer == 0 else 12):]
        (q_ref, k_ref, v_ref, z_ref, ps_ref, cwq_ref, cwk_ref, cwv_ref, alog_ref, dtb_ref, ng_ref,
         o_ref, sfin_ref, q_s, k_s, v_s, bt_s, g_s, oacc_s, qe_s, th_s, psi_s, egl_s, st_s, pad_s) = refs
        s0_ref = None
    n_units = seq_len // DN_UNIT
    n_chunks = seq_len // DN_CHUNK
    c = DN_CHUNK
    sel_r = lax.broadcasted_iota(jnp.int32, (LANES, 4 * LANES), 0)
    sel_blk = lax.broadcasted_iota(jnp.int32, (LANES, 4 * LANES), 1) // LANES
    sels = []
    for j in range(hps):
        head = pl.program_id(1) * hps + j
        src_col = head + DN_HEADS * jnp.where(sel_blk == 0, 0, jnp.where(sel_blk == 1, 2, jnp.where(sel_blk == 2, 1, 3)))
        sels.append((sel_r == src_col).astype(BF16))

    def gate_rows(r, carry):
        rows = pl.ds(pl.multiple_of(r * ROW_CHUNK, ROW_CHUNK), ROW_CHUNK)
        ps = ps_ref[rows, :]
        lane = lax.broadcasted_iota(jnp.int32, ps.shape, 1)
        gate_cols = jnp.where(lane < 2 * DN_HEADS, jax.nn.sigmoid(ps),
                              -(jnp.exp(alog_ref[...]) * _softplus(ps + dtb_ref[...])))
        parts = _split3(gate_cols)
        for j in range(hps):
            ext = functools.reduce(lambda a, b: a + b,
                                   [jnp.dot(part, sels[j], preferred_element_type=F32) for part in parts])
            for d in range(2):
                bt_s[j, d, rows, :] = ext[:, (2 * d) * LANES:(2 * d + 1) * LANES]
                g_s[j, d, rows, :] = ext[:, (2 * d + 1) * LANES:(2 * d + 2) * LANES]
        return carry

    lax.fori_loop(0, seq_len // ROW_CHUNK, gate_rows, 0, unroll=min(seq_len // ROW_CHUNK, 2))

    _zero_conv_borders(pad_s)
    for j in range(hps):
        cols = slice(j * LANES, (j + 1) * LANES)
        for x_ref, w_ref, out_s, post in (
                (q_ref, cwq_ref, q_s, lambda y: _l2norm(_silu(y)) * (DN_DK ** -0.5)),
                (k_ref, cwk_ref, k_s, lambda y: _l2norm(_silu(y))),
                (v_ref, cwv_ref, v_s, _silu)):
            def store(r0, y, out_s=out_s, post=post):
                out_s[j, r0:r0 + y.shape[0], :] = post(y)
            _centred_conv(pad_s, lambda r0, rb, x_ref=x_ref: x_ref[r0:r0 + rb, cols].astype(F32), store,
                          seq_len, w_ref.at[:, cols], DN_CONV)
        for d in range(2):
            st_s[j, d] = s0_ref[0, 0, d, j] if has_state else jnp.zeros((DN_DK, DN_DV), F32)

    def unit_group(grp, carry):
        where, chains = [], []
        for t in range(DN_GROUP):
            idx = grp * DN_GROUP + t
            j = idx // n_units
            n = idx % n_units
            rows = pl.ds(pl.multiple_of(n * DN_UNIT, DN_UNIT), DN_UNIT)
            qc, kc, vc = q_s[j, rows, :], k_s[j, rows, :], v_s[j, rows, :]
            kcb = kc.astype(BF16)
            kk = _bdot_nt(kcb, kcb)
            qk_raw = _bdot_nt(qc, kcb)
            for d in range(2):
                where.append((j, n, rows, d))
                chains.append(dict(q=qc, k=kc, v=vc, kk=kk, qk_raw=qk_raw, beta=bt_s[j, d, rows, :],
                                   g=g_s[j, d, rows, :], backward=(d == 1)))
        results = _dn_units(chains)
        for (j, n, rows, d), (o_loc, q_eff, per_chunk) in zip(where, results):
            qe_s[j, d, rows, :] = q_eff.astype(BF16)
            for half, (theta, psi, egl) in enumerate(per_chunk):
                th_s[j, d, 2 * n + half] = theta.astype(BF16)
                psi_s[j, d, 2 * n + half] = psi
                egl_s[j, d, 2 * n + half] = egl
            if d == 0:
                o_fwd = o_loc
            else:
                oacc_s[j, rows, :] = o_fwd + o_loc
        return carry

    lax.fori_loop(0, hps * n_units // DN_GROUP, unit_group, 0)

    def scan_step(i, carry):
        for j in range(hps):
            for d, n in ((0, i), (1, n_chunks - 1 - i)):
                rows = pl.ds(pl.multiple_of(n * c, c), c)
                s = st_s[j, d]
                sb = s.astype(BF16)
                oacc_s[j, rows, :] += jnp.dot(qe_s[j, d, rows, :], sb, preferred_element_type=F32)
                st_s[j, d] = (egl_s[j, d, n] * s + jnp.dot(th_s[j, d, n], sb, preferred_element_type=F32)
                              + psi_s[j, d, n])
        return carry

    lax.fori_loop(0, n_chunks, scan_step, 0, unroll=min(n_chunks, 8))

    if sfin_ref is not None:
        for j in range(hps):
            for d in range(2):
                sfin_ref[0, 0, d, j] = st_s[j, d]
                if layer == 0:
                    for later in range(1, DEPTH):
                        sfin_ref[0, later, d, j] = jnp.zeros((DN_DK, DN_DV), F32)

    def out_rows(r, carry):
        rows = pl.ds(pl.multiple_of(r * ROW_CHUNK, ROW_CHUNK), ROW_CHUNK)
        for j in range(hps):
            cols = slice(j * LANES, (j + 1) * LANES)
            o = _rms(oacc_s[j, rows, :], ng_ref[...]) * _silu(z_ref[rows, cols].astype(F32))
            o_ref[rows, cols] = o.astype(BF16)
        return carry

    lax.fori_loop(0, seq_len // ROW_CHUNK, out_rows, 0, unroll=min(seq_len // ROW_CHUNK, 4))


def _deltanet(p_main, p_small, conv_w, a_log, dt_bias, norm_g, s0, *, n_seq, seq_len, row_blk0, hps, layer,
              states_so_far=None):
    has_state = s0 is not None
    n_chunks = seq_len // DN_CHUNK
    width = hps * LANES
    tok = lambda cb: pl.BlockSpec((seq_len, width), lambda b, h: (row_blk0 + b, cb // hps + h))
    cw = lambda cb: pl.BlockSpec((DN_CONV, width), lambda b, h: (0, cb // hps + h))
    in_specs = [tok(CB_QA), tok(CB_KA), tok(CB_VA), tok(CB_ZA),
                pl.BlockSpec((seq_len, LANES), lambda b, h: (row_blk0 + b, 0)),
                cw(0), cw(DN_HEADS), cw(2 * DN_HEADS),
                pl.BlockSpec((1, LANES), lambda b, h: (0, 0)),
                pl.BlockSpec((1, LANES), lambda b, h: (0, 0)),
                pl.BlockSpec((1, DN_DV), lambda b, h: (0, 0))]
    pad8 = lambda t: jnp.pad(t.reshape(1, 2 * DN_HEADS), ((0, 0), (2 * DN_HEADS, LANES - 4 * DN_HEADS)))
    args = [p_main, p_main, p_main, p_main, p_small, conv_w, conv_w, conv_w,
            pad8(a_log), pad8(dt_bias), norm_g.reshape(1, DN_DV)]
    o_spec = pl.BlockSpec((seq_len, width), lambda b, h: (b, h))
    o_shape = jax.ShapeDtypeStruct((n_seq * seq_len, DN_W), BF16)
    aliases = {}
    if has_state:
        in_specs.append(pl.BlockSpec((1, 1, 2, hps, DN_DK, DN_DV), lambda b, h: (b, layer, 0, h, 0, 0)))
        args.append(s0)
        out_specs, out_shape = o_spec, o_shape
    else:
        if layer == 0:
            s_spec = pl.BlockSpec((1, DEPTH, 2, hps, DN_DK, DN_DV), lambda b, h: (b, 0, 0, h, 0, 0))
        else:
            s_spec = pl.BlockSpec((1, 1, 2, hps, DN_DK, DN_DV), lambda b, h: (b, layer, 0, h, 0, 0))
            in_specs.append(pl.BlockSpec(memory_space=pl.ANY))
            args.append(states_so_far)
            aliases = {len(args) - 1: 1}
        out_specs = [o_spec, s_spec]
        out_shape = [o_shape, jax.ShapeDtypeStruct((n_seq, DEPTH, 2, DN_HEADS, DN_DK, DN_DV), F32)]
    scratch = [pltpu.VMEM((hps, seq_len, LANES), F32),
               pltpu.VMEM((hps, seq_len, LANES), F32),
               pltpu.VMEM((hps, seq_len, LANES), F32),
               pltpu.VMEM((hps, 2, seq_len, LANES), F32),
               pltpu.VMEM((hps, 2, seq_len, LANES), F32),
               pltpu.VMEM((hps, seq_len, LANES), F32),
               pltpu.VMEM((hps, 2, seq_len, LANES), BF16),
               pltpu.VMEM((hps, 2, n_chunks, DN_DK, DN_DK), BF16),
               pltpu.VMEM((hps, 2, n_chunks, DN_DK, DN_DV), F32),
               pltpu.VMEM((hps, 2, n_chunks, 1, LANES), F32),
               pltpu.VMEM((hps, 2, DN_DK, DN_DV), F32),
               pltpu.VMEM((seq_len + 2 * CONV_PAD, LANES), F32)]
    res = pl.pallas_call(
        functools.partial(_dn_kernel, seq_len=seq_len, has_state=has_state, hps=hps, layer=layer),
        grid=(n_seq, DN_HEADS // hps),
        in_specs=in_specs, out_specs=out_specs, out_shape=out_shape,
        input_output_aliases=aliases,
        scratch_shapes=scratch,
        compiler_params=_cparams(("arbitrary", "arbitrary")),
        name=f"deltanet_{seq_len}",
    )(*args)
    return (res, None) if has_state else (res[0], res[1])


def _rope(x, cos, sin_signed):
    lane = lax.broadcasted_iota(jnp.int32, x.shape, 1)
    partner = jnp.where((lane % 32) < 16, pltpu.roll(x, LANES - 16, axis=1), pltpu.roll(x, 16, axis=1))
    return x * cos + partner * sin_signed


DA_ROWS = 128


def _exp2_rows(s_parts):
    m = functools.reduce(jnp.maximum, [jnp.max(s, axis=-1, keepdims=True) for s in s_parts])
    return [jnp.exp2(s - m).astype(BF16) for s in s_parts]


def _da_kernel(*refs, layer, latent):
    if latent:
        (q_ref, k_ref, v_ref, lam_ref, sg_ref, cq_ref, sq_ref, ck_ref, sk_ref, ctxk_ref, ctxv_ref,
         o_ref, krot_s, vext_s) = refs
    else:
        refs = refs[:5] + refs[(5 if layer == 0 else 7):]
        q_ref, k_ref, v_ref, lam_ref, sg_ref, o_ref, ko_ref, vo_ref = refs
    with_ones = lambda v: jnp.concatenate([v, jnp.ones(v.shape, BF16)], axis=1)
    lam_init = 0.8 - 0.6 * math.exp(-0.3 * layer)
    lp = lam_ref[...]
    dots = jnp.sum(jnp.concatenate([lp[0:1] * lp[1:2], lp[2:3] * lp[3:4]], axis=0), axis=1, keepdims=True)
    e = jnp.exp(dots)
    lam = e[0:1, :] - e[1:2, :] + lam_init

    n_heads = q_ref.shape[1] // LANES
    tq = q_ref.shape[0]
    heads = []
    if latent:
        @pl.when(pl.program_id(2) == 0)
        def _():
            krot_s[...] = _rope(k_ref[...].astype(F32), ck_ref[...], sk_ref[...]).astype(BF16)
            vext_s[...] = with_ones(v_ref[...])
        heads.append((_rope(q_ref[...].astype(F32), cq_ref[...], sq_ref[...]),
                      [ctxk_ref[0, 0, 0].astype(BF16), krot_s[...]],
                      [with_ones(ctxv_ref[0, 0, 0].astype(BF16)), vext_s[...]]))
    else:
        for j in range(n_heads):
            cols = slice(j * LANES, (j + 1) * LANES)
            heads.append((q_ref[:, cols].astype(F32), [k_ref[:, cols]], [with_ones(v_ref[:, cols])]))
            ko_ref[0, 0, j] = k_ref[:, cols].astype(F32)
            vo_ref[0, 0, j] = v_ref[:, cols].astype(F32)
            if layer == 0:
                for later in range(1, DEPTH):
                    ko_ref[0, later, j] = jnp.zeros((tq, LANES), F32)
                    vo_ref[0, later, j] = jnp.zeros((tq, LANES), F32)
    lane = lax.broadcasted_iota(jnp.int32, (tq, LANES), 1)
    groups = []
    for q, keys, vals in heads:
        q = q * (DA_DH ** -0.5 * math.log2(math.e))
        q12 = jnp.concatenate([jnp.where(lane < DA_DH, q, 0.0), jnp.where(lane >= DA_DH, q, 0.0)],
                              axis=0).astype(BF16)
        groups += [(q12[r:r + DA_ROWS], keys, vals) for r in range(0, 2 * tq, DA_ROWS)]
    scores = [[_bdot_nt(qg, kk) for kk in keys] for qg, keys, _ in groups]
    soft = [_exp2_rows(sg) for sg in scores]
    pv = []
    for e_parts, (_, _, vals) in zip(soft, groups):
        acc = None
        for e, vv in zip(e_parts, vals):
            part = _bdot(e, vv)
            acc = part if acc is None else acc + part
        pv.append(acc[:, :DA_DV] * (1.0 / acc[:, DA_DV:]))
    per_head = 2 * tq // DA_ROWS
    for j in range(len(heads)):
        hp = jnp.concatenate(pv[j * per_head:(j + 1) * per_head], axis=0)
        o = hp[:tq] - lam * hp[tq:]
        o_ref[:, j * LANES:(j + 1) * LANES] = (_rms(o, sg_ref[...]) * (1.0 - lam_init)).astype(BF16)


def _diff_attention(p_main, lam_p, subln_g, layer, *, n_seq, seq_len, row_blk0, tq, rope=None, ctx=None,
                    caches_so_far=None):
    latent = ctx is not None
    aliases = {}
    hps = 1 if latent else DA_HEADS
    width = hps * LANES
    nq = seq_len // tq
    qpb = seq_len // tq
    in_specs = [pl.BlockSpec((tq, width), lambda b, h, i: ((row_blk0 + b) * qpb + i, CB_QB // hps + h)),
                pl.BlockSpec((seq_len, width), lambda b, h, i: (row_blk0 + b, CB_KB // hps + h)),
                pl.BlockSpec((seq_len, width), lambda b, h, i: (row_blk0 + b, CB_VB // hps + h)),
                pl.BlockSpec((4, LANES), lambda b, h, i: (0, 0)),
                pl.BlockSpec((1, DA_DV), lambda b, h, i: (0, 0))]
    args = [p_main, p_main, p_main, jnp.pad(lam_p, ((0, 0), (0, LANES - DA_DH))), subln_g.reshape(1, DA_DV)]
    o_spec = pl.BlockSpec((tq, width), lambda b, h, i: (b * qpb + i, h))
    o_shape = jax.ShapeDtypeStruct((n_seq * seq_len, DA_HEADS * DA_DV), BF16)
    scratch = []
    if latent:
        cos, sin_signed = rope
        ctx_k, ctx_v = ctx
        n_ctx = ctx_k.shape[3]
        in_specs += [pl.BlockSpec((tq, LANES), lambda b, h, i: (i, 0)),
                     pl.BlockSpec((tq, LANES), lambda b, h, i: (i, 0)),
                     pl.BlockSpec((seq_len, LANES), lambda b, h, i: (0, 0)),
                     pl.BlockSpec((seq_len, LANES), lambda b, h, i: (0, 0)),
                     pl.BlockSpec((1, 1, 1, n_ctx, DA_DV), lambda b, h, i: (b, layer, h, 0, 0)),
                     pl.BlockSpec((1, 1, 1, n_ctx, DA_DV), lambda b, h, i: (b, layer, h, 0, 0))]
        args += [cos, sin_signed, cos, sin_signed, ctx_k, ctx_v]
        out_specs, out_shape = o_spec, o_shape
        scratch = [pltpu.VMEM((seq_len, LANES), BF16), pltpu.VMEM((seq_len, 2 * DA_DV), BF16)]
    else:
        if layer == 0:
            kv_spec = pl.BlockSpec((1, DEPTH, hps, seq_len, DA_DV), lambda b, h, i: (b, 0, h, 0, 0))
        else:
            kv_spec = pl.BlockSpec((1, 1, hps, seq_len, DA_DV), lambda b, h, i: (b, layer, h, 0, 0))
            in_specs += [pl.BlockSpec(memory_space=pl.ANY), pl.BlockSpec(memory_space=pl.ANY)]
            args += list(caches_so_far)
            aliases = {len(args) - 2: 1, len(args) - 1: 2}
        kv_shape = jax.ShapeDtypeStruct((n_seq, DEPTH, DA_HEADS, seq_len, DA_DV), F32)
        out_specs = [o_spec, kv_spec, kv_spec]
        out_shape = [o_shape, kv_shape, kv_shape]
    res = pl.pallas_call(
        functools.partial(_da_kernel, layer=layer, latent=latent),
        grid=(n_seq, DA_HEADS // hps, nq),
        in_specs=in_specs, out_specs=out_specs, out_shape=out_shape,
        input_output_aliases=aliases,
        scratch_shapes=scratch,
        compiler_params=_cparams(("arbitrary", "arbitrary", "arbitrary")),
        name=f"diff_attention_{seq_len}",
    )(*args)
    return (res, None, None) if latent else tuple(res)


def _rope_tables(n_tok):
    half = DA_DH // 2
    inv = ROPE_BASE ** (-jnp.arange(0, half, 2, dtype=F32) / half)
    t = jnp.arange(n_tok)
    ang_r = (t // GRID_W).astype(F32)[:, None] * inv
    ang_c = (t % GRID_W).astype(F32)[:, None] * inv
    cos32 = lambda a: jnp.concatenate([jnp.cos(a), jnp.cos(a)], axis=-1)
    sin32 = lambda a: jnp.concatenate([-jnp.sin(a), jnp.sin(a)], axis=-1)
    cos = jnp.concatenate([cos32(ang_r), cos32(ang_c)] * 2, axis=-1)
    sin_signed = jnp.concatenate([sin32(ang_r), sin32(ang_c)] * 2, axis=-1)
    return cos, sin_signed


def _dft_tables(n):
    t_lo = 64
    k = jnp.arange(n, dtype=jnp.int32)[:, None]
    ang = lambda m: ((k * m[None, :]) % (2 * n)).astype(F32) * (math.pi / n)
    a = ang(t_lo * jnp.arange(n // t_lo, dtype=jnp.int32))
    b = ang(jnp.arange(t_lo, dtype=jnp.int32))
    ca, sa, cb, sb = jnp.cos(a)[:, :, None], jnp.sin(a)[:, :, None], jnp.cos(b)[:, None, :], jnp.sin(b)[:, None, :]
    cos_t = (ca * cb - sa * sb).reshape(n, n)
    nsin_t = -(sa * cb + ca * sb).reshape(n, n)
    return cos_t.astype(BF16), nsin_t.astype(BF16)


def _hy_embedding(n):
    j = jnp.arange(n, dtype=F32)
    t = j / (n - 1)
    ang = (2.0 * math.pi * j / n)[:, None] * jnp.linspace(1e-4, HY_BANDS - 1, HY_BANDS, dtype=F32)
    z = jnp.concatenate([t[:, None], jnp.cos(ang), -jnp.sin(ang)], axis=-1)
    half = n // 2
    dist = jnp.abs(j - half) / half
    max_decay = math.log(HY_TARGET) / HY_FAST_DECAY
    min_decay = math.log(HY_TARGET) / HY_SLOW_DECAY
    deltas = jnp.abs(jnp.linspace(min_decay, max_decay, HY_W, dtype=F32))
    return jnp.pad(z, ((0, 0), (0, LANES - HY_EMB))), dist[:, None], deltas[None, :]


def _alt_rows(n):
    t = lax.broadcasted_iota(jnp.int32, (8, n), 1)
    return (1 - 2 * (t % 2)).astype(F32)


def _hy_filter_kernel(z_ref, dist_ref, delta_ref, w1_ref, b1_ref, fr_ref, w2_ref, b2_ref, w3_ref,
                      cos_ref, nsin_ref, hre_ref, him_ref, hny_ref, h_s):
    @pl.when(pl.program_id(1) == 0)
    def _():
        fr = fr_ref[...]
        hdn = jnp.sin(fr * (_dot3(z_ref[...], w1_ref[...]) + b1_ref[...]))
        hdn = jnp.sin(fr * (_dot3(hdn, w2_ref[...]) + b2_ref[...]))
        h = _dot3(hdn, w3_ref[...])
        h = h * jnp.exp(-dist_ref[...] * delta_ref[...])
        h = h / jnp.sum(jnp.abs(h), axis=0, keepdims=True)
        h_s[...] = h.astype(BF16)
        hny_ref[...] = _dot_exact_lhs(_alt_rows(h.shape[0]), h)

    hre_ref[...] = jnp.dot(cos_ref[...], h_s[...], preferred_element_type=F32)
    him_ref[...] = jnp.dot(nsin_ref[...], h_s[...], preferred_element_type=F32)


def _hyena_filter_spectrum(n, emb, tables, w1, b1, freq, w2, b2, w3):
    z, dist, deltas = emb
    pad_o = LANES - HY_ORDER
    w1p = jnp.pad(w1, ((0, LANES - HY_EMB), (0, pad_o)))
    w2p = jnp.pad(w2, ((0, pad_o), (0, pad_o)))
    w3p = jnp.pad(w3, ((0, pad_o), (0, 0)))
    row = lambda t: jnp.pad(t.reshape(1, HY_ORDER), ((0, 0), (0, pad_o)))
    tc = HY_W
    kt = min(n, 512)
    full = lambda shape: pl.BlockSpec(shape, lambda j, k: (0, 0))
    tab = pl.BlockSpec((kt, n), lambda j, k: (k, 0))
    spec = pl.BlockSpec((kt, tc), lambda j, k: (k, j))
    return pl.pallas_call(
        _hy_filter_kernel,
        grid=(HY_W // tc, n // kt),
        in_specs=[full((n, LANES)), full((n, 1)), pl.BlockSpec((1, tc), lambda j, k: (0, j)),
                  full((LANES, LANES)), full((1, LANES)), full((1, LANES)),
                  full((LANES, LANES)), full((1, LANES)), pl.BlockSpec((LANES, tc), lambda j, k: (0, j)),
                  tab, tab],
        out_specs=[spec, spec, pl.BlockSpec((8, tc), lambda j, k: (0, j))],
        out_shape=[jax.ShapeDtypeStruct((n, HY_W), F32), jax.ShapeDtypeStruct((n, HY_W), F32),
                   jax.ShapeDtypeStruct((8, HY_W), F32)],
        scratch_shapes=[pltpu.VMEM((n, tc), BF16)],
        compiler_params=_cparams(("arbitrary", "arbitrary")),
        name=f"hyena_filter_{n}",
    )(z, dist, deltas, w1p, row(b1), row(freq), w2p, row(b2), w3p, *tables)


def _hy_pre_kernel(x0_ref, x1_ref, v_ref, w0_ref, w1_ref, w2_ref, b0_ref, b1_ref, b2_ref, x0c_ref, gv_ref,
                   pad_s, x1_s, *, seq_len):
    _zero_conv_borders(pad_s)
    for base in range(0, x0_ref.shape[0], seq_len):
        def conv(x_ref, w_ref, b_ref, store):
            _centred_conv(pad_s, lambda r0, rb: x_ref[base + r0:base + r0 + rb, :].astype(F32), store, seq_len,
                          w_ref, HY_SHORT, b_ref[...])

        def store_x0(r0, y):
            x0c_ref[base + r0:base + r0 + y.shape[0], :] = y.astype(BF16)

        def store_x1(r0, y):
            x1_s[r0:r0 + y.shape[0], :] = y

        def store_gv(r0, y):
            gv_ref[base + r0:base + r0 + y.shape[0], :] = (y * x1_s[r0:r0 + y.shape[0], :]).astype(BF16)

        conv(x0_ref, w0_ref, b0_ref, store_x0)
        conv(x1_ref, w1_ref, b1_ref, store_x1)
        conv(v_ref, w2_ref, b2_ref, store_gv)


def _hyena_pre(p_main, conv_w, conv_b):
    outs = []
    for n_seq, seq_len, row_blk0, spb, tc in ((BATCH, SEQ, 0, 4, HY_W),
                                               (DEC_BATCH, DEC_SEQ, N_PROMPT // DEC_SEQ, 1, HY_W // 2)):
        ncb = HY_W // tc
        rows = spb * seq_len
        tok = lambda cb: pl.BlockSpec((rows, tc), lambda b, j, cb=cb: (row_blk0 + b, cb * LANES // tc + j))
        cw = lambda s: pl.BlockSpec((HY_SHORT, tc), lambda b, j, s=s: (0, s * ncb + j))
        cb_ = lambda s: pl.BlockSpec((1, tc), lambda b, j, s=s: (0, s * ncb + j))
        o_spec = pl.BlockSpec((rows, tc), lambda b, j: (b, j))
        o_shape = jax.ShapeDtypeStruct((n_seq * seq_len, HY_W), BF16)
        outs.append(pl.pallas_call(
            functools.partial(_hy_pre_kernel, seq_len=seq_len),
            grid=(n_seq // spb, ncb),
            in_specs=[tok(CB_X0), tok(CB_X1), tok(CB_HV), cw(0), cw(1), cw(2), cb_(0), cb_(1), cb_(2)],
            out_specs=[o_spec, o_spec], out_shape=[o_shape, o_shape],
            scratch_shapes=[pltpu.VMEM((seq_len + 2 * CONV_PAD, tc), F32), pltpu.VMEM((seq_len, tc), F32)],
            compiler_params=_cparams(("arbitrary", "arbitrary")),
            name=f"hyena_pre_{seq_len}",
        )(p_main, p_main, p_main, conv_w, conv_w, conv_w,
          conv_b.reshape(1, -1), conv_b.reshape(1, -1), conv_b.reshape(1, -1)))
    return outs


HY_FREQ_BLOCK = 512


def _hy_conv_kernel(gv_ref, x0_ref, d_ref, hre_ref, him_ref, hny_ref, cos_ref, nsin_ref, o_ref, *,
                    seq_len, n_seq):
    n = seq_len
    fb = min(n, HY_FREQ_BLOCK)
    chains = [(b, f0) for b in range(n_seq) for f0 in range(0, n, fb)]
    gvs = [gv_ref[pl.ds(b * seq_len, seq_len), :] for b in range(n_seq)]
    gre = [jnp.dot(cos_ref[f0:f0 + fb, :], gvs[b], preferred_element_type=F32) for b, f0 in chains]
    gim = [jnp.dot(nsin_ref[f0:f0 + fb, :], gvs[b], preferred_element_type=F32) for b, f0 in chains]
    spec = []
    for (b, f0), gr, gi in zip(chains, gre, gim):
        hre, him = hre_ref[f0:f0 + fb, :], him_ref[f0:f0 + fb, :]
        freq = lax.broadcasted_iota(jnp.int32, hre.shape, 0) + f0
        wk = jnp.where(freq == 0, 0.5 / n, 1.0 / n)
        q4 = freq % 4
        yre = (gr * hre - gi * him) * wk
        yim = (gr * him + gi * hre) * wk
        spec.append((jnp.where(q4 == 0, yre, jnp.where(q4 == 1, -yim, jnp.where(q4 == 2, -yre, yim))),
                     jnp.where(q4 == 0, yim, jnp.where(q4 == 1, yre, jnp.where(q4 == 2, -yim, -yre)))))
    inv = [_bdot(cos_ref[:, f0:f0 + fb], are) + _bdot(nsin_ref[:, f0:f0 + fb], aim)
           for (b, f0), (are, aim) in zip(chains, spec)]
    t = lax.broadcasted_iota(jnp.int32, (n, gvs[0].shape[1]), 0)
    alt = (1 - 2 * (t % 2)).astype(F32)
    for b in range(n_seq):
        rows = pl.ds(b * seq_len, seq_len)
        gvf = gvs[b].astype(F32)
        g_ny = jnp.sum(gvf * alt, axis=0, keepdims=True)
        y_ny = g_ny * hny_ref[0:1, :] * ((-1.0) ** (n // 2) / (2 * n))
        y = functools.reduce(lambda x, z: x + z, [p for (cb, _), p in zip(chains, inv) if cb == b])
        y = y + alt * y_ny + gvf * d_ref[...]
        o_ref[rows, :] = (y * x0_ref[rows, :].astype(F32)).astype(BF16)


def _hyena_conv(gv, x0c, d_skip, h_spec, tables, *, n_seq_total, seq_len, n_seq_blk):
    tc = 256
    hre, him, hny = h_spec
    cos_t, nsin_t = tables
    rows = n_seq_blk * seq_len
    grid = (HY_W // tc, n_seq_total // n_seq_blk)
    tok = pl.BlockSpec((rows, tc), lambda j, g: (g, j))
    hs = pl.BlockSpec((seq_len, tc), lambda j, g: (0, j))
    table = pl.BlockSpec((seq_len, seq_len), lambda j, g: (0, 0), pipeline_mode=pl.Buffered(1))
    return pl.pallas_call(
        functools.partial(_hy_conv_kernel, seq_len=seq_len, n_seq=n_seq_blk),
        grid=grid,
        in_specs=[tok, tok, pl.BlockSpec((1, tc), lambda j, g: (0, j)), hs, hs,
                  pl.BlockSpec((8, tc), lambda j, g: (0, j)), table, table],
        out_specs=tok,
        out_shape=jax.ShapeDtypeStruct((n_seq_total * seq_len, HY_W), BF16),
        compiler_params=_cparams(("arbitrary", "arbitrary")),
        name=f"hyena_conv_{seq_len}",
    )(gv, x0c, d_skip.reshape(1, HY_W), hre, him, hny, cos_t, nsin_t)


MERGE_TM = 512


def _merge_kernel(x_ref, oap_ref, obp_ref, ocp_ref, oas_ref, obs_ref, ocs_ref, ga_ref, gb_ref, gc_ref,
                  wa_ref, wb_ref, wc_ref, wo_ref, mod_ref, y_ref, wa_s, wb_s, wc_s, wo_s):
    i = pl.program_id(0)

    @pl.when(i == 0)
    def _():
        for src, dst in ((wa_ref, wa_s), (wb_ref, wb_s), (wc_ref, wc_s), (wo_ref, wo_s)):
            dst[...] = src[...].astype(BF16)

    is_prompt = i < N_PROMPT // MERGE_TM
    pick = lambda p_ref, s_ref: jnp.where(is_prompt, p_ref[...], s_ref[...])
    sig = lambda r: jax.nn.sigmoid(r[...].astype(F32))
    merged = (sig(ga_ref) * jnp.dot(pick(oap_ref, oas_ref), wa_s[...], preferred_element_type=F32)
              + sig(gb_ref) * jnp.dot(pick(obp_ref, obs_ref), wb_s[...], preferred_element_type=F32)
              + sig(gc_ref) * jnp.dot(pick(ocp_ref, ocs_ref), wc_s[...], preferred_element_type=F32))
    out = _bdot(merged, wo_s[...])
    y_ref[...] = x_ref[...] + mod_ref[0, 2:3, :] * out


def _merge(x, branches_p, branches_s, p_main, mods, w_a, w_b, w_c, w_out, layer):
    tm = MERGE_TM
    npt = N_PROMPT // tm
    tok = lambda w: pl.BlockSpec((tm, w), lambda i: (i, 0))
    tok_p = pl.BlockSpec((tm, DN_W), lambda i: (jnp.minimum(i, npt - 1), 0))
    tok_s = pl.BlockSpec((tm, DN_W), lambda i: (jnp.maximum(i - npt, 0), 0))
    gate = lambda s: pl.BlockSpec((tm, D_MODEL), lambda i, s=s: (i, CB_GATE // 8 + s))
    wfull = lambda r: pl.BlockSpec((None, r, D_MODEL), lambda i: (layer, 0, 0))
    wscr = lambda r: pltpu.VMEM((r, D_MODEL), BF16)
    return pl.pallas_call(
        _merge_kernel,
        grid=(N_TOK // tm,),
        in_specs=[tok(D_MODEL), tok_p, tok_p, tok_p, tok_s, tok_s, tok_s, gate(0), gate(1), gate(2),
                  wfull(DN_W), wfull(DN_W), wfull(HY_W), wfull(D_MODEL),
                  pl.BlockSpec((1, 6, D_MODEL), lambda i: (_mod_row(i, tm), 0, 0))],
        out_specs=tok(D_MODEL),
        out_shape=jax.ShapeDtypeStruct((N_TOK, D_MODEL), F32),
        scratch_shapes=[wscr(DN_W), wscr(DN_W), wscr(HY_W), wscr(D_MODEL)],
        compiler_params=_cparams(("arbitrary",)),
        name="merge_out_projection",
    )(x, *branches_p, *branches_s, p_main, p_main, p_main, w_a, w_b, w_c, w_out, mods)


FFN_TM = 512
FFN_TF = D_FF


def _ffn_kernel(x_ref, g_ref, mod_ref, wg_ref, wu_ref, wo_ref, fg_ref, *rest, final):
    if final:
        yp_ref, ys_ref, h_scr, acc_scr = rest
    else:
        y_ref, h_scr, acc_scr = rest
    f = pl.program_id(1)
    nf = pl.num_programs(1)

    @pl.when(f == 0)
    def _():
        g = g_ref[...]
        sh = mod_ref[0, 3:4, :]
        sc1 = 1.0 + mod_ref[0, 4:5, :]

        def body(r, carry):
            rows = pl.ds(pl.multiple_of(r * ROW_CHUNK, ROW_CHUNK), ROW_CHUNK)
            h_scr[rows, :] = (_rms(x_ref[rows, :], g) * sc1 + sh).astype(BF16)
            return carry

        lax.fori_loop(0, FFN_TM // ROW_CHUNK, body, 0)

    h = h_scr[...]
    gate = jnp.dot(h, wg_ref[...], preferred_element_type=F32)
    up = jnp.dot(h, wu_ref[...], preferred_element_type=F32)
    part = _bdot(_silu(gate) * up, wo_ref[...])

    @pl.when(f == 0)
    def _():
        acc_scr[...] = part

    @pl.when(f > 0)
    def _():
        acc_scr[...] += part

    @pl.when(f == nf - 1)
    def _():
        y = x_ref[...] + mod_ref[0, 5:6, :] * acc_scr[...]
        if final:
            y = _rms(y, fg_ref[...])
            is_prompt = pl.program_id(0) < N_PROMPT // FFN_TM

            @pl.when(is_prompt)
            def _():
                yp_ref[...] = y

            @pl.when(jnp.logical_not(is_prompt))
            def _():
                ys_ref[...] = y
        else:
            y_ref[...] = y


def _ffn(x, norm_g, mods, w_in, w_out, final_g, layer, *, final):
    tm, tf = FFN_TM, FFN_TF
    nf = D_FF // tf
    if final:
        npt = N_PROMPT // tm
        out_specs = [pl.BlockSpec((tm, D_MODEL), lambda i, f: (jnp.minimum(i, npt - 1), 0)),
                     pl.BlockSpec((tm, D_MODEL), lambda i, f: (jnp.maximum(i - npt, 0), 0))]
        out_shape = [jax.ShapeDtypeStruct((N_PROMPT, D_MODEL), F32), jax.ShapeDtypeStruct((N_SAMPLE, D_MODEL), F32)]
    else:
        out_specs = pl.BlockSpec((tm, D_MODEL), lambda i, f: (i, 0))
        out_shape = jax.ShapeDtypeStruct((N_TOK, D_MODEL), F32)
    return pl.pallas_call(
        functools.partial(_ffn_kernel, final=final),
        grid=(N_TOK // tm, nf),
        in_specs=[pl.BlockSpec((tm, D_MODEL), lambda i, f: (i, 0)),
                  pl.BlockSpec((1, D_MODEL), lambda i, f: (0, 0)),
                  pl.BlockSpec((1, 6, D_MODEL), lambda i, f: (_mod_row(i, tm), 0, 0)),
                  pl.BlockSpec((None, D_MODEL, tf), lambda i, f: (layer, 0, f), pipeline_mode=pl.Buffered(1)),
                  pl.BlockSpec((None, D_MODEL, tf), lambda i, f: (layer, 0, nf + f), pipeline_mode=pl.Buffered(1)),
                  pl.BlockSpec((None, tf, D_MODEL), lambda i, f: (layer, f, 0), pipeline_mode=pl.Buffered(1)),
                  pl.BlockSpec((1, D_MODEL), lambda i, f: (0, 0))],
        out_specs=out_specs,
        out_shape=out_shape,
        scratch_shapes=[pltpu.VMEM((tm, D_MODEL), BF16), pltpu.VMEM((tm, D_MODEL), F32)],
        compiler_params=_cparams(("arbitrary", "arbitrary")),
        name="ffn",
    )(x, norm_g.reshape(1, D_MODEL), mods, w_in, w_in, w_out, final_g.reshape(1, D_MODEL))


def kernel(x_prompt, x_sample, cache_k, cache_v, state_dn, c, c_ctx, norm1_g, norm2_g, w_mod, b_mod,
           w_in, dn_conv_w, dn_a_log, dn_dt_bias, dn_norm_g, da_lambda, da_subln_g, hy_conv_w,
           hy_conv_b, hy_w1, hy_b1, hy_freq, hy_w2, hy_b2, hy_w3, hy_d, w_br_a, w_br_b, w_br_c,
           w_out, w_ffn_in, w_ffn_out, final_g):
    x = jnp.concatenate([x_prompt.reshape(N_PROMPT, D_MODEL), x_sample.reshape(N_SAMPLE, D_MODEL)], axis=0)
    cond8 = jnp.concatenate([c_ctx[None, :], c, jnp.zeros((8 - 1 - DEC_BATCH, D_MODEL), F32)], axis=0)
    rope = _rope_tables(DEC_SEQ)
    tab_p = _dft_tables(SEQ)
    tab_s = _dft_tables(DEC_SEQ)
    emb_p, emb_s = _hy_embedding(SEQ), _hy_embedding(DEC_SEQ)
    sample_blk0 = N_PROMPT // DEC_SEQ
    w_main, w_small = _pack_w_in(w_in)
    w_ffn_in_b, w_ffn_out_b = w_ffn_in.astype(BF16), w_ffn_out.astype(BF16)

    new_k = new_v = new_s = None
    for l in range(DEPTH):
        mods = _modulation(cond8, w_mod, b_mod[l], l).reshape(8, 6, D_MODEL)
        p_main, p_small = _in_projection(x, norm1_g[l], mods, w_main, w_small, l)

        dn_args = (p_main, p_small, dn_conv_w[l], dn_a_log[l], dn_dt_bias[l], dn_norm_g[l])
        oa_p, new_s = _deltanet(*dn_args, None, n_seq=BATCH, seq_len=SEQ, row_blk0=0, hps=4, layer=l,
                                states_so_far=new_s)
        oa_s, _ = _deltanet(*dn_args, state_dn, n_seq=DEC_BATCH, seq_len=DEC_SEQ, row_blk0=sample_blk0,
                            hps=2, layer=l)

        ob_p, new_k, new_v = _diff_attention(p_main, da_lambda[l], da_subln_g[l], l, n_seq=BATCH, seq_len=SEQ,
                                             row_blk0=0, tq=SEQ, caches_so_far=(new_k, new_v))
        ob_s, _, _ = _diff_attention(p_main, da_lambda[l], da_subln_g[l], l, n_seq=DEC_BATCH,
                                     seq_len=DEC_SEQ, row_blk0=sample_blk0, tq=1024, rope=rope,
                                     ctx=(cache_k, cache_v))

        hy_w = (hy_w1[l], hy_b1[l], hy_freq[l], hy_w2[l], hy_b2[l], hy_w3[l])
        hspec_p = _hyena_filter_spectrum(SEQ, emb_p, tab_p, *hy_w)
        hspec_s = _hyena_filter_spectrum(DEC_SEQ, emb_s, tab_s, *hy_w)
        (x0_p, gv_p), (x0_s, gv_s) = _hyena_pre(p_main, hy_conv_w[l], hy_conv_b[l])
        oc_p = _hyena_conv(gv_p, x0_p, hy_d[l], hspec_p, tab_p, n_seq_total=BATCH, seq_len=SEQ,
                           n_seq_blk=BATCH)
        oc_s = _hyena_conv(gv_s, x0_s, hy_d[l], hspec_s, tab_s, n_seq_total=DEC_BATCH, seq_len=DEC_SEQ,
                           n_seq_blk=1)

        x = _merge(x, (oa_p, ob_p, oc_p), (oa_s, ob_s, oc_s), p_main, mods,
                   w_br_a, w_br_b, w_br_c, w_out, l)
        x = _ffn(x, norm2_g[l], mods, w_ffn_in_b, w_ffn_out_b, final_g, l, final=(l == DEPTH - 1))

    y_prompt = x[0].reshape(BATCH, SEQ, D_MODEL)
    y_sample = x[1].reshape(DEC_BATCH, DEC_SEQ, D_MODEL)
    return (y_prompt, y_sample, new_k, new_v, new_s)
```
